```python
import math
import jax
import jax.numpy as jnp
from jax import lax
import numpy as np

D_MODEL = 2048
BATCH = 2
SEQ = 4096
DEPTH = 2

F32 = jnp.float32
EPS = 1e-6
ATTN_HEADS = 6
ATTN_QK_DIM = 64
ATTN_V_DIM = 2 * ATTN_QK_DIM
ATTN_WIDTH = ATTN_HEADS * ATTN_V_DIM
Q_BLOCK = 128
REL_BUCKETS = 32
REL_MAX_DIST = 128
S5_GROUPS = 40
S5_GROUP_CH = 16
S5_STATE = 64
S5_WIDTH = S5_GROUPS * S5_GROUP_CH
HGRN_HEADS = 5
HGRN_DIM = 128
HGRN_WIDTH = HGRN_HEADS * HGRN_DIM
HGRN_CHUNK = 16
MIX_WIDTH = ATTN_WIDTH + S5_WIDTH + HGRN_WIDTH
SPLIT_SIZES = (ATTN_WIDTH, ATTN_WIDTH, ATTN_WIDTH, S5_WIDTH, HGRN_WIDTH, HGRN_WIDTH, HGRN_WIDTH, HGRN_WIDTH)
IN_COLS = 3 * ATTN_WIDTH + S5_WIDTH + 4 * HGRN_WIDTH
N_GROUPS = 4
EXPERTS_PER_GROUP = 8
N_EXPERTS = N_GROUPS * EXPERTS_PER_GROUP
TOP_K_IN_GROUP = 2
D_EXPERT = 512

kernel_name = 'hymba_style_diffattn_s5_hgrn2_hmoe'


def rms_norm(x, g):
    xf = x.astype(F32)
    y = xf * lax.rsqrt(jnp.mean(xf * xf, axis=-1, keepdims=True) + EPS)
    return (y * g.astype(F32)).astype(x.dtype)


def t5_bucket(dist):
    n = jnp.maximum(dist, 0)
    max_exact = REL_BUCKETS // 2
    large = max_exact + (jnp.log(jnp.maximum(n, 1).astype(F32) / max_exact)
                         / math.log(REL_MAX_DIST / max_exact) * (REL_BUCKETS - max_exact)).astype(jnp.int32)
    large = jnp.minimum(large, REL_BUCKETS - 1)
    return jnp.where(n < max_exact, n, large)


def diff_attention(q, k, v, rel_bias, lam_vecs, subln_g, layer_idx):
    B_, S_ = q.shape[0], q.shape[1]
    n_blocks = S_ // Q_BLOCK
    scale = ATTN_QK_DIM ** -0.5
    lam_init = 0.8 - 0.6 * math.exp(-0.3 * layer_idx)
    lv = lam_vecs.astype(F32)
    lam = jnp.exp(jnp.sum(lv[0] * lv[1])) - jnp.exp(jnp.sum(lv[2] * lv[3])) + lam_init
    qf = q.astype(F32).reshape(B_, n_blocks, Q_BLOCK, ATTN_HEADS, 2, ATTN_QK_DIM).transpose(1, 0, 2, 3, 4, 5)
    kf = k.astype(F32).reshape(B_, S_, ATTN_HEADS, 2, ATTN_QK_DIM)
    vf = v.astype(F32).reshape(B_, S_, ATTN_HEADS, ATTN_V_DIM)
    table = rel_bias.astype(F32)
    k_pos = jnp.arange(S_)

    def one_block(args):
        i, q_blk = args
        q_pos = i * Q_BLOCK + jnp.arange(Q_BLOCK)
        dist = q_pos[:, None] - k_pos[None, :]
        bias = jnp.transpose(table[t5_bucket(dist)], (2, 0, 1))
        logits = jnp.einsum('bqhmd,bkhmd->bhmqk', q_blk, kf) * scale + bias[None, :, None]
        logits = jnp.where((dist >= 0)[None, None, None], logits, -jnp.inf)
        p = jax.nn.softmax(logits, axis=-1)
        w = p[:, :, 0] - lam * p[:, :, 1]
        return jnp.einsum('bhqk,bkhv->bqhv', w, vf)

    out = lax.map(one_block, (jnp.arange(n_blocks), qf))
    out = out.transpose(1, 0, 2, 3, 4).reshape(B_, S_, ATTN_HEADS, ATTN_V_DIM)
    out = rms_norm(out, subln_g) * (1.0 - lam_init)
    return out.reshape(B_, S_, ATTN_WIDTH).astype(q.dtype)


def s5_mixer(u, a_re, a_im, log_dt, b_re, b_im, c_re, c_im, d_skip, w_glu):
    B_, S_ = u.shape[0], u.shape[1]
    uf = u.astype(F32).reshape(B_, S_, S5_GROUPS, S5_GROUP_CH)
    lr = jnp.minimum(a_re.astype(F32), -1e-4)
    li = a_im.astype(F32)
    dt = jnp.exp(log_dt.astype(F32))[:, None]
    mag = jnp.exp(lr * dt)
    ab_re = mag * jnp.cos(li * dt)
    ab_im = mag * jnp.sin(li * dt)
    den = lr * lr + li * li
    nr = ab_re - 1.0
    ni = ab_im
    f_re = (nr * lr + ni * li) / den
    f_im = (ni * lr - nr * li) / den
    br = b_re.astype(F32)
    bi = b_im.astype(F32)
    bb_re = f_re[..., None] * br - f_im[..., None] * bi
    bb_im = f_re[..., None] * bi + f_im[..., None] * br
    bu_re = jnp.einsum('bsgc,gnc->bsgn', uf, bb_re)
    bu_im = jnp.einsum('bsgc,gnc->bsgn', uf, bb_im)
    ar = jnp.broadcast_to(ab_re, bu_re.shape)
    ai = jnp.broadcast_to(ab_im, bu_re.shape)

    def combine(e1, e2):
        a1r, a1i, b1r, b1i = e1
        a2r, a2i, b2r, b2i = e2
        return (a1r * a2r - a1i * a2i, a1r * a2i + a1i * a2r,
                a2r * b1r - a2i * b1i + b2r, a2r * b1i + a2i * b1r + b2i)

    _, _, xr, xi = lax.associative_scan(combine, (ar, ai, bu_re, bu_im), axis=1)
    y = jnp.einsum('bsgn,gcn->bsgc', xr, c_re.astype(F32)) - jnp.einsum('bsgn,gcn->bsgc', xi, c_im.astype(F32))
    y = y + d_skip.astype(F32).reshape(S5_GROUPS, S5_GROUP_CH) * uf
    z = jax.nn.gelu(y.reshape(B_, S_, S5_WIDTH))
    out = z * jax.nn.sigmoid(jnp.einsum('bsc,ce->bse', z, w_glu.astype(F32)))
    return out.astype(u.dtype)


def hgrn2_mixer(q, f_logit, i_val, g, lb, norm_g):
    B_, S_ = q.shape[0], q.shape[1]
    N_ = S_ // HGRN_CHUNK
    shp = (B_, N_, HGRN_CHUNK, HGRN_HEADS, HGRN_DIM)
    lbf = lb.astype(F32)
    fl = f_logit.astype(F32)
    logf = jnp.logaddexp(jnp.log(lbf), jnp.log1p(-lbf) + jax.nn.log_sigmoid(fl)).reshape(shp)
    kf = ((1.0 - lbf) * jax.nn.sigmoid(-fl)).reshape(shp)
    qf = (jax.nn.silu(q.astype(F32)) * HGRN_DIM ** -0.5).reshape(shp)
    vf = i_val.astype(F32).reshape(shp)
    G = jnp.cumsum(logf, axis=2)
    causal = jnp.tril(jnp.ones((HGRN_CHUNK, HGRN_CHUNK), dtype=bool))
    diff = G[:, :, :, None] - G[:, :, None, :]
    decay = jnp.where(causal[None, None, :, :, None, None], jnp.exp(jnp.minimum(diff, 0.0)), 0.0)
    scores = jnp.einsum('bnthd,bnshd,bntshd->bnhts', qf, kf, decay)
    o_intra = jnp.einsum('bnhts,bnshv->bnthv', scores, vf)
    g_last = G[:, :, -1]
    k_dec = kf * jnp.exp(g_last[:, :, None] - G)
    U = jnp.einsum('bnshd,bnshv->bnhdv', k_dec, vf)
    chunk_decay = jnp.exp(g_last)

    def step(S, inp):
        dec, inc = inp
        return dec[..., None] * S + inc, S

    S0 = jnp.zeros((B_, HGRN_HEADS, HGRN_DIM, HGRN_DIM), F32)
    _, S_prev = lax.scan(step, S0, (jnp.moveaxis(chunk_decay, 1, 0), jnp.moveaxis(U, 1, 0)))
    o_inter = jnp.einsum('bnthd,nbhdv->bnthv', qf * jnp.exp(G), S_prev)
    o = (o_intra + o_inter).reshape(B_, S_, HGRN_HEADS, HGRN_DIM)
    o = rms_norm(o, norm_g) * jax.nn.sigmoid(g.astype(F32)).reshape(B_, S_, HGRN_HEADS, HGRN_DIM)
    return o.reshape(B_, S_, HGRN_WIDTH).astype(q.dtype)


def hier_moe(x, w_group, w_router, w_gate, w_up, w_down):
    B_, S_, D_ = x.shape
    xt = x.reshape(-1, D_)
    T_ = xt.shape[0]
    xr = xt.astype(F32)
    gl = jnp.einsum('td,dg->tg', xr, w_group.astype(F32))
    gp = jax.nn.softmax(gl, axis=-1)
    g_idx = jnp.argmax(gl, axis=-1)
    p_g = jnp.take_along_axis(gp, g_idx[:, None], axis=1)[:, 0]
    el = jnp.einsum('td,de->te', xr, w_router.astype(F32)).reshape(T_, N_GROUPS, EXPERTS_PER_GROUP)
    el_sel = jnp.take_along_axis(el, g_idx[:, None, None], axis=1)[:, 0]
    top_vals, top_idx = lax.top_k(el_sel, TOP_K_IN_GROUP)
    gates = jax.nn.softmax(top_vals, axis=-1) * p_g[:, None]
    expert_id = (g_idx[:, None] * EXPERTS_PER_GROUP + top_idx).reshape(-1)
    gates_flat = gates.reshape(-1)
    order = jnp.argsort(expert_id)
    token_of = order // TOP_K_IN_GROUP
    xs = xt[token_of]
    sizes = jnp.bincount(expert_id, length=N_EXPERTS).astype(jnp.int32)
    h = jax.nn.silu(lax.ragged_dot(xs, w_gate, sizes)) * lax.ragged_dot(xs, w_up, sizes)
    y = lax.ragged_dot(h, w_down, sizes)
    y = y * gates_flat[order][:, None].astype(y.dtype)
    out = jnp.zeros_like(xt).at[token_of].add(y)
    return out.reshape(B_, S_, D_)


def setup_inputs(seed: int = 0) -> dict:
    key = jax.random.key(seed)
    ks = jax.random.split(key, 26)

    def nrm(k, shape, scale):
        return jax.random.normal(k, shape, F32) * scale

    return {
        'x': nrm(ks[0], (BATCH, SEQ, D_MODEL), 1.0),
        'w_in': nrm(ks[1], (DEPTH, D_MODEL, IN_COLS), D_MODEL ** -0.5),
        'w_out': nrm(ks[2], (DEPTH, MIX_WIDTH, D_MODEL), MIX_WIDTH ** -0.5),
        'mix_norm_g': 1.0 + nrm(ks[3], (DEPTH, D_MODEL), 0.02),
        'ffn_norm_g': 1.0 + nrm(ks[4], (DEPTH, D_MODEL), 0.02),
        'rel_bias': nrm(ks[5], (REL_BUCKETS, ATTN_HEADS), 0.5),
        'diff_lambda': nrm(ks[6], (DEPTH, 4, ATTN_QK_DIM), 0.1),
        'attn_subln_g': 1.0 + nrm(ks[7], (DEPTH, ATTN_V_DIM), 0.02),
        's5_a_re': -0.5 + nrm(ks[8], (DEPTH, S5_GROUPS, S5_STATE), 0.01),
        's5_a_im': math.pi * jnp.arange(S5_STATE, dtype=F32) + nrm(ks[9], (DEPTH, S5_GROUPS, S5_STATE), 0.01),
        's5_log_dt': jax.random.uniform(ks[10], (DEPTH, S5_GROUPS), F32, math.log(1e-3), math.log(1e-1)),
        's5_b_re': nrm(ks[11], (DEPTH, S5_GROUPS, S5_STATE, S5_GROUP_CH), (2 * S5_GROUP_CH) ** -0.5),
        's5_b_im': nrm(ks[12], (DEPTH, S5_GROUPS, S5_STATE, S5_GROUP_CH), (2 * S5_GROUP_CH) ** -0.5),
        's5_c_re': nrm(ks[13], (DEPTH, S5_GROUPS, S5_GROUP_CH, S5_STATE), S5_STATE ** -0.5),
        's5_c_im': nrm(ks[14], (DEPTH, S5_GROUPS, S5_GROUP_CH, S5_STATE), S5_STATE ** -0.5),
        's5_d': nrm(ks[15], (DEPTH, S5_WIDTH), 1.0),
        's5_w_glu': nrm(ks[16], (DEPTH, S5_WIDTH, S5_WIDTH), S5_WIDTH ** -0.5),
        'hgrn_lb_logits': nrm(ks[17], (DEPTH, HGRN_WIDTH), 0.5),
        'hgrn_norm_g': 1.0 + nrm(ks[18], (DEPTH, HGRN_DIM), 0.02),
        'moe_w_group': nrm(ks[19], (DEPTH, D_MODEL, N_GROUPS), D_MODEL ** -0.5),
        'moe_w_router': nrm(ks[20], (DEPTH, D_MODEL, N_EXPERTS), D_MODEL ** -0.5),
        'moe_w_gate': nrm(ks[21], (DEPTH, N_EXPERTS, D_MODEL, D_EXPERT), D_MODEL ** -0.5),
        'moe_w_up': nrm(ks[22], (DEPTH, N_EXPERTS, D_MODEL, D_EXPERT), D_MODEL ** -0.5),
        'moe_w_down': nrm(ks[23], (DEPTH, N_EXPERTS, D_EXPERT, D_MODEL), D_EXPERT ** -0.5),
        'final_norm_g': 1.0 + nrm(ks[24], (D_MODEL,), 0.02),
    }


def reference(x, w_in, w_out, mix_norm_g, ffn_norm_g, rel_bias, diff_lambda, attn_subln_g,
              s5_a_re, s5_a_im, s5_log_dt, s5_b_re, s5_b_im, s5_c_re, s5_c_im, s5_d, s5_w_glu,
              hgrn_lb_logits, hgrn_norm_g, moe_w_group, moe_w_router, moe_w_gate, moe_w_up,
              moe_w_down, final_norm_g):
    lb_cum = jnp.cumsum(jax.nn.softmax(hgrn_lb_logits.astype(F32), axis=0), axis=0)
    lb_all = lb_cum - lb_cum[0:1]
    split_at = np.cumsum(SPLIT_SIZES)[:-1]
    h = x
    for l in range(DEPTH):
        xn = rms_norm(h, mix_norm_g[l])
        proj = jnp.einsum('bsd,de->bse', xn, w_in[l])
        q_a, k_a, v_a, u_s5, q_h, f_h, i_h, g_h = jnp.split(proj, split_at, axis=-1)
        attn_out = diff_attention(q_a, k_a, v_a, rel_bias, diff_lambda[l], attn_subln_g[l], l)
        s5_out = s5_mixer(u_s5, s5_a_re[l], s5_a_im[l], s5_log_dt[l], s5_b_re[l], s5_b_im[l],
                          s5_c_re[l], s5_c_im[l], s5_d[l], s5_w_glu[l])
        hgrn_out = hgrn2_mixer(q_h, f_h, i_h, g_h, lb_all[l], hgrn_norm_g[l])
        mix = jnp.concatenate([attn_out, s5_out, hgrn_out], axis=-1)
        h = h + jnp.einsum('bse,ed->bsd', mix, w_out[l]).astype(h.dtype)
        hn = rms_norm(h, ffn_norm_g[l])
        h = h + hier_moe(hn, moe_w_group[l], moe_w_router[l], moe_w_gate[l], moe_w_up[l], moe_w_down[l]).astype(h.dtype)
    return rms_norm(h, final_norm_g)
```

```python
import functools
import math

import numpy as np
import jax
import jax.numpy as jnp
from jax import lax
from jax.experimental import pallas as pl
from jax.experimental.pallas import tpu as pltpu

F32 = jnp.float32
BF16 = jnp.bfloat16
EPS = 1e-6
LOG2E = 1.4426950408889634

D_MODEL = 2048
ATTN_HEADS = 6
ATTN_QK_DIM = 64
ATTN_V_DIM = 128
ATTN_WIDTH = ATTN_HEADS * ATTN_V_DIM
REL_BUCKETS = 32
REL_MAX_DIST = 128
S5_GROUPS = 40
S5_GROUP_CH = 16
S5_STATE = 64
S5_WIDTH = S5_GROUPS * S5_GROUP_CH
S5_LANES = S5_GROUPS * S5_STATE
HGRN_HEADS = 5
HGRN_DIM = 128
HGRN_WIDTH = HGRN_HEADS * HGRN_DIM
N_GROUPS = 4
EXPERTS_PER_GROUP = 8
N_EXPERTS = N_GROUPS * EXPERTS_PER_GROUP
TOP_K = 2
D_EXPERT = 512

LANES = 128
SUBLANES = 8

IN_COLS = 3 * ATTN_WIDTH + S5_WIDTH + 4 * HGRN_WIDTH
PROJ_TN = 512
IN_COLS_PAD = -(-IN_COLS // PROJ_TN) * PROJ_TN
ATTN_COL_BLOCK = (S5_WIDTH + 4 * HGRN_WIDTH) // LANES

PROJ_TM = 512
ATTN_TQ = 256
ATTN_TK = 256
S5_TB = 256
S5_CHUNK = 512
S5_BLK = 128
HGRN_C = 128
OUT_TM = 256
MOE_TM = 128
COMB_TM = 256
GATHER_CHUNK = 256

VMEM_LIMIT = 56 * 1024 * 1024


def _params(sem, vmem=None):
    return pltpu.CompilerParams(dimension_semantics=sem, vmem_limit_bytes=vmem)


def _in_proj_kernel(x_ref, g_ref, w_ref, o_ref, xn_ref):
    @pl.when(pl.program_id(1) == 0)
    def _():
        x = x_ref[...]
        ms = jnp.mean(x * x, axis=-1, keepdims=True)
        xn_ref[...] = (x * lax.rsqrt(ms + EPS) * g_ref[...]).astype(BF16)

    o_ref[...] = jnp.dot(xn_ref[...], w_ref[...], preferred_element_type=F32)


def _in_proj(h, g, w):
    t, d = h.shape
    n = w.shape[1]
    tm = min(PROJ_TM, t)
    return pl.pallas_call(
        _in_proj_kernel,
        grid=(t // tm, n // PROJ_TN),
        in_specs=[
            pl.BlockSpec((tm, d), lambda i, j: (i, 0)),
            pl.BlockSpec((1, d), lambda i, j: (0, 0)),
            pl.BlockSpec((d, PROJ_TN), lambda i, j: (0, j)),
        ],
        out_specs=pl.BlockSpec((tm, PROJ_TN), lambda i, j: (i, j)),
        out_shape=jax.ShapeDtypeStruct((t, n), F32),
        scratch_shapes=[pltpu.VMEM((tm, d), BF16)],
        compiler_params=_params(("parallel", "arbitrary"), VMEM_LIMIT),
        name="in_proj",
    )(h, g.reshape(1, d), w)


def _attn_kernel(lam_ref, q_ref, k_ref, v_ref, bd_ref, bo_ref, g_ref, o_ref,
                 kb_ref, vb_ref, qz_ref, m_ref, l_ref, acc_ref, *, tq, tk, out_scale):
    qi = pl.program_id(2)

    @pl.when(qi == 0)
    def _():
        kb_ref[...] = k_ref[...].astype(BF16)
        vb_ref[...] = v_ref[...].astype(BF16)

    q = q_ref[...] * (ATTN_QK_DIM ** -0.5 * LOG2E)
    lane = lax.broadcasted_iota(jnp.int32, q.shape, 1)
    qz_ref[0:tq, :] = jnp.where(lane < ATTN_QK_DIM, q, 0.0).astype(BF16)
    qz_ref[tq:2 * tq, :] = jnp.where(lane >= ATTN_QK_DIM, q, 0.0).astype(BF16)
    m_ref[...] = jnp.full(m_ref.shape, -jnp.inf, F32)
    l_ref[...] = jnp.zeros(l_ref.shape, F32)
    acc_ref[...] = jnp.zeros(acc_ref.shape, F32)

    def update(start, bias):
        k = kb_ref[pl.ds(start, tk), :]
        v = vb_ref[pl.ds(start, tk), :]
        s = lax.dot_general(qz_ref[...], k, (((1,), (1,)), ((), ())), preferred_element_type=F32)
        if bias is not None:
            s = (s.reshape(2, tq, tk) + bias[None]).reshape(2 * tq, tk)
        m_old = m_ref[...]
        m_new = jnp.maximum(m_old, jnp.max(s, axis=-1, keepdims=True))
        alpha = jnp.exp2(m_old - m_new)
        p = jnp.exp2(s - m_new)
        l_ref[...] = alpha * l_ref[...] + jnp.sum(p, axis=-1, keepdims=True)
        acc_ref[...] = alpha * acc_ref[...] + jnp.dot(p.astype(BF16), v, preferred_element_type=F32)
        m_ref[...] = m_new

    def far_body(j, carry):
        update(pl.multiple_of(j * tk, tk), None)
        return carry

    lax.fori_loop(0, jnp.maximum(qi - 1, 0), far_body, 0)

    @pl.when(qi > 0)
    def _():
        update(pl.multiple_of((qi - 1) * tk, tk), bo_ref[0])

    update(pl.multiple_of(qi * tk, tk), bd_ref[0])

    inv_l = 1.0 / l_ref[...]
    o1 = acc_ref[0:tq, :] * inv_l[0:tq]
    o2 = acc_ref[tq:2 * tq, :] * inv_l[tq:2 * tq]
    o = o1 - lam_ref[0] * o2
    ms = jnp.mean(o * o, axis=-1, keepdims=True)
    o_ref[...] = (o * lax.rsqrt(ms + EPS) * g_ref[...] * out_scale).astype(o_ref.dtype)


def _t5_bucket(dist):
    n = jnp.maximum(dist, 0)
    max_exact = REL_BUCKETS // 2
    large = max_exact + (jnp.log(jnp.maximum(n, 1).astype(F32) / max_exact)
                         / math.log(REL_MAX_DIST / max_exact) * (REL_BUCKETS - max_exact)).astype(jnp.int32)
    large = jnp.minimum(large, REL_BUCKETS - 1)
    return jnp.where(n < max_exact, n, large)


def _attn_bias_tiles(rel_bias, t):
    table = rel_bias.astype(F32)
    far = table[REL_BUCKETS - 1]
    r = jnp.arange(t)[:, None]
    c = jnp.arange(t)[None, :]
    d_diag = r - c
    b_diag = (jnp.transpose(table[_t5_bucket(d_diag)], (2, 0, 1)) - far[:, None, None]) * LOG2E
    b_diag = jnp.where((d_diag >= 0)[None], b_diag, -1e30)
    b_off = (jnp.transpose(table[_t5_bucket(d_diag + t)], (2, 0, 1)) - far[:, None, None]) * LOG2E
    return b_diag, b_off


def _diff_attention(proj, b_diag, b_off, lam, subln_g, lam_init, batch, seq):
    t = proj.shape[0]
    tq = tk = min(ATTN_TQ, seq)
    assert tq >= REL_MAX_DIST, "far-field bias folding needs tiles of at least 128 positions"
    nq = seq // tq
    kernel = functools.partial(_attn_kernel, tq=tq, tk=tk, out_scale=1.0 - lam_init)
    qb, kb, vb = ATTN_COL_BLOCK, ATTN_COL_BLOCK + ATTN_HEADS, ATTN_COL_BLOCK + 2 * ATTN_HEADS
    return pl.pallas_call(
        kernel,
        grid=(batch, ATTN_HEADS, nq),
        in_specs=[
            pl.BlockSpec(memory_space=pltpu.SMEM),
            pl.BlockSpec((tq, LANES), lambda b, h, i: (b * nq + i, qb + h)),
            pl.BlockSpec((seq, LANES), lambda b, h, i: (b, kb + h)),
            pl.BlockSpec((seq, LANES), lambda b, h, i: (b, vb + h)),
            pl.BlockSpec((1, tq, tk), lambda b, h, i: (h, 0, 0)),
            pl.BlockSpec((1, tq, tk), lambda b, h, i: (h, 0, 0)),
            pl.BlockSpec((1, LANES), lambda b, h, i: (0, 0)),
        ],
        out_specs=pl.BlockSpec((tq, LANES), lambda b, h, i: (b * nq + i, h)),
        out_shape=jax.ShapeDtypeStruct((t, ATTN_WIDTH), BF16),
        scratch_shapes=[
            pltpu.VMEM((seq, LANES), BF16),
            pltpu.VMEM((seq, LANES), BF16),
            pltpu.VMEM((2 * tq, LANES), BF16),
            pltpu.VMEM((2 * tq, 1), F32),
            pltpu.VMEM((2 * tq, 1), F32),
            pltpu.VMEM((2 * tq, LANES), F32),
        ],
        compiler_params=_params(("parallel", "parallel", "arbitrary"), VMEM_LIMIT),
        name="diff_attn",
    )(lam.reshape(1), proj, proj, proj, b_diag, b_off, subln_g.reshape(1, LANES))


def _s5_kernel(u_ref, wb_ref, wc_ref, tab_ref, d_ref, wglu_ref, o_ref,
               xr_ref, xi_ref, cr_ref, ci_ref, *, tb):
    @pl.when(pl.program_id(1) == 0)
    def _():
        cr_ref[...] = jnp.zeros(cr_ref.shape, F32)
        ci_ref[...] = jnp.zeros(ci_ref.shape, F32)

    u = u_ref[...]
    ub = u.astype(BF16)
    nblk = S5_WIDTH // S5_BLK
    sw = S5_LANES // nblk
    for i in range(nblk):
        bu = jnp.dot(ub[:, i * S5_BLK:(i + 1) * S5_BLK], wb_ref[i], preferred_element_type=F32)
        xr_ref[:, i * sw:(i + 1) * sw] = bu[:, :sw]
        xi_ref[:, i * sw:(i + 1) * sw] = bu[:, sw:]

    for c in range(S5_LANES // S5_CHUNK):
        ls = slice(c * S5_CHUNK, (c + 1) * S5_CHUNK)
        a1r, a1i = tab_ref[0, :, ls], tab_ref[1, :, ls]
        a2r, a2i = tab_ref[2, :, ls], tab_ref[3, :, ls]
        a4r, a4i = tab_ref[4, :, ls], tab_ref[5, :, ls]
        pwr, pwi = tab_ref[6, :, ls], tab_ref[7, :, ls]

        def body(g, carry, ls=ls, a1r=a1r, a1i=a1i, a2r=a2r, a2i=a2i, a4r=a4r, a4i=a4i, pwr=pwr, pwi=pwi):
            cr, ci = carry
            rows = pl.ds(pl.multiple_of(g * SUBLANES, SUBLANES), SUBLANES)
            xr = xr_ref[rows, ls]
            xi = xi_ref[rows, ls]
            for sh, ar, ai in ((1, a1r, a1i), (2, a2r, a2i), (4, a4r, a4i)):
                sr = pltpu.roll(xr, sh, 0)
                si = pltpu.roll(xi, sh, 0)
                xr, xi = xr + ar * sr - ai * si, xi + ar * si + ai * sr
            xr, xi = xr + pwr * cr - pwi * ci, xi + pwr * ci + pwi * cr
            xr_ref[rows, ls] = xr
            xi_ref[rows, ls] = xi
            return xr[SUBLANES - 1:SUBLANES, :], xi[SUBLANES - 1:SUBLANES, :]

        cr, ci = lax.fori_loop(0, tb // SUBLANES, body, (cr_ref[:, ls], ci_ref[:, ls]))
        cr_ref[:, ls] = cr
        ci_ref[:, ls] = ci

    ys = []
    for i in range(nblk):
        xc = jnp.concatenate([xr_ref[:, i * sw:(i + 1) * sw], xi_ref[:, i * sw:(i + 1) * sw]], axis=1)
        ys.append(jnp.dot(xc.astype(BF16), wc_ref[i], preferred_element_type=F32))
    y = jnp.concatenate(ys, axis=1) + d_ref[...] * u
    z = 0.5 * y * (1.0 + jnp.tanh(math.sqrt(2.0 / math.pi) * (y + 0.044715 * (y * y * y))))
    gate = jax.nn.sigmoid(jnp.dot(z.astype(BF16), wglu_ref[...], preferred_element_type=F32))
    o_ref[...] = (z * gate).astype(o_ref.dtype)


def _s5_prep(a_re, a_im, log_dt, b_re, b_im, c_re, c_im):
    lr = jnp.minimum(a_re.astype(F32), -1e-4)
    li = a_im.astype(F32)
    dt = jnp.exp(log_dt.astype(F32))[:, None]
    mag = jnp.exp(lr * dt)
    ab_re = mag * jnp.cos(li * dt)
    ab_im = mag * jnp.sin(li * dt)
    den = lr * lr + li * li
    nr = ab_re - 1.0
    ni = ab_im
    f_re = (nr * lr + ni * li) / den
    f_im = (ni * lr - nr * li) / den
    br = b_re.astype(F32)
    bi = b_im.astype(F32)
    bb_re = f_re[..., None] * br - f_im[..., None] * bi
    bb_im = f_re[..., None] * bi + f_im[..., None] * br

    nblk = S5_WIDTH // S5_BLK
    gpb = S5_GROUPS // nblk
    eye = jnp.eye(gpb, dtype=F32)

    def bmat(bb):
        bb = bb.reshape(nblk, gpb, S5_STATE, S5_GROUP_CH)
        m = jnp.einsum('ignc,gh->igchn', bb, eye)
        return m.reshape(nblk, gpb * S5_GROUP_CH, gpb * S5_STATE)

    def cmat(cc):
        cc = cc.astype(F32).reshape(nblk, gpb, S5_GROUP_CH, S5_STATE)
        m = jnp.einsum('igcn,gh->ignhc', cc, eye)
        return m.reshape(nblk, gpb * S5_STATE, gpb * S5_GROUP_CH)

    w_b = jnp.concatenate([bmat(bb_re), bmat(bb_im)], axis=2).astype(BF16)
    w_c = jnp.concatenate([cmat(c_re), -cmat(c_im)], axis=1).astype(BF16)

    def power(j):
        m = jnp.exp(lr * dt * j)
        return (m * jnp.cos(li * dt * j)).reshape(-1), (m * jnp.sin(li * dt * j)).reshape(-1)

    row = jnp.arange(SUBLANES)[:, None]
    tabs = []
    for sh in (1, 2, 4):
        pr, pi = power(float(sh))
        keep = (row >= sh).astype(F32)
        tabs += [keep * pr[None, :], keep * pi[None, :]]
    pows = [power(float(j + 1)) for j in range(SUBLANES)]
    tabs += [jnp.stack([p[0] for p in pows]), jnp.stack([p[1] for p in pows])]
    return w_b, w_c, jnp.stack(tabs)


def _s5_mixer(proj, w_b, w_c, tabs, d_skip, w_glu, batch, seq):
    t = proj.shape[0]
    tb = min(S5_TB, seq)
    nt = seq // tb
    nblk = S5_WIDTH // S5_BLK
    return pl.pallas_call(
        functools.partial(_s5_kernel, tb=tb),
        grid=(batch, nt),
        in_specs=[
            pl.BlockSpec((tb, S5_WIDTH), lambda b, i: (b * nt + i, 0)),
            pl.BlockSpec(w_b.shape, lambda b, i: (0, 0, 0)),
            pl.BlockSpec(w_c.shape, lambda b, i: (0, 0, 0)),
            pl.BlockSpec(tabs.shape, lambda b, i: (0, 0, 0)),
            pl.BlockSpec((1, S5_WIDTH), lambda b, i: (0, 0)),
            pl.BlockSpec((S5_WIDTH, S5_WIDTH), lambda b, i: (0, 0)),
        ],
        out_specs=pl.BlockSpec((tb, S5_WIDTH), lambda b, i: (b * nt + i, 0)),
        out_shape=jax.ShapeDtypeStruct((t, S5_WIDTH), BF16),
        scratch_shapes=[
            pltpu.VMEM((tb, S5_LANES), F32),
            pltpu.VMEM((tb, S5_LANES), F32),
            pltpu.VMEM((1, S5_LANES), F32),
            pltpu.VMEM((1, S5_LANES), F32),
        ],
        compiler_params=_params(("parallel", "arbitrary"), VMEM_LIMIT),
        name="s5_mixer",
    )(proj, w_b, w_c, tabs, d_skip.reshape(1, S5_WIDTH), w_glu)


def _hgrn_consts(c):
    t = np.arange(c)[:, None]
    u = np.arange(c)[None, :]
    mats = [u <= t, u > t]
    masks = [np.eye(c, dtype=bool)]
    h = 1
    while h < c:
        mid = (t // (2 * h)) * 2 * h + h
        mats.append(np.where(t >= mid, (u >= mid) & (u <= t), (u > t) & (u < mid)))
        masks.append(((u // (2 * h)) == (t // (2 * h))) & (t >= mid) & (u < mid))
        h *= 2
    return (np.concatenate(mats, axis=0).astype(np.float32), np.stack(masks).astype(np.float32))


def _hgrn_kernel(q_ref, f_ref, i_ref, g_ref, lb_ref, ng_ref, mall_ref, mask_ref, o_ref,
                 e_ref, st_ref, *, c):
    @pl.when(pl.program_id(1) == 0)
    def _():
        st_ref[...] = jnp.zeros(st_ref.shape, F32)

    lb = lb_ref[...]
    fl = f_ref[...]
    la = jnp.log(lb)
    lbb = jnp.log1p(-lb) + (jnp.minimum(fl, 0.0) - jnp.log1p(jnp.exp(-jnp.abs(fl))))
    logf = jnp.maximum(la, lbb) + jnp.log1p(jnp.exp(-jnp.abs(la - lbb)))
    hi = logf.astype(BF16)
    lo = (logf - hi.astype(F32)).astype(BF16)
    e2 = jnp.dot(mall_ref[...], jnp.concatenate([hi, lo], axis=1), preferred_element_type=F32)
    e_ref[...] = e2[:, :HGRN_WIDTH] + e2[:, HGRN_WIDTH:]

    n_levels = mask_ref.shape[0] - 1
    nt = (((1,), (1,)), ((), ()))
    for hd in range(HGRN_HEADS):
        ls = slice(hd * HGRN_DIM, (hd + 1) * HGRN_DIM)
        qr = q_ref[:, ls]
        q = qr * jax.nn.sigmoid(qr) * (HGRN_DIM ** -0.5)
        k = (1.0 - lb[:, ls]) * jax.nn.sigmoid(-fl[:, ls])
        vb = i_ref[:, ls].astype(BF16)
        p = lax.dot_general(q.astype(BF16), k.astype(BF16), nt, preferred_element_type=F32) * mask_ref[0]
        for lv in range(n_levels):
            fac = jnp.exp(e_ref[(2 + lv) * c:(3 + lv) * c, ls])
            s = lax.dot_general((q * fac).astype(BF16), (k * fac).astype(BF16), nt,
                                preferred_element_type=F32)
            p = p + s * mask_ref[lv + 1]
        o = jnp.dot(p.astype(BF16), vb, preferred_element_type=F32)
        gcum = e_ref[0:c, ls]
        st = st_ref[hd]
        o = o + lax.dot_general((q * jnp.exp(gcum)).astype(BF16), st.astype(BF16), nt,
                                preferred_element_type=F32)
        kd = (k * jnp.exp(e_ref[c:2 * c, ls])).astype(BF16)
        st_ref[hd] = st * jnp.exp(gcum[c - 1:c, :]) + lax.dot_general(
            vb, kd, (((0,), (0,)), ((), ())), preferred_element_type=F32)
        ms = jnp.mean(o * o, axis=-1, keepdims=True)
        o = o * lax.rsqrt(ms + EPS) * ng_ref[...] * jax.nn.sigmoid(g_ref[:, ls])
        o_ref[:, ls] = o.astype(o_ref.dtype)


def _hgrn_mixer(proj, lb, norm_g, batch, seq):
    t = proj.shape[0]
    c = min(HGRN_C, seq)
    nc = seq // c
    m_all, masks = _hgrn_consts(c)
    m_all = jnp.asarray(m_all, BF16)
    masks = jnp.asarray(masks, F32)

    def col(j):
        return pl.BlockSpec((c, HGRN_WIDTH), lambda b, i: (b * nc + i, j))

    return pl.pallas_call(
        functools.partial(_hgrn_kernel, c=c),
        grid=(batch, nc),
        in_specs=[
            col(1), col(2), col(3), col(4),
            pl.BlockSpec((1, HGRN_WIDTH), lambda b, i: (0, 0)),
            pl.BlockSpec((1, HGRN_DIM), lambda b, i: (0, 0)),
            pl.BlockSpec(m_all.shape, lambda b, i: (0, 0)),
            pl.BlockSpec(masks.shape, lambda b, i: (0, 0, 0)),
        ],
        out_specs=pl.BlockSpec((c, HGRN_WIDTH), lambda b, i: (b * nc + i, 0)),
        out_shape=jax.ShapeDtypeStruct((t, HGRN_WIDTH), BF16),
        scratch_shapes=[
            pltpu.VMEM((m_all.shape[0], HGRN_WIDTH), F32),
            pltpu.VMEM((HGRN_HEADS, HGRN_DIM, HGRN_DIM), F32),
        ],
        compiler_params=_params(("parallel", "arbitrary"), VMEM_LIMIT),
        name="hgrn2_mixer",
    )(proj, proj, proj, proj, lb.reshape(1, HGRN_WIDTH), norm_g.reshape(1, HGRN_DIM), m_all, masks)


def _out_proj_kernel(h_ref, a_ref, s_ref, r_ref, wa_ref, ws_ref, wr_ref, gn_ref, rh_ref, rl_ref,
                     ho_ref, hn_ref, eid_ref, gate_ref):
    acc = jnp.dot(a_ref[...], wa_ref[...], preferred_element_type=F32)
    acc = acc + jnp.dot(s_ref[...], ws_ref[...], preferred_element_type=F32)
    acc = acc + jnp.dot(r_ref[...], wr_ref[...], preferred_element_type=F32)
    h = h_ref[...] + acc
    ho_ref[...] = h
    ms = jnp.mean(h * h, axis=-1, keepdims=True)
    hn = h * lax.rsqrt(ms + EPS) * gn_ref[...]
    hn_ref[...] = hn

    hh = hn.astype(BF16)
    hl = (hn - hh.astype(F32)).astype(BF16)
    logits = (jnp.dot(hh, rh_ref[...], preferred_element_type=F32)
              + jnp.dot(hh, rl_ref[...], preferred_element_type=F32)
              + jnp.dot(hl, rh_ref[...], preferred_element_type=F32))
    lane = lax.broadcasted_iota(jnp.int32, logits.shape, 1).astype(F32)
    neg = -jnp.inf
    big = 1e9
    is_group = jnp.where(lane >= N_EXPERTS, jnp.where(lane < N_EXPERTS + N_GROUPS, 1.0, 0.0), 0.0)
    gl = jnp.where(is_group > 0, logits, neg)
    gmax = jnp.max(gl, axis=-1, keepdims=True)
    g_lane = jnp.min(jnp.where(gl == gmax, lane, big), axis=-1, keepdims=True)
    p_g = 1.0 / jnp.sum(jnp.exp(gl - gmax), axis=-1, keepdims=True)
    lo_lane = (g_lane - N_EXPERTS) * EXPERTS_PER_GROUP
    in_group = jnp.where(lane >= lo_lane, jnp.where(lane < lo_lane + EXPERTS_PER_GROUP, 1.0, 0.0), 0.0)
    el = jnp.where(in_group > 0, logits, neg)
    t1 = jnp.max(el, axis=-1, keepdims=True)
    i1 = jnp.min(jnp.where(el == t1, lane, big), axis=-1, keepdims=True)
    el2 = jnp.where(lane == i1, neg, el)
    t2 = jnp.max(el2, axis=-1, keepdims=True)
    i2 = jnp.min(jnp.where(el2 == t2, lane, big), axis=-1, keepdims=True)
    e21 = jnp.exp(t2 - t1)
    g1 = p_g / (1.0 + e21)
    g2 = p_g * e21 / (1.0 + e21)
    eid_ref[...] = jnp.where(lane == 0, i1, jnp.where(lane == 1, i2, 0.0)).astype(jnp.int32)
    gate_ref[...] = jnp.where(lane == 0, g1, jnp.where(lane == 1, g2, 0.0))


def _out_proj(h, attn, s5, hg, w_a, w_s, w_r, gn, r_hi, r_lo):
    t, d = h.shape
    tm = min(OUT_TM, t)

    def rows(w):
        return pl.BlockSpec((tm, w), lambda i: (i, 0))

    def full(a):
        return pl.BlockSpec(a.shape, lambda i: (0,) * a.ndim)

    return pl.pallas_call(
        _out_proj_kernel,
        grid=(t // tm,),
        in_specs=[rows(d), rows(ATTN_WIDTH), rows(S5_WIDTH), rows(HGRN_WIDTH),
                  full(w_a), full(w_s), full(w_r), pl.BlockSpec((1, d), lambda i: (0, 0)),
                  full(r_hi), full(r_lo)],
        out_specs=[rows(d), rows(d), rows(LANES), rows(LANES)],
        out_shape=[jax.ShapeDtypeStruct((t, d), F32), jax.ShapeDtypeStruct((t, d), F32),
                   jax.ShapeDtypeStruct((t, LANES), jnp.int32), jax.ShapeDtypeStruct((t, LANES), F32)],
        compiler_params=_params(("parallel",), VMEM_LIMIT),
        name="out_proj_router",
    )(h, attn, s5, hg, w_a, w_s, w_r, gn.reshape(1, d), r_hi, r_lo)


def _moe_plan(eid, tm):
    flat = eid.reshape(-1)
    n_slots = flat.shape[0]
    onehot = (flat[:, None] == jnp.arange(N_EXPERTS, dtype=jnp.int32)[None, :]).astype(jnp.int32)
    csum = jnp.cumsum(onehot, axis=0)
    rank = jnp.sum(onehot * csum, axis=1) - 1
    counts = csum[-1]
    tiles = (counts + tm - 1) // tm
    tile_end = jnp.cumsum(tiles)
    tile_start = tile_end - tiles
    slot_start = jnp.cumsum(counts) - counts
    pos = jnp.sum(onehot * tile_start[None, :], axis=1) * tm + rank
    tok_sorted = jnp.argsort(flat, stable=True).astype(jnp.int32) // TOP_K
    nt_max = (n_slots + N_EXPERTS * (tm - 1)) // tm
    n_used = tile_end[-1]
    j = jnp.minimum(jnp.arange(nt_max, dtype=jnp.int32), n_used - 1)
    tile_expert = jnp.sum((j[:, None] >= tile_end[None, :]).astype(jnp.int32), axis=1)
    tile_slot0 = slot_start[tile_expert] + (j - tile_start[tile_expert]) * tm
    return (pos.astype(jnp.int32), tok_sorted, tile_expert.astype(jnp.int32),
            tile_slot0.astype(jnp.int32), n_used.reshape(1).astype(jnp.int32), nt_max)


def _expert_kernel(tok_ref, te_ref, s0_ref, nu_ref, hn_ref, wg_ref, wu_ref, wd_ref, y_ref,
                   xbuf_ref, wgb_ref, wub_ref, wdb_ref, sem, *, tm, n_slots):
    j = pl.program_id(0)
    n_used = nu_ref[0]

    def row_copy(buf, r, tok):
        return pltpu.make_async_copy(hn_ref.at[pl.ds(tok, 1)], xbuf_ref.at[buf, pl.ds(r, 1)], sem.at[buf])

    def start_gather(tile):
        buf = lax.rem(tile, 2)
        base = s0_ref[tile]

        def issue(r, carry):
            row_copy(buf, r, tok_ref[jnp.minimum(base + r, n_slots - 1)]).start()
            return carry

        lax.fori_loop(0, tm, issue, 0)

    @pl.when(j == 0)
    def _():
        start_gather(0)

    @pl.when(j + 1 < n_used)
    def _():
        start_gather(j + 1)

    @pl.when(j < n_used)
    def _():
        buf = lax.rem(j, 2)

        def wait(r, carry):
            row_copy(buf, r, 0).wait()
            return carry

        lax.fori_loop(0, tm, wait, 0)

        prev = te_ref[jnp.maximum(j - 1, 0)]

        @pl.when(jnp.logical_or(j == 0, te_ref[j] != prev))
        def _():
            wgb_ref[...] = wg_ref[0].astype(BF16)
            wub_ref[...] = wu_ref[0].astype(BF16)
            wdb_ref[...] = wd_ref[0].astype(BF16)

        x = xbuf_ref[buf].astype(BF16)
        g = jnp.dot(x, wgb_ref[...], preferred_element_type=F32)
        u = jnp.dot(x, wub_ref[...], preferred_element_type=F32)
        hmid = (g * jax.nn.sigmoid(g) * u).astype(BF16)
        y_ref[...] = jnp.dot(hmid, wdb_ref[...], preferred_element_type=F32)

    @pl.when(j >= n_used)
    def _():
        y_ref[...] = jnp.zeros(y_ref.shape, F32)


def _moe_experts(hn, tok_sorted, tile_expert, tile_slot0, n_used, w_gate, w_up, w_down, tm, nt_max):
    t, d = hn.shape
    de = w_gate.shape[2]
    grid_spec = pltpu.PrefetchScalarGridSpec(
        num_scalar_prefetch=4,
        grid=(nt_max,),
        in_specs=[
            pl.BlockSpec(memory_space=pl.ANY),
            pl.BlockSpec((1, d, de), lambda j, tok, te, s0, nu: (te[j], 0, 0)),
            pl.BlockSpec((1, d, de), lambda j, tok, te, s0, nu: (te[j], 0, 0)),
            pl.BlockSpec((1, de, d), lambda j, tok, te, s0, nu: (te[j], 0, 0)),
        ],
        out_specs=pl.BlockSpec((tm, d), lambda j, tok, te, s0, nu: (j, 0)),
        scratch_shapes=[pltpu.VMEM((2, tm, d), F32),
                        pltpu.VMEM((d, de), BF16), pltpu.VMEM((d, de), BF16), pltpu.VMEM((de, d), BF16),
                        pltpu.SemaphoreType.DMA((2,))],
    )
    return pl.pallas_call(
        functools.partial(_expert_kernel, tm=tm, n_slots=t * TOP_K),
        grid_spec=grid_spec,
        out_shape=jax.ShapeDtypeStruct((nt_max * tm, d), F32),
        compiler_params=_params(("arbitrary",), VMEM_LIMIT),
        name="moe_experts",
    )(tok_sorted, tile_expert, tile_slot0, n_used, hn, w_gate, w_up, w_down)


def _combine_kernel(pos_ref, h_ref, gate_ref, gn_ref, y_ref, o_ref, buf_ref, sem, *, tm, final_norm):
    i = pl.program_id(0)

    def row_copy(k, r, src_row):
        return pltpu.make_async_copy(y_ref.at[pl.ds(src_row, 1)], buf_ref.at[k, pl.ds(r, 1)], sem)

    def issue(r, carry):
        slot = (i * tm + r) * TOP_K
        for k in range(TOP_K):
            row_copy(k, r, pos_ref[slot + k]).start()
        return carry

    lax.fori_loop(0, tm, issue, 0)

    def wait(r, carry):
        for k in range(TOP_K):
            row_copy(k, r, 0).wait()
        return carry

    lax.fori_loop(0, tm, wait, 0)

    gates = gate_ref[...]
    h = h_ref[...] + gates[:, 0:1] * buf_ref[0] + gates[:, 1:2] * buf_ref[1]
    if final_norm:
        ms = jnp.mean(h * h, axis=-1, keepdims=True)
        h = h * lax.rsqrt(ms + EPS) * gn_ref[...]
    o_ref[...] = h


def _moe_combine(h, gates, ys, pos, final_g, final_norm):
    t, d = h.shape
    tm = min(COMB_TM, t)
    grid_spec = pltpu.PrefetchScalarGridSpec(
        num_scalar_prefetch=1,
        grid=(t // tm,),
        in_specs=[
            pl.BlockSpec((tm, d), lambda i, p: (i, 0)),
            pl.BlockSpec((tm, LANES), lambda i, p: (i, 0)),
            pl.BlockSpec((1, d), lambda i, p: (0, 0)),
            pl.BlockSpec(memory_space=pl.ANY),
        ],
        out_specs=pl.BlockSpec((tm, d), lambda i, p: (i, 0)),
        scratch_shapes=[pltpu.VMEM((TOP_K, tm, d), F32), pltpu.SemaphoreType.DMA(())],
    )
    return pl.pallas_call(
        functools.partial(_combine_kernel, tm=tm, final_norm=final_norm),
        grid_spec=grid_spec,
        out_shape=jax.ShapeDtypeStruct((t, d), F32),
        compiler_params=_params(("arbitrary",), VMEM_LIMIT),
        name="moe_combine",
    )(pos, h, gates, final_g.reshape(1, d), ys)


def _permute_w_in(w):
    a = 3 * ATTN_WIDTH
    w = jnp.concatenate([w[:, a:], w[:, :a]], axis=1)
    return jnp.pad(w, ((0, 0), (0, IN_COLS_PAD - IN_COLS))).astype(BF16)


def kernel(x, w_in, w_out, mix_norm_g, ffn_norm_g, rel_bias, diff_lambda, attn_subln_g, s5_a_re, s5_a_im, s5_log_dt, s5_b_re, s5_b_im, s5_c_re, s5_c_im, s5_d, s5_w_glu, hgrn_lb_logits, hgrn_norm_g, moe_w_group, moe_w_router, moe_w_gate, moe_w_up, moe_w_down, final_norm_g):
    batch, seq, d = x.shape
    depth = w_in.shape[0]
    t = batch * seq
    h = x.reshape(t, d)

    lb_cum = jnp.cumsum(jax.nn.softmax(hgrn_lb_logits.astype(F32), axis=0), axis=0)
    lb_all = lb_cum - lb_cum[0:1]
    b_diag, b_off = _attn_bias_tiles(rel_bias, min(ATTN_TQ, seq))

    for l in range(depth):
        proj = _in_proj(h, mix_norm_g[l], _permute_w_in(w_in[l]))

        lam_init = 0.8 - 0.6 * math.exp(-0.3 * l)
        lv = diff_lambda[l].astype(F32)
        lam = jnp.exp(jnp.sum(lv[0] * lv[1])) - jnp.exp(jnp.sum(lv[2] * lv[3])) + lam_init
        attn = _diff_attention(proj, b_diag, b_off, lam, attn_subln_g[l], lam_init, batch, seq)

        w_b, w_c, tabs = _s5_prep(s5_a_re[l], s5_a_im[l], s5_log_dt[l], s5_b_re[l], s5_b_im[l],
                                  s5_c_re[l], s5_c_im[l])
        s5 = _s5_mixer(proj, w_b, w_c, tabs, s5_d[l], s5_w_glu[l].astype(BF16), batch, seq)

        hg = _hgrn_mixer(proj, lb_all[l], hgrn_norm_g[l], batch, seq)

        wo = w_out[l].astype(BF16)
        w_r = jnp.concatenate([moe_w_router[l], moe_w_group[l]], axis=1).astype(F32)
        w_r = jnp.pad(w_r, ((0, 0), (0, LANES - w_r.shape[1])))
        r_hi = w_r.astype(BF16)
        r_lo = (w_r - r_hi.astype(F32)).astype(BF16)
        h, hn, eid, gates = _out_proj(
            h, attn, s5, hg, wo[:ATTN_WIDTH], wo[ATTN_WIDTH:ATTN_WIDTH + S5_WIDTH],
            wo[ATTN_WIDTH + S5_WIDTH:], ffn_norm_g[l], r_hi, r_lo)

        pos, tok_sorted, tile_expert, tile_slot0, n_used, nt_max = _moe_plan(eid[:, :TOP_K], MOE_TM)
        ys = _moe_experts(hn, tok_sorted, tile_expert, tile_slot0, n_used,
                          moe_w_gate[l], moe_w_up[l], moe_w_down[l], MOE_TM, nt_max)
        h = _moe_combine(h, gates, ys, pos, final_norm_g, final_norm=(l == depth - 1))

    return h.reshape(batch, seq, d)
```

```python
import functools
import math

import numpy as np
import jax
import jax.numpy as jnp
from jax import lax
from jax.experimental import pallas as pl
from jax.experimental.pallas import tpu as pltpu

F32 = jnp.float32
BF16 = jnp.bfloat16
EPS = 1e-6
LOG2E = 1.4426950408889634

D_MODEL = 2048
ATTN_HEADS = 6
ATTN_QK_DIM = 64
ATTN_V_DIM = 128
ATTN_WIDTH = ATTN_HEADS * ATTN_V_DIM
REL_BUCKETS = 32
REL_MAX_DIST = 128
S5_GROUPS = 40
S5_GROUP_CH = 16
S5_STATE = 64
S5_WIDTH = S5_GROUPS * S5_GROUP_CH
S5_LANES = S5_GROUPS * S5_STATE
HGRN_HEADS = 5
HGRN_DIM = 128
HGRN_WIDTH = HGRN_HEADS * HGRN_DIM
N_GROUPS = 4
EXPERTS_PER_GROUP = 8
N_EXPERTS = N_GROUPS * EXPERTS_PER_GROUP
TOP_K = 2
D_EXPERT = 512

LANES = 128
SUBLANES = 8

IN_COLS = 3 * ATTN_WIDTH + S5_WIDTH + 4 * HGRN_WIDTH
PROJ_TN = 512
IN_COLS_PAD = -(-IN_COLS // PROJ_TN) * PROJ_TN
ATTN_COL_BLOCK = (S5_WIDTH + 4 * HGRN_WIDTH) // LANES

PROJ_TM = 1024
ATTN_TQ = 256
ATTN_TK = 256
ATTN_ROWS = 32
S5_TB = 256
S5_CHUNK = 512
S5_BLK = 128
HGRN_C = 128
OUT_TM = 256
MOE_TM = 256
COMB_TM = 128
DMA_UNROLL = 8

VMEM_LIMIT = 56 * 1024 * 1024


def _params(sem, vmem=None):
    return pltpu.CompilerParams(dimension_semantics=sem, vmem_limit_bytes=vmem)


def _in_proj_kernel(x_ref, g_ref, w_ref, o_ref, xn_ref):
    @pl.when(pl.program_id(1) == 0)
    def _():
        x = x_ref[...]
        ms = jnp.mean(x * x, axis=-1, keepdims=True)
        xn_ref[...] = (x * lax.rsqrt(ms + EPS) * g_ref[...]).astype(BF16)

    o_ref[...] = jnp.dot(xn_ref[...], w_ref[...], preferred_element_type=F32)


def _in_proj(h, g, w):
    t, d = h.shape
    n = w.shape[1]
    tm = min(PROJ_TM, t)
    return pl.pallas_call(
        _in_proj_kernel,
        grid=(t // tm, n // PROJ_TN),
        in_specs=[
            pl.BlockSpec((tm, d), lambda i, j: (i, 0)),
            pl.BlockSpec((1, d), lambda i, j: (0, 0)),
            pl.BlockSpec((d, PROJ_TN), lambda i, j: (0, j)),
        ],
        out_specs=pl.BlockSpec((tm, PROJ_TN), lambda i, j: (i, j)),
        out_shape=jax.ShapeDtypeStruct((t, n), F32),
        scratch_shapes=[pltpu.VMEM((tm, d), BF16)],
        compiler_params=_params(("parallel", "arbitrary"), VMEM_LIMIT),
        name="in_proj",
    )(h, g.reshape(1, d), w)


def _attn_kernel(lam_ref, q_ref, k_ref, v_ref, bd_ref, bo_ref, g_ref, o_ref,
                 kb_ref, vb_ref, qz_ref, m_ref, a_ref, acc_ref, s0_ref, s1_ref, p_ref, *, tq, tk, out_scale):
    qi = pl.program_id(2)

    @pl.when(qi == 0)
    def _():
        kb_ref[...] = k_ref[...].astype(BF16)
        vb_ref[:, 0:LANES] = v_ref[...].astype(BF16)
        vb_ref[:, LANES:2 * LANES] = jnp.ones((vb_ref.shape[0], LANES), BF16)

    q = q_ref[...] * (ATTN_QK_DIM ** -0.5 * LOG2E)
    lane = lax.broadcasted_iota(jnp.int32, q.shape, 1)
    qz_ref[0:tq, :] = jnp.where(lane < ATTN_QK_DIM, q, 0.0).astype(BF16)
    qz_ref[tq:2 * tq, :] = jnp.where(lane >= ATTN_QK_DIM, q, 0.0).astype(BF16)
    m_ref[...] = jnp.full(m_ref.shape, -jnp.inf, F32)
    acc_ref[...] = jnp.zeros(acc_ref.shape, F32)

    n_col = tk // LANES
    groups = [slice(g * ATTN_ROWS, (g + 1) * ATTN_ROWS) for g in range(2 * tq // ATTN_ROWS)]

    def scores(j, s_ref):
        k = kb_ref[pl.ds(pl.multiple_of(j * tk, tk), tk), :]
        s_ref[...] = lax.dot_general(qz_ref[...], k, (((1,), (1,)), ((), ())), preferred_element_type=F32)

    def softmax_pv(j, s_ref, bias_ref):
        def load(rows, i):
            x = s_ref[rows, i * LANES:(i + 1) * LANES]
            if bias_ref is not None:
                b0 = rows.start % tq
                x = x + bias_ref[0, b0:b0 + ATTN_ROWS, i * LANES:(i + 1) * LANES]
            return x

        for rows in groups:
            mx = functools.reduce(jnp.maximum, [load(rows, i) for i in range(n_col)])
            m_old = m_ref[rows, :]
            m_new = jnp.maximum(m_old, jnp.max(mx, axis=-1, keepdims=True))
            m_ref[rows, :] = m_new
            a_ref[rows, :] = jnp.exp2(m_old - m_new)
        for rows in groups:
            m_new = m_ref[rows, :]
            for i in range(n_col):
                p_ref[rows, i * LANES:(i + 1) * LANES] = jnp.exp2(load(rows, i) - m_new).astype(BF16)
        v = vb_ref[pl.ds(pl.multiple_of(j * tk, tk), tk), :]
        alpha = a_ref[...]
        acc_ref[...] = (jnp.concatenate([alpha, alpha], axis=1) * acc_ref[...]
                        + jnp.dot(p_ref[...], v, preferred_element_type=F32))

    n_far = jnp.maximum(qi - 1, 0)
    n_pair = n_far // 2
    scores(0, s0_ref)

    def pair_body(jj, carry):
        j = 2 * jj
        scores(j + 1, s1_ref)
        softmax_pv(j, s0_ref, None)
        scores(j + 2, s0_ref)
        softmax_pv(j + 1, s1_ref, None)
        return carry

    lax.fori_loop(0, n_pair, pair_body, 0)
    odd = n_far - 2 * n_pair

    @pl.when(qi == 0)
    def _():
        softmax_pv(0, s0_ref, bd_ref)

    @pl.when(jnp.logical_and(qi > 0, odd == 0))
    def _():
        scores(qi, s1_ref)
        softmax_pv(qi - 1, s0_ref, bo_ref)
        softmax_pv(qi, s1_ref, bd_ref)

    @pl.when(odd == 1)
    def _():
        scores(qi - 1, s1_ref)
        softmax_pv(qi - 2, s0_ref, None)
        scores(qi, s0_ref)
        softmax_pv(qi - 1, s1_ref, bo_ref)
        softmax_pv(qi, s0_ref, bd_ref)

    inv_l = 1.0 / acc_ref[:, LANES:2 * LANES]
    o1 = acc_ref[0:tq, 0:LANES] * inv_l[0:tq]
    o2 = acc_ref[tq:2 * tq, 0:LANES] * inv_l[tq:2 * tq]
    o = o1 - lam_ref[0] * o2
    ms = jnp.mean(o * o, axis=-1, keepdims=True)
    o_ref[...] = (o * lax.rsqrt(ms + EPS) * g_ref[...] * out_scale).astype(o_ref.dtype)


def _t5_bucket(dist):
    n = jnp.maximum(dist, 0)
    max_exact = REL_BUCKETS // 2
    large = max_exact + (jnp.log(jnp.maximum(n, 1).astype(F32) / max_exact)
                         / math.log(REL_MAX_DIST / max_exact) * (REL_BUCKETS - max_exact)).astype(jnp.int32)
    large = jnp.minimum(large, REL_BUCKETS - 1)
    return jnp.where(n < max_exact, n, large)


def _attn_bias_tiles(rel_bias, t):
    table = rel_bias.astype(F32)
    far = table[REL_BUCKETS - 1]
    r = jnp.arange(t)[:, None]
    c = jnp.arange(t)[None, :]
    d_diag = r - c

    def lookup(dist):
        bucket = _t5_bucket(dist)
        out = jnp.zeros((ATTN_HEADS,) + dist.shape, F32)
        for b in range(REL_BUCKETS):
            out = jnp.where((bucket == b)[None], table[b][:, None, None], out)
        return out

    b_diag = (lookup(d_diag) - far[:, None, None]) * LOG2E
    b_diag = jnp.where((d_diag >= 0)[None], b_diag, -1e30)
    b_off = (lookup(d_diag + t) - far[:, None, None]) * LOG2E
    return b_diag, b_off


def _diff_attention(proj, b_diag, b_off, lam, subln_g, lam_init, batch, seq):
    t = proj.shape[0]
    tq = tk = min(ATTN_TQ, seq)
    assert tq >= REL_MAX_DIST, "far-field bias folding needs tiles of at least 128 positions"
    nq = seq // tq
    kernel = functools.partial(_attn_kernel, tq=tq, tk=tk, out_scale=1.0 - lam_init)
    qb, kb, vb = ATTN_COL_BLOCK, ATTN_COL_BLOCK + ATTN_HEADS, ATTN_COL_BLOCK + 2 * ATTN_HEADS
    return pl.pallas_call(
        kernel,
        grid=(batch, ATTN_HEADS, nq),
        in_specs=[
            pl.BlockSpec(memory_space=pltpu.SMEM),
            pl.BlockSpec((tq, LANES), lambda b, h, i: (b * nq + i, qb + h)),
            pl.BlockSpec((seq, LANES), lambda b, h, i: (b, kb + h)),
            pl.BlockSpec((seq, LANES), lambda b, h, i: (b, vb + h)),
            pl.BlockSpec((1, tq, tk), lambda b, h, i: (h, 0, 0)),
            pl.BlockSpec((1, tq, tk), lambda b, h, i: (h, 0, 0)),
            pl.BlockSpec((1, LANES), lambda b, h, i: (0, 0)),
        ],
        out_specs=pl.BlockSpec((tq, LANES), lambda b, h, i: (b * nq + i, h)),
        out_shape=jax.ShapeDtypeStruct((t, ATTN_WIDTH), BF16),
        scratch_shapes=[
            pltpu.VMEM((seq, LANES), BF16),
            pltpu.VMEM((seq, 2 * LANES), BF16),
            pltpu.VMEM((2 * tq, LANES), BF16),
            pltpu.VMEM((2 * tq, LANES), F32),
            pltpu.VMEM((2 * tq, LANES), F32),
            pltpu.VMEM((2 * tq, 2 * LANES), F32),
            pltpu.VMEM((2 * tq, tk), F32),
            pltpu.VMEM((2 * tq, tk), F32),
            pltpu.VMEM((2 * tq, tk), BF16),
        ],
        compiler_params=_params(("parallel", "parallel", "arbitrary"), VMEM_LIMIT),
        name="diff_attn",
    )(lam.reshape(1), proj, proj, proj, b_diag, b_off, subln_g.reshape(1, LANES))


def _s5_kernel(u_ref, wb_ref, wc_ref, tab_ref, d_ref, wglu_ref, o_ref,
               xr_ref, xi_ref, cr_ref, ci_ref, *, tb):
    @pl.when(pl.program_id(1) == 0)
    def _():
        cr_ref[...] = jnp.zeros(cr_ref.shape, F32)
        ci_ref[...] = jnp.zeros(ci_ref.shape, F32)

    u = u_ref[...]
    ub = u.astype(BF16)
    nblk = S5_WIDTH // S5_BLK
    sw = S5_LANES // nblk
    for i in range(nblk):
        bu = jnp.dot(ub[:, i * S5_BLK:(i + 1) * S5_BLK], wb_ref[i], preferred_element_type=F32)
        xr_ref[:, i * sw:(i + 1) * sw] = bu[:, :sw]
        xi_ref[:, i * sw:(i + 1) * sw] = bu[:, sw:]

    for c in range(S5_LANES // S5_CHUNK):
        ls = slice(c * S5_CHUNK, (c + 1) * S5_CHUNK)
        a1r, a1i = tab_ref[0, :, ls], tab_ref[1, :, ls]
        a2r, a2i = tab_ref[2, :, ls], tab_ref[3, :, ls]
        a4r, a4i = tab_ref[4, :, ls], tab_ref[5, :, ls]
        pwr, pwi = tab_ref[6, :, ls], tab_ref[7, :, ls]

        def body(g, carry, ls=ls, a1r=a1r, a1i=a1i, a2r=a2r, a2i=a2i, a4r=a4r, a4i=a4i, pwr=pwr, pwi=pwi):
            cr, ci = carry
            rows = pl.ds(pl.multiple_of(g * SUBLANES, SUBLANES), SUBLANES)
            xr = xr_ref[rows, ls]
            xi = xi_ref[rows, ls]
            for sh, ar, ai in ((1, a1r, a1i), (2, a2r, a2i), (4, a4r, a4i)):
                sr = pltpu.roll(xr, sh, 0)
                si = pltpu.roll(xi, sh, 0)
                xr, xi = xr + ar * sr - ai * si, xi + ar * si + ai * sr
            xr, xi = xr + pwr * cr - pwi * ci, xi + pwr * ci + pwi * cr
            xr_ref[rows, ls] = xr
            xi_ref[rows, ls] = xi
            return xr[SUBLANES - 1:SUBLANES, :], xi[SUBLANES - 1:SUBLANES, :]

        cr, ci = lax.fori_loop(0, tb // SUBLANES, body, (cr_ref[:, ls], ci_ref[:, ls]))
        cr_ref[:, ls] = cr
        ci_ref[:, ls] = ci

    ys = []
    for i in range(nblk):
        xc = jnp.concatenate([xr_ref[:, i * sw:(i + 1) * sw], xi_ref[:, i * sw:(i + 1) * sw]], axis=1)
        ys.append(jnp.dot(xc.astype(BF16), wc_ref[i], preferred_element_type=F32))
    y = jnp.concatenate(ys, axis=1) + d_ref[...] * u
    z = 0.5 * y * (1.0 + jnp.tanh(math.sqrt(2.0 / math.pi) * (y + 0.044715 * (y * y * y))))
    gate = jax.nn.sigmoid(jnp.dot(z.astype(BF16), wglu_ref[...], preferred_element_type=F32))
    o_ref[...] = (z * gate).astype(o_ref.dtype)


def _s5_prep(a_re, a_im, log_dt, b_re, b_im, c_re, c_im):
    lr = jnp.minimum(a_re.astype(F32), -1e-4)
    li = a_im.astype(F32)
    dt = jnp.exp(log_dt.astype(F32))[:, None]
    mag = jnp.exp(lr * dt)
    ab_re = mag * jnp.cos(li * dt)
    ab_im = mag * jnp.sin(li * dt)
    den = lr * lr + li * li
    nr = ab_re - 1.0
    ni = ab_im
    f_re = (nr * lr + ni * li) / den
    f_im = (ni * lr - nr * li) / den
    br = b_re.astype(F32)
    bi = b_im.astype(F32)
    bb_re = f_re[..., None] * br - f_im[..., None] * bi
    bb_im = f_re[..., None] * bi + f_im[..., None] * br

    nblk = S5_WIDTH // S5_BLK
    gpb = S5_GROUPS // nblk
    eye = jnp.eye(gpb, dtype=F32)

    def bmat(bb):
        bb = bb.reshape(nblk, gpb, S5_STATE, S5_GROUP_CH)
        m = jnp.einsum('ignc,gh->igchn', bb, eye)
        return m.reshape(nblk, gpb * S5_GROUP_CH, gpb * S5_STATE)

    def cmat(cc):
        cc = cc.astype(F32).reshape(nblk, gpb, S5_GROUP_CH, S5_STATE)
        m = jnp.einsum('igcn,gh->ignhc', cc, eye)
        return m.reshape(nblk, gpb * S5_STATE, gpb * S5_GROUP_CH)

    w_b = jnp.concatenate([bmat(bb_re), bmat(bb_im)], axis=2).astype(BF16)
    w_c = jnp.concatenate([cmat(c_re), -cmat(c_im)], axis=1).astype(BF16)

    def power(j):
        m = jnp.exp(lr * dt * j)
        return (m * jnp.cos(li * dt * j)).reshape(-1), (m * jnp.sin(li * dt * j)).reshape(-1)

    row = jnp.arange(SUBLANES)[:, None]
    tabs = []
    for sh in (1, 2, 4):
        pr, pi = power(float(sh))
        keep = (row >= sh).astype(F32)
        tabs += [keep * pr[None, :], keep * pi[None, :]]
    pows = [power(float(j + 1)) for j in range(SUBLANES)]
    tabs += [jnp.stack([p[0] for p in pows]), jnp.stack([p[1] for p in pows])]
    return w_b, w_c, jnp.stack(tabs)


def _s5_mixer(proj, w_b, w_c, tabs, d_skip, w_glu, batch, seq):
    t = proj.shape[0]
    tb = min(S5_TB, seq)
    nt = seq // tb
    nblk = S5_WIDTH // S5_BLK
    return pl.pallas_call(
        functools.partial(_s5_kernel, tb=tb),
        grid=(batch, nt),
        in_specs=[
            pl.BlockSpec((tb, S5_WIDTH), lambda b, i: (b * nt + i, 0)),
            pl.BlockSpec(w_b.shape, lambda b, i: (0, 0, 0)),
            pl.BlockSpec(w_c.shape, lambda b, i: (0, 0, 0)),
            pl.BlockSpec(tabs.shape, lambda b, i: (0, 0, 0)),
            pl.BlockSpec((1, S5_WIDTH), lambda b, i: (0, 0)),
            pl.BlockSpec((S5_WIDTH, S5_WIDTH), lambda b, i: (0, 0)),
        ],
        out_specs=pl.BlockSpec((tb, S5_WIDTH), lambda b, i: (b * nt + i, 0)),
        out_shape=jax.ShapeDtypeStruct((t, S5_WIDTH), BF16),
        scratch_shapes=[
            pltpu.VMEM((tb, S5_LANES), F32),
            pltpu.VMEM((tb, S5_LANES), F32),
            pltpu.VMEM((1, S5_LANES), F32),
            pltpu.VMEM((1, S5_LANES), F32),
        ],
        compiler_params=_params(("parallel", "arbitrary"), VMEM_LIMIT),
        name="s5_mixer",
    )(proj, w_b, w_c, tabs, d_skip.reshape(1, S5_WIDTH), w_glu)


def _hgrn_consts(c):
    t = np.arange(c)[:, None]
    u = np.arange(c)[None, :]
    mats = [u <= t, u > t]
    masks = [np.eye(c, dtype=bool)]
    h = 1
    while h < c:
        mid = (t // (2 * h)) * 2 * h + h
        mats.append(np.where(t >= mid, (u >= mid) & (u <= t), (u > t) & (u < mid)))
        masks.append(((u // (2 * h)) == (t // (2 * h))) & (t >= mid) & (u < mid))
        h *= 2
    return (np.concatenate(mats, axis=0).astype(np.float32), np.stack(masks).astype(np.float32))


def _hgrn_kernel(q_ref, f_ref, i_ref, g_ref, lb_ref, ng_ref, mall_ref, mask_ref, o_ref,
                 e_ref, st_ref, *, c):
    @pl.when(pl.program_id(1) == 0)
    def _():
        st_ref[...] = jnp.zeros(st_ref.shape, F32)

    lb = lb_ref[...]
    fl = f_ref[...]
    la = jnp.log(lb)
    lbb = jnp.log1p(-lb) + (jnp.minimum(fl, 0.0) - jnp.log1p(jnp.exp(-jnp.abs(fl))))
    logf = jnp.maximum(la, lbb) + jnp.log1p(jnp.exp(-jnp.abs(la - lbb)))
    hi = logf.astype(BF16)
    lo = (logf - hi.astype(F32)).astype(BF16)
    e2 = jnp.dot(mall_ref[...], jnp.concatenate([hi, lo], axis=1), preferred_element_type=F32)
    e_ref[...] = e2[:, :HGRN_WIDTH] + e2[:, HGRN_WIDTH:]

    n_levels = mask_ref.shape[0] - 1
    nt = (((1,), (1,)), ((), ()))
    for hd in range(HGRN_HEADS):
        ls = slice(hd * HGRN_DIM, (hd + 1) * HGRN_DIM)
        qr = q_ref[:, ls]
        q = qr * jax.nn.sigmoid(qr) * (HGRN_DIM ** -0.5)
        k = (1.0 - lb[:, ls]) * jax.nn.sigmoid(-fl[:, ls])
        vb = i_ref[:, ls].astype(BF16)
        p = lax.dot_general(q.astype(BF16), k.astype(BF16), nt, preferred_element_type=F32) * mask_ref[0]
        for lv in range(n_levels):
            fac = jnp.exp(e_ref[(2 + lv) * c:(3 + lv) * c, ls])
            s = lax.dot_general((q * fac).astype(BF16), (k * fac).astype(BF16), nt,
                                preferred_element_type=F32)
            p = p + s * mask_ref[lv + 1]
        o = jnp.dot(p.astype(BF16), vb, preferred_element_type=F32)
        gcum = e_ref[0:c, ls]
        st = st_ref[hd]
        o = o + lax.dot_general((q * jnp.exp(gcum)).astype(BF16), st.astype(BF16), nt,
                                preferred_element_type=F32)
        kd = (k * jnp.exp(e_ref[c:2 * c, ls])).astype(BF16)
        st_ref[hd] = st * jnp.exp(gcum[c - 1:c, :]) + lax.dot_general(
            vb, kd, (((0,), (0,)), ((), ())), preferred_element_type=F32)
        ms = jnp.mean(o * o, axis=-1, keepdims=True)
        o = o * lax.rsqrt(ms + EPS) * ng_ref[...] * jax.nn.sigmoid(g_ref[:, ls])
        o_ref[:, ls] = o.astype(o_ref.dtype)


def _hgrn_mixer(proj, lb, norm_g, batch, seq):
    t = proj.shape[0]
    c = min(HGRN_C, seq)
    nc = seq // c
    m_all, masks = _hgrn_consts(c)
    m_all = jnp.asarray(m_all, BF16)
    masks = jnp.asarray(masks, F32)

    def col(j):
        return pl.BlockSpec((c, HGRN_WIDTH), lambda b, i: (b * nc + i, j))

    return pl.pallas_call(
        functools.partial(_hgrn_kernel, c=c),
        grid=(batch, nc),
        in_specs=[
            col(1), col(2), col(3), col(4),
            pl.BlockSpec((1, HGRN_WIDTH), lambda b, i: (0, 0)),
            pl.BlockSpec((1, HGRN_DIM), lambda b, i: (0, 0)),
            pl.BlockSpec(m_all.shape, lambda b, i: (0, 0)),
            pl.BlockSpec(masks.shape, lambda b, i: (0, 0, 0)),
        ],
        out_specs=pl.BlockSpec((c, HGRN_WIDTH), lambda b, i: (b * nc + i, 0)),
        out_shape=jax.ShapeDtypeStruct((t, HGRN_WIDTH), BF16),
        scratch_shapes=[
            pltpu.VMEM((m_all.shape[0], HGRN_WIDTH), F32),
            pltpu.VMEM((HGRN_HEADS, HGRN_DIM, HGRN_DIM), F32),
        ],
        compiler_params=_params(("parallel", "arbitrary"), VMEM_LIMIT),
        name="hgrn2_mixer",
    )(proj, proj, proj, proj, lb.reshape(1, HGRN_WIDTH), norm_g.reshape(1, HGRN_DIM), m_all, masks)


def _out_proj_kernel(h_ref, a_ref, s_ref, r_ref, wa_ref, ws_ref, wr_ref, gn_ref, rh_ref, rl_ref,
                     ho_ref, hn_ref, eid_ref, gate_ref):
    acc = jnp.dot(a_ref[...], wa_ref[...], preferred_element_type=F32)
    acc = acc + jnp.dot(s_ref[...], ws_ref[...], preferred_element_type=F32)
    acc = acc + jnp.dot(r_ref[...], wr_ref[...], preferred_element_type=F32)
    h = h_ref[...] + acc
    ho_ref[...] = h
    ms = jnp.mean(h * h, axis=-1, keepdims=True)
    hn = h * lax.rsqrt(ms + EPS) * gn_ref[...]
    hn_ref[...] = hn

    hh = hn.astype(BF16)
    hl = (hn - hh.astype(F32)).astype(BF16)
    logits = (jnp.dot(hh, rh_ref[...], preferred_element_type=F32)
              + jnp.dot(hh, rl_ref[...], preferred_element_type=F32)
              + jnp.dot(hl, rh_ref[...], preferred_element_type=F32))
    lane = lax.broadcasted_iota(jnp.int32, logits.shape, 1).astype(F32)
    neg = -jnp.inf
    big = 1e9
    is_group = jnp.where(lane >= N_EXPERTS, jnp.where(lane < N_EXPERTS + N_GROUPS, 1.0, 0.0), 0.0)
    gl = jnp.where(is_group > 0, logits, neg)
    gmax = jnp.max(gl, axis=-1, keepdims=True)
    g_lane = jnp.min(jnp.where(gl == gmax, lane, big), axis=-1, keepdims=True)
    p_g = 1.0 / jnp.sum(jnp.exp(gl - gmax), axis=-1, keepdims=True)
    lo_lane = (g_lane - N_EXPERTS) * EXPERTS_PER_GROUP
    in_group = jnp.where(lane >= lo_lane, jnp.where(lane < lo_lane + EXPERTS_PER_GROUP, 1.0, 0.0), 0.0)
    el = jnp.where(in_group > 0, logits, neg)
    t1 = jnp.max(el, axis=-1, keepdims=True)
    i1 = jnp.min(jnp.where(el == t1, lane, big), axis=-1, keepdims=True)
    el2 = jnp.where(lane == i1, neg, el)
    t2 = jnp.max(el2, axis=-1, keepdims=True)
    i2 = jnp.min(jnp.where(el2 == t2, lane, big), axis=-1, keepdims=True)
    e21 = jnp.exp(t2 - t1)
    g1 = p_g / (1.0 + e21)
    g2 = p_g * e21 / (1.0 + e21)
    eid_ref[...] = jnp.where(lane == 0, i1, jnp.where(lane == 1, i2, 0.0)).astype(jnp.int32)
    gate_ref[...] = jnp.where(lane == 0, g1, jnp.where(lane == 1, g2, 0.0))


def _out_proj(h, attn, s5, hg, w_a, w_s, w_r, gn, r_hi, r_lo):
    t, d = h.shape
    tm = min(OUT_TM, t)

    def rows(w):
        return pl.BlockSpec((tm, w), lambda i: (i, 0))

    def full(a):
        return pl.BlockSpec(a.shape, lambda i: (0,) * a.ndim)

    return pl.pallas_call(
        _out_proj_kernel,
        grid=(t // tm,),
        in_specs=[rows(d), rows(ATTN_WIDTH), rows(S5_WIDTH), rows(HGRN_WIDTH),
                  full(w_a), full(w_s), full(w_r), pl.BlockSpec((1, d), lambda i: (0, 0)),
                  full(r_hi), full(r_lo)],
        out_specs=[rows(d), rows(d), rows(LANES), rows(LANES)],
        out_shape=[jax.ShapeDtypeStruct((t, d), F32), jax.ShapeDtypeStruct((t, d), F32),
                   jax.ShapeDtypeStruct((t, LANES), jnp.int32), jax.ShapeDtypeStruct((t, LANES), F32)],
        compiler_params=_params(("parallel",), VMEM_LIMIT),
        name="out_proj_router",
    )(h, attn, s5, hg, w_a, w_s, w_r, gn.reshape(1, d), r_hi, r_lo)


def _moe_plan(eid, tm):
    flat = eid.reshape(-1)
    n_slots = flat.shape[0]
    onehot = (flat[:, None] == jnp.arange(N_EXPERTS, dtype=jnp.int32)[None, :]).astype(jnp.int32)
    csum = jnp.cumsum(onehot, axis=0)
    rank = jnp.sum(onehot * csum, axis=1) - 1
    counts = csum[-1]
    tiles = (counts + tm - 1) // tm
    tile_end = jnp.cumsum(tiles)
    tile_start = tile_end - tiles
    slot_start = jnp.cumsum(counts) - counts
    pos = jnp.sum(onehot * tile_start[None, :], axis=1) * tm + rank
    tok_sorted = jnp.argsort(flat, stable=True).astype(jnp.int32) // TOP_K
    tok_sorted = jnp.pad(tok_sorted, (0, tm))
    nt_max = (n_slots + N_EXPERTS * (tm - 1)) // tm
    n_used = tile_end[-1]
    j = jnp.minimum(jnp.arange(nt_max, dtype=jnp.int32), n_used - 1)
    tile_expert = jnp.sum((j[:, None] >= tile_end[None, :]).astype(jnp.int32), axis=1)
    tile_slot0 = slot_start[tile_expert] + (j - tile_start[tile_expert]) * tm
    return (pos.astype(jnp.int32), tok_sorted, tile_expert.astype(jnp.int32),
            tile_slot0.astype(jnp.int32), n_used.reshape(1).astype(jnp.int32), nt_max)


def _expert_kernel(tok_ref, te_ref, s0_ref, nu_ref, hn_ref, wg_ref, wu_ref, wd_ref, y_ref,
                   xbuf_ref, wgb_ref, wub_ref, wdb_ref, sem, *, tm, n_slots):
    j = pl.program_id(0)
    n_used = nu_ref[0]

    def row_copy(buf, r, tok):
        return pltpu.make_async_copy(hn_ref.at[pl.ds(tok, 1)], xbuf_ref.at[buf, pl.ds(r, 1)], sem.at[buf])

    def start_gather(tile):
        buf = lax.rem(tile, 2)
        base = s0_ref[tile]

        def issue(r, carry):
            row_copy(buf, r, tok_ref[base + r]).start()
            return carry

        lax.fori_loop(0, tm, issue, 0, unroll=DMA_UNROLL)

    @pl.when(j == 0)
    def _():
        start_gather(0)

    @pl.when(j + 1 < n_used)
    def _():
        start_gather(j + 1)

    @pl.when(j < n_used)
    def _():
        buf = lax.rem(j, 2)

        def wait(r, carry):
            row_copy(buf, r, 0).wait()
            return carry

        lax.fori_loop(0, tm, wait, 0, unroll=DMA_UNROLL)

        prev = te_ref[jnp.maximum(j - 1, 0)]

        @pl.when(jnp.logical_or(j == 0, te_ref[j] != prev))
        def _():
            wgb_ref[...] = wg_ref[0, 0].astype(BF16)
            wub_ref[...] = wu_ref[0, 0].astype(BF16)
            wdb_ref[...] = wd_ref[0, 0].astype(BF16)

        x = xbuf_ref[buf].astype(BF16)
        g = jnp.dot(x, wgb_ref[...], preferred_element_type=F32)
        u = jnp.dot(x, wub_ref[...], preferred_element_type=F32)
        hmid = (g * jax.nn.sigmoid(g) * u).astype(BF16)
        y_ref[...] = jnp.dot(hmid, wdb_ref[...], preferred_element_type=F32)

    @pl.when(j >= n_used)
    def _():
        y_ref[...] = jnp.zeros(y_ref.shape, F32)


def _moe_experts(hn, tok_sorted, tile_expert, tile_slot0, n_used, w_gate, w_up, w_down, layer, tm, nt_max):
    t, d = hn.shape
    de = w_gate.shape[3]
    grid_spec = pltpu.PrefetchScalarGridSpec(
        num_scalar_prefetch=4,
        grid=(nt_max,),
        in_specs=[
            pl.BlockSpec(memory_space=pl.ANY),
            pl.BlockSpec((1, 1, d, de), lambda j, tok, te, s0, nu: (layer, te[j], 0, 0)),
            pl.BlockSpec((1, 1, d, de), lambda j, tok, te, s0, nu: (layer, te[j], 0, 0)),
            pl.BlockSpec((1, 1, de, d), lambda j, tok, te, s0, nu: (layer, te[j], 0, 0)),
        ],
        out_specs=pl.BlockSpec((tm, d), lambda j, tok, te, s0, nu: (j, 0)),
        scratch_shapes=[pltpu.VMEM((2, tm, d), F32),
                        pltpu.VMEM((d, de), BF16), pltpu.VMEM((d, de), BF16), pltpu.VMEM((de, d), BF16),
                        pltpu.SemaphoreType.DMA((2,))],
    )
    return pl.pallas_call(
        functools.partial(_expert_kernel, tm=tm, n_slots=t * TOP_K),
        grid_spec=grid_spec,
        out_shape=jax.ShapeDtypeStruct((nt_max * tm, d), F32),
        compiler_params=_params(("arbitrary",), VMEM_LIMIT),
        name="moe_experts",
    )(tok_sorted, tile_expert, tile_slot0, n_used, hn, w_gate, w_up, w_down)


def _combine_kernel(pos_ref, h_ref, gate_ref, gn_ref, y_ref, o_ref, buf_ref, sem, *, tm, final_norm):
    i = pl.program_id(0)
    cur = lax.rem(i, 2)

    def row_copy(b, k, r, src_row):
        return pltpu.make_async_copy(y_ref.at[pl.ds(src_row, 1)], buf_ref.at[b, k, pl.ds(r, 1)], sem.at[b])

    def start_gather(tile):
        b = lax.rem(tile, 2)

        def issue(r, carry):
            slot = (tile * tm + r) * TOP_K
            for k in range(TOP_K):
                row_copy(b, k, r, pos_ref[slot + k]).start()
            return carry

        lax.fori_loop(0, tm, issue, 0, unroll=DMA_UNROLL)

    @pl.when(i == 0)
    def _():
        start_gather(0)

    @pl.when(i + 1 < pl.num_programs(0))
    def _():
        start_gather(i + 1)

    def wait(r, carry):
        for k in range(TOP_K):
            row_copy(cur, k, r, 0).wait()
        return carry

    lax.fori_loop(0, tm, wait, 0, unroll=DMA_UNROLL)

    gates = gate_ref[...]
    h = h_ref[...] + gates[:, 0:1] * buf_ref[cur, 0] + gates[:, 1:2] * buf_ref[cur, 1]
    if final_norm:
        ms = jnp.mean(h * h, axis=-1, keepdims=True)
        h = h * lax.rsqrt(ms + EPS) * gn_ref[...]
    o_ref[...] = h


def _moe_combine(h, gates, ys, pos, final_g, final_norm):
    t, d = h.shape
    tm = min(COMB_TM, t)
    grid_spec = pltpu.PrefetchScalarGridSpec(
        num_scalar_prefetch=1,
        grid=(t // tm,),
        in_specs=[
            pl.BlockSpec((tm, d), lambda i, p: (i, 0)),
            pl.BlockSpec((tm, LANES), lambda i, p: (i, 0)),
            pl.BlockSpec((1, d), lambda i, p: (0, 0)),
            pl.BlockSpec(memory_space=pl.ANY),
        ],
        out_specs=pl.BlockSpec((tm, d), lambda i, p: (i, 0)),
        scratch_shapes=[pltpu.VMEM((2, TOP_K, tm, d), F32), pltpu.SemaphoreType.DMA((2,))],
    )
    return pl.pallas_call(
        functools.partial(_combine_kernel, tm=tm, final_norm=final_norm),
        grid_spec=grid_spec,
        out_shape=jax.ShapeDtypeStruct((t, d), F32),
        compiler_params=_params(("arbitrary",), VMEM_LIMIT),
        name="moe_combine",
    )(pos, h, gates, final_g.reshape(1, d), ys)


def _permute_w_in(w):
    a = 3 * ATTN_WIDTH
    w = jnp.concatenate([w[:, a:], w[:, :a]], axis=1)
    return jnp.pad(w, ((0, 0), (0, IN_COLS_PAD - IN_COLS))).astype(BF16)


def kernel(x, w_in, w_out, mix_norm_g, ffn_norm_g, rel_bias, diff_lambda, attn_subln_g, s5_a_re, s5_a_im, s5_log_dt, s5_b_re, s5_b_im, s5_c_re, s5_c_im, s5_d, s5_w_glu, hgrn_lb_logits, hgrn_norm_g, moe_w_group, moe_w_router, moe_w_gate, moe_w_up, moe_w_down, final_norm_g):
    batch, seq, d = x.shape
    depth = w_in.shape[0]
    t = batch * seq
    h = x.reshape(t, d)

    lb_cum = jnp.cumsum(jax.nn.softmax(hgrn_lb_logits.astype(F32), axis=0), axis=0)
    lb_all = lb_cum - lb_cum[0:1]
    b_diag, b_off = _attn_bias_tiles(rel_bias, min(ATTN_TQ, seq))

    for l in range(depth):
        proj = _in_proj(h, mix_norm_g[l], _permute_w_in(w_in[l]))

        lam_init = 0.8 - 0.6 * math.exp(-0.3 * l)
        lv = diff_lambda[l].astype(F32)
        lam = jnp.exp(jnp.sum(lv[0] * lv[1])) - jnp.exp(jnp.sum(lv[2] * lv[3])) + lam_init
        attn = _diff_attention(proj, b_diag, b_off, lam, attn_subln_g[l], lam_init, batch, seq)

        w_b, w_c, tabs = _s5_prep(s5_a_re[l], s5_a_im[l], s5_log_dt[l], s5_b_re[l], s5_b_im[l],
                                  s5_c_re[l], s5_c_im[l])
        s5 = _s5_mixer(proj, w_b, w_c, tabs, s5_d[l], s5_w_glu[l].astype(BF16), batch, seq)

        hg = _hgrn_mixer(proj, lb_all[l], hgrn_norm_g[l], batch, seq)

        wo = w_out[l].astype(BF16)
        w_r = jnp.concatenate([moe_w_router[l], moe_w_group[l]], axis=1).astype(F32)
        w_r = jnp.pad(w_r, ((0, 0), (0, LANES - w_r.shape[1])))
        r_hi = w_r.astype(BF16)
        r_lo = (w_r - r_hi.astype(F32)).astype(BF16)
        h, hn, eid, gates = _out_proj(
            h, attn, s5, hg, wo[:ATTN_WIDTH], wo[ATTN_WIDTH:ATTN_WIDTH + S5_WIDTH],
            wo[ATTN_WIDTH + S5_WIDTH:], ffn_norm_g[l], r_hi, r_lo)

        pos, tok_sorted, tile_expert, tile_slot0, n_used, nt_max = _moe_plan(eid[:, :TOP_K], MOE_TM)
        ys = _moe_experts(hn, tok_sorted, tile_expert, tile_slot0, n_used,
                          moe_w_gate, moe_w_up, moe_w_down, l, MOE_TM, nt_max)
        h = _moe_combine(h, gates, ys, pos, final_norm_g, final_norm=(l == depth - 1))

    return h.reshape(batch, seq, d)
```

```python
import functools
import math

import numpy as np
import jax
import jax.numpy as jnp
from jax import lax
from jax.experimental import pallas as pl
from jax.experimental.pallas import tpu as pltpu

F32 = jnp.float32
BF16 = jnp.bfloat16
EPS = 1e-6
LOG2E = 1.4426950408889634

D_MODEL = 2048
ATTN_HEADS = 6
ATTN_QK_DIM = 64
ATTN_V_DIM = 128
ATTN_WIDTH = ATTN_HEADS * ATTN_V_DIM
REL_BUCKETS = 32
REL_MAX_DIST = 128
S5_GROUPS = 40
S5_GROUP_CH = 16
S5_STATE = 64
S5_WIDTH = S5_GROUPS * S5_GROUP_CH
S5_LANES = S5_GROUPS * S5_STATE
HGRN_HEADS = 5
HGRN_DIM = 128
HGRN_WIDTH = HGRN_HEADS * HGRN_DIM
N_GROUPS = 4
EXPERTS_PER_GROUP = 8
N_EXPERTS = N_GROUPS * EXPERTS_PER_GROUP
TOP_K = 2
D_EXPERT = 512

LANES = 128
SUBLANES = 8

IN_COLS = 3 * ATTN_WIDTH + S5_WIDTH + 4 * HGRN_WIDTH
PROJ_TN = 512
IN_COLS_PAD = -(-IN_COLS // PROJ_TN) * PROJ_TN
ATTN_COL_BLOCK = (S5_WIDTH + 4 * HGRN_WIDTH) // LANES

PROJ_TM = 1024
ATTN_TQ = 512
ATTN_TK = 512
ATTN_NEAR = 2
ATTN_ROWS = 32
S5_TB = 256
S5_CHUNK = 640
S5_BLK = 128
HGRN_C = 128
OUT_TM = 256
MOE_TM = 256
COMB_TM = 128
DMA_UNROLL = 8

VMEM_LIMIT = 56 * 1024 * 1024


def _params(sem, vmem=None):
    return pltpu.CompilerParams(dimension_semantics=sem, vmem_limit_bytes=vmem)


def _in_proj_kernel(x_ref, g_ref, w_ref, o_ref, xn_ref):
    @pl.when(pl.program_id(1) == 0)
    def _():
        x = x_ref[...]
        ms = jnp.mean(x * x, axis=-1, keepdims=True)
        xn_ref[...] = (x * lax.rsqrt(ms + EPS) * g_ref[...]).astype(BF16)

    o_ref[...] = jnp.dot(xn_ref[...], w_ref[...], preferred_element_type=F32)


def _in_proj(h, g, w):
    t, d = h.shape
    n = w.shape[1]
    tm = min(PROJ_TM, t)
    return pl.pallas_call(
        _in_proj_kernel,
        grid=(t // tm, n // PROJ_TN),
        in_specs=[
            pl.BlockSpec((tm, d), lambda i, j: (i, 0)),
            pl.BlockSpec((1, d), lambda i, j: (0, 0)),
            pl.BlockSpec((d, PROJ_TN), lambda i, j: (0, j)),
        ],
        out_specs=pl.BlockSpec((tm, PROJ_TN), lambda i, j: (i, j)),
        out_shape=jax.ShapeDtypeStruct((t, n), F32),
        scratch_shapes=[pltpu.VMEM((tm, d), BF16)],
        compiler_params=_params(("parallel", "arbitrary"), VMEM_LIMIT),
        name="in_proj",
    )(h, g.reshape(1, d), w)


def _attn_kernel(lam_ref, q_ref, k_ref, v_ref, bias_ref, g_ref, o_ref,
                 kb_ref, vb_ref, qz_ref, m_ref, a_ref, acc_ref, s0_ref, s1_ref, p_ref, *, tq, tk, out_scale):
    qi = pl.program_id(2)
    ratio = tq // tk

    @pl.when(qi == 0)
    def _():
        kb_ref[...] = k_ref[...].astype(BF16)
        vb_ref[:, 0:LANES] = v_ref[...].astype(BF16)
        vb_ref[:, LANES:2 * LANES] = jnp.ones((vb_ref.shape[0], LANES), BF16)

    q = q_ref[...] * (ATTN_QK_DIM ** -0.5 * LOG2E)
    lane = lax.broadcasted_iota(jnp.int32, q.shape, 1)
    qz_ref[0:tq, :] = jnp.where(lane < ATTN_QK_DIM, q, 0.0).astype(BF16)
    qz_ref[tq:2 * tq, :] = jnp.where(lane >= ATTN_QK_DIM, q, 0.0).astype(BF16)
    m_ref[...] = jnp.full(m_ref.shape, -jnp.inf, F32)
    acc_ref[...] = jnp.zeros(acc_ref.shape, F32)

    n_col = tk // LANES
    groups = [slice(g * ATTN_ROWS, (g + 1) * ATTN_ROWS) for g in range(2 * tq // ATTN_ROWS)]

    def scores(j, s_ref):
        k = kb_ref[pl.ds(pl.multiple_of(j * tk, tk), tk), :]
        s_ref[...] = lax.dot_general(qz_ref[...], k, (((1,), (1,)), ((), ())), preferred_element_type=F32)

    def softmax_pv(j, s_ref, biased):
        if biased:
            kind = jnp.clip(j - ratio * qi + ATTN_NEAR, 0, ATTN_NEAR + ratio - 1)
        for rows in groups:
            cols = [s_ref[rows, i * LANES:(i + 1) * LANES] for i in range(n_col)]
            if biased:
                b0 = rows.start % tq
                cols = [x + bias_ref[0, kind, b0:b0 + ATTN_ROWS, i * LANES:(i + 1) * LANES]
                        for i, x in enumerate(cols)]
                for i, x in enumerate(cols):
                    s_ref[rows, i * LANES:(i + 1) * LANES] = x
            mx = functools.reduce(jnp.maximum, cols)
            m_old = m_ref[rows, :]
            m_new = jnp.maximum(m_old, jnp.max(mx, axis=-1, keepdims=True))
            m_ref[rows, :] = m_new
            a_ref[rows, :] = jnp.exp2(m_old - m_new)
        for rows in groups:
            m_new = m_ref[rows, :]
            for i in range(n_col):
                p_ref[rows, i * LANES:(i + 1) * LANES] = jnp.exp2(
                    s_ref[rows, i * LANES:(i + 1) * LANES] - m_new).astype(BF16)
        v = vb_ref[pl.ds(pl.multiple_of(j * tk, tk), tk), :]
        alpha = a_ref[...]
        acc_ref[...] = (jnp.concatenate([alpha, alpha], axis=1) * acc_ref[...]
                        + jnp.dot(p_ref[...], v, preferred_element_type=F32))

    n_tiles = ratio * (qi + 1)
    first = lax.rem(n_tiles, 2)
    n_far = jnp.maximum(ratio * qi - ATTN_NEAR + 1, 0)
    n_far_pairs = jnp.maximum(n_far - first, 0) // 2
    n_pairs = (n_tiles - first) // 2

    @pl.when(first == 1)
    def _():
        scores(0, s1_ref)
        scores(jnp.minimum(1, n_tiles - 1), s0_ref)
        softmax_pv(0, s1_ref, True)

    @pl.when(first == 0)
    def _():
        scores(0, s0_ref)

    def pair_body(jj, carry, biased):
        j = first + 2 * jj
        scores(j + 1, s1_ref)
        softmax_pv(j, s0_ref, biased)
        scores(jnp.minimum(j + 2, n_tiles - 1), s0_ref)
        softmax_pv(j + 1, s1_ref, biased)
        return carry

    lax.fori_loop(0, n_far_pairs, functools.partial(pair_body, biased=False), 0)
    lax.fori_loop(n_far_pairs, n_pairs, functools.partial(pair_body, biased=True), 0)

    inv_l = 1.0 / acc_ref[:, LANES:2 * LANES]
    o1 = acc_ref[0:tq, 0:LANES] * inv_l[0:tq]
    o2 = acc_ref[tq:2 * tq, 0:LANES] * inv_l[tq:2 * tq]
    o = o1 - lam_ref[0] * o2
    ms = jnp.mean(o * o, axis=-1, keepdims=True)
    o_ref[...] = (o * lax.rsqrt(ms + EPS) * g_ref[...] * out_scale).astype(o_ref.dtype)


def _t5_bucket(dist):
    n = jnp.maximum(dist, 0)
    max_exact = REL_BUCKETS // 2
    large = max_exact + (jnp.log(jnp.maximum(n, 1).astype(F32) / max_exact)
                         / math.log(REL_MAX_DIST / max_exact) * (REL_BUCKETS - max_exact)).astype(jnp.int32)
    large = jnp.minimum(large, REL_BUCKETS - 1)
    return jnp.where(n < max_exact, n, large)


def _attn_bias_tiles(rel_bias, tq, tk):
    table = rel_bias.astype(F32)
    far = table[REL_BUCKETS - 1]
    r = jnp.arange(tq)[:, None]
    c = jnp.arange(tk)[None, :]

    def lookup(dist):
        bucket = _t5_bucket(dist)
        out = jnp.zeros((ATTN_HEADS,) + dist.shape, F32)
        for b in range(REL_BUCKETS):
            out = jnp.where((bucket == b)[None], table[b][:, None, None], out)
        return out

    tiles = [jnp.zeros((ATTN_HEADS, tq, tk), F32)]
    for rel in range(-ATTN_NEAR + 1, tq // tk):
        dist = r - c - rel * tk
        tiles.append(jnp.where((dist >= 0)[None], (lookup(dist) - far[:, None, None]) * LOG2E, -1e30))
    return jnp.stack(tiles, axis=1)


def _diff_attention(proj, bias, lam, subln_g, lam_init, batch, seq):
    t = proj.shape[0]
    tq = min(ATTN_TQ, seq)
    tk = min(ATTN_TK, seq)
    ratio = tq // tk
    assert (ATTN_NEAR - 1) * tk >= REL_MAX_DIST and ratio * tk == tq
    nq = seq // tq
    kernel = functools.partial(_attn_kernel, tq=tq, tk=tk, out_scale=1.0 - lam_init)
    qb, kb, vb = ATTN_COL_BLOCK, ATTN_COL_BLOCK + ATTN_HEADS, ATTN_COL_BLOCK + 2 * ATTN_HEADS
    return pl.pallas_call(
        kernel,
        grid=(batch, ATTN_HEADS, nq),
        in_specs=[
            pl.BlockSpec(memory_space=pltpu.SMEM),
            pl.BlockSpec((tq, LANES), lambda b, h, i: (b * nq + i, qb + h)),
            pl.BlockSpec((seq, LANES), lambda b, h, i: (b, kb + h)),
            pl.BlockSpec((seq, LANES), lambda b, h, i: (b, vb + h)),
            pl.BlockSpec((1,) + bias.shape[1:], lambda b, h, i: (h, 0, 0, 0)),
            pl.BlockSpec((1, LANES), lambda b, h, i: (0, 0)),
        ],
        out_specs=pl.BlockSpec((tq, LANES), lambda b, h, i: (b * nq + i, h)),
        out_shape=jax.ShapeDtypeStruct((t, ATTN_WIDTH), BF16),
        scratch_shapes=[
            pltpu.VMEM((seq, LANES), BF16),
            pltpu.VMEM((seq, 2 * LANES), BF16),
            pltpu.VMEM((2 * tq, LANES), BF16),
            pltpu.VMEM((2 * tq, LANES), F32),
            pltpu.VMEM((2 * tq, LANES), F32),
            pltpu.VMEM((2 * tq, 2 * LANES), F32),
            pltpu.VMEM((2 * tq, tk), F32),
            pltpu.VMEM((2 * tq, tk), F32),
            pltpu.VMEM((2 * tq, tk), BF16),
        ],
        compiler_params=_params(("parallel", "parallel", "arbitrary"), VMEM_LIMIT),
        name="diff_attn",
    )(lam.reshape(1), proj, proj, proj, bias, subln_g.reshape(1, LANES))


def _s5_kernel(u_ref, perm_ref, permt_ref, wb_ref, wc_ref, pw_ref, d_ref, wglu_ref, o_ref,
               xr_ref, xi_ref, cr_ref, ci_ref, kr_ref, ki_ref, *, tb):
    @pl.when(pl.program_id(1) == 0)
    def _():
        cr_ref[...] = jnp.zeros(cr_ref.shape, F32)
        ci_ref[...] = jnp.zeros(ci_ref.shape, F32)

    un = u_ref[...]
    uh = un.astype(BF16)
    ul = (un - uh.astype(F32)).astype(BF16)
    up = jnp.dot(perm_ref[...], jnp.concatenate([uh, ul], axis=1), preferred_element_type=F32)
    ub = up[:, :S5_WIDTH].astype(BF16)
    u = up[:, :S5_WIDTH] + up[:, S5_WIDTH:]
    nblk = S5_WIDTH // S5_BLK
    sw = S5_LANES // nblk
    tpb = sw // LANES
    for i in range(nblk):
        bu = jnp.dot(ub[:, i * S5_BLK:(i + 1) * S5_BLK], wb_ref[i], preferred_element_type=F32)
        for k in range(tpb):
            xr_ref[i * tpb + k] = bu[:, k * LANES:(k + 1) * LANES]
            xi_ref[i * tpb + k] = bu[:, sw + k * LANES:sw + (k + 1) * LANES]

    sl = tb // SUBLANES
    for c in range(S5_LANES // S5_CHUNK):
        tiles = range(c * S5_CHUNK // LANES, (c + 1) * S5_CHUNK // LANES)
        lss = [slice(n * LANES, (n + 1) * LANES) for n in tiles]
        a = [(pw_ref[0, 0, :, ls], pw_ref[1, 0, :, ls]) for ls in lss]

        def local_step(t, carry, tiles=tiles, a=a):
            rows = pl.ds(pl.multiple_of(t * SUBLANES, SUBLANES), SUBLANES)
            out = []
            for n, (ar, ai), (pr, pi) in zip(tiles, a, carry):
                xr = ar * pr - ai * pi + xr_ref[n, rows, :]
                xi = ar * pi + ai * pr + xi_ref[n, rows, :]
                xr_ref[n, rows, :] = xr
                xi_ref[n, rows, :] = xi
                out.append((xr, xi))
            return tuple(out)

        zero = jnp.zeros((SUBLANES, LANES), F32)
        ends = lax.fori_loop(0, sl, local_step, tuple((zero, zero) for _ in tiles), unroll=2)

        for ls, (er, ei) in zip(lss, ends):
            lr, li = pw_ref[0, sl - 1, 0:1, ls], pw_ref[1, sl - 1, 0:1, ls]
            kr, ki = cr_ref[:, ls], ci_ref[:, ls]
            for sub in range(SUBLANES):
                kr_ref[sub:sub + 1, ls] = kr
                ki_ref[sub:sub + 1, ls] = ki
                kr, ki = (er[sub:sub + 1, :] + lr * kr - li * ki, ei[sub:sub + 1, :] + lr * ki + li * kr)
            cr_ref[:, ls] = kr
            ci_ref[:, ls] = ki

        k8 = [(kr_ref[:, ls], ki_ref[:, ls]) for ls in lss]

        def fix_step(t, carry, tiles=tiles, lss=lss, k8=k8):
            rows = pl.ds(pl.multiple_of(t * SUBLANES, SUBLANES), SUBLANES)
            for n, ls, (kr8, ki8) in zip(tiles, lss, k8):
                pr = pw_ref[0, t, :, ls]
                pi = pw_ref[1, t, :, ls]
                xr_ref[n, rows, :] = xr_ref[n, rows, :] + pr * kr8 - pi * ki8
                xi_ref[n, rows, :] = xi_ref[n, rows, :] + pr * ki8 + pi * kr8
            return carry

        lax.fori_loop(0, sl, fix_step, 0, unroll=2)

    ys = []
    for i in range(nblk):
        xc = jnp.concatenate([xr_ref[i * tpb + k] for k in range(tpb)]
                             + [xi_ref[i * tpb + k] for k in range(tpb)], axis=1)
        ys.append(jnp.dot(xc.astype(BF16), wc_ref[i], preferred_element_type=F32))
    y = jnp.concatenate(ys, axis=1) + d_ref[...] * u
    z = 0.5 * y * (1.0 + jnp.tanh(math.sqrt(2.0 / math.pi) * (y + 0.044715 * (y * y * y))))
    gate = jax.nn.sigmoid(jnp.dot(z.astype(BF16), wglu_ref[...], preferred_element_type=F32))
    o_ref[...] = jnp.dot(permt_ref[...], (z * gate).astype(BF16),
                         preferred_element_type=F32).astype(o_ref.dtype)


def _s5_prep(a_re, a_im, log_dt, b_re, b_im, c_re, c_im, steps):
    lr = jnp.minimum(a_re.astype(F32), -1e-4)
    li = a_im.astype(F32)
    dt = jnp.exp(log_dt.astype(F32))[:, None]
    mag = jnp.exp(lr * dt)
    ab_re = mag * jnp.cos(li * dt)
    ab_im = mag * jnp.sin(li * dt)
    den = lr * lr + li * li
    nr = ab_re - 1.0
    ni = ab_im
    f_re = (nr * lr + ni * li) / den
    f_im = (ni * lr - nr * li) / den
    br = b_re.astype(F32)
    bi = b_im.astype(F32)
    bb_re = f_re[..., None] * br - f_im[..., None] * bi
    bb_im = f_re[..., None] * bi + f_im[..., None] * br

    nblk = S5_WIDTH // S5_BLK
    gpb = S5_GROUPS // nblk
    eye = jnp.eye(gpb, dtype=F32)

    def bmat(bb):
        bb = bb.reshape(nblk, gpb, S5_STATE, S5_GROUP_CH)
        m = jnp.einsum('ignc,gh->igchn', bb, eye)
        return m.reshape(nblk, gpb * S5_GROUP_CH, gpb * S5_STATE)

    def cmat(cc):
        cc = cc.astype(F32).reshape(nblk, gpb, S5_GROUP_CH, S5_STATE)
        m = jnp.einsum('igcn,gh->ignhc', cc, eye)
        return m.reshape(nblk, gpb * S5_STATE, gpb * S5_GROUP_CH)

    w_b = jnp.concatenate([bmat(bb_re), bmat(bb_im)], axis=2).astype(BF16)
    w_c = jnp.concatenate([cmat(c_re), -cmat(c_im)], axis=1).astype(BF16)

    j = jnp.arange(1, steps + 1, dtype=F32)[:, None, None]
    m = jnp.exp(lr * dt * j)
    pows = jnp.stack([(m * jnp.cos(li * dt * j)).reshape(steps, 1, -1),
                      (m * jnp.sin(li * dt * j)).reshape(steps, 1, -1)])
    pows = jnp.broadcast_to(pows, (2, steps, SUBLANES, S5_LANES))
    return w_b, w_c, pows


def _s5_mixer(proj, w_b, w_c, tabs, d_skip, w_glu, batch, seq):
    t = proj.shape[0]
    tb = min(S5_TB, seq)
    nt = seq // tb
    sl = tb // SUBLANES
    r = np.arange(tb)
    perm = np.zeros((tb, tb), np.float32)
    perm[r, (r % SUBLANES) * sl + r // SUBLANES] = 1.0
    perm_t = jnp.asarray(perm.T, BF16)
    perm = jnp.asarray(perm, BF16)
    return pl.pallas_call(
        functools.partial(_s5_kernel, tb=tb),
        grid=(batch, nt),
        in_specs=[
            pl.BlockSpec((tb, S5_WIDTH), lambda b, i: (b * nt + i, 0)),
            pl.BlockSpec((tb, tb), lambda b, i: (0, 0)),
            pl.BlockSpec((tb, tb), lambda b, i: (0, 0)),
            pl.BlockSpec(w_b.shape, lambda b, i: (0, 0, 0)),
            pl.BlockSpec(w_c.shape, lambda b, i: (0, 0, 0)),
            pl.BlockSpec(tabs.shape, lambda b, i: (0, 0, 0, 0)),
            pl.BlockSpec((1, S5_WIDTH), lambda b, i: (0, 0)),
            pl.BlockSpec((S5_WIDTH, S5_WIDTH), lambda b, i: (0, 0)),
        ],
        out_specs=pl.BlockSpec((tb, S5_WIDTH), lambda b, i: (b * nt + i, 0)),
        out_shape=jax.ShapeDtypeStruct((t, S5_WIDTH), BF16),
        scratch_shapes=[
            pltpu.VMEM((S5_LANES // LANES, tb, LANES), F32),
            pltpu.VMEM((S5_LANES // LANES, tb, LANES), F32),
            pltpu.VMEM((1, S5_LANES), F32),
            pltpu.VMEM((1, S5_LANES), F32),
            pltpu.VMEM((SUBLANES, S5_LANES), F32),
            pltpu.VMEM((SUBLANES, S5_LANES), F32),
        ],
        compiler_params=_params(("parallel", "arbitrary"), VMEM_LIMIT),
        name="s5_mixer",
    )(proj, perm, perm_t, w_b, w_c, tabs, d_skip.reshape(1, S5_WIDTH), w_glu)


def _hgrn_consts(c):
    t = np.arange(c)[:, None]
    u = np.arange(c)[None, :]
    mats = [u <= t, u > t]
    masks = [np.eye(c, dtype=bool)]
    h = 1
    while h < c:
        mid = (t // (2 * h)) * 2 * h + h
        mats.append(np.where(t >= mid, (u >= mid) & (u <= t), (u > t) & (u < mid)))
        masks.append(((u // (2 * h)) == (t // (2 * h))) & (t >= mid) & (u < mid))
        h *= 2
    return (np.concatenate(mats, axis=0).astype(np.float32), np.stack(masks).astype(np.float32))


def _hgrn_kernel(q_ref, f_ref, i_ref, g_ref, lb_ref, ng_ref, mall_ref, mask_ref, o_ref,
                 e_ref, st_ref, *, c):
    @pl.when(pl.program_id(1) == 0)
    def _():
        st_ref[...] = jnp.zeros(st_ref.shape, F32)

    lb = lb_ref[...]
    fl = f_ref[...]
    la = jnp.log(lb)
    lbb = jnp.log1p(-lb) + (jnp.minimum(fl, 0.0) - jnp.log1p(jnp.exp(-jnp.abs(fl))))
    logf = jnp.maximum(la, lbb) + jnp.log1p(jnp.exp(-jnp.abs(la - lbb)))
    hi = logf.astype(BF16)
    lo = (logf - hi.astype(F32)).astype(BF16)
    e2 = jnp.dot(mall_ref[...], jnp.concatenate([hi, lo], axis=1), preferred_element_type=F32)
    e_ref[...] = e2[:, :HGRN_WIDTH] + e2[:, HGRN_WIDTH:]

    n_levels = mask_ref.shape[0] - 1
    nt = (((1,), (1,)), ((), ()))
    for hd in range(HGRN_HEADS):
        ls = slice(hd * HGRN_DIM, (hd + 1) * HGRN_DIM)
        qr = q_ref[:, ls]
        q = qr * jax.nn.sigmoid(qr) * (HGRN_DIM ** -0.5)
        k = (1.0 - lb[:, ls]) * jax.nn.sigmoid(-fl[:, ls])
        vb = i_ref[:, ls].astype(BF16)
        p = lax.dot_general(q.astype(BF16), k.astype(BF16), nt, preferred_element_type=F32) * mask_ref[0]
        for lv in range(n_levels):
            fac = jnp.exp(e_ref[(2 + lv) * c:(3 + lv) * c, ls])
            s = lax.dot_general((q * fac).astype(BF16), (k * fac).astype(BF16), nt,
                                preferred_element_type=F32)
            p = p + s * mask_ref[lv + 1]
        o = jnp.dot(p.astype(BF16), vb, preferred_element_type=F32)
        gcum = e_ref[0:c, ls]
        st = st_ref[hd]
        o = o + lax.dot_general((q * jnp.exp(gcum)).astype(BF16), st.astype(BF16), nt,
                                preferred_element_type=F32)
        kd = (k * jnp.exp(e_ref[c:2 * c, ls])).astype(BF16)
        st_ref[hd] = st * jnp.exp(gcum[c - 1:c, :]) + lax.dot_general(
            vb, kd, (((0,), (0,)), ((), ())), preferred_element_type=F32)
        ms = jnp.mean(o * o, axis=-1, keepdims=True)
        o = o * lax.rsqrt(ms + EPS) * ng_ref[...] * jax.nn.sigmoid(g_ref[:, ls])
        o_ref[:, ls] = o.astype(o_ref.dtype)


def _hgrn_mixer(proj, lb, norm_g, batch, seq):
    t = proj.shape[0]
    c = min(HGRN_C, seq)
    nc = seq // c
    m_all, masks = _hgrn_consts(c)
    m_all = jnp.asarray(m_all, BF16)
    masks = jnp.asarray(masks, F32)

    def col(j):
        return pl.BlockSpec((c, HGRN_WIDTH), lambda b, i: (b * nc + i, j))

    return pl.pallas_call(
        functools.partial(_hgrn_kernel, c=c),
        grid=(batch, nc),
        in_specs=[
            col(1), col(2), col(3), col(4),
            pl.BlockSpec((1, HGRN_WIDTH), lambda b, i: (0, 0)),
            pl.BlockSpec((1, HGRN_DIM), lambda b, i: (0, 0)),
            pl.BlockSpec(m_all.shape, lambda b, i: (0, 0)),
            pl.BlockSpec(masks.shape, lambda b, i: (0, 0, 0)),
        ],
        out_specs=pl.BlockSpec((c, HGRN_WIDTH), lambda b, i: (b * nc + i, 0)),
        out_shape=jax.ShapeDtypeStruct((t, HGRN_WIDTH), BF16),
        scratch_shapes=[
            pltpu.VMEM((m_all.shape[0], HGRN_WIDTH), F32),
            pltpu.VMEM((HGRN_HEADS, HGRN_DIM, HGRN_DIM), F32),
        ],
        compiler_params=_params(("parallel", "arbitrary"), VMEM_LIMIT),
        name="hgrn2_mixer",
    )(proj, proj, proj, proj, lb.reshape(1, HGRN_WIDTH), norm_g.reshape(1, HGRN_DIM), m_all, masks)


def _out_proj_kernel(h_ref, a_ref, s_ref, r_ref, wa_ref, ws_ref, wr_ref, gn_ref, rh_ref, rl_ref,
                     ho_ref, hn_ref, eid_ref, gate_ref):
    acc = jnp.dot(a_ref[...], wa_ref[...], preferred_element_type=F32)
    acc = acc + jnp.dot(s_ref[...], ws_ref[...], preferred_element_type=F32)
    acc = acc + jnp.dot(r_ref[...], wr_ref[...], preferred_element_type=F32)
    h = h_ref[...] + acc
    ho_ref[...] = h
    ms = jnp.mean(h * h, axis=-1, keepdims=True)
    hn = h * lax.rsqrt(ms + EPS) * gn_ref[...]
    hn_ref[...] = hn

    hh = hn.astype(BF16)
    hl = (hn - hh.astype(F32)).astype(BF16)
    logits = (jnp.dot(hh, rh_ref[...], preferred_element_type=F32)
              + jnp.dot(hh, rl_ref[...], preferred_element_type=F32)
              + jnp.dot(hl, rh_ref[...], preferred_element_type=F32))
    lane = lax.broadcasted_iota(jnp.int32, logits.shape, 1).astype(F32)
    neg = -jnp.inf
    big = 1e9
    is_group = jnp.where(lane >= N_EXPERTS, jnp.where(lane < N_EXPERTS + N_GROUPS, 1.0, 0.0), 0.0)
    gl = jnp.where(is_group > 0, logits, neg)
    gmax = jnp.max(gl, axis=-1, keepdims=True)
    g_lane = jnp.min(jnp.where(gl == gmax, lane, big), axis=-1, keepdims=True)
    p_g = 1.0 / jnp.sum(jnp.exp(gl - gmax), axis=-1, keepdims=True)
    lo_lane = (g_lane - N_EXPERTS) * EXPERTS_PER_GROUP
    in_group = jnp.where(lane >= lo_lane, jnp.where(lane < lo_lane + EXPERTS_PER_GROUP, 1.0, 0.0), 0.0)
    el = jnp.where(in_group > 0, logits, neg)
    t1 = jnp.max(el, axis=-1, keepdims=True)
    i1 = jnp.min(jnp.where(el == t1, lane, big), axis=-1, keepdims=True)
    el2 = jnp.where(lane == i1, neg, el)
    t2 = jnp.max(el2, axis=-1, keepdims=True)
    i2 = jnp.min(jnp.where(el2 == t2, lane, big), axis=-1, keepdims=True)
    e21 = jnp.exp(t2 - t1)
    g1 = p_g / (1.0 + e21)
    g2 = p_g * e21 / (1.0 + e21)
    eid_ref[...] = jnp.where(lane == 0, i1, jnp.where(lane == 1, i2, 0.0)).astype(jnp.int32)
    gate_ref[...] = jnp.where(lane == 0, g1, jnp.where(lane == 1, g2, 0.0))


def _out_proj(h, attn, s5, hg, w_a, w_s, w_r, gn, r_hi, r_lo):
    t, d = h.shape
    tm = min(OUT_TM, t)

    def rows(w):
        return pl.BlockSpec((tm, w), lambda i: (i, 0))

    def full(a):
        return pl.BlockSpec(a.shape, lambda i: (0,) * a.ndim)

    return pl.pallas_call(
        _out_proj_kernel,
        grid=(t // tm,),
        in_specs=[rows(d), rows(ATTN_WIDTH), rows(S5_WIDTH), rows(HGRN_WIDTH),
                  full(w_a), full(w_s), full(w_r), pl.BlockSpec((1, d), lambda i: (0, 0)),
                  full(r_hi), full(r_lo)],
        out_specs=[rows(d), rows(d), rows(LANES), rows(LANES)],
        out_shape=[jax.ShapeDtypeStruct((t, d), F32), jax.ShapeDtypeStruct((t, d), F32),
                   jax.ShapeDtypeStruct((t, LANES), jnp.int32), jax.ShapeDtypeStruct((t, LANES), F32)],
        compiler_params=_params(("parallel",), VMEM_LIMIT),
        name="out_proj_router",
    )(h, attn, s5, hg, w_a, w_s, w_r, gn.reshape(1, d), r_hi, r_lo)


def _moe_plan(eid, tm):
    flat = eid.reshape(-1)
    n_slots = flat.shape[0]
    onehot = (flat[:, None] == jnp.arange(N_EXPERTS, dtype=jnp.int32)[None, :]).astype(jnp.int32)
    csum = jnp.cumsum(onehot, axis=0)
    rank = jnp.sum(onehot * csum, axis=1) - 1
    counts = csum[-1]
    tiles = (counts + tm - 1) // tm
    tile_end = jnp.cumsum(tiles)
    tile_start = tile_end - tiles
    slot_start = jnp.cumsum(counts) - counts
    pos = jnp.sum(onehot * tile_start[None, :], axis=1) * tm + rank
    tok_sorted = jnp.argsort(flat, stable=True).astype(jnp.int32) // TOP_K
    tok_sorted = jnp.pad(tok_sorted, (0, tm))
    nt_max = (n_slots + N_EXPERTS * (tm - 1)) // tm
    n_used = tile_end[-1]
    j = jnp.minimum(jnp.arange(nt_max, dtype=jnp.int32), n_used - 1)
    tile_expert = jnp.sum((j[:, None] >= tile_end[None, :]).astype(jnp.int32), axis=1)
    tile_slot0 = slot_start[tile_expert] + (j - tile_start[tile_expert]) * tm
    return (pos.astype(jnp.int32), tok_sorted, tile_expert.astype(jnp.int32),
            tile_slot0.astype(jnp.int32), n_used.reshape(1).astype(jnp.int32), nt_max)


def _expert_kernel(tok_ref, te_ref, s0_ref, nu_ref, hn_ref, wg_ref, wu_ref, wd_ref, y_ref,
                   xa_ref, xb_ref, wgb_ref, wub_ref, wdb_ref, sem, *, tm):
    j = pl.program_id(0)
    n_used = nu_ref[0]
    bufs = (xa_ref, xb_ref)

    def row_copy(buf, r, tok):
        return pltpu.make_async_copy(hn_ref.at[pl.ds(tok, 1)], bufs[buf].at[pl.ds(r, 1)], sem.at[buf])

    def wait_gather(buf):
        def wait(r, carry):
            row_copy(buf, r, 0).wait()
            return carry

        lax.fori_loop(0, tm, wait, 0, unroll=DMA_UNROLL)

    @pl.when(j == 0)
    def _():
        base = s0_ref[0]

        def issue(r, carry):
            row_copy(0, r, tok_ref[base + r]).start()
            return carry

        lax.fori_loop(0, tm, issue, 0, unroll=DMA_UNROLL)

    def tile(buf):
        wait_gather(buf)
        prev = te_ref[jnp.maximum(j - 1, 0)]

        @pl.when(jnp.logical_or(j == 0, te_ref[j] != prev))
        def _():
            wgb_ref[...] = wg_ref[0, 0].astype(BF16)
            wub_ref[...] = wu_ref[0, 0].astype(BF16)
            wdb_ref[...] = wd_ref[0, 0].astype(BF16)

        nbase = s0_ref[jnp.minimum(j + 1, n_used - 1)]
        for r in range(tm):
            row_copy(1 - buf, r, tok_ref[nbase + r]).start()

        x = bufs[buf][...].astype(BF16)
        g = jnp.dot(x, wgb_ref[...], preferred_element_type=F32)
        u = jnp.dot(x, wub_ref[...], preferred_element_type=F32)
        hmid = (g * jax.nn.sigmoid(g) * u).astype(BF16)
        y_ref[...] = jnp.dot(hmid, wdb_ref[...], preferred_element_type=F32)

        @pl.when(j == n_used - 1)
        def _():
            wait_gather(1 - buf)

    for parity in range(2):
        @pl.when(jnp.logical_and(j < n_used, lax.rem(j, 2) == parity))
        def _(parity=parity):
            tile(parity)

    @pl.when(j >= n_used)
    def _():
        y_ref[...] = jnp.zeros(y_ref.shape, F32)


def _moe_experts(hn, tok_sorted, tile_expert, tile_slot0, n_used, w_gate, w_up, w_down, layer, tm, nt_max):
    t, d = hn.shape
    de = w_gate.shape[3]
    grid_spec = pltpu.PrefetchScalarGridSpec(
        num_scalar_prefetch=4,
        grid=(nt_max,),
        in_specs=[
            pl.BlockSpec(memory_space=pl.ANY),
            pl.BlockSpec((1, 1, d, de), lambda j, tok, te, s0, nu: (layer, te[j], 0, 0)),
            pl.BlockSpec((1, 1, d, de), lambda j, tok, te, s0, nu: (layer, te[j], 0, 0)),
            pl.BlockSpec((1, 1, de, d), lambda j, tok, te, s0, nu: (layer, te[j], 0, 0)),
        ],
        out_specs=pl.BlockSpec((tm, d), lambda j, tok, te, s0, nu: (j, 0)),
        scratch_shapes=[pltpu.VMEM((tm, d), F32), pltpu.VMEM((tm, d), F32),
                        pltpu.VMEM((d, de), BF16), pltpu.VMEM((d, de), BF16), pltpu.VMEM((de, d), BF16),
                        pltpu.SemaphoreType.DMA((2,))],
    )
    return pl.pallas_call(
        functools.partial(_expert_kernel, tm=tm),
        grid_spec=grid_spec,
        out_shape=jax.ShapeDtypeStruct((nt_max * tm, d), F32),
        compiler_params=_params(("arbitrary",), VMEM_LIMIT),
        name="moe_experts",
    )(tok_sorted, tile_expert, tile_slot0, n_used, hn, w_gate, w_up, w_down)


def _combine_kernel(pos_ref, h_ref, gate_ref, gn_ref, y_ref, o_ref, buf_ref, sem, *, tm, final_norm):
    i = pl.program_id(0)
    cur = lax.rem(i, 2)

    def row_copy(b, k, r, src_row):
        return pltpu.make_async_copy(y_ref.at[pl.ds(src_row, 1)], buf_ref.at[b, k, pl.ds(r, 1)], sem.at[b])

    def start_gather(tile):
        b = lax.rem(tile, 2)

        def issue(r, carry):
            slot = (tile * tm + r) * TOP_K
            for k in range(TOP_K):
                row_copy(b, k, r, pos_ref[slot + k]).start()
            return carry

        lax.fori_loop(0, tm, issue, 0, unroll=DMA_UNROLL)

    @pl.when(i == 0)
    def _():
        start_gather(0)

    @pl.when(i + 1 < pl.num_programs(0))
    def _():
        start_gather(i + 1)

    def wait(r, carry):
        for k in range(TOP_K):
            row_copy(cur, k, r, 0).wait()
        return carry

    lax.fori_loop(0, tm, wait, 0, unroll=DMA_UNROLL)

    gates = gate_ref[...]
    h = h_ref[...] + gates[:, 0:1] * buf_ref[cur, 0] + gates[:, 1:2] * buf_ref[cur, 1]
    if final_norm:
        ms = jnp.mean(h * h, axis=-1, keepdims=True)
        h = h * lax.rsqrt(ms + EPS) * gn_ref[...]
    o_ref[...] = h


def _moe_combine(h, gates, ys, pos, final_g, final_norm):
    t, d = h.shape
    tm = min(COMB_TM, t)
    grid_spec = pltpu.PrefetchScalarGridSpec(
        num_scalar_prefetch=1,
        grid=(t // tm,),
        in_specs=[
            pl.BlockSpec((tm, d), lambda i, p: (i, 0)),
            pl.BlockSpec((tm, LANES), lambda i, p: (i, 0)),
            pl.BlockSpec((1, d), lambda i, p: (0, 0)),
            pl.BlockSpec(memory_space=pl.ANY),
        ],
        out_specs=pl.BlockSpec((tm, d), lambda i, p: (i, 0)),
        scratch_shapes=[pltpu.VMEM((2, TOP_K, tm, d), F32), pltpu.SemaphoreType.DMA((2,))],
    )
    return pl.pallas_call(
        functools.partial(_combine_kernel, tm=tm, final_norm=final_norm),
        grid_spec=grid_spec,
        out_shape=jax.ShapeDtypeStruct((t, d), F32),
        compiler_params=_params(("arbitrary",), VMEM_LIMIT),
        name="moe_combine",
    )(pos, h, gates, final_g.reshape(1, d), ys)


def _permute_w_in(w):
    a = 3 * ATTN_WIDTH
    w = jnp.concatenate([w[:, a:], w[:, :a]], axis=1)
    return jnp.pad(w, ((0, 0), (0, IN_COLS_PAD - IN_COLS))).astype(BF16)


def kernel(x, w_in, w_out, mix_norm_g, ffn_norm_g, rel_bias, diff_lambda, attn_subln_g, s5_a_re, s5_a_im, s5_log_dt, s5_b_re, s5_b_im, s5_c_re, s5_c_im, s5_d, s5_w_glu, hgrn_lb_logits, hgrn_norm_g, moe_w_group, moe_w_router, moe_w_gate, moe_w_up, moe_w_down, final_norm_g):
    batch, seq, d = x.shape
    depth = w_in.shape[0]
    t = batch * seq
    h = x.reshape(t, d)

    lb_cum = jnp.cumsum(jax.nn.softmax(hgrn_lb_logits.astype(F32), axis=0), axis=0)
    lb_all = lb_cum - lb_cum[0:1]
    attn_bias = _attn_bias_tiles(rel_bias, min(ATTN_TQ, seq), min(ATTN_TK, seq))

    for l in range(depth):
        proj = _in_proj(h, mix_norm_g[l], _permute_w_in(w_in[l]))

        lam_init = 0.8 - 0.6 * math.exp(-0.3 * l)
        lv = diff_lambda[l].astype(F32)
        lam = jnp.exp(jnp.sum(lv[0] * lv[1])) - jnp.exp(jnp.sum(lv[2] * lv[3])) + lam_init
        attn = _diff_attention(proj, attn_bias, lam, attn_subln_g[l], lam_init, batch, seq)

        w_b, w_c, tabs = _s5_prep(s5_a_re[l], s5_a_im[l], s5_log_dt[l], s5_b_re[l], s5_b_im[l],
                                  s5_c_re[l], s5_c_im[l], min(S5_TB, seq) // SUBLANES)
        s5 = _s5_mixer(proj, w_b, w_c, tabs, s5_d[l], s5_w_glu[l].astype(BF16), batch, seq)

        hg = _hgrn_mixer(proj, lb_all[l], hgrn_norm_g[l], batch, seq)

        wo = w_out[l].astype(BF16)
        w_r = jnp.concatenate([moe_w_router[l], moe_w_group[l]], axis=1).astype(F32)
        w_r = jnp.pad(w_r, ((0, 0), (0, LANES - w_r.shape[1])))
        r_hi = w_r.astype(BF16)
        r_lo = (w_r - r_hi.astype(F32)).astype(BF16)
        h, hn, eid, gates = _out_proj(
            h, attn, s5, hg, wo[:ATTN_WIDTH], wo[ATTN_WIDTH:ATTN_WIDTH + S5_WIDTH],
            wo[ATTN_WIDTH + S5_WIDTH:], ffn_norm_g[l], r_hi, r_lo)

        pos, tok_sorted, tile_expert, tile_slot0, n_used, nt_max = _moe_plan(eid[:, :TOP_K], MOE_TM)
        ys = _moe_experts(hn, tok_sorted, tile_expert, tile_slot0, n_used,
                          moe_w_gate, moe_w_up, moe_w_down, l, MOE_TM, nt_max)
        h = _moe_combine(h, gates, ys, pos, final_norm_g, final_norm=(l == depth - 1))

    return h.reshape(batch, seq, d)
```

```python
import functools
import math

import numpy as np
import jax
import jax.numpy as jnp
from jax import lax
from jax.experimental import pallas as pl
from jax.experimental.pallas import tpu as pltpu

F32 = jnp.float32
BF16 = jnp.bfloat16
EPS = 1e-6
LOG2E = 1.4426950408889634

D_MODEL = 2048
ATTN_HEADS = 6
ATTN_QK_DIM = 64
ATTN_V_DIM = 128
ATTN_WIDTH = ATTN_HEADS * ATTN_V_DIM
REL_BUCKETS = 32
REL_MAX_DIST = 128
S5_GROUPS = 40
S5_GROUP_CH = 16
S5_STATE = 64
S5_WIDTH = S5_GROUPS * S5_GROUP_CH
S5_LANES = S5_GROUPS * S5_STATE
HGRN_HEADS = 5
HGRN_DIM = 128
HGRN_WIDTH = HGRN_HEADS * HGRN_DIM
N_GROUPS = 4
EXPERTS_PER_GROUP = 8
N_EXPERTS = N_GROUPS * EXPERTS_PER_GROUP
TOP_K = 2
D_EXPERT = 512

LANES = 128
SUBLANES = 8

IN_COLS = 3 * ATTN_WIDTH + S5_WIDTH + 4 * HGRN_WIDTH
PROJ_TN = 512
IN_COLS_PAD = -(-IN_COLS // PROJ_TN) * PROJ_TN
ATTN_COL_BLOCK = (S5_WIDTH + 4 * HGRN_WIDTH) // LANES

PROJ_TM = 1024
ATTN_TQ = 512
ATTN_TK = 512
ATTN_NEAR = 2
ATTN_ROWS = 32
S5_TB = 256
S5_CHUNK = 640
S5_BLK = 128
HGRN_C = 128
OUT_TM = 256
MOE_TM = 256
COMB_TM = 128
DMA_UNROLL = 8

VMEM_LIMIT = 56 * 1024 * 1024


def _params(sem, vmem=None):
    return pltpu.CompilerParams(dimension_semantics=sem, vmem_limit_bytes=vmem)


def _in_proj_kernel(x_ref, g_ref, w_ref, o_ref, xn_ref):
    @pl.when(pl.program_id(1) == 0)
    def _():
        x = x_ref[...]
        ms = jnp.mean(x * x, axis=-1, keepdims=True)
        xn_ref[...] = (x * lax.rsqrt(ms + EPS) * g_ref[...]).astype(BF16)

    o_ref[...] = jnp.dot(xn_ref[...], w_ref[...], preferred_element_type=F32)


def _in_proj(h, g, w):
    t, d = h.shape
    n = w.shape[1]
    tm = min(PROJ_TM, t)
    return pl.pallas_call(
        _in_proj_kernel,
        grid=(t // tm, n // PROJ_TN),
        in_specs=[
            pl.BlockSpec((tm, d), lambda i, j: (i, 0)),
            pl.BlockSpec((1, d), lambda i, j: (0, 0)),
            pl.BlockSpec((d, PROJ_TN), lambda i, j: (0, j)),
        ],
        out_specs=pl.BlockSpec((tm, PROJ_TN), lambda i, j: (i, j)),
        out_shape=jax.ShapeDtypeStruct((t, n), F32),
        scratch_shapes=[pltpu.VMEM((tm, d), BF16)],
        compiler_params=_params(("parallel", "arbitrary"), VMEM_LIMIT),
        name="in_proj",
    )(h, g.reshape(1, d), w)


def _attn_kernel(lam_ref, q_ref, k_ref, v_ref, bias_ref, g_ref, o_ref,
                 kb_ref, vb_ref, qz_ref, m_ref, a_ref, acc_ref, s0_ref, s1_ref, p_ref, *, tq, tk, out_scale):
    qi = pl.program_id(2)
    ratio = tq // tk

    @pl.when(qi == 0)
    def _():
        kb_ref[...] = k_ref[...].astype(BF16)
        vb_ref[:, 0:LANES] = v_ref[...].astype(BF16)
        vb_ref[:, LANES:2 * LANES] = jnp.ones((vb_ref.shape[0], LANES), BF16)

    q = q_ref[...] * (ATTN_QK_DIM ** -0.5 * LOG2E)
    lane = lax.broadcasted_iota(jnp.int32, q.shape, 1)
    qz_ref[0:tq, :] = jnp.where(lane < ATTN_QK_DIM, q, 0.0).astype(BF16)
    qz_ref[tq:2 * tq, :] = jnp.where(lane >= ATTN_QK_DIM, q, 0.0).astype(BF16)
    m_ref[...] = jnp.full(m_ref.shape, -jnp.inf, F32)
    acc_ref[...] = jnp.zeros(acc_ref.shape, F32)

    n_col = tk // LANES
    groups = [slice(g * ATTN_ROWS, (g + 1) * ATTN_ROWS) for g in range(2 * tq // ATTN_ROWS)]

    def scores(j, s_ref):
        k = kb_ref[pl.ds(pl.multiple_of(j * tk, tk), tk), :]
        s_ref[...] = lax.dot_general(qz_ref[...], k, (((1,), (1,)), ((), ())), preferred_element_type=F32)

    def softmax_pv(j, s_ref, biased):
        if biased:
            kind = jnp.clip(j - ratio * qi + ATTN_NEAR, 0, ATTN_NEAR + ratio - 1)
        for rows in groups:
            cols = [s_ref[rows, i * LANES:(i + 1) * LANES] for i in range(n_col)]
            if biased:
                b0 = rows.start % tq
                cols = [x + bias_ref[0, kind, b0:b0 + ATTN_ROWS, i * LANES:(i + 1) * LANES]
                        for i, x in enumerate(cols)]
                for i, x in enumerate(cols):
                    s_ref[rows, i * LANES:(i + 1) * LANES] = x
            mx = functools.reduce(jnp.maximum, cols)
            m_old = m_ref[rows, :]
            m_new = jnp.maximum(m_old, jnp.max(mx, axis=-1, keepdims=True))
            m_ref[rows, :] = m_new
            a_ref[rows, :] = jnp.exp2(m_old - m_new)
        for rows in groups:
            m_new = m_ref[rows, :]
            for i in range(n_col):
                p_ref[rows, i * LANES:(i + 1) * LANES] = jnp.exp2(
                    s_ref[rows, i * LANES:(i + 1) * LANES] - m_new).astype(BF16)
        v = vb_ref[pl.ds(pl.multiple_of(j * tk, tk), tk), :]
        alpha = a_ref[...]
        acc_ref[...] = (jnp.concatenate([alpha, alpha], axis=1) * acc_ref[...]
                        + jnp.dot(p_ref[...], v, preferred_element_type=F32))

    n_tiles = ratio * (qi + 1)
    first = lax.rem(n_tiles, 2)
    n_far = jnp.maximum(ratio * qi - ATTN_NEAR + 1, 0)
    n_far_pairs = jnp.maximum(n_far - first, 0) // 2
    n_pairs = (n_tiles - first) // 2

    @pl.when(first == 1)
    def _():
        scores(0, s1_ref)
        scores(jnp.minimum(1, n_tiles - 1), s0_ref)
        softmax_pv(0, s1_ref, True)

    @pl.when(first == 0)
    def _():
        scores(0, s0_ref)

    def pair_body(jj, carry, biased):
        j = first + 2 * jj
        scores(j + 1, s1_ref)
        softmax_pv(j, s0_ref, biased)
        scores(jnp.minimum(j + 2, n_tiles - 1), s0_ref)
        softmax_pv(j + 1, s1_ref, biased)
        return carry

    lax.fori_loop(0, n_far_pairs, functools.partial(pair_body, biased=False), 0)
    lax.fori_loop(n_far_pairs, n_pairs, functools.partial(pair_body, biased=True), 0)

    inv_l = 1.0 / acc_ref[:, LANES:2 * LANES]
    o1 = acc_ref[0:tq, 0:LANES] * inv_l[0:tq]
    o2 = acc_ref[tq:2 * tq, 0:LANES] * inv_l[tq:2 * tq]
    o = o1 - lam_ref[0] * o2
    ms = jnp.mean(o * o, axis=-1, keepdims=True)
    o_ref[...] = (o * lax.rsqrt(ms + EPS) * g_ref[...] * out_scale).astype(o_ref.dtype)


def _t5_bucket(dist):
    n = jnp.maximum(dist, 0)
    max_exact = REL_BUCKETS // 2
    large = max_exact + (jnp.log(jnp.maximum(n, 1).astype(F32) / max_exact)
                         / math.log(REL_MAX_DIST / max_exact) * (REL_BUCKETS - max_exact)).astype(jnp.int32)
    large = jnp.minimum(large, REL_BUCKETS - 1)
    return jnp.where(n < max_exact, n, large)


def _attn_bias_tiles(rel_bias, tq, tk):
    table = rel_bias.astype(F32)
    far = table[REL_BUCKETS - 1]

    def lookup(dist):
        bucket = _t5_bucket(dist)
        out = jnp.zeros((ATTN_HEADS,) + dist.shape, F32)
        for b in range(REL_BUCKETS):
            out = jnp.where((bucket == b)[None], table[b][:, None], out)
        return out

    n = tq + tk
    tiles = [jnp.zeros((ATTN_HEADS, tq, tk), F32)]
    for rel in range(-ATTN_NEAR + 1, tq // tk):
        dist = jnp.arange(n) - tk - rel * tk
        vec = jnp.where((dist >= 0)[None], (lookup(dist) - far[:, None]) * LOG2E, -1e30)
        skew = jnp.tile(vec[:, ::-1], (1, tq))[:, :tq * (n - 1)].reshape(ATTN_HEADS, tq, n - 1)
        tiles.append(skew[:, :, tq - 1:tq - 1 + tk])
    return jnp.stack(tiles, axis=1)


def _diff_attention(proj, bias, lam, subln_g, lam_init, batch, seq):
    t = proj.shape[0]
    tq = min(ATTN_TQ, seq)
    tk = min(ATTN_TK, seq)
    ratio = tq // tk
    assert (ATTN_NEAR - 1) * tk >= REL_MAX_DIST and ratio * tk == tq
    nq = seq // tq
    kernel = functools.partial(_attn_kernel, tq=tq, tk=tk, out_scale=1.0 - lam_init)
    qb, kb, vb = ATTN_COL_BLOCK, ATTN_COL_BLOCK + ATTN_HEADS, ATTN_COL_BLOCK + 2 * ATTN_HEADS
    return pl.pallas_call(
        kernel,
        grid=(batch, ATTN_HEADS, nq),
        in_specs=[
            pl.BlockSpec(memory_space=pltpu.SMEM),
            pl.BlockSpec((tq, LANES), lambda b, h, i: (b * nq + i, qb + h)),
            pl.BlockSpec((seq, LANES), lambda b, h, i: (b, kb + h)),
            pl.BlockSpec((seq, LANES), lambda b, h, i: (b, vb + h)),
            pl.BlockSpec((1,) + bias.shape[1:], lambda b, h, i: (h, 0, 0, 0)),
            pl.BlockSpec((1, LANES), lambda b, h, i: (0, 0)),
        ],
        out_specs=pl.BlockSpec((tq, LANES), lambda b, h, i: (b * nq + i, h)),
        out_shape=jax.ShapeDtypeStruct((t, ATTN_WIDTH), BF16),
        scratch_shapes=[
            pltpu.VMEM((seq, LANES), BF16),
            pltpu.VMEM((seq, 2 * LANES), BF16),
            pltpu.VMEM((2 * tq, LANES), BF16),
            pltpu.VMEM((2 * tq, LANES), F32),
            pltpu.VMEM((2 * tq, LANES), F32),
            pltpu.VMEM((2 * tq, 2 * LANES), F32),
            pltpu.VMEM((2 * tq, tk), F32),
            pltpu.VMEM((2 * tq, tk), F32),
            pltpu.VMEM((2 * tq, tk), BF16),
        ],
        compiler_params=_params(("parallel", "parallel", "arbitrary"), VMEM_LIMIT),
        name="diff_attn",
    )(lam.reshape(1), proj, proj, proj, bias, subln_g.reshape(1, LANES))


def _s5_kernel(u_ref, perm_ref, permt_ref, wb_ref, wc_ref, pw_ref, d_ref, wglu_ref, o_ref,
               xr_ref, xi_ref, cr_ref, ci_ref, kr_ref, ki_ref, *, tb):
    @pl.when(pl.program_id(1) == 0)
    def _():
        cr_ref[...] = jnp.zeros(cr_ref.shape, F32)
        ci_ref[...] = jnp.zeros(ci_ref.shape, F32)

    un = u_ref[...]
    uh = un.astype(BF16)
    ul = (un - uh.astype(F32)).astype(BF16)
    up = jnp.dot(perm_ref[...], jnp.concatenate([uh, ul], axis=1), preferred_element_type=F32)
    ub = up[:, :S5_WIDTH].astype(BF16)
    u = up[:, :S5_WIDTH] + up[:, S5_WIDTH:]
    nblk = S5_WIDTH // S5_BLK
    sw = S5_LANES // nblk
    tpb = sw // LANES
    for i in range(nblk):
        bu = jnp.dot(ub[:, i * S5_BLK:(i + 1) * S5_BLK], wb_ref[i], preferred_element_type=F32)
        for k in range(tpb):
            xr_ref[i * tpb + k] = bu[:, k * LANES:(k + 1) * LANES]
            xi_ref[i * tpb + k] = bu[:, sw + k * LANES:sw + (k + 1) * LANES]

    sl = tb // SUBLANES
    for c in range(S5_LANES // S5_CHUNK):
        tiles = range(c * S5_CHUNK // LANES, (c + 1) * S5_CHUNK // LANES)
        lss = [slice(n * LANES, (n + 1) * LANES) for n in tiles]
        a = [(pw_ref[0, 0, :, ls], pw_ref[1, 0, :, ls]) for ls in lss]

        def local_step(t, carry, tiles=tiles, a=a):
            rows = pl.ds(pl.multiple_of(t * SUBLANES, SUBLANES), SUBLANES)
            out = []
            for n, (ar, ai), (pr, pi) in zip(tiles, a, carry):
                xr = ar * pr - ai * pi + xr_ref[n, rows, :]
                xi = ar * pi + ai * pr + xi_ref[n, rows, :]
                xr_ref[n, rows, :] = xr
                xi_ref[n, rows, :] = xi
                out.append((xr, xi))
            return tuple(out)

        zero = jnp.zeros((SUBLANES, LANES), F32)
        ends = lax.fori_loop(0, sl, local_step, tuple((zero, zero) for _ in tiles), unroll=2)

        for ls, (er, ei) in zip(lss, ends):
            lr, li = pw_ref[0, sl - 1, 0:1, ls], pw_ref[1, sl - 1, 0:1, ls]
            kr, ki = cr_ref[:, ls], ci_ref[:, ls]
            for sub in range(SUBLANES):
                kr_ref[sub:sub + 1, ls] = kr
                ki_ref[sub:sub + 1, ls] = ki
                kr, ki = (er[sub:sub + 1, :] + lr * kr - li * ki, ei[sub:sub + 1, :] + lr * ki + li * kr)
            cr_ref[:, ls] = kr
            ci_ref[:, ls] = ki

        k8 = [(kr_ref[:, ls], ki_ref[:, ls]) for ls in lss]

        def fix_step(t, carry, tiles=tiles, lss=lss, k8=k8):
            rows = pl.ds(pl.multiple_of(t * SUBLANES, SUBLANES), SUBLANES)
            for n, ls, (kr8, ki8) in zip(tiles, lss, k8):
                pr = pw_ref[0, t, :, ls]
                pi = pw_ref[1, t, :, ls]
                xr_ref[n, rows, :] = xr_ref[n, rows, :] + pr * kr8 - pi * ki8
                xi_ref[n, rows, :] = xi_ref[n, rows, :] + pr * ki8 + pi * kr8
            return carry

        lax.fori_loop(0, sl, fix_step, 0, unroll=2)

    ys = []
    for i in range(nblk):
        xc = jnp.concatenate([xr_ref[i * tpb + k] for k in range(tpb)]
                             + [xi_ref[i * tpb + k] for k in range(tpb)], axis=1)
        ys.append(jnp.dot(xc.astype(BF16), wc_ref[i], preferred_element_type=F32))
    y = jnp.concatenate(ys, axis=1) + d_ref[...] * u
    z = 0.5 * y * (1.0 + jnp.tanh(math.sqrt(2.0 / math.pi) * (y + 0.044715 * (y * y * y))))
    gate = jax.nn.sigmoid(jnp.dot(z.astype(BF16), wglu_ref[...], preferred_element_type=F32))
    o_ref[...] = jnp.dot(permt_ref[...], (z * gate).astype(BF16),
                         preferred_element_type=F32).astype(o_ref.dtype)


def _s5_prep(a_re, a_im, log_dt, b_re, b_im, c_re, c_im, steps):
    lr = jnp.minimum(a_re.astype(F32), -1e-4)
    li = a_im.astype(F32)
    dt = jnp.exp(log_dt.astype(F32))[:, None]
    mag = jnp.exp(lr * dt)
    ab_re = mag * jnp.cos(li * dt)
    ab_im = mag * jnp.sin(li * dt)
    den = lr * lr + li * li
    nr = ab_re - 1.0
    ni = ab_im
    f_re = (nr * lr + ni * li) / den
    f_im = (ni * lr - nr * li) / den
    br = b_re.astype(F32)
    bi = b_im.astype(F32)
    bb_re = f_re[..., None] * br - f_im[..., None] * bi
    bb_im = f_re[..., None] * bi + f_im[..., None] * br

    nblk = S5_WIDTH // S5_BLK
    gpb = S5_GROUPS // nblk
    eye = jnp.eye(gpb, dtype=F32)

    def bmat(bb):
        bb = bb.reshape(nblk, gpb, S5_STATE, S5_GROUP_CH)
        m = jnp.einsum('ignc,gh->igchn', bb, eye)
        return m.reshape(nblk, gpb * S5_GROUP_CH, gpb * S5_STATE)

    def cmat(cc):
        cc = cc.astype(F32).reshape(nblk, gpb, S5_GROUP_CH, S5_STATE)
        m = jnp.einsum('igcn,gh->ignhc', cc, eye)
        return m.reshape(nblk, gpb * S5_STATE, gpb * S5_GROUP_CH)

    w_b = jnp.concatenate([bmat(bb_re), bmat(bb_im)], axis=2).astype(BF16)
    w_c = jnp.concatenate([cmat(c_re), -cmat(c_im)], axis=1).astype(BF16)

    j = jnp.arange(1, steps + 1, dtype=F32)[:, None, None]
    m = jnp.exp(lr * dt * j)
    pows = jnp.stack([(m * jnp.cos(li * dt * j)).reshape(steps, 1, -1),
                      (m * jnp.sin(li * dt * j)).reshape(steps, 1, -1)])
    pows = jnp.broadcast_to(pows, (2, steps, SUBLANES, S5_LANES))
    return w_b, w_c, pows


def _s5_mixer(proj, w_b, w_c, tabs, d_skip, w_glu, batch, seq):
    t = proj.shape[0]
    tb = min(S5_TB, seq)
    nt = seq // tb
    sl = tb // SUBLANES
    r = np.arange(tb)
    perm = np.zeros((tb, tb), np.float32)
    perm[r, (r % SUBLANES) * sl + r // SUBLANES] = 1.0
    perm_t = jnp.asarray(perm.T, BF16)
    perm = jnp.asarray(perm, BF16)
    return pl.pallas_call(
        functools.partial(_s5_kernel, tb=tb),
        grid=(batch, nt),
        in_specs=[
            pl.BlockSpec((tb, S5_WIDTH), lambda b, i: (b * nt + i, 0)),
            pl.BlockSpec((tb, tb), lambda b, i: (0, 0)),
            pl.BlockSpec((tb, tb), lambda b, i: (0, 0)),
            pl.BlockSpec(w_b.shape, lambda b, i: (0, 0, 0)),
            pl.BlockSpec(w_c.shape, lambda b, i: (0, 0, 0)),
            pl.BlockSpec(tabs.shape, lambda b, i: (0, 0, 0, 0)),
            pl.BlockSpec((1, S5_WIDTH), lambda b, i: (0, 0)),
            pl.BlockSpec((S5_WIDTH, S5_WIDTH), lambda b, i: (0, 0)),
        ],
        out_specs=pl.BlockSpec((tb, S5_WIDTH), lambda b, i: (b * nt + i, 0)),
        out_shape=jax.ShapeDtypeStruct((t, S5_WIDTH), BF16),
        scratch_shapes=[
            pltpu.VMEM((S5_LANES // LANES, tb, LANES), F32),
            pltpu.VMEM((S5_LANES // LANES, tb, LANES), F32),
            pltpu.VMEM((1, S5_LANES), F32),
            pltpu.VMEM((1, S5_LANES), F32),
            pltpu.VMEM((SUBLANES, S5_LANES), F32),
            pltpu.VMEM((SUBLANES, S5_LANES), F32),
        ],
        compiler_params=_params(("parallel", "arbitrary"), VMEM_LIMIT),
        name="s5_mixer",
    )(proj, perm, perm_t, w_b, w_c, tabs, d_skip.reshape(1, S5_WIDTH), w_glu)


def _hgrn_consts(c):
    t = np.arange(c)[:, None]
    u = np.arange(c)[None, :]
    mats = [u <= t, u > t]
    masks = [np.eye(c, dtype=bool)]
    h = 1
    while h < c:
        mid = (t // (2 * h)) * 2 * h + h
        mats.append(np.where(t >= mid, (u >= mid) & (u <= t), (u > t) & (u < mid)))
        masks.append(((u // (2 * h)) == (t // (2 * h))) & (t >= mid) & (u < mid))
        h *= 2
    return (np.concatenate(mats, axis=0).astype(np.float32), np.stack(masks).astype(np.float32))


def _hgrn_kernel(q_ref, f_ref, i_ref, g_ref, lb_ref, ng_ref, mall_ref, mask_ref, o_ref,
                 e_ref, st_ref, *, c):
    @pl.when(pl.program_id(1) == 0)
    def _():
        st_ref[...] = jnp.zeros(st_ref.shape, F32)

    lb = lb_ref[...]
    fl = f_ref[...]
    la = jnp.log(lb)
    lbb = jnp.log1p(-lb) + (jnp.minimum(fl, 0.0) - jnp.log1p(jnp.exp(-jnp.abs(fl))))
    logf = jnp.maximum(la, lbb) + jnp.log1p(jnp.exp(-jnp.abs(la - lbb)))
    hi = logf.astype(BF16)
    lo = (logf - hi.astype(F32)).astype(BF16)
    e2 = jnp.dot(mall_ref[...], jnp.concatenate([hi, lo], axis=1), preferred_element_type=F32)
    e_ref[...] = e2[:, :HGRN_WIDTH] + e2[:, HGRN_WIDTH:]

    n_levels = mask_ref.shape[0] - 1
    nt = (((1,), (1,)), ((), ()))
    for hd in range(HGRN_HEADS):
        ls = slice(hd * HGRN_DIM, (hd + 1) * HGRN_DIM)
        qr = q_ref[:, ls]
        q = qr * jax.nn.sigmoid(qr) * (HGRN_DIM ** -0.5)
        k = (1.0 - lb[:, ls]) * jax.nn.sigmoid(-fl[:, ls])
        vb = i_ref[:, ls].astype(BF16)
        p = lax.dot_general(q.astype(BF16), k.astype(BF16), nt, preferred_element_type=F32) * mask_ref[0]
        for lv in range(n_levels):
            fac = jnp.exp(e_ref[(2 + lv) * c:(3 + lv) * c, ls])
            s = lax.dot_general((q * fac).astype(BF16), (k * fac).astype(BF16), nt,
                                preferred_element_type=F32)
            p = p + s * mask_ref[lv + 1]
        o = jnp.dot(p.astype(BF16), vb, preferred_element_type=F32)
        gcum = e_ref[0:c, ls]
        st = st_ref[hd]
        o = o + lax.dot_general((q * jnp.exp(gcum)).astype(BF16), st.astype(BF16), nt,
                                preferred_element_type=F32)
        kd = (k * jnp.exp(e_ref[c:2 * c, ls])).astype(BF16)
        st_ref[hd] = st * jnp.exp(gcum[c - 1:c, :]) + lax.dot_general(
            vb, kd, (((0,), (0,)), ((), ())), preferred_element_type=F32)
        ms = jnp.mean(o * o, axis=-1, keepdims=True)
        o = o * lax.rsqrt(ms + EPS) * ng_ref[...] * jax.nn.sigmoid(g_ref[:, ls])
        o_ref[:, ls] = o.astype(o_ref.dtype)


def _hgrn_mixer(proj, lb, norm_g, batch, seq):
    t = proj.shape[0]
    c = min(HGRN_C, seq)
    nc = seq // c
    m_all, masks = _hgrn_consts(c)
    m_all = jnp.asarray(m_all, BF16)
    masks = jnp.asarray(masks, F32)

    def col(j):
        return pl.BlockSpec((c, HGRN_WIDTH), lambda b, i: (b * nc + i, j))

    return pl.pallas_call(
        functools.partial(_hgrn_kernel, c=c),
        grid=(batch, nc),
        in_specs=[
            col(1), col(2), col(3), col(4),
            pl.BlockSpec((1, HGRN_WIDTH), lambda b, i: (0, 0)),
            pl.BlockSpec((1, HGRN_DIM), lambda b, i: (0, 0)),
            pl.BlockSpec(m_all.shape, lambda b, i: (0, 0)),
            pl.BlockSpec(masks.shape, lambda b, i: (0, 0, 0)),
        ],
        out_specs=pl.BlockSpec((c, HGRN_WIDTH), lambda b, i: (b * nc + i, 0)),
        out_shape=jax.ShapeDtypeStruct((t, HGRN_WIDTH), BF16),
        scratch_shapes=[
            pltpu.VMEM((m_all.shape[0], HGRN_WIDTH), F32),
            pltpu.VMEM((HGRN_HEADS, HGRN_DIM, HGRN_DIM), F32),
        ],
        compiler_params=_params(("parallel", "arbitrary"), VMEM_LIMIT),
        name="hgrn2_mixer",
    )(proj, proj, proj, proj, lb.reshape(1, HGRN_WIDTH), norm_g.reshape(1, HGRN_DIM), m_all, masks)


def _out_proj_kernel(h_ref, a_ref, s_ref, r_ref, w_ref, gn_ref, rh_ref, rl_ref,
                     ho_ref, hn_ref, eid_ref, gate_ref):
    s_lo, r_lo = ATTN_WIDTH, ATTN_WIDTH + S5_WIDTH
    acc = jnp.dot(a_ref[...], w_ref[0:s_lo, :], preferred_element_type=F32)
    acc = acc + jnp.dot(s_ref[...], w_ref[s_lo:r_lo, :], preferred_element_type=F32)
    acc = acc + jnp.dot(r_ref[...], w_ref[r_lo:, :], preferred_element_type=F32)
    h = h_ref[...] + acc
    ho_ref[...] = h
    ms = jnp.mean(h * h, axis=-1, keepdims=True)
    hn = h * lax.rsqrt(ms + EPS) * gn_ref[...]
    hn_ref[...] = hn

    hh = hn.astype(BF16)
    hl = (hn - hh.astype(F32)).astype(BF16)
    logits = (jnp.dot(hh, rh_ref[...], preferred_element_type=F32)
              + jnp.dot(hh, rl_ref[...], preferred_element_type=F32)
              + jnp.dot(hl, rh_ref[...], preferred_element_type=F32))
    lane = lax.broadcasted_iota(jnp.int32, logits.shape, 1).astype(F32)
    neg = -jnp.inf
    big = 1e9
    is_group = jnp.where(lane >= N_EXPERTS, jnp.where(lane < N_EXPERTS + N_GROUPS, 1.0, 0.0), 0.0)
    gl = jnp.where(is_group > 0, logits, neg)
    gmax = jnp.max(gl, axis=-1, keepdims=True)
    g_lane = jnp.min(jnp.where(gl == gmax, lane, big), axis=-1, keepdims=True)
    p_g = 1.0 / jnp.sum(jnp.exp(gl - gmax), axis=-1, keepdims=True)
    lo_lane = (g_lane - N_EXPERTS) * EXPERTS_PER_GROUP
    in_group = jnp.where(lane >= lo_lane, jnp.where(lane < lo_lane + EXPERTS_PER_GROUP, 1.0, 0.0), 0.0)
    el = jnp.where(in_group > 0, logits, neg)
    t1 = jnp.max(el, axis=-1, keepdims=True)
    i1 = jnp.min(jnp.where(el == t1, lane, big), axis=-1, keepdims=True)
    el2 = jnp.where(lane == i1, neg, el)
    t2 = jnp.max(el2, axis=-1, keepdims=True)
    i2 = jnp.min(jnp.where(el2 == t2, lane, big), axis=-1, keepdims=True)
    e21 = jnp.exp(t2 - t1)
    g1 = p_g / (1.0 + e21)
    g2 = p_g * e21 / (1.0 + e21)
    eid_ref[...] = jnp.where(lane == 0, i1, jnp.where(lane == 1, i2, 0.0)).astype(jnp.int32)
    gate_ref[...] = jnp.where(lane == 0, g1, jnp.where(lane == 1, g2, 0.0))


def _out_proj(h, attn, s5, hg, w_out, gn, r_hi, r_lo):
    t, d = h.shape
    tm = min(OUT_TM, t)

    def rows(w):
        return pl.BlockSpec((tm, w), lambda i: (i, 0))

    def full(a):
        return pl.BlockSpec(a.shape, lambda i: (0,) * a.ndim)

    return pl.pallas_call(
        _out_proj_kernel,
        grid=(t // tm,),
        in_specs=[rows(d), rows(ATTN_WIDTH), rows(S5_WIDTH), rows(HGRN_WIDTH),
                  full(w_out), pl.BlockSpec((1, d), lambda i: (0, 0)),
                  full(r_hi), full(r_lo)],
        out_specs=[rows(d), rows(d), rows(LANES), rows(LANES)],
        out_shape=[jax.ShapeDtypeStruct((t, d), F32), jax.ShapeDtypeStruct((t, d), F32),
                   jax.ShapeDtypeStruct((t, LANES), jnp.int32), jax.ShapeDtypeStruct((t, LANES), F32)],
        compiler_params=_params(("parallel",), VMEM_LIMIT),
        name="out_proj_router",
    )(h, attn, s5, hg, w_out, gn.reshape(1, d), r_hi, r_lo)


def _moe_plan(eid, tm):
    flat = eid.reshape(-1)
    n_slots = flat.shape[0]
    onehot = (flat[:, None] == jnp.arange(N_EXPERTS, dtype=jnp.int32)[None, :]).astype(jnp.int32)
    csum = jnp.cumsum(onehot, axis=0)
    rank = jnp.sum(onehot * csum, axis=1) - 1
    counts = csum[-1]
    tiles = (counts + tm - 1) // tm
    tile_end = jnp.cumsum(tiles)
    tile_start = tile_end - tiles
    slot_start = jnp.cumsum(counts) - counts
    pos = jnp.sum(onehot * tile_start[None, :], axis=1) * tm + rank
    tok_sorted = jnp.argsort(flat, stable=True).astype(jnp.int32) // TOP_K
    tok_sorted = jnp.pad(tok_sorted, (0, tm))
    nt_max = (n_slots + N_EXPERTS * (tm - 1)) // tm
    n_used = tile_end[-1]
    j = jnp.minimum(jnp.arange(nt_max, dtype=jnp.int32), n_used - 1)
    tile_expert = jnp.sum((j[:, None] >= tile_end[None, :]).astype(jnp.int32), axis=1)
    onehot_t = (tile_expert[:, None] == jnp.arange(N_EXPERTS, dtype=jnp.int32)[None, :]).astype(jnp.int32)
    tile_slot0 = jnp.sum(onehot_t * (slot_start - tile_start * tm)[None, :], axis=1) + j * tm
    used = (tiles > 0).astype(jnp.int32)
    ordinal = jnp.cumsum(used) - used
    ids = jnp.arange(N_EXPERTS, dtype=jnp.int32)
    later = jnp.where((ids[None, :] > ids[:, None]) & (used[None, :] > 0), ids[None, :], N_EXPERTS)
    nxt = jnp.min(later, axis=1)
    nxt = jnp.where(nxt >= N_EXPERTS, -1, nxt)
    tile_wslot = jnp.sum(onehot_t * (ordinal % 2)[None, :], axis=1)
    tile_next = jnp.sum(onehot_t * nxt[None, :], axis=1)
    return dict(pos=pos.astype(jnp.int32), tok_sorted=tok_sorted, tile_expert=tile_expert.astype(jnp.int32),
                tile_slot0=tile_slot0.astype(jnp.int32), tile_wslot=tile_wslot.astype(jnp.int32),
                tile_next=tile_next.astype(jnp.int32), n_used=n_used.reshape(1).astype(jnp.int32),
                nt_max=nt_max)


def _expert_kernel(tok_ref, te_ref, s0_ref, ws_ref, ne_ref, nu_ref, hn_ref, wg_ref, wu_ref, wd_ref, y_ref,
                   xa_ref, xb_ref, wgf_ref, wuf_ref, wdf_ref, wgb_ref, wub_ref, wdb_ref, sem, wsem,
                   *, tm, layer):
    j = pl.program_id(0)
    n_used = nu_ref[0]
    bufs = (xa_ref, xb_ref)

    def weight_copies(expert, slot):
        return [pltpu.make_async_copy(src.at[layer, expert], dst.at[slot], wsem.at[slot])
                for src, dst in ((wg_ref, wgf_ref), (wu_ref, wuf_ref), (wd_ref, wdf_ref))]

    def row_copy(buf, r, tok):
        return pltpu.make_async_copy(hn_ref.at[pl.ds(tok, 1)], bufs[buf].at[pl.ds(r, 1)], sem.at[buf])

    def wait_gather(buf):
        def wait(r, carry):
            row_copy(buf, r, 0).wait()
            return carry

        lax.fori_loop(0, tm, wait, 0, unroll=DMA_UNROLL)

    @pl.when(j == 0)
    def _():
        for cp in weight_copies(te_ref[0], ws_ref[0]):
            cp.start()
        base = s0_ref[0]

        def issue(r, carry):
            row_copy(0, r, tok_ref[base + r]).start()
            return carry

        lax.fori_loop(0, tm, issue, 0, unroll=DMA_UNROLL)

    def tile(buf):
        wait_gather(buf)
        prev = te_ref[jnp.maximum(j - 1, 0)]

        @pl.when(jnp.logical_or(j == 0, te_ref[j] != prev))
        def _():
            slot = ws_ref[j]

            @pl.when(ne_ref[j] >= 0)
            def _():
                for cp in weight_copies(ne_ref[j], 1 - slot):
                    cp.start()

            for cp in weight_copies(te_ref[j], slot):
                cp.wait()
            wgb_ref[...] = wgf_ref[slot].astype(BF16)
            wub_ref[...] = wuf_ref[slot].astype(BF16)
            wdb_ref[...] = wdf_ref[slot].astype(BF16)

        nbase = s0_ref[jnp.minimum(j + 1, n_used - 1)]
        for r in range(tm):
            row_copy(1 - buf, r, tok_ref[nbase + r]).start()

        x = bufs[buf][...].astype(BF16)
        g = jnp.dot(x, wgb_ref[...], preferred_element_type=F32)
        u = jnp.dot(x, wub_ref[...], preferred_element_type=F32)
        hmid = (g * jax.nn.sigmoid(g) * u).astype(BF16)
        y_ref[...] = jnp.dot(hmid, wdb_ref[...], preferred_element_type=F32)

        @pl.when(j == n_used - 1)
        def _():
            wait_gather(1 - buf)

    for parity in range(2):
        @pl.when(jnp.logical_and(j < n_used, lax.rem(j, 2) == parity))
        def _(parity=parity):
            tile(parity)

    @pl.when(j >= n_used)
    def _():
        y_ref[...] = jnp.zeros(y_ref.shape, F32)


def _moe_experts(hn, plan, w_gate, w_up, w_down, layer, tm):
    t, d = hn.shape
    de = w_gate.shape[3]
    nt_max = plan["nt_max"]
    grid_spec = pltpu.PrefetchScalarGridSpec(
        num_scalar_prefetch=6,
        grid=(nt_max,),
        in_specs=[pl.BlockSpec(memory_space=pl.ANY)] * 4,
        out_specs=pl.BlockSpec((tm, d), lambda j, *_: (j, 0)),
        scratch_shapes=[pltpu.VMEM((tm, d), F32), pltpu.VMEM((tm, d), F32),
                        pltpu.VMEM((2, d, de), F32), pltpu.VMEM((2, d, de), F32), pltpu.VMEM((2, de, d), F32),
                        pltpu.VMEM((d, de), BF16), pltpu.VMEM((d, de), BF16), pltpu.VMEM((de, d), BF16),
                        pltpu.SemaphoreType.DMA((2,)), pltpu.SemaphoreType.DMA((2,))],
    )
    return pl.pallas_call(
        functools.partial(_expert_kernel, tm=tm, layer=layer),
        grid_spec=grid_spec,
        out_shape=jax.ShapeDtypeStruct((nt_max * tm, d), F32),
        compiler_params=_params(("arbitrary",), VMEM_LIMIT),
        name="moe_experts",
    )(plan["tok_sorted"], plan["tile_expert"], plan["tile_slot0"], plan["tile_wslot"], plan["tile_next"],
      plan["n_used"], hn, w_gate, w_up, w_down)


def _combine_kernel(pos_ref, h_ref, gate_ref, gn_ref, y_ref, o_ref, buf_ref, sem, *, tm, final_norm):
    i = pl.program_id(0)
    cur = lax.rem(i, 2)

    def row_copy(b, k, r, src_row):
        return pltpu.make_async_copy(y_ref.at[pl.ds(src_row, 1)], buf_ref.at[b, k, pl.ds(r, 1)], sem.at[b])

    def start_gather(tile):
        b = lax.rem(tile, 2)

        def issue(r, carry):
            slot = (tile * tm + r) * TOP_K
            for k in range(TOP_K):
                row_copy(b, k, r, pos_ref[slot + k]).start()
            return carry

        lax.fori_loop(0, tm, issue, 0, unroll=DMA_UNROLL)

    @pl.when(i == 0)
    def _():
        start_gather(0)

    @pl.when(i + 1 < pl.num_programs(0))
    def _():
        start_gather(i + 1)

    def wait(r, carry):
        for k in range(TOP_K):
            row_copy(cur, k, r, 0).wait()
        return carry

    lax.fori_loop(0, tm, wait, 0, unroll=DMA_UNROLL)

    gates = gate_ref[...]
    h = h_ref[...] + gates[:, 0:1] * buf_ref[cur, 0] + gates[:, 1:2] * buf_ref[cur, 1]
    if final_norm:
        ms = jnp.mean(h * h, axis=-1, keepdims=True)
        h = h * lax.rsqrt(ms + EPS) * gn_ref[...]
    o_ref[...] = h


def _moe_combine(h, gates, ys, pos, final_g, final_norm):
    t, d = h.shape
    tm = min(COMB_TM, t)
    grid_spec = pltpu.PrefetchScalarGridSpec(
        num_scalar_prefetch=1,
        grid=(t // tm,),
        in_specs=[
            pl.BlockSpec((tm, d), lambda i, p: (i, 0)),
            pl.BlockSpec((tm, LANES), lambda i, p: (i, 0)),
            pl.BlockSpec((1, d), lambda i, p: (0, 0)),
            pl.BlockSpec(memory_space=pl.ANY),
        ],
        out_specs=pl.BlockSpec((tm, d), lambda i, p: (i, 0)),
        scratch_shapes=[pltpu.VMEM((2, TOP_K, tm, d), F32), pltpu.SemaphoreType.DMA((2,))],
    )
    return pl.pallas_call(
        functools.partial(_combine_kernel, tm=tm, final_norm=final_norm),
        grid_spec=grid_spec,
        out_shape=jax.ShapeDtypeStruct((t, d), F32),
        compiler_params=_params(("arbitrary",), VMEM_LIMIT),
        name="moe_combine",
    )(pos, h, gates, final_g.reshape(1, d), ys)


def _permute_w_in(w):
    a = 3 * ATTN_WIDTH
    w = jnp.concatenate([w[:, a:], w[:, :a]], axis=1)
    return jnp.pad(w, ((0, 0), (0, IN_COLS_PAD - IN_COLS))).astype(BF16)


def kernel(x, w_in, w_out, mix_norm_g, ffn_norm_g, rel_bias, diff_lambda, attn_subln_g, s5_a_re, s5_a_im, s5_log_dt, s5_b_re, s5_b_im, s5_c_re, s5_c_im, s5_d, s5_w_glu, hgrn_lb_logits, hgrn_norm_g, moe_w_group, moe_w_router, moe_w_gate, moe_w_up, moe_w_down, final_norm_g):
    batch, seq, d = x.shape
    depth = w_in.shape[0]
    t = batch * seq
    h = x.reshape(t, d)

    lb_cum = jnp.cumsum(jax.nn.softmax(hgrn_lb_logits.astype(F32), axis=0), axis=0)
    lb_all = lb_cum - lb_cum[0:1]
    attn_bias = _attn_bias_tiles(rel_bias, min(ATTN_TQ, seq), min(ATTN_TK, seq))

    for l in range(depth):
        proj = _in_proj(h, mix_norm_g[l], _permute_w_in(w_in[l]))

        lam_init = 0.8 - 0.6 * math.exp(-0.3 * l)
        lv = diff_lambda[l].astype(F32)
        lam = jnp.exp(jnp.sum(lv[0] * lv[1])) - jnp.exp(jnp.sum(lv[2] * lv[3])) + lam_init
        attn = _diff_attention(proj, attn_bias, lam, attn_subln_g[l], lam_init, batch, seq)

        w_b, w_c, tabs = _s5_prep(s5_a_re[l], s5_a_im[l], s5_log_dt[l], s5_b_re[l], s5_b_im[l],
                                  s5_c_re[l], s5_c_im[l], min(S5_TB, seq) // SUBLANES)
        s5 = _s5_mixer(proj, w_b, w_c, tabs, s5_d[l], s5_w_glu[l].astype(BF16), batch, seq)

        hg = _hgrn_mixer(proj, lb_all[l], hgrn_norm_g[l], batch, seq)

        wo = w_out[l].astype(BF16)
        w_r = jnp.concatenate([moe_w_router[l], moe_w_group[l]], axis=1).astype(F32)
        w_r = jnp.pad(w_r, ((0, 0), (0, LANES - w_r.shape[1])))
        r_hi = w_r.astype(BF16)
        r_lo = (w_r - r_hi.astype(F32)).astype(BF16)
        h, hn, eid, gates = _out_proj(h, attn, s5, hg, wo, ffn_norm_g[l], r_hi, r_lo)

        plan = _moe_plan(eid[:, :TOP_K], MOE_TM)
        ys = _moe_experts(hn, plan, moe_w_gate, moe_w_up, moe_w_down, l, MOE_TM)
        h = _moe_combine(h, gates, ys, plan["pos"], final_norm_g, final_norm=(l == depth - 1))

    return h.reshape(batch, seq, d)
```

```python
import functools
import math

import numpy as np
import jax
import jax.numpy as jnp
from jax import lax
from jax.experimental import pallas as pl
from jax.experimental.pallas import tpu as pltpu

F32 = jnp.float32
BF16 = jnp.bfloat16
EPS = 1e-6
LOG2E = 1.4426950408889634

D_MODEL = 2048
ATTN_HEADS = 6
ATTN_QK_DIM = 64
ATTN_V_DIM = 128
ATTN_WIDTH = ATTN_HEADS * ATTN_V_DIM
REL_BUCKETS = 32
REL_MAX_DIST = 128
S5_GROUPS = 40
S5_GROUP_CH = 16
S5_STATE = 64
S5_WIDTH = S5_GROUPS * S5_GROUP_CH
S5_LANES = S5_GROUPS * S5_STATE
HGRN_HEADS = 5
HGRN_DIM = 128
HGRN_WIDTH = HGRN_HEADS * HGRN_DIM
N_GROUPS = 4
EXPERTS_PER_GROUP = 8
N_EXPERTS = N_GROUPS * EXPERTS_PER_GROUP
TOP_K = 2
D_EXPERT = 512

LANES = 128
SUBLANES = 8

IN_COLS = 3 * ATTN_WIDTH + S5_WIDTH + 4 * HGRN_WIDTH
PROJ_TN = 512
IN_COLS_PAD = -(-IN_COLS // PROJ_TN) * PROJ_TN
ATTN_COL_BLOCK = (S5_WIDTH + 4 * HGRN_WIDTH) // LANES

PROJ_TM = 1024
ATTN_TQ = 512
ATTN_TK = 512
ATTN_NEAR = 2
ATTN_ROWS = 32
S5_TB = 256
S5_CHUNK = 640
S5_BLK = 128
HGRN_C = 128
OUT_TM = 256
MOE_TM = 256
COMB_TM = 128
DMA_UNROLL = 8

VMEM_LIMIT = 56 * 1024 * 1024


def _params(sem, vmem=None):
    return pltpu.CompilerParams(dimension_semantics=sem, vmem_limit_bytes=vmem)


def _pack_bf16_pair(lo, hi):
    lo_bits = lax.bitcast_convert_type(lo.astype(BF16).astype(F32), jnp.uint32) >> 16
    hi_bits = lax.bitcast_convert_type(hi.astype(BF16).astype(F32), jnp.uint32) & jnp.uint32(0xFFFF0000)
    return lo_bits | hi_bits


def _unpack_bf16_pair(w):
    return (lax.bitcast_convert_type(w << 16, F32),
            lax.bitcast_convert_type(w & jnp.uint32(0xFFFF0000), F32))


def _in_proj_kernel(x_ref, g_ref, w_ref, o_ref, xn_ref):
    @pl.when(pl.program_id(1) == 0)
    def _():
        x = x_ref[...]
        ms = jnp.mean(x * x, axis=-1, keepdims=True)
        xn_ref[...] = (x * lax.rsqrt(ms + EPS) * g_ref[...]).astype(BF16)

    o_ref[...] = jnp.dot(xn_ref[...], w_ref[...], preferred_element_type=F32)


def _in_proj(h, g, w):
    t, d = h.shape
    n = w.shape[1]
    tm = min(PROJ_TM, t)
    return pl.pallas_call(
        _in_proj_kernel,
        grid=(t // tm, n // PROJ_TN),
        in_specs=[
            pl.BlockSpec((tm, d), lambda i, j: (i, 0)),
            pl.BlockSpec((1, d), lambda i, j: (0, 0)),
            pl.BlockSpec((d, PROJ_TN), lambda i, j: (0, j)),
        ],
        out_specs=pl.BlockSpec((tm, PROJ_TN), lambda i, j: (i, j)),
        out_shape=jax.ShapeDtypeStruct((t, n), F32),
        scratch_shapes=[pltpu.VMEM((tm, d), BF16)],
        compiler_params=_params(("parallel", "arbitrary"), VMEM_LIMIT),
        name="in_proj",
    )(h, g.reshape(1, d), w)


def _attn_kernel(lam_ref, q_ref, k_ref, v_ref, bias_ref, g_ref, o_ref,
                 kb_ref, vb_ref, qz_ref, m_ref, a_ref, acc_ref, s0_ref, s1_ref, p_ref, *, tq, tk, out_scale):
    qi = pl.program_id(2)
    ratio = tq // tk

    @pl.when(qi == 0)
    def _():
        kb_ref[...] = k_ref[...].astype(BF16)
        vb_ref[:, 0:LANES] = v_ref[...].astype(BF16)
        vb_ref[:, LANES:2 * LANES] = jnp.ones((vb_ref.shape[0], LANES), BF16)

    q = q_ref[...] * (ATTN_QK_DIM ** -0.5 * LOG2E)
    lane = lax.broadcasted_iota(jnp.int32, q.shape, 1)
    qz_ref[0:tq, :] = jnp.where(lane < ATTN_QK_DIM, q, 0.0).astype(BF16)
    qz_ref[tq:2 * tq, :] = jnp.where(lane >= ATTN_QK_DIM, q, 0.0).astype(BF16)
    m_ref[...] = jnp.full(m_ref.shape, -jnp.inf, F32)
    acc_ref[...] = jnp.zeros(acc_ref.shape, F32)

    n_col = tk // LANES
    groups = [slice(g * ATTN_ROWS, (g + 1) * ATTN_ROWS) for g in range(2 * tq // ATTN_ROWS)]

    def scores(j, s_ref):
        k = kb_ref[pl.ds(pl.multiple_of(j * tk, tk), tk), :]
        s_ref[...] = lax.dot_general(qz_ref[...], k, (((1,), (1,)), ((), ())), preferred_element_type=F32)

    def softmax_pv(j, s_ref, biased):
        if biased:
            kind = jnp.clip(j - ratio * qi + ATTN_NEAR, 0, ATTN_NEAR + ratio - 1)
        for rows in groups:
            cols = [s_ref[rows, i * LANES:(i + 1) * LANES] for i in range(n_col)]
            if biased:
                b0 = rows.start % tq
                cols = [x + bias_ref[0, kind, b0:b0 + ATTN_ROWS, i * LANES:(i + 1) * LANES]
                        for i, x in enumerate(cols)]
                for i, x in enumerate(cols):
                    s_ref[rows, i * LANES:(i + 1) * LANES] = x
            mx = functools.reduce(jnp.maximum, cols)
            m_old = m_ref[rows, :]
            m_new = jnp.maximum(m_old, jnp.max(mx, axis=-1, keepdims=True))
            m_ref[rows, :] = m_new
            a_ref[rows, :] = jnp.exp2(m_old - m_new)
        for rows in groups:
            m_new = m_ref[rows, :]
            for i in range(n_col):
                p_ref[rows, i * LANES:(i + 1) * LANES] = jnp.exp2(
                    s_ref[rows, i * LANES:(i + 1) * LANES] - m_new).astype(BF16)
        v = vb_ref[pl.ds(pl.multiple_of(j * tk, tk), tk), :]
        alpha = a_ref[...]
        acc_ref[...] = (jnp.concatenate([alpha, alpha], axis=1) * acc_ref[...]
                        + jnp.dot(p_ref[...], v, preferred_element_type=F32))

    n_tiles = ratio * (qi + 1)
    first = lax.rem(n_tiles, 2)
    n_far = jnp.maximum(ratio * qi - ATTN_NEAR + 1, 0)
    n_far_pairs = jnp.maximum(n_far - first, 0) // 2
    n_pairs = (n_tiles - first) // 2

    @pl.when(first == 1)
    def _():
        scores(0, s1_ref)
        scores(jnp.minimum(1, n_tiles - 1), s0_ref)
        softmax_pv(0, s1_ref, True)

    @pl.when(first == 0)
    def _():
        scores(0, s0_ref)

    def pair_body(jj, carry, biased):
        j = first + 2 * jj
        scores(j + 1, s1_ref)
        softmax_pv(j, s0_ref, biased)
        scores(jnp.minimum(j + 2, n_tiles - 1), s0_ref)
        softmax_pv(j + 1, s1_ref, biased)
        return carry

    lax.fori_loop(0, n_far_pairs, functools.partial(pair_body, biased=False), 0)
    lax.fori_loop(n_far_pairs, n_pairs, functools.partial(pair_body, biased=True), 0)

    inv_l = 1.0 / acc_ref[:, LANES:2 * LANES]
    o1 = acc_ref[0:tq, 0:LANES] * inv_l[0:tq]
    o2 = acc_ref[tq:2 * tq, 0:LANES] * inv_l[tq:2 * tq]
    o = o1 - lam_ref[0] * o2
    ms = jnp.mean(o * o, axis=-1, keepdims=True)
    o_ref[...] = (o * lax.rsqrt(ms + EPS) * g_ref[...] * out_scale).astype(o_ref.dtype)


def _t5_bucket(dist):
    n = jnp.maximum(dist, 0)
    max_exact = REL_BUCKETS // 2
    large = max_exact + (jnp.log(jnp.maximum(n, 1).astype(F32) / max_exact)
                         / math.log(REL_MAX_DIST / max_exact) * (REL_BUCKETS - max_exact)).astype(jnp.int32)
    large = jnp.minimum(large, REL_BUCKETS - 1)
    return jnp.where(n < max_exact, n, large)


def _attn_bias_tiles(rel_bias, tq, tk):
    table = rel_bias.astype(F32)
    far = table[REL_BUCKETS - 1]
    r = jnp.arange(tq)[:, None]
    c = jnp.arange(tk)[None, :]

    def lookup(dist):
        bucket = _t5_bucket(dist)
        out = jnp.zeros((ATTN_HEADS,) + dist.shape, F32)
        for b in range(REL_BUCKETS):
            out = jnp.where((bucket == b)[None], table[b][:, None, None], out)
        return out

    tiles = [jnp.zeros((ATTN_HEADS, tq, tk), F32)]
    for rel in range(-ATTN_NEAR + 1, tq // tk):
        dist = r - c - rel * tk
        tiles.append(jnp.where((dist >= 0)[None], (lookup(dist) - far[:, None, None]) * LOG2E, -1e30))
    return jnp.stack(tiles, axis=1)


def _diff_attention(proj, bias, lam, subln_g, lam_init, batch, seq):
    t = proj.shape[0]
    tq = min(ATTN_TQ, seq)
    tk = min(ATTN_TK, seq)
    ratio = tq // tk
    assert (ATTN_NEAR - 1) * tk >= REL_MAX_DIST and ratio * tk == tq
    nq = seq // tq
    kernel = functools.partial(_attn_kernel, tq=tq, tk=tk, out_scale=1.0 - lam_init)
    qb, kb, vb = ATTN_COL_BLOCK, ATTN_COL_BLOCK + ATTN_HEADS, ATTN_COL_BLOCK + 2 * ATTN_HEADS
    return pl.pallas_call(
        kernel,
        grid=(batch, ATTN_HEADS, nq),
        in_specs=[
            pl.BlockSpec(memory_space=pltpu.SMEM),
            pl.BlockSpec((tq, LANES), lambda b, h, i: (b * nq + i, qb + h)),
            pl.BlockSpec((seq, LANES), lambda b, h, i: (b, kb + h)),
            pl.BlockSpec((seq, LANES), lambda b, h, i: (b, vb + h)),
            pl.BlockSpec((1,) + bias.shape[1:], lambda b, h, i: (h, 0, 0, 0)),
            pl.BlockSpec((1, LANES), lambda b, h, i: (0, 0)),
        ],
        out_specs=pl.BlockSpec((tq, LANES), lambda b, h, i: (b * nq + i, h)),
        out_shape=jax.ShapeDtypeStruct((t, ATTN_WIDTH), BF16),
        scratch_shapes=[
            pltpu.VMEM((seq, LANES), BF16),
            pltpu.VMEM((seq, 2 * LANES), BF16),
            pltpu.VMEM((2 * tq, LANES), BF16),
            pltpu.VMEM((2 * tq, LANES), F32),
            pltpu.VMEM((2 * tq, LANES), F32),
            pltpu.VMEM((2 * tq, 2 * LANES), F32),
            pltpu.VMEM((2 * tq, tk), F32),
            pltpu.VMEM((2 * tq, tk), F32),
            pltpu.VMEM((2 * tq, tk), BF16),
        ],
        compiler_params=_params(("parallel", "parallel", "arbitrary"), VMEM_LIMIT),
        name="diff_attn",
    )(lam.reshape(1), proj, proj, proj, bias, subln_g.reshape(1, LANES))


def _s5_kernel(u_ref, perm_ref, permt_ref, wb_ref, wc_ref, pw_ref, d_ref, wglu_ref, o_ref,
               xr_ref, xi_ref, cr_ref, ci_ref, kr_ref, ki_ref, *, tb):
    @pl.when(pl.program_id(1) == 0)
    def _():
        cr_ref[...] = jnp.zeros(cr_ref.shape, F32)
        ci_ref[...] = jnp.zeros(ci_ref.shape, F32)

    un = u_ref[...]
    uh = un.astype(BF16)
    ul = (un - uh.astype(F32)).astype(BF16)
    up = jnp.dot(perm_ref[...], jnp.concatenate([uh, ul], axis=1), preferred_element_type=F32)
    ub = up[:, :S5_WIDTH].astype(BF16)
    u = up[:, :S5_WIDTH] + up[:, S5_WIDTH:]
    nblk = S5_WIDTH // S5_BLK
    sw = S5_LANES // nblk
    tpb = sw // LANES
    for i in range(nblk):
        bu = jnp.dot(ub[:, i * S5_BLK:(i + 1) * S5_BLK], wb_ref[i], preferred_element_type=F32)
        for k in range(tpb):
            xr_ref[i * tpb + k] = bu[:, k * LANES:(k + 1) * LANES]
            xi_ref[i * tpb + k] = bu[:, sw + k * LANES:sw + (k + 1) * LANES]

    sl = tb // SUBLANES
    for c in range(S5_LANES // S5_CHUNK):
        tiles = range(c * S5_CHUNK // LANES, (c + 1) * S5_CHUNK // LANES)
        lss = [slice(n * LANES, (n + 1) * LANES) for n in tiles]
        a = [(pw_ref[0, 0, :, ls], pw_ref[1, 0, :, ls]) for ls in lss]

        def local_step(t, carry, tiles=tiles, a=a):
            rows = pl.ds(pl.multiple_of(t * SUBLANES, SUBLANES), SUBLANES)
            out = []
            for n, (ar, ai), (pr, pi) in zip(tiles, a, carry):
                xr = ar * pr - ai * pi + xr_ref[n, rows, :]
                xi = ar * pi + ai * pr + xi_ref[n, rows, :]
                xr_ref[n, rows, :] = xr
                xi_ref[n, rows, :] = xi
                out.append((xr, xi))
            return tuple(out)

        zero = jnp.zeros((SUBLANES, LANES), F32)
        ends = lax.fori_loop(0, sl, local_step, tuple((zero, zero) for _ in tiles), unroll=2)

        for ls, (er, ei) in zip(lss, ends):
            lr, li = pw_ref[0, sl - 1, 0:1, ls], pw_ref[1, sl - 1, 0:1, ls]
            kr, ki = cr_ref[:, ls], ci_ref[:, ls]
            for sub in range(SUBLANES):
                kr_ref[sub:sub + 1, ls] = kr
                ki_ref[sub:sub + 1, ls] = ki
                kr, ki = (er[sub:sub + 1, :] + lr * kr - li * ki, ei[sub:sub + 1, :] + lr * ki + li * kr)
            cr_ref[:, ls] = kr
            ci_ref[:, ls] = ki

        k8 = [(kr_ref[:, ls], ki_ref[:, ls]) for ls in lss]

        def fix_step(t, carry, tiles=tiles, lss=lss, k8=k8):
            rows = pl.ds(pl.multiple_of(t * SUBLANES, SUBLANES), SUBLANES)
            for n, ls, (kr8, ki8) in zip(tiles, lss, k8):
                pr = pw_ref[0, t, :, ls]
                pi = pw_ref[1, t, :, ls]
                xr_ref[n, rows, :] = xr_ref[n, rows, :] + pr * kr8 - pi * ki8
                xi_ref[n, rows, :] = xi_ref[n, rows, :] + pr * ki8 + pi * kr8
            return carry

        lax.fori_loop(0, sl, fix_step, 0, unroll=2)

    ys = []
    for i in range(nblk):
        xc = jnp.concatenate([xr_ref[i * tpb + k] for k in range(tpb)]
                             + [xi_ref[i * tpb + k] for k in range(tpb)], axis=1)
        ys.append(jnp.dot(xc.astype(BF16), wc_ref[i], preferred_element_type=F32))
    y = jnp.concatenate(ys, axis=1) + d_ref[...] * u
    z = 0.5 * y * (1.0 + jnp.tanh(math.sqrt(2.0 / math.pi) * (y + 0.044715 * (y * y * y))))
    gate = jax.nn.sigmoid(jnp.dot(z.astype(BF16), wglu_ref[...], preferred_element_type=F32))
    o_ref[...] = jnp.dot(permt_ref[...], (z * gate).astype(BF16),
                         preferred_element_type=F32).astype(o_ref.dtype)


def _s5_prep(a_re, a_im, log_dt, b_re, b_im, c_re, c_im, steps):
    lr = jnp.minimum(a_re.astype(F32), -1e-4)
    li = a_im.astype(F32)
    dt = jnp.exp(log_dt.astype(F32))[:, None]
    mag = jnp.exp(lr * dt)
    ab_re = mag * jnp.cos(li * dt)
    ab_im = mag * jnp.sin(li * dt)
    den = lr * lr + li * li
    nr = ab_re - 1.0
    ni = ab_im
    f_re = (nr * lr + ni * li) / den
    f_im = (ni * lr - nr * li) / den
    br = b_re.astype(F32)
    bi = b_im.astype(F32)
    bb_re = f_re[..., None] * br - f_im[..., None] * bi
    bb_im = f_re[..., None] * bi + f_im[..., None] * br

    nblk = S5_WIDTH // S5_BLK
    gpb = S5_GROUPS // nblk
    eye = jnp.eye(gpb, dtype=F32)

    def bmat(bb):
        bb = bb.reshape(nblk, gpb, S5_STATE, S5_GROUP_CH)
        m = jnp.einsum('ignc,gh->igchn', bb, eye)
        return m.reshape(nblk, gpb * S5_GROUP_CH, gpb * S5_STATE)

    def cmat(cc):
        cc = cc.astype(F32).reshape(nblk, gpb, S5_GROUP_CH, S5_STATE)
        m = jnp.einsum('igcn,gh->ignhc', cc, eye)
        return m.reshape(nblk, gpb * S5_STATE, gpb * S5_GROUP_CH)

    w_b = jnp.concatenate([bmat(bb_re), bmat(bb_im)], axis=2).astype(BF16)
    w_c = jnp.concatenate([cmat(c_re), -cmat(c_im)], axis=1).astype(BF16)

    j = jnp.arange(1, steps + 1, dtype=F32)[:, None, None]
    m = jnp.exp(lr * dt * j)
    pows = jnp.stack([(m * jnp.cos(li * dt * j)).reshape(steps, 1, -1),
                      (m * jnp.sin(li * dt * j)).reshape(steps, 1, -1)])
    pows = jnp.broadcast_to(pows, (2, steps, SUBLANES, S5_LANES))
    return w_b, w_c, pows


def _s5_mixer(proj, w_b, w_c, tabs, d_skip, w_glu, batch, seq):
    t = proj.shape[0]
    tb = min(S5_TB, seq)
    nt = seq // tb
    sl = tb // SUBLANES
    r = np.arange(tb)
    perm = np.zeros((tb, tb), np.float32)
    perm[r, (r % SUBLANES) * sl + r // SUBLANES] = 1.0
    perm_t = jnp.asarray(perm.T, BF16)
    perm = jnp.asarray(perm, BF16)
    return pl.pallas_call(
        functools.partial(_s5_kernel, tb=tb),
        grid=(batch, nt),
        in_specs=[
            pl.BlockSpec((tb, S5_WIDTH), lambda b, i: (b * nt + i, 0)),
            pl.BlockSpec((tb, tb), lambda b, i: (0, 0)),
            pl.BlockSpec((tb, tb), lambda b, i: (0, 0)),
            pl.BlockSpec(w_b.shape, lambda b, i: (0, 0, 0)),
            pl.BlockSpec(w_c.shape, lambda b, i: (0, 0, 0)),
            pl.BlockSpec(tabs.shape, lambda b, i: (0, 0, 0, 0)),
            pl.BlockSpec((1, S5_WIDTH), lambda b, i: (0, 0)),
            pl.BlockSpec((S5_WIDTH, S5_WIDTH), lambda b, i: (0, 0)),
        ],
        out_specs=pl.BlockSpec((tb, S5_WIDTH), lambda b, i: (b * nt + i, 0)),
        out_shape=jax.ShapeDtypeStruct((t, S5_WIDTH), BF16),
        scratch_shapes=[
            pltpu.VMEM((S5_LANES // LANES, tb, LANES), F32),
            pltpu.VMEM((S5_LANES // LANES, tb, LANES), F32),
            pltpu.VMEM((1, S5_LANES), F32),
            pltpu.VMEM((1, S5_LANES), F32),
            pltpu.VMEM((SUBLANES, S5_LANES), F32),
            pltpu.VMEM((SUBLANES, S5_LANES), F32),
        ],
        compiler_params=_params(("parallel", "arbitrary"), VMEM_LIMIT),
        name="s5_mixer",
    )(proj, perm, perm_t, w_b, w_c, tabs, d_skip.reshape(1, S5_WIDTH), w_glu)


def _hgrn_consts(c):
    t = np.arange(c)[:, None]
    u = np.arange(c)[None, :]
    mats = [u <= t]
    masks = [np.eye(c, dtype=bool)]
    h = 1
    while h < c:
        mid = (t // (2 * h)) * 2 * h + h
        if 2 * h < SUBLANES:
            mats.append(np.where(t >= mid, (u >= mid) & (u <= t), (u > t) & (u < mid)))
        masks.append(((u // (2 * h)) == (t // (2 * h))) & (t >= mid) & (u < mid))
        h *= 2
    return (np.concatenate(mats, axis=0).astype(np.float32), np.stack(masks).astype(np.float32))


def _hgrn_kernel(q_ref, f_ref, i_ref, g_ref, lb_ref, ng_ref, mall_ref, mask_ref, o_ref,
                 e_ref, st_ref, *, c):
    @pl.when(pl.program_id(1) == 0)
    def _():
        st_ref[...] = jnp.zeros(st_ref.shape, F32)

    lb = lb_ref[...]
    fl = f_ref[...]
    la = jnp.log(lb)
    lbb = jnp.log1p(-lb) + (jnp.minimum(fl, 0.0) - jnp.log1p(jnp.exp(-jnp.abs(fl))))
    logf = jnp.maximum(la, lbb) + jnp.log1p(jnp.exp(-jnp.abs(la - lbb)))
    hi = logf.astype(BF16)
    lo = (logf - hi.astype(F32)).astype(BF16)
    e2 = jnp.dot(mall_ref[...], jnp.concatenate([hi, lo], axis=1), preferred_element_type=F32)
    es = e2[:, :HGRN_WIDTH] + e2[:, HGRN_WIDTH:]
    n_levels = mask_ref.shape[0] - 1
    n_small = mall_ref.shape[0] // c - 1
    gall = es[0:c]
    e_ref[0:c, :] = gall
    e_ref[c:2 * c, :] = gall[c - 1:c, :] - gall
    e_ref[2 * c:(2 + n_small) * c, :] = es[c:]
    for lv in range(n_small, n_levels):
        blk = 2 << lv
        g3 = gall.reshape(c // blk, blk, HGRN_WIDTH)
        gm = jnp.broadcast_to(g3[:, blk // 2 - 1:blk // 2, :], g3.shape)
        e_ref[(2 + lv) * c:(3 + lv) * c, :] = (-jnp.abs(g3 - gm)).reshape(c, HGRN_WIDTH)

    nt = (((1,), (1,)), ((), ()))
    for hd in range(HGRN_HEADS):
        ls = slice(hd * HGRN_DIM, (hd + 1) * HGRN_DIM)
        qr = q_ref[:, ls]
        q = qr * jax.nn.sigmoid(qr) * (HGRN_DIM ** -0.5)
        k = (1.0 - lb[:, ls]) * jax.nn.sigmoid(-fl[:, ls])
        vb = i_ref[:, ls].astype(BF16)
        p = lax.dot_general(q.astype(BF16), k.astype(BF16), nt, preferred_element_type=F32) * mask_ref[0]
        for lv in range(n_levels):
            fac = jnp.exp(e_ref[(2 + lv) * c:(3 + lv) * c, ls])
            s = lax.dot_general((q * fac).astype(BF16), (k * fac).astype(BF16), nt,
                                preferred_element_type=F32)
            p = p + s * mask_ref[lv + 1]
        o = jnp.dot(p.astype(BF16), vb, preferred_element_type=F32)
        gcum = e_ref[0:c, ls]
        st = st_ref[hd]
        o = o + lax.dot_general((q * jnp.exp(gcum)).astype(BF16), st.astype(BF16), nt,
                                preferred_element_type=F32)
        kd = (k * jnp.exp(e_ref[c:2 * c, ls])).astype(BF16)
        st_ref[hd] = st * jnp.exp(gcum[c - 1:c, :]) + lax.dot_general(
            vb, kd, (((0,), (0,)), ((), ())), preferred_element_type=F32)
        ms = jnp.mean(o * o, axis=-1, keepdims=True)
        o = o * lax.rsqrt(ms + EPS) * ng_ref[...] * jax.nn.sigmoid(g_ref[:, ls])
        o_ref[:, ls] = o.astype(o_ref.dtype)


def _hgrn_mixer(proj, lb, norm_g, batch, seq):
    t = proj.shape[0]
    c = min(HGRN_C, seq)
    nc = seq // c
    m_all, masks = _hgrn_consts(c)
    m_all = jnp.asarray(m_all, BF16)
    masks = jnp.asarray(masks, F32)

    def col(j):
        return pl.BlockSpec((c, HGRN_WIDTH), lambda b, i: (b * nc + i, j))

    return pl.pallas_call(
        functools.partial(_hgrn_kernel, c=c),
        grid=(batch, nc),
        in_specs=[
            col(1), col(2), col(3), col(4),
            pl.BlockSpec((1, HGRN_WIDTH), lambda b, i: (0, 0)),
            pl.BlockSpec((1, HGRN_DIM), lambda b, i: (0, 0)),
            pl.BlockSpec(m_all.shape, lambda b, i: (0, 0)),
            pl.BlockSpec(masks.shape, lambda b, i: (0, 0, 0)),
        ],
        out_specs=pl.BlockSpec((c, HGRN_WIDTH), lambda b, i: (b * nc + i, 0)),
        out_shape=jax.ShapeDtypeStruct((t, HGRN_WIDTH), BF16),
        scratch_shapes=[
            pltpu.VMEM(((masks.shape[0] + 1) * c, HGRN_WIDTH), F32),
            pltpu.VMEM((HGRN_HEADS, HGRN_DIM, HGRN_DIM), F32),
        ],
        compiler_params=_params(("parallel", "arbitrary"), VMEM_LIMIT),
        name="hgrn2_mixer",
    )(proj, proj, proj, proj, lb.reshape(1, HGRN_WIDTH), norm_g.reshape(1, HGRN_DIM), m_all, masks)


def _out_proj_kernel(h_ref, a_ref, s_ref, r_ref, w_ref, gn_ref, rh_ref, rl_ref,
                     ho_ref, hn_ref, eid_ref, gate_ref):
    s_lo, r_lo = ATTN_WIDTH, ATTN_WIDTH + S5_WIDTH
    acc = jnp.dot(a_ref[...], w_ref[0:s_lo, :], preferred_element_type=F32)
    acc = acc + jnp.dot(s_ref[...], w_ref[s_lo:r_lo, :], preferred_element_type=F32)
    acc = acc + jnp.dot(r_ref[...], w_ref[r_lo:, :], preferred_element_type=F32)
    h = h_ref[...] + acc
    ho_ref[...] = h
    ms = jnp.mean(h * h, axis=-1, keepdims=True)
    hn = h * lax.rsqrt(ms + EPS) * gn_ref[...]
    half = hn.shape[1] // 2
    hn_ref[...] = _pack_bf16_pair(hn[:, :half], hn[:, half:])

    hh = hn.astype(BF16)
    hl = (hn - hh.astype(F32)).astype(BF16)
    logits = (jnp.dot(hh, rh_ref[...], preferred_element_type=F32)
              + jnp.dot(hh, rl_ref[...], preferred_element_type=F32)
              + jnp.dot(hl, rh_ref[...], preferred_element_type=F32))
    lane = lax.broadcasted_iota(jnp.int32, logits.shape, 1).astype(F32)
    neg = -jnp.inf
    big = 1e9
    is_group = jnp.where(lane >= N_EXPERTS, jnp.where(lane < N_EXPERTS + N_GROUPS, 1.0, 0.0), 0.0)
    gl = jnp.where(is_group > 0, logits, neg)
    gmax = jnp.max(gl, axis=-1, keepdims=True)
    g_lane = jnp.min(jnp.where(gl == gmax, lane, big), axis=-1, keepdims=True)
    p_g = 1.0 / jnp.sum(jnp.exp(gl - gmax), axis=-1, keepdims=True)
    lo_lane = (g_lane - N_EXPERTS) * EXPERTS_PER_GROUP
    in_group = jnp.where(lane >= lo_lane, jnp.where(lane < lo_lane + EXPERTS_PER_GROUP, 1.0, 0.0), 0.0)
    el = jnp.where(in_group > 0, logits, neg)
    t1 = jnp.max(el, axis=-1, keepdims=True)
    i1 = jnp.min(jnp.where(el == t1, lane, big), axis=-1, keepdims=True)
    el2 = jnp.where(lane == i1, neg, el)
    t2 = jnp.max(el2, axis=-1, keepdims=True)
    i2 = jnp.min(jnp.where(el2 == t2, lane, big), axis=-1, keepdims=True)
    e21 = jnp.exp(t2 - t1)
    g1 = p_g / (1.0 + e21)
    g2 = p_g * e21 / (1.0 + e21)
    eid_ref[...] = jnp.where(lane == 0, i1, jnp.where(lane == 1, i2, 0.0)).astype(jnp.int32)
    gate_ref[...] = jnp.where(lane == 0, g1, jnp.where(lane == 1, g2, 0.0))


def _out_proj(h, attn, s5, hg, w_out, gn, r_hi, r_lo):
    t, d = h.shape
    tm = min(OUT_TM, t)

    def rows(w):
        return pl.BlockSpec((tm, w), lambda i: (i, 0))

    def full(a):
        return pl.BlockSpec(a.shape, lambda i: (0,) * a.ndim)

    return pl.pallas_call(
        _out_proj_kernel,
        grid=(t // tm,),
        in_specs=[rows(d), rows(ATTN_WIDTH), rows(S5_WIDTH), rows(HGRN_WIDTH),
                  full(w_out), pl.BlockSpec((1, d), lambda i: (0, 0)),
                  full(r_hi), full(r_lo)],
        out_specs=[rows(d), rows(d // 2), rows(LANES), rows(LANES)],
        out_shape=[jax.ShapeDtypeStruct((t, d), F32), jax.ShapeDtypeStruct((t, d // 2), jnp.uint32),
                   jax.ShapeDtypeStruct((t, LANES), jnp.int32), jax.ShapeDtypeStruct((t, LANES), F32)],
        compiler_params=_params(("parallel",), VMEM_LIMIT),
        name="out_proj_router",
    )(h, attn, s5, hg, w_out, gn.reshape(1, d), r_hi, r_lo)


def _moe_plan(eid, tm):
    flat = eid.reshape(-1)
    n_slots = flat.shape[0]
    onehot = (flat[:, None] == jnp.arange(N_EXPERTS, dtype=jnp.int32)[None, :]).astype(jnp.int32)
    csum = jnp.cumsum(onehot, axis=0)
    rank = jnp.sum(onehot * csum, axis=1) - 1
    counts = csum[-1]
    tiles = (counts + tm - 1) // tm
    tile_end = jnp.cumsum(tiles)
    tile_start = tile_end - tiles
    slot_start = jnp.cumsum(counts) - counts
    pos = jnp.sum(onehot * tile_start[None, :], axis=1) * tm + rank
    tok_sorted = jnp.argsort(flat, stable=True).astype(jnp.int32) // TOP_K
    tok_sorted = jnp.pad(tok_sorted, (0, tm))
    nt_max = (n_slots + N_EXPERTS * (tm - 1)) // tm
    n_used = tile_end[-1]
    j = jnp.minimum(jnp.arange(nt_max, dtype=jnp.int32), n_used - 1)
    tile_expert = jnp.sum((j[:, None] >= tile_end[None, :]).astype(jnp.int32), axis=1)
    onehot_t = (tile_expert[:, None] == jnp.arange(N_EXPERTS, dtype=jnp.int32)[None, :]).astype(jnp.int32)
    tile_slot0 = jnp.sum(onehot_t * (slot_start - tile_start * tm)[None, :], axis=1) + j * tm
    used = (tiles > 0).astype(jnp.int32)
    ordinal = jnp.cumsum(used) - used
    ids = jnp.arange(N_EXPERTS, dtype=jnp.int32)
    later = jnp.where((ids[None, :] > ids[:, None]) & (used[None, :] > 0), ids[None, :], N_EXPERTS)
    nxt = jnp.min(later, axis=1)
    nxt = jnp.where(nxt >= N_EXPERTS, -1, nxt)
    tile_wslot = jnp.sum(onehot_t * (ordinal % 2)[None, :], axis=1)
    tile_next = jnp.sum(onehot_t * nxt[None, :], axis=1)
    return dict(pos=pos.astype(jnp.int32), tok_sorted=tok_sorted, tile_expert=tile_expert.astype(jnp.int32),
                tile_slot0=tile_slot0.astype(jnp.int32), tile_wslot=tile_wslot.astype(jnp.int32),
                tile_next=tile_next.astype(jnp.int32), n_used=n_used.reshape(1).astype(jnp.int32),
                nt_max=nt_max)


def _expert_kernel(tok_ref, te_ref, s0_ref, ws_ref, ne_ref, nu_ref, hn_ref, wg_ref, wu_ref, wd_ref, y_ref,
                   xa_ref, xb_ref, wgf_ref, wuf_ref, wdf_ref, wgb_ref, wub_ref, wdb_ref, sem, wsem,
                   *, tm, layer):
    j = pl.program_id(0)
    n_used = nu_ref[0]
    bufs = (xa_ref, xb_ref)

    def weight_copies(expert, slot):
        return [pltpu.make_async_copy(src.at[layer, expert], dst.at[slot], wsem.at[slot])
                for src, dst in ((wg_ref, wgf_ref), (wu_ref, wuf_ref), (wd_ref, wdf_ref))]

    def row_copy(buf, r, tok):
        return pltpu.make_async_copy(hn_ref.at[pl.ds(tok, 1)], bufs[buf].at[pl.ds(r, 1)], sem.at[buf])

    def wait_gather(buf):
        def wait(r, carry):
            row_copy(buf, r, 0).wait()
            return carry

        lax.fori_loop(0, tm, wait, 0, unroll=DMA_UNROLL)

    @pl.when(j == 0)
    def _():
        for cp in weight_copies(te_ref[0], ws_ref[0]):
            cp.start()
        base = s0_ref[0]

        def issue(r, carry):
            row_copy(0, r, tok_ref[base + r]).start()
            return carry

        lax.fori_loop(0, tm, issue, 0, unroll=DMA_UNROLL)

    def tile(buf):
        wait_gather(buf)
        prev = te_ref[jnp.maximum(j - 1, 0)]

        @pl.when(jnp.logical_or(j == 0, te_ref[j] != prev))
        def _():
            slot = ws_ref[j]

            @pl.when(ne_ref[j] >= 0)
            def _():
                for cp in weight_copies(ne_ref[j], 1 - slot):
                    cp.start()

            for cp in weight_copies(te_ref[j], slot):
                cp.wait()
            wgb_ref[...] = wgf_ref[slot].astype(BF16)
            wub_ref[...] = wuf_ref[slot].astype(BF16)
            wdb_ref[...] = wdf_ref[slot].astype(BF16)

        nbase = s0_ref[jnp.minimum(j + 1, n_used - 1)]
        for r in range(tm):
            row_copy(1 - buf, r, tok_ref[nbase + r]).start()

        x_lo, x_hi = _unpack_bf16_pair(bufs[buf][...])
        x = jnp.concatenate([x_lo.astype(BF16), x_hi.astype(BF16)], axis=1)
        g = jnp.dot(x, wgb_ref[...], preferred_element_type=F32)
        u = jnp.dot(x, wub_ref[...], preferred_element_type=F32)
        hmid = (g * jax.nn.sigmoid(g) * u).astype(BF16)
        y = jnp.dot(hmid, wdb_ref[...], preferred_element_type=F32)
        half = y.shape[1] // 2
        y_ref[...] = _pack_bf16_pair(y[:, :half], y[:, half:])

        @pl.when(j == n_used - 1)
        def _():
            wait_gather(1 - buf)

    for parity in range(2):
        @pl.when(jnp.logical_and(j < n_used, lax.rem(j, 2) == parity))
        def _(parity=parity):
            tile(parity)

    @pl.when(j >= n_used)
    def _():
        y_ref[...] = jnp.zeros(y_ref.shape, y_ref.dtype)


def _moe_experts(hn, plan, w_gate, w_up, w_down, layer, tm):
    t, dp = hn.shape
    d, de = w_gate.shape[2], w_gate.shape[3]
    nt_max = plan["nt_max"]
    grid_spec = pltpu.PrefetchScalarGridSpec(
        num_scalar_prefetch=6,
        grid=(nt_max,),
        in_specs=[pl.BlockSpec(memory_space=pl.ANY)] * 4,
        out_specs=pl.BlockSpec((tm, dp), lambda j, *_: (j, 0)),
        scratch_shapes=[pltpu.VMEM((tm, dp), jnp.uint32), pltpu.VMEM((tm, dp), jnp.uint32),
                        pltpu.VMEM((2, d, de), F32), pltpu.VMEM((2, d, de), F32), pltpu.VMEM((2, de, d), F32),
                        pltpu.VMEM((d, de), BF16), pltpu.VMEM((d, de), BF16), pltpu.VMEM((de, d), BF16),
                        pltpu.SemaphoreType.DMA((2,)), pltpu.SemaphoreType.DMA((2,))],
    )
    return pl.pallas_call(
        functools.partial(_expert_kernel, tm=tm, layer=layer),
        grid_spec=grid_spec,
        out_shape=jax.ShapeDtypeStruct((nt_max * tm, dp), jnp.uint32),
        compiler_params=_params(("arbitrary",), VMEM_LIMIT),
        name="moe_experts",
    )(plan["tok_sorted"], plan["tile_expert"], plan["tile_slot0"], plan["tile_wslot"], plan["tile_next"],
      plan["n_used"], hn, w_gate, w_up, w_down)


def _combine_kernel(pos_ref, h_ref, gate_ref, gn_ref, y_ref, o_ref, buf_ref, sem, *, tm, final_norm):
    i = pl.program_id(0)
    cur = lax.rem(i, 2)

    def row_copy(b, k, r, src_row):
        return pltpu.make_async_copy(y_ref.at[pl.ds(src_row, 1)], buf_ref.at[b, k, pl.ds(r, 1)], sem.at[b])

    def start_gather(tile):
        b = lax.rem(tile, 2)

        def issue(r, carry):
            slot = (tile * tm + r) * TOP_K
            for k in range(TOP_K):
                row_copy(b, k, r, pos_ref[slot + k]).start()
            return carry

        lax.fori_loop(0, tm, issue, 0, unroll=DMA_UNROLL)

    @pl.when(i == 0)
    def _():
        start_gather(0)

    @pl.when(i + 1 < pl.num_programs(0))
    def _():
        start_gather(i + 1)

    def wait(r, carry):
        for k in range(TOP_K):
            row_copy(cur, k, r, 0).wait()
        return carry

    lax.fori_loop(0, tm, wait, 0, unroll=DMA_UNROLL)

    gates = gate_ref[...]
    y0_lo, y0_hi = _unpack_bf16_pair(buf_ref[cur, 0])
    y1_lo, y1_hi = _unpack_bf16_pair(buf_ref[cur, 1])
    g0, g1 = gates[:, 0:1], gates[:, 1:2]
    h = h_ref[...] + jnp.concatenate([g0 * y0_lo + g1 * y1_lo, g0 * y0_hi + g1 * y1_hi], axis=1)
    if final_norm:
        ms = jnp.mean(h * h, axis=-1, keepdims=True)
        h = h * lax.rsqrt(ms + EPS) * gn_ref[...]
    o_ref[...] = h


def _moe_combine(h, gates, ys, pos, final_g, final_norm):
    t, d = h.shape
    tm = min(COMB_TM, t)
    grid_spec = pltpu.PrefetchScalarGridSpec(
        num_scalar_prefetch=1,
        grid=(t // tm,),
        in_specs=[
            pl.BlockSpec((tm, d), lambda i, p: (i, 0)),
            pl.BlockSpec((tm, LANES), lambda i, p: (i, 0)),
            pl.BlockSpec((1, d), lambda i, p: (0, 0)),
            pl.BlockSpec(memory_space=pl.ANY),
        ],
        out_specs=pl.BlockSpec((tm, d), lambda i, p: (i, 0)),
        scratch_shapes=[pltpu.VMEM((2, TOP_K, tm, d // 2), jnp.uint32), pltpu.SemaphoreType.DMA((2,))],
    )
    return pl.pallas_call(
        functools.partial(_combine_kernel, tm=tm, final_norm=final_norm),
        grid_spec=grid_spec,
        out_shape=jax.ShapeDtypeStruct((t, d), F32),
        compiler_params=_params(("arbitrary",), VMEM_LIMIT),
        name="moe_combine",
    )(pos, h, gates, final_g.reshape(1, d), ys)


def _permute_w_in(w):
    a = 3 * ATTN_WIDTH
    w = jnp.concatenate([w[:, a:], w[:, :a]], axis=1)
    return jnp.pad(w, ((0, 0), (0, IN_COLS_PAD - IN_COLS))).astype(BF16)


def kernel(x, w_in, w_out, mix_norm_g, ffn_norm_g, rel_bias, diff_lambda, attn_subln_g, s5_a_re, s5_a_im, s5_log_dt, s5_b_re, s5_b_im, s5_c_re, s5_c_im, s5_d, s5_w_glu, hgrn_lb_logits, hgrn_norm_g, moe_w_group, moe_w_router, moe_w_gate, moe_w_up, moe_w_down, final_norm_g):
    batch, seq, d = x.shape
    depth = w_in.shape[0]
    t = batch * seq
    h = x.reshape(t, d)

    lb_cum = jnp.cumsum(jax.nn.softmax(hgrn_lb_logits.astype(F32), axis=0), axis=0)
    lb_all = lb_cum - lb_cum[0:1]
    attn_bias = _attn_bias_tiles(rel_bias, min(ATTN_TQ, seq), min(ATTN_TK, seq))

    for l in range(depth):
        proj = _in_proj(h, mix_norm_g[l], _permute_w_in(w_in[l]))

        lam_init = 0.8 - 0.6 * math.exp(-0.3 * l)
        lv = diff_lambda[l].astype(F32)
        lam = jnp.exp(jnp.sum(lv[0] * lv[1])) - jnp.exp(jnp.sum(lv[2] * lv[3])) + lam_init
        attn = _diff_attention(proj, attn_bias, lam, attn_subln_g[l], lam_init, batch, seq)

        w_b, w_c, tabs = _s5_prep(s5_a_re[l], s5_a_im[l], s5_log_dt[l], s5_b_re[l], s5_b_im[l],
                                  s5_c_re[l], s5_c_im[l], min(S5_TB, seq) // SUBLANES)
        s5 = _s5_mixer(proj, w_b, w_c, tabs, s5_d[l], s5_w_glu[l].astype(BF16), batch, seq)

        hg = _hgrn_mixer(proj, lb_all[l], hgrn_norm_g[l], batch, seq)

        wo = w_out[l].astype(BF16)
        w_r = jnp.concatenate([moe_w_router[l], moe_w_group[l]], axis=1).astype(F32)
        w_r = jnp.pad(w_r, ((0, 0), (0, LANES - w_r.shape[1])))
        r_hi = w_r.astype(BF16)
        r_lo = (w_r - r_hi.astype(F32)).astype(BF16)
        h, hn, eid, gates = _out_proj(h, attn, s5, hg, wo, ffn_norm_g[l], r_hi, r_lo)

        plan = _moe_plan(eid[:, :TOP_K], MOE_TM)
        ys = _moe_experts(hn, plan, moe_w_gate, moe_w_up, moe_w_down, l, MOE_TM)
        h = _moe_combine(h, gates, ys, plan["pos"], final_norm_g, final_norm=(l == depth - 1))

    return h.reshape(batch, seq, d)
```

```python
import functools
import math

import numpy as np
import jax
import jax.numpy as jnp
from jax import lax
from jax.experimental import pallas as pl
from jax.experimental.pallas import tpu as pltpu

F32 = jnp.float32
BF16 = jnp.bfloat16
EPS = 1e-6
LOG2E = 1.4426950408889634

D_MODEL = 2048
ATTN_HEADS = 6
ATTN_QK_DIM = 64
ATTN_V_DIM = 128
ATTN_WIDTH = ATTN_HEADS * ATTN_V_DIM
REL_BUCKETS = 32
REL_MAX_DIST = 128
S5_GROUPS = 40
S5_GROUP_CH = 16
S5_STATE = 64
S5_WIDTH = S5_GROUPS * S5_GROUP_CH
S5_LANES = S5_GROUPS * S5_STATE
HGRN_HEADS = 5
HGRN_DIM = 128
HGRN_WIDTH = HGRN_HEADS * HGRN_DIM
N_GROUPS = 4
EXPERTS_PER_GROUP = 8
N_EXPERTS = N_GROUPS * EXPERTS_PER_GROUP
TOP_K = 2
D_EXPERT = 512

LANES = 128
SUBLANES = 8

IN_COLS = 3 * ATTN_WIDTH + S5_WIDTH + 4 * HGRN_WIDTH
PROJ_TN = 512
IN_COLS_PAD = -(-IN_COLS // PROJ_TN) * PROJ_TN
ATTN_COL_BLOCK = (S5_WIDTH + 4 * HGRN_WIDTH) // LANES

PROJ_TM = 1024
ATTN_TQ = 512
ATTN_TK = 512
ATTN_NEAR = 2
ATTN_ROWS = 32
S5_TB = 256
S5_CHUNK = 640
S5_BLK = 128
HGRN_C = 128
OUT_TM = 256
MOE_TM = 256
COMB_TM = 128
DMA_UNROLL = 8
WEIGHT_DMA_PRIORITY = 1

VMEM_LIMIT = 56 * 1024 * 1024


def _params(sem, vmem=None):
    return pltpu.CompilerParams(dimension_semantics=sem, vmem_limit_bytes=vmem)


def _pack_bf16_pair(lo, hi):
    lo_bits = lax.bitcast_convert_type(lo.astype(BF16).astype(F32), jnp.uint32) >> 16
    hi_bits = lax.bitcast_convert_type(hi.astype(BF16).astype(F32), jnp.uint32) & jnp.uint32(0xFFFF0000)
    return lo_bits | hi_bits


def _unpack_bf16_pair(w):
    return (lax.bitcast_convert_type(w << 16, F32),
            lax.bitcast_convert_type(w & jnp.uint32(0xFFFF0000), F32))


def _in_proj_kernel(x_ref, g_ref, w_ref, o_ref, xn_ref):
    @pl.when(pl.program_id(1) == 0)
    def _():
        x = x_ref[...]
        ms = jnp.mean(x * x, axis=-1, keepdims=True)
        xn_ref[...] = (x * lax.rsqrt(ms + EPS) * g_ref[...]).astype(BF16)

    o_ref[...] = jnp.dot(xn_ref[...], w_ref[...], preferred_element_type=F32)


def _in_proj(h, g, w):
    t, d = h.shape
    n = w.shape[1]
    tm = min(PROJ_TM, t)
    return pl.pallas_call(
        _in_proj_kernel,
        grid=(t // tm, n // PROJ_TN),
        in_specs=[
            pl.BlockSpec((tm, d), lambda i, j: (i, 0)),
            pl.BlockSpec((1, d), lambda i, j: (0, 0)),
            pl.BlockSpec((d, PROJ_TN), lambda i, j: (0, j)),
        ],
        out_specs=pl.BlockSpec((tm, PROJ_TN), lambda i, j: (i, j)),
        out_shape=jax.ShapeDtypeStruct((t, n), F32),
        scratch_shapes=[pltpu.VMEM((tm, d), BF16)],
        compiler_params=_params(("parallel", "arbitrary"), VMEM_LIMIT),
        name="in_proj",
    )(h, g.reshape(1, d), w)


def _attn_kernel(lam_ref, q_ref, k_ref, v_ref, bias_ref, g_ref, o_ref,
                 kb_ref, vb_ref, qz_ref, m_ref, a_ref, acc_ref, s0_ref, s1_ref, p_ref, *, tq, tk, out_scale):
    qi = pl.program_id(2)
    ratio = tq // tk

    @pl.when(qi == 0)
    def _():
        kb_ref[...] = k_ref[...].astype(BF16)
        vb_ref[:, 0:LANES] = v_ref[...].astype(BF16)
        vb_ref[:, LANES:2 * LANES] = jnp.ones((vb_ref.shape[0], LANES), BF16)

    q = q_ref[...] * (ATTN_QK_DIM ** -0.5 * LOG2E)
    lane = lax.broadcasted_iota(jnp.int32, q.shape, 1)
    qz_ref[0:tq, :] = jnp.where(lane < ATTN_QK_DIM, q, 0.0).astype(BF16)
    qz_ref[tq:2 * tq, :] = jnp.where(lane >= ATTN_QK_DIM, q, 0.0).astype(BF16)
    m_ref[...] = jnp.full(m_ref.shape, -jnp.inf, F32)
    acc_ref[...] = jnp.zeros(acc_ref.shape, F32)

    n_col = tk // LANES
    groups = [slice(g * ATTN_ROWS, (g + 1) * ATTN_ROWS) for g in range(2 * tq // ATTN_ROWS)]

    def scores(j, s_ref):
        k = kb_ref[pl.ds(pl.multiple_of(j * tk, tk), tk), :]
        s_ref[...] = lax.dot_general(qz_ref[...], k, (((1,), (1,)), ((), ())), preferred_element_type=F32)

    def softmax_pv(j, s_ref, biased):
        if biased:
            kind = jnp.clip(j - ratio * qi + ATTN_NEAR, 0, ATTN_NEAR + ratio - 1)
        for rows in groups:
            cols = [s_ref[rows, i * LANES:(i + 1) * LANES] for i in range(n_col)]
            if biased:
                b0 = rows.start % tq
                cols = [x + bias_ref[0, kind, b0:b0 + ATTN_ROWS, i * LANES:(i + 1) * LANES]
                        for i, x in enumerate(cols)]
                for i, x in enumerate(cols):
                    s_ref[rows, i * LANES:(i + 1) * LANES] = x
            mx = functools.reduce(jnp.maximum, cols)
            m_old = m_ref[rows, :]
            m_new = jnp.maximum(m_old, jnp.max(mx, axis=-1, keepdims=True))
            m_ref[rows, :] = m_new
            a_ref[rows, :] = jnp.exp2(m_old - m_new)
        for rows in groups:
            m_new = m_ref[rows, :]
            for i in range(n_col):
                p_ref[rows, i * LANES:(i + 1) * LANES] = jnp.exp2(
                    s_ref[rows, i * LANES:(i + 1) * LANES] - m_new).astype(BF16)
        v = vb_ref[pl.ds(pl.multiple_of(j * tk, tk), tk), :]
        alpha = a_ref[...]
        acc_ref[...] = (jnp.concatenate([alpha, alpha], axis=1) * acc_ref[...]
                        + jnp.dot(p_ref[...], v, preferred_element_type=F32))

    n_tiles = ratio * (qi + 1)
    first = lax.rem(n_tiles, 2)
    n_far = jnp.maximum(ratio * qi - ATTN_NEAR + 1, 0)
    n_far_pairs = jnp.maximum(n_far - first, 0) // 2
    n_pairs = (n_tiles - first) // 2

    @pl.when(first == 1)
    def _():
        scores(0, s1_ref)
        scores(jnp.minimum(1, n_tiles - 1), s0_ref)
        softmax_pv(0, s1_ref, True)

    @pl.when(first == 0)
    def _():
        scores(0, s0_ref)

    def pair_body(jj, carry, biased):
        j = first + 2 * jj
        scores(j + 1, s1_ref)
        softmax_pv(j, s0_ref, biased)
        scores(jnp.minimum(j + 2, n_tiles - 1), s0_ref)
        softmax_pv(j + 1, s1_ref, biased)
        return carry

    lax.fori_loop(0, n_far_pairs, functools.partial(pair_body, biased=False), 0)
    lax.fori_loop(n_far_pairs, n_pairs, functools.partial(pair_body, biased=True), 0)

    inv_l = 1.0 / acc_ref[:, LANES:2 * LANES]
    o1 = acc_ref[0:tq, 0:LANES] * inv_l[0:tq]
    o2 = acc_ref[tq:2 * tq, 0:LANES] * inv_l[tq:2 * tq]
    o = o1 - lam_ref[0] * o2
    ms = jnp.mean(o * o, axis=-1, keepdims=True)
    o_ref[...] = (o * lax.rsqrt(ms + EPS) * g_ref[...] * out_scale).astype(o_ref.dtype)


def _t5_bucket(dist):
    n = jnp.maximum(dist, 0)
    max_exact = REL_BUCKETS // 2
    large = max_exact + (jnp.log(jnp.maximum(n, 1).astype(F32) / max_exact)
                         / math.log(REL_MAX_DIST / max_exact) * (REL_BUCKETS - max_exact)).astype(jnp.int32)
    large = jnp.minimum(large, REL_BUCKETS - 1)
    return jnp.where(n < max_exact, n, large)


def _attn_bias_tiles(rel_bias, tq, tk):
    table = rel_bias.astype(F32)
    far = table[REL_BUCKETS - 1]
    r = jnp.arange(tq)[:, None]
    c = jnp.arange(tk)[None, :]

    def lookup(dist):
        bucket = _t5_bucket(dist)
        out = jnp.zeros((ATTN_HEADS,) + dist.shape, F32)
        for b in range(REL_BUCKETS):
            out = jnp.where((bucket == b)[None], table[b][:, None, None], out)
        return out

    tiles = [jnp.zeros((ATTN_HEADS, tq, tk), F32)]
    for rel in range(-ATTN_NEAR + 1, tq // tk):
        dist = r - c - rel * tk
        tiles.append(jnp.where((dist >= 0)[None], (lookup(dist) - far[:, None, None]) * LOG2E, -1e30))
    return jnp.stack(tiles, axis=1)


def _diff_attention(proj, bias, lam, subln_g, lam_init, batch, seq):
    t = proj.shape[0]
    tq = min(ATTN_TQ, seq)
    tk = min(ATTN_TK, seq)
    ratio = tq // tk
    assert (ATTN_NEAR - 1) * tk >= REL_MAX_DIST and ratio * tk == tq
    nq = seq // tq
    kernel = functools.partial(_attn_kernel, tq=tq, tk=tk, out_scale=1.0 - lam_init)
    qb, kb, vb = ATTN_COL_BLOCK, ATTN_COL_BLOCK + ATTN_HEADS, ATTN_COL_BLOCK + 2 * ATTN_HEADS
    return pl.pallas_call(
        kernel,
        grid=(batch, ATTN_HEADS, nq),
        in_specs=[
            pl.BlockSpec(memory_space=pltpu.SMEM),
            pl.BlockSpec((tq, LANES), lambda b, h, i: (b * nq + i, qb + h)),
            pl.BlockSpec((seq, LANES), lambda b, h, i: (b, kb + h)),
            pl.BlockSpec((seq, LANES), lambda b, h, i: (b, vb + h)),
            pl.BlockSpec((1,) + bias.shape[1:], lambda b, h, i: (h, 0, 0, 0)),
            pl.BlockSpec((1, LANES), lambda b, h, i: (0, 0)),
        ],
        out_specs=pl.BlockSpec((tq, LANES), lambda b, h, i: (b * nq + i, h)),
        out_shape=jax.ShapeDtypeStruct((t, ATTN_WIDTH), BF16),
        scratch_shapes=[
            pltpu.VMEM((seq, LANES), BF16),
            pltpu.VMEM((seq, 2 * LANES), BF16),
            pltpu.VMEM((2 * tq, LANES), BF16),
            pltpu.VMEM((2 * tq, LANES), F32),
            pltpu.VMEM((2 * tq, LANES), F32),
            pltpu.VMEM((2 * tq, 2 * LANES), F32),
            pltpu.VMEM((2 * tq, tk), F32),
            pltpu.VMEM((2 * tq, tk), F32),
            pltpu.VMEM((2 * tq, tk), BF16),
        ],
        compiler_params=_params(("parallel", "parallel", "arbitrary"), VMEM_LIMIT),
        name="diff_attn",
    )(lam.reshape(1), proj, proj, proj, bias, subln_g.reshape(1, LANES))


def _s5_kernel(u_ref, perm_ref, permt_ref, wb_ref, wc_ref, pw_ref, d_ref, wglu_ref, o_ref,
               xr_ref, xi_ref, cr_ref, ci_ref, kr_ref, ki_ref, *, tb):
    @pl.when(pl.program_id(1) == 0)
    def _():
        cr_ref[...] = jnp.zeros(cr_ref.shape, F32)
        ci_ref[...] = jnp.zeros(ci_ref.shape, F32)

    un = u_ref[...]
    uh = un.astype(BF16)
    ul = (un - uh.astype(F32)).astype(BF16)
    up = jnp.dot(perm_ref[...], jnp.concatenate([uh, ul], axis=1), preferred_element_type=F32)
    ub = up[:, :S5_WIDTH].astype(BF16)
    u = up[:, :S5_WIDTH] + up[:, S5_WIDTH:]
    nblk = S5_WIDTH // S5_BLK
    sw = S5_LANES // nblk
    tpb = sw // LANES
    for i in range(nblk):
        bu = jnp.dot(ub[:, i * S5_BLK:(i + 1) * S5_BLK], wb_ref[i], preferred_element_type=F32)
        for k in range(tpb):
            xr_ref[i * tpb + k] = bu[:, k * LANES:(k + 1) * LANES]
            xi_ref[i * tpb + k] = bu[:, sw + k * LANES:sw + (k + 1) * LANES]

    sl = tb // SUBLANES
    for c in range(S5_LANES // S5_CHUNK):
        tiles = range(c * S5_CHUNK // LANES, (c + 1) * S5_CHUNK // LANES)
        lss = [slice(n * LANES, (n + 1) * LANES) for n in tiles]
        a = [(pw_ref[0, 0, :, ls], pw_ref[1, 0, :, ls]) for ls in lss]

        def local_step(t, carry, tiles=tiles, a=a):
            rows = pl.ds(pl.multiple_of(t * SUBLANES, SUBLANES), SUBLANES)
            out = []
            for n, (ar, ai), (pr, pi) in zip(tiles, a, carry):
                xr = ar * pr - ai * pi + xr_ref[n, rows, :]
                xi = ar * pi + ai * pr + xi_ref[n, rows, :]
                xr_ref[n, rows, :] = xr
                xi_ref[n, rows, :] = xi
                out.append((xr, xi))
            return tuple(out)

        zero = jnp.zeros((SUBLANES, LANES), F32)
        ends = lax.fori_loop(0, sl, local_step, tuple((zero, zero) for _ in tiles), unroll=2)

        for ls, (er, ei) in zip(lss, ends):
            lr, li = pw_ref[0, sl - 1, 0:1, ls], pw_ref[1, sl - 1, 0:1, ls]
            kr, ki = cr_ref[:, ls], ci_ref[:, ls]
            for sub in range(SUBLANES):
                kr_ref[sub:sub + 1, ls] = kr
                ki_ref[sub:sub + 1, ls] = ki
                kr, ki = (er[sub:sub + 1, :] + lr * kr - li * ki, ei[sub:sub + 1, :] + lr * ki + li * kr)
            cr_ref[:, ls] = kr
            ci_ref[:, ls] = ki

        k8 = [(kr_ref[:, ls], ki_ref[:, ls]) for ls in lss]

        def fix_step(t, carry, tiles=tiles, lss=lss, k8=k8):
            rows = pl.ds(pl.multiple_of(t * SUBLANES, SUBLANES), SUBLANES)
            for n, ls, (kr8, ki8) in zip(tiles, lss, k8):
                pr = pw_ref[0, t, :, ls]
                pi = pw_ref[1, t, :, ls]
                xr_ref[n, rows, :] = xr_ref[n, rows, :] + pr * kr8 - pi * ki8
                xi_ref[n, rows, :] = xi_ref[n, rows, :] + pr * ki8 + pi * kr8
            return carry

        lax.fori_loop(0, sl, fix_step, 0, unroll=2)

    ys = []
    for i in range(nblk):
        xc = jnp.concatenate([xr_ref[i * tpb + k] for k in range(tpb)]
                             + [xi_ref[i * tpb + k] for k in range(tpb)], axis=1)
        ys.append(jnp.dot(xc.astype(BF16), wc_ref[i], preferred_element_type=F32))
    y = jnp.concatenate(ys, axis=1) + d_ref[...] * u
    z = 0.5 * y * (1.0 + jnp.tanh(math.sqrt(2.0 / math.pi) * (y + 0.044715 * (y * y * y))))
    gate = jax.nn.sigmoid(jnp.dot(z.astype(BF16), wglu_ref[...], preferred_element_type=F32))
    o_ref[...] = jnp.dot(permt_ref[...], (z * gate).astype(BF16),
                         preferred_element_type=F32).astype(o_ref.dtype)


def _s5_prep(a_re, a_im, log_dt, b_re, b_im, c_re, c_im, steps):
    lr = jnp.minimum(a_re.astype(F32), -1e-4)
    li = a_im.astype(F32)
    dt = jnp.exp(log_dt.astype(F32))[:, None]
    mag = jnp.exp(lr * dt)
    ab_re = mag * jnp.cos(li * dt)
    ab_im = mag * jnp.sin(li * dt)
    den = lr * lr + li * li
    nr = ab_re - 1.0
    ni = ab_im
    f_re = (nr * lr + ni * li) / den
    f_im = (ni * lr - nr * li) / den
    br = b_re.astype(F32)
    bi = b_im.astype(F32)
    bb_re = f_re[..., None] * br - f_im[..., None] * bi
    bb_im = f_re[..., None] * bi + f_im[..., None] * br

    nblk = S5_WIDTH // S5_BLK
    gpb = S5_GROUPS // nblk
    eye = jnp.eye(gpb, dtype=F32)

    def bmat(bb):
        bb = bb.reshape(nblk, gpb, S5_STATE, S5_GROUP_CH)
        m = jnp.einsum('ignc,gh->igchn', bb, eye)
        return m.reshape(nblk, gpb * S5_GROUP_CH, gpb * S5_STATE)

    def cmat(cc):
        cc = cc.astype(F32).reshape(nblk, gpb, S5_GROUP_CH, S5_STATE)
        m = jnp.einsum('igcn,gh->ignhc', cc, eye)
        return m.reshape(nblk, gpb * S5_STATE, gpb * S5_GROUP_CH)

    w_b = jnp.concatenate([bmat(bb_re), bmat(bb_im)], axis=2).astype(BF16)
    w_c = jnp.concatenate([cmat(c_re), -cmat(c_im)], axis=1).astype(BF16)

    j = jnp.arange(1, steps + 1, dtype=F32)[:, None, None]
    m = jnp.exp(lr * dt * j)
    pows = jnp.stack([(m * jnp.cos(li * dt * j)).reshape(steps, 1, -1),
                      (m * jnp.sin(li * dt * j)).reshape(steps, 1, -1)])
    pows = jnp.broadcast_to(pows, (2, steps, SUBLANES, S5_LANES))
    return w_b, w_c, pows


def _s5_mixer(proj, w_b, w_c, tabs, d_skip, w_glu, batch, seq):
    t = proj.shape[0]
    tb = min(S5_TB, seq)
    nt = seq // tb
    sl = tb // SUBLANES
    r = np.arange(tb)
    perm = np.zeros((tb, tb), np.float32)
    perm[r, (r % SUBLANES) * sl + r // SUBLANES] = 1.0
    perm_t = jnp.asarray(perm.T, BF16)
    perm = jnp.asarray(perm, BF16)
    return pl.pallas_call(
        functools.partial(_s5_kernel, tb=tb),
        grid=(batch, nt),
        in_specs=[
            pl.BlockSpec((tb, S5_WIDTH), lambda b, i: (b * nt + i, 0)),
            pl.BlockSpec((tb, tb), lambda b, i: (0, 0)),
            pl.BlockSpec((tb, tb), lambda b, i: (0, 0)),
            pl.BlockSpec(w_b.shape, lambda b, i: (0, 0, 0)),
            pl.BlockSpec(w_c.shape, lambda b, i: (0, 0, 0)),
            pl.BlockSpec(tabs.shape, lambda b, i: (0, 0, 0, 0)),
            pl.BlockSpec((1, S5_WIDTH), lambda b, i: (0, 0)),
            pl.BlockSpec((S5_WIDTH, S5_WIDTH), lambda b, i: (0, 0)),
        ],
        out_specs=pl.BlockSpec((tb, S5_WIDTH), lambda b, i: (b * nt + i, 0)),
        out_shape=jax.ShapeDtypeStruct((t, S5_WIDTH), BF16),
        scratch_shapes=[
            pltpu.VMEM((S5_LANES // LANES, tb, LANES), F32),
            pltpu.VMEM((S5_LANES // LANES, tb, LANES), F32),
            pltpu.VMEM((1, S5_LANES), F32),
            pltpu.VMEM((1, S5_LANES), F32),
            pltpu.VMEM((SUBLANES, S5_LANES), F32),
            pltpu.VMEM((SUBLANES, S5_LANES), F32),
        ],
        compiler_params=_params(("parallel", "arbitrary"), VMEM_LIMIT),
        name="s5_mixer",
    )(proj, perm, perm_t, w_b, w_c, tabs, d_skip.reshape(1, S5_WIDTH), w_glu)


def _hgrn_consts(c):
    t = np.arange(c)[:, None]
    u = np.arange(c)[None, :]
    mats = [u <= t]
    masks = [np.eye(c, dtype=bool)]
    h = 1
    while h < c:
        mid = (t // (2 * h)) * 2 * h + h
        if 2 * h < SUBLANES:
            mats.append(np.where(t >= mid, (u >= mid) & (u <= t), (u > t) & (u < mid)))
        masks.append(((u // (2 * h)) == (t // (2 * h))) & (t >= mid) & (u < mid))
        h *= 2
    return (np.concatenate(mats, axis=0).astype(np.float32), np.stack(masks).astype(np.float32))


def _hgrn_kernel(q_ref, f_ref, i_ref, g_ref, lb_ref, ng_ref, mall_ref, mask_ref, o_ref,
                 e_ref, st_ref, *, c):
    @pl.when(pl.program_id(1) == 0)
    def _():
        st_ref[...] = jnp.zeros(st_ref.shape, F32)

    lb = lb_ref[...]
    fl = f_ref[...]
    la = jnp.log(lb)
    lbb = jnp.log1p(-lb) + (jnp.minimum(fl, 0.0) - jnp.log1p(jnp.exp(-jnp.abs(fl))))
    logf = jnp.maximum(la, lbb) + jnp.log1p(jnp.exp(-jnp.abs(la - lbb)))
    hi = logf.astype(BF16)
    lo = (logf - hi.astype(F32)).astype(BF16)
    e2 = jnp.dot(mall_ref[...], jnp.concatenate([hi, lo], axis=1), preferred_element_type=F32)
    es = e2[:, :HGRN_WIDTH] + e2[:, HGRN_WIDTH:]
    n_levels = mask_ref.shape[0] - 1
    n_small = mall_ref.shape[0] // c - 1
    gall = es[0:c]
    e_ref[0:c, :] = gall
    e_ref[c:2 * c, :] = gall[c - 1:c, :] - gall
    e_ref[2 * c:(2 + n_small) * c, :] = es[c:]
    for lv in range(n_small, n_levels):
        blk = 2 << lv
        g3 = gall.reshape(c // blk, blk, HGRN_WIDTH)
        gm = jnp.broadcast_to(g3[:, blk // 2 - 1:blk // 2, :], g3.shape)
        e_ref[(2 + lv) * c:(3 + lv) * c, :] = (-jnp.abs(g3 - gm)).reshape(c, HGRN_WIDTH)

    nt = (((1,), (1,)), ((), ()))
    for hd in range(HGRN_HEADS):
        ls = slice(hd * HGRN_DIM, (hd + 1) * HGRN_DIM)
        qr = q_ref[:, ls]
        q = qr * jax.nn.sigmoid(qr) * (HGRN_DIM ** -0.5)
        k = (1.0 - lb[:, ls]) * jax.nn.sigmoid(-fl[:, ls])
        vb = i_ref[:, ls].astype(BF16)
        p = lax.dot_general(q.astype(BF16), k.astype(BF16), nt, preferred_element_type=F32) * mask_ref[0]
        for lv in range(n_levels):
            fac = jnp.exp(e_ref[(2 + lv) * c:(3 + lv) * c, ls])
            s = lax.dot_general((q * fac).astype(BF16), (k * fac).astype(BF16), nt,
                                preferred_element_type=F32)
            p = p + s * mask_ref[lv + 1]
        o = jnp.dot(p.astype(BF16), vb, preferred_element_type=F32)
        gcum = e_ref[0:c, ls]
        st = st_ref[hd]
        o = o + lax.dot_general((q * jnp.exp(gcum)).astype(BF16), st.astype(BF16), nt,
                                preferred_element_type=F32)
        kd = (k * jnp.exp(e_ref[c:2 * c, ls])).astype(BF16)
        st_ref[hd] = st * jnp.exp(gcum[c - 1:c, :]) + lax.dot_general(
            vb, kd, (((0,), (0,)), ((), ())), preferred_element_type=F32)
        ms = jnp.mean(o * o, axis=-1, keepdims=True)
        o = o * lax.rsqrt(ms + EPS) * ng_ref[...] * jax.nn.sigmoid(g_ref[:, ls])
        o_ref[:, ls] = o.astype(o_ref.dtype)


def _hgrn_mixer(proj, lb, norm_g, batch, seq):
    t = proj.shape[0]
    c = min(HGRN_C, seq)
    nc = seq // c
    m_all, masks = _hgrn_consts(c)
    m_all = jnp.asarray(m_all, BF16)
    masks = jnp.asarray(masks, F32)

    def col(j):
        return pl.BlockSpec((c, HGRN_WIDTH), lambda b, i: (b * nc + i, j))

    return pl.pallas_call(
        functools.partial(_hgrn_kernel, c=c),
        grid=(batch, nc),
        in_specs=[
            col(1), col(2), col(3), col(4),
            pl.BlockSpec((1, HGRN_WIDTH), lambda b, i: (0, 0)),
            pl.BlockSpec((1, HGRN_DIM), lambda b, i: (0, 0)),
            pl.BlockSpec(m_all.shape, lambda b, i: (0, 0)),
            pl.BlockSpec(masks.shape, lambda b, i: (0, 0, 0)),
        ],
        out_specs=pl.BlockSpec((c, HGRN_WIDTH), lambda b, i: (b * nc + i, 0)),
        out_shape=jax.ShapeDtypeStruct((t, HGRN_WIDTH), BF16),
        scratch_shapes=[
            pltpu.VMEM(((masks.shape[0] + 1) * c, HGRN_WIDTH), F32),
            pltpu.VMEM((HGRN_HEADS, HGRN_DIM, HGRN_DIM), F32),
        ],
        compiler_params=_params(("parallel", "arbitrary"), VMEM_LIMIT),
        name="hgrn2_mixer",
    )(proj, proj, proj, proj, lb.reshape(1, HGRN_WIDTH), norm_g.reshape(1, HGRN_DIM), m_all, masks)


def _out_proj_kernel(h_ref, a_ref, s_ref, r_ref, w_ref, gn_ref, rh_ref, rl_ref,
                     ho_ref, hn_ref, eid_ref, gate_ref):
    s_lo, r_lo = ATTN_WIDTH, ATTN_WIDTH + S5_WIDTH
    acc = jnp.dot(a_ref[...], w_ref[0:s_lo, :], preferred_element_type=F32)
    acc = acc + jnp.dot(s_ref[...], w_ref[s_lo:r_lo, :], preferred_element_type=F32)
    acc = acc + jnp.dot(r_ref[...], w_ref[r_lo:, :], preferred_element_type=F32)
    h = h_ref[...] + acc
    ho_ref[...] = h
    ms = jnp.mean(h * h, axis=-1, keepdims=True)
    hn = h * lax.rsqrt(ms + EPS) * gn_ref[...]
    half = hn.shape[1] // 2
    hn_ref[...] = _pack_bf16_pair(hn[:, :half], hn[:, half:])

    hh = hn.astype(BF16)
    hl = (hn - hh.astype(F32)).astype(BF16)
    logits = (jnp.dot(hh, rh_ref[...], preferred_element_type=F32)
              + jnp.dot(hh, rl_ref[...], preferred_element_type=F32)
              + jnp.dot(hl, rh_ref[...], preferred_element_type=F32))
    lane = lax.broadcasted_iota(jnp.int32, logits.shape, 1).astype(F32)
    neg = -jnp.inf
    big = 1e9
    is_group = jnp.where(lane >= N_EXPERTS, jnp.where(lane < N_EXPERTS + N_GROUPS, 1.0, 0.0), 0.0)
    gl = jnp.where(is_group > 0, logits, neg)
    gmax = jnp.max(gl, axis=-1, keepdims=True)
    g_lane = jnp.min(jnp.where(gl == gmax, lane, big), axis=-1, keepdims=True)
    p_g = 1.0 / jnp.sum(jnp.exp(gl - gmax), axis=-1, keepdims=True)
    lo_lane = (g_lane - N_EXPERTS) * EXPERTS_PER_GROUP
    in_group = jnp.where(lane >= lo_lane, jnp.where(lane < lo_lane + EXPERTS_PER_GROUP, 1.0, 0.0), 0.0)
    el = jnp.where(in_group > 0, logits, neg)
    t1 = jnp.max(el, axis=-1, keepdims=True)
    i1 = jnp.min(jnp.where(el == t1, lane, big), axis=-1, keepdims=True)
    el2 = jnp.where(lane == i1, neg, el)
    t2 = jnp.max(el2, axis=-1, keepdims=True)
    i2 = jnp.min(jnp.where(el2 == t2, lane, big), axis=-1, keepdims=True)
    e21 = jnp.exp(t2 - t1)
    g1 = p_g / (1.0 + e21)
    g2 = p_g * e21 / (1.0 + e21)
    eid_ref[...] = jnp.where(lane == 0, i1, jnp.where(lane == 1, i2, 0.0)).astype(jnp.int32)
    gate_ref[...] = jnp.where(lane == 0, g1, jnp.where(lane == 1, g2, 0.0))


def _out_proj(h, attn, s5, hg, w_out, gn, r_hi, r_lo):
    t, d = h.shape
    tm = min(OUT_TM, t)

    def rows(w):
        return pl.BlockSpec((tm, w), lambda i: (i, 0))

    def full(a):
        return pl.BlockSpec(a.shape, lambda i: (0,) * a.ndim)

    return pl.pallas_call(
        _out_proj_kernel,
        grid=(t // tm,),
        in_specs=[rows(d), rows(ATTN_WIDTH), rows(S5_WIDTH), rows(HGRN_WIDTH),
                  full(w_out), pl.BlockSpec((1, d), lambda i: (0, 0)),
                  full(r_hi), full(r_lo)],
        out_specs=[rows(d), rows(d // 2), rows(LANES), rows(LANES)],
        out_shape=[jax.ShapeDtypeStruct((t, d), F32), jax.ShapeDtypeStruct((t, d // 2), jnp.uint32),
                   jax.ShapeDtypeStruct((t, LANES), jnp.int32), jax.ShapeDtypeStruct((t, LANES), F32)],
        compiler_params=_params(("parallel",), VMEM_LIMIT),
        name="out_proj_router",
    )(h, attn, s5, hg, w_out, gn.reshape(1, d), r_hi, r_lo)


def _moe_plan(eid, tm):
    flat = eid.reshape(-1)
    n_slots = flat.shape[0]
    onehot = (flat[:, None] == jnp.arange(N_EXPERTS, dtype=jnp.int32)[None, :]).astype(jnp.int32)
    csum = jnp.cumsum(onehot, axis=0)
    rank = jnp.sum(onehot * csum, axis=1) - 1
    counts = csum[-1]
    tiles = (counts + tm - 1) // tm
    tile_end = jnp.cumsum(tiles)
    tile_start = tile_end - tiles
    slot_start = jnp.cumsum(counts) - counts
    pos = jnp.sum(onehot * tile_start[None, :], axis=1) * tm + rank
    tok_sorted = jnp.argsort(flat, stable=True).astype(jnp.int32) // TOP_K
    tok_sorted = jnp.pad(tok_sorted, (0, tm))
    nt_max = (n_slots + N_EXPERTS * (tm - 1)) // tm
    n_used = tile_end[-1]
    j = jnp.minimum(jnp.arange(nt_max, dtype=jnp.int32), n_used - 1)
    tile_expert = jnp.sum((j[:, None] >= tile_end[None, :]).astype(jnp.int32), axis=1)
    onehot_t = (tile_expert[:, None] == jnp.arange(N_EXPERTS, dtype=jnp.int32)[None, :]).astype(jnp.int32)
    tile_slot0 = jnp.sum(onehot_t * (slot_start - tile_start * tm)[None, :], axis=1) + j * tm
    used = (tiles > 0).astype(jnp.int32)
    ordinal = jnp.cumsum(used) - used
    ids = jnp.arange(N_EXPERTS, dtype=jnp.int32)
    later = jnp.where((ids[None, :] > ids[:, None]) & (used[None, :] > 0), ids[None, :], N_EXPERTS)
    nxt = jnp.min(later, axis=1)
    nxt = jnp.where(nxt >= N_EXPERTS, -1, nxt)
    tile_wslot = jnp.sum(onehot_t * (ordinal % 2)[None, :], axis=1)
    tile_next = jnp.sum(onehot_t * nxt[None, :], axis=1)
    return dict(pos=pos.astype(jnp.int32), tok_sorted=tok_sorted, tile_expert=tile_expert.astype(jnp.int32),
                tile_slot0=tile_slot0.astype(jnp.int32), tile_wslot=tile_wslot.astype(jnp.int32),
                tile_next=tile_next.astype(jnp.int32), n_used=n_used.reshape(1).astype(jnp.int32),
                nt_max=nt_max)


def _expert_kernel(tok_ref, te_ref, s0_ref, ws_ref, ne_ref, nu_ref, hn_ref, wg_ref, wu_ref, wd_ref, y_ref,
                   xa_ref, xb_ref, wgf_ref, wuf_ref, wdf_ref, wgb_ref, wub_ref, wdb_ref, sem, wsem,
                   *, tm, layer):
    j = pl.program_id(0)
    n_used = nu_ref[0]
    bufs = (xa_ref, xb_ref)

    def weight_copies(expert, slot):
        return [pltpu.make_async_copy(src.at[layer, expert], dst.at[slot], wsem.at[slot])
                for src, dst in ((wg_ref, wgf_ref), (wu_ref, wuf_ref), (wd_ref, wdf_ref))]

    def row_copy(buf, r, tok):
        return pltpu.make_async_copy(hn_ref.at[pl.ds(tok, 1)], bufs[buf].at[pl.ds(r, 1)], sem.at[buf])

    def wait_gather(buf):
        def wait(r, carry):
            row_copy(buf, r, 0).wait()
            return carry

        lax.fori_loop(0, tm, wait, 0, unroll=DMA_UNROLL)

    @pl.when(j == 0)
    def _():
        for cp in weight_copies(te_ref[0], ws_ref[0]):
            cp.start(priority=WEIGHT_DMA_PRIORITY)
        base = s0_ref[0]

        def issue(r, carry):
            row_copy(0, r, tok_ref[base + r]).start()
            return carry

        lax.fori_loop(0, tm, issue, 0, unroll=DMA_UNROLL)

    def tile(buf):
        wait_gather(buf)
        prev = te_ref[jnp.maximum(j - 1, 0)]

        @pl.when(jnp.logical_or(j == 0, te_ref[j] != prev))
        def _():
            slot = ws_ref[j]

            @pl.when(ne_ref[j] >= 0)
            def _():
                for cp in weight_copies(ne_ref[j], 1 - slot):
                    cp.start(priority=WEIGHT_DMA_PRIORITY)

            for cp in weight_copies(te_ref[j], slot):
                cp.wait()
            wgb_ref[...] = wgf_ref[slot].astype(BF16)
            wub_ref[...] = wuf_ref[slot].astype(BF16)
            wdb_ref[...] = wdf_ref[slot].astype(BF16)

        nbase = s0_ref[jnp.minimum(j + 1, n_used - 1)]
        for r in range(tm):
            row_copy(1 - buf, r, tok_ref[nbase + r]).start()

        x_lo, x_hi = _unpack_bf16_pair(bufs[buf][...])
        x = jnp.concatenate([x_lo.astype(BF16), x_hi.astype(BF16)], axis=1)
        g = jnp.dot(x, wgb_ref[...], preferred_element_type=F32)
        u = jnp.dot(x, wub_ref[...], preferred_element_type=F32)
        hmid = (g * jax.nn.sigmoid(g) * u).astype(BF16)
        y = jnp.dot(hmid, wdb_ref[...], preferred_element_type=F32)
        half = y.shape[1] // 2
        y_ref[...] = _pack_bf16_pair(y[:, :half], y[:, half:])

        @pl.when(j == n_used - 1)
        def _():
            wait_gather(1 - buf)

    for parity in range(2):
        @pl.when(jnp.logical_and(j < n_used, lax.rem(j, 2) == parity))
        def _(parity=parity):
            tile(parity)

    @pl.when(j >= n_used)
    def _():
        y_ref[...] = jnp.zeros(y_ref.shape, y_ref.dtype)


def _moe_experts(hn, plan, w_gate, w_up, w_down, layer, tm):
    t, dp = hn.shape
    d, de = w_gate.shape[2], w_gate.shape[3]
    nt_max = plan["nt_max"]
    grid_spec = pltpu.PrefetchScalarGridSpec(
        num_scalar_prefetch=6,
        grid=(nt_max,),
        in_specs=[pl.BlockSpec(memory_space=pl.ANY)] * 4,
        out_specs=pl.BlockSpec((tm, dp), lambda j, *_: (j, 0)),
        scratch_shapes=[pltpu.VMEM((tm, dp), jnp.uint32), pltpu.VMEM((tm, dp), jnp.uint32),
                        pltpu.VMEM((2, d, de), F32), pltpu.VMEM((2, d, de), F32), pltpu.VMEM((2, de, d), F32),
                        pltpu.VMEM((d, de), BF16), pltpu.VMEM((d, de), BF16), pltpu.VMEM((de, d), BF16),
                        pltpu.SemaphoreType.DMA((2,)), pltpu.SemaphoreType.DMA((2,))],
    )
    return pl.pallas_call(
        functools.partial(_expert_kernel, tm=tm, layer=layer),
        grid_spec=grid_spec,
        out_shape=jax.ShapeDtypeStruct((nt_max * tm, dp), jnp.uint32),
        compiler_params=_params(("arbitrary",), VMEM_LIMIT),
        name="moe_experts",
    )(plan["tok_sorted"], plan["tile_expert"], plan["tile_slot0"], plan["tile_wslot"], plan["tile_next"],
      plan["n_used"], hn, w_gate, w_up, w_down)


def _combine_kernel(pos_ref, h_ref, gate_ref, gn_ref, y_ref, o_ref, ba_ref, bb_ref, sem, *, tm, final_norm):
    i = pl.program_id(0)
    n = pl.num_programs(0)
    bufs = (ba_ref, bb_ref)

    def row_copy(b, k, r, src_row):
        return pltpu.make_async_copy(y_ref.at[pl.ds(src_row, 1)], bufs[b].at[k, pl.ds(r, 1)], sem.at[b])

    def wait_gather(b):
        def wait(r, carry):
            for k in range(TOP_K):
                row_copy(b, k, r, 0).wait()
            return carry

        lax.fori_loop(0, tm, wait, 0, unroll=DMA_UNROLL)

    @pl.when(i == 0)
    def _():
        def issue(r, carry):
            for k in range(TOP_K):
                row_copy(0, k, r, pos_ref[r * TOP_K + k]).start()
            return carry

        lax.fori_loop(0, tm, issue, 0, unroll=DMA_UNROLL)

    def tile(b):
        wait_gather(b)
        base = jnp.minimum(i + 1, n - 1) * (tm * TOP_K)
        for r in range(tm):
            for k in range(TOP_K):
                row_copy(1 - b, k, r, pos_ref[base + r * TOP_K + k]).start()

        gates = gate_ref[...]
        y0_lo, y0_hi = _unpack_bf16_pair(bufs[b][0])
        y1_lo, y1_hi = _unpack_bf16_pair(bufs[b][1])
        g0, g1 = gates[:, 0:1], gates[:, 1:2]
        h = h_ref[...] + jnp.concatenate([g0 * y0_lo + g1 * y1_lo, g0 * y0_hi + g1 * y1_hi], axis=1)
        if final_norm:
            ms = jnp.mean(h * h, axis=-1, keepdims=True)
            h = h * lax.rsqrt(ms + EPS) * gn_ref[...]
        o_ref[...] = h

        @pl.when(i == n - 1)
        def _():
            wait_gather(1 - b)

    for parity in range(2):
        @pl.when(lax.rem(i, 2) == parity)
        def _(parity=parity):
            tile(parity)


def _moe_combine(h, gates, ys, pos, final_g, final_norm):
    t, d = h.shape
    tm = min(COMB_TM, t)
    grid_spec = pltpu.PrefetchScalarGridSpec(
        num_scalar_prefetch=1,
        grid=(t // tm,),
        in_specs=[
            pl.BlockSpec((tm, d), lambda i, p: (i, 0)),
            pl.BlockSpec((tm, LANES), lambda i, p: (i, 0)),
            pl.BlockSpec((1, d), lambda i, p: (0, 0)),
            pl.BlockSpec(memory_space=pl.ANY),
        ],
        out_specs=pl.BlockSpec((tm, d), lambda i, p: (i, 0)),
        scratch_shapes=[pltpu.VMEM((TOP_K, tm, d // 2), jnp.uint32), pltpu.VMEM((TOP_K, tm, d // 2), jnp.uint32),
                        pltpu.SemaphoreType.DMA((2,))],
    )
    return pl.pallas_call(
        functools.partial(_combine_kernel, tm=tm, final_norm=final_norm),
        grid_spec=grid_spec,
        out_shape=jax.ShapeDtypeStruct((t, d), F32),
        compiler_params=_params(("arbitrary",), VMEM_LIMIT),
        name="moe_combine",
    )(pos, h, gates, final_g.reshape(1, d), ys)


def _permute_w_in(w):
    a = 3 * ATTN_WIDTH
    w = jnp.concatenate([w[:, a:], w[:, :a]], axis=1)
    return jnp.pad(w, ((0, 0), (0, IN_COLS_PAD - IN_COLS))).astype(BF16)


def kernel(x, w_in, w_out, mix_norm_g, ffn_norm_g, rel_bias, diff_lambda, attn_subln_g, s5_a_re, s5_a_im, s5_log_dt, s5_b_re, s5_b_im, s5_c_re, s5_c_im, s5_d, s5_w_glu, hgrn_lb_logits, hgrn_norm_g, moe_w_group, moe_w_router, moe_w_gate, moe_w_up, moe_w_down, final_norm_g):
    batch, seq, d = x.shape
    depth = w_in.shape[0]
    t = batch * seq
    h = x.reshape(t, d)

    lb_cum = jnp.cumsum(jax.nn.softmax(hgrn_lb_logits.astype(F32), axis=0), axis=0)
    lb_all = lb_cum - lb_cum[0:1]
    attn_bias = _attn_bias_tiles(rel_bias, min(ATTN_TQ, seq), min(ATTN_TK, seq))

    for l in range(depth):
        proj = _in_proj(h, mix_norm_g[l], _permute_w_in(w_in[l]))

        lam_init = 0.8 - 0.6 * math.exp(-0.3 * l)
        lv = diff_lambda[l].astype(F32)
        lam = jnp.exp(jnp.sum(lv[0] * lv[1])) - jnp.exp(jnp.sum(lv[2] * lv[3])) + lam_init
        attn = _diff_attention(proj, attn_bias, lam, attn_subln_g[l], lam_init, batch, seq)

        w_b, w_c, tabs = _s5_prep(s5_a_re[l], s5_a_im[l], s5_log_dt[l], s5_b_re[l], s5_b_im[l],
                                  s5_c_re[l], s5_c_im[l], min(S5_TB, seq) // SUBLANES)
        s5 = _s5_mixer(proj, w_b, w_c, tabs, s5_d[l], s5_w_glu[l].astype(BF16), batch, seq)

        hg = _hgrn_mixer(proj, lb_all[l], hgrn_norm_g[l], batch, seq)

        wo = w_out[l].astype(BF16)
        w_r = jnp.concatenate([moe_w_router[l], moe_w_group[l]], axis=1).astype(F32)
        w_r = jnp.pad(w_r, ((0, 0), (0, LANES - w_r.shape[1])))
        r_hi = w_r.astype(BF16)
        r_lo = (w_r - r_hi.astype(F32)).astype(BF16)
        h, hn, eid, gates = _out_proj(h, attn, s5, hg, wo, ffn_norm_g[l], r_hi, r_lo)

        plan = _moe_plan(eid[:, :TOP_K], MOE_TM)
        ys = _moe_experts(hn, plan, moe_w_gate, moe_w_up, moe_w_down, l, MOE_TM)
        h = _moe_combine(h, gates, ys, plan["pos"], final_norm_g, final_norm=(l == depth - 1))

    return h.reshape(batch, seq, d)
```

```python
import functools
import math

import numpy as np
import jax
import jax.numpy as jnp
from jax import lax
from jax.experimental import pallas as pl
from jax.experimental.pallas import tpu as pltpu

F32 = jnp.float32
BF16 = jnp.bfloat16
EPS = 1e-6
LOG2E = 1.4426950408889634

D_MODEL = 2048
ATTN_HEADS = 6
ATTN_QK_DIM = 64
ATTN_V_DIM = 128
ATTN_WIDTH = ATTN_HEADS * ATTN_V_DIM
REL_BUCKETS = 32
REL_MAX_DIST = 128
S5_GROUPS = 40
S5_GROUP_CH = 16
S5_STATE = 64
S5_WIDTH = S5_GROUPS * S5_GROUP_CH
S5_LANES = S5_GROUPS * S5_STATE
HGRN_HEADS = 5
HGRN_DIM = 128
HGRN_WIDTH = HGRN_HEADS * HGRN_DIM
N_GROUPS = 4
EXPERTS_PER_GROUP = 8
N_EXPERTS = N_GROUPS * EXPERTS_PER_GROUP
TOP_K = 2
D_EXPERT = 512

LANES = 128
SUBLANES = 8

IN_COLS = 3 * ATTN_WIDTH + S5_WIDTH + 4 * HGRN_WIDTH
PROJ_TN = 512
IN_COLS_PAD = -(-IN_COLS // PROJ_TN) * PROJ_TN
ATTN_COL_BLOCK = (IN_COLS_PAD - 3 * ATTN_WIDTH) // LANES

PROJ_TM = 1024
ATTN_TQ = 512
ATTN_TK = 512
ATTN_NEAR = 2
ATTN_ROWS = 32
ATTN_HPS = 1
S5_TB = 256
S5_CHUNK = 640
S5_BLK = 128
HGRN_C = 128
OUT_TM = 256
MOE_TM = 256
COMB_TM = 128
DMA_UNROLL = 8
WEIGHT_DMA_PRIORITY = 1

VMEM_LIMIT = 56 * 1024 * 1024


def _params(sem, vmem=None):
    return pltpu.CompilerParams(dimension_semantics=sem, vmem_limit_bytes=vmem)


def _pack_bf16_pair(lo, hi):
    lo_bits = lax.bitcast_convert_type(lo.astype(BF16).astype(F32), jnp.uint32) >> 16
    hi_bits = lax.bitcast_convert_type(hi.astype(BF16).astype(F32), jnp.uint32) & jnp.uint32(0xFFFF0000)
    return lo_bits | hi_bits


def _unpack_bf16_pair(w):
    return (lax.bitcast_convert_type(w << 16, F32),
            lax.bitcast_convert_type(w & jnp.uint32(0xFFFF0000), F32))


def _in_proj_kernel(x_ref, g_ref, w_ref, o_ref, xn_ref):
    @pl.when(pl.program_id(1) == 0)
    def _():
        x = x_ref[...]
        ms = jnp.mean(x * x, axis=-1, keepdims=True)
        xn_ref[...] = (x * lax.rsqrt(ms + EPS) * g_ref[...]).astype(BF16)

    o_ref[...] = jnp.dot(xn_ref[...], w_ref[...], preferred_element_type=F32)


def _in_proj(h, g, w):
    t, d = h.shape
    n = w.shape[1]
    tm = min(PROJ_TM, t)
    return pl.pallas_call(
        _in_proj_kernel,
        grid=(t // tm, n // PROJ_TN),
        in_specs=[
            pl.BlockSpec((tm, d), lambda i, j: (i, 0)),
            pl.BlockSpec((1, d), lambda i, j: (0, 0)),
            pl.BlockSpec((d, PROJ_TN), lambda i, j: (0, j)),
        ],
        out_specs=pl.BlockSpec((tm, PROJ_TN), lambda i, j: (i, j)),
        out_shape=jax.ShapeDtypeStruct((t, n), F32),
        scratch_shapes=[pltpu.VMEM((tm, d), BF16)],
        compiler_params=_params(("parallel", "arbitrary"), VMEM_LIMIT),
        name="in_proj",
    )(h, g.reshape(1, d), w)


def _attn_kernel(lam_ref, q_ref, k_ref, v_ref, bias_ref, g_ref, o_ref,
                 kb_ref, vb_ref, qz_ref, m_ref, a_ref, acc_ref, s0_ref, s1_ref, p_ref, *, tq, tk, out_scale):
    qi = pl.program_id(2)
    ratio = tq // tk
    heads = range(ATTN_HPS)

    @pl.when(qi == 0)
    def _():
        for h in heads:
            hl = slice(h * LANES, (h + 1) * LANES)
            kb_ref[h] = k_ref[:, hl].astype(BF16)
            vb_ref[h, :, 0:LANES] = v_ref[:, hl].astype(BF16)
            vb_ref[h, :, LANES:2 * LANES] = jnp.ones((vb_ref.shape[1], LANES), BF16)

    for h in heads:
        q = q_ref[:, h * LANES:(h + 1) * LANES] * (ATTN_QK_DIM ** -0.5 * LOG2E)
        lane = lax.broadcasted_iota(jnp.int32, q.shape, 1)
        qz_ref[h, 0:tq, :] = jnp.where(lane < ATTN_QK_DIM, q, 0.0).astype(BF16)
        qz_ref[h, tq:2 * tq, :] = jnp.where(lane >= ATTN_QK_DIM, q, 0.0).astype(BF16)
    m_ref[...] = jnp.full(m_ref.shape, -jnp.inf, F32)
    acc_ref[...] = jnp.zeros(acc_ref.shape, F32)

    n_col = tk // LANES
    groups = [slice(g * ATTN_ROWS, (g + 1) * ATTN_ROWS) for g in range(2 * tq // ATTN_ROWS)]

    def scores(j, s_ref):
        for h in heads:
            k = kb_ref[h, pl.ds(pl.multiple_of(j * tk, tk), tk), :]
            s_ref[h] = lax.dot_general(qz_ref[h], k, (((1,), (1,)), ((), ())), preferred_element_type=F32)

    def softmax_pv(j, s_ref, biased):
        if biased:
            kind = jnp.clip(j - ratio * qi + ATTN_NEAR, 0, ATTN_NEAR + ratio - 1)
        for h in heads:
            for rows in groups:
                cols = [s_ref[h, rows, i * LANES:(i + 1) * LANES] for i in range(n_col)]
                if biased:
                    b0 = rows.start % tq
                    cols = [x + bias_ref[h, kind, b0:b0 + ATTN_ROWS, i * LANES:(i + 1) * LANES]
                            for i, x in enumerate(cols)]
                    for i, x in enumerate(cols):
                        s_ref[h, rows, i * LANES:(i + 1) * LANES] = x
                mx = functools.reduce(jnp.maximum, cols)
                m_old = m_ref[h, rows, :]
                m_new = jnp.maximum(m_old, jnp.max(mx, axis=-1, keepdims=True))
                m_ref[h, rows, :] = m_new
                a_ref[h, rows, :] = jnp.exp2(m_old - m_new)
        for h in heads:
            for rows in groups:
                m_new = m_ref[h, rows, :]
                for i in range(n_col):
                    p_ref[h, rows, i * LANES:(i + 1) * LANES] = jnp.exp2(
                        s_ref[h, rows, i * LANES:(i + 1) * LANES] - m_new).astype(BF16)
        for h in heads:
            v = vb_ref[h, pl.ds(pl.multiple_of(j * tk, tk), tk), :]
            alpha = a_ref[h]
            acc_ref[h] = (jnp.concatenate([alpha, alpha], axis=1) * acc_ref[h]
                          + jnp.dot(p_ref[h], v, preferred_element_type=F32))

    n_tiles = ratio * (qi + 1)
    first = lax.rem(n_tiles, 2)
    n_far = jnp.maximum(ratio * qi - ATTN_NEAR + 1, 0)
    n_far_pairs = jnp.maximum(n_far - first, 0) // 2
    n_pairs = (n_tiles - first) // 2

    @pl.when(first == 1)
    def _():
        scores(0, s1_ref)
        scores(jnp.minimum(1, n_tiles - 1), s0_ref)
        softmax_pv(0, s1_ref, True)

    @pl.when(first == 0)
    def _():
        scores(0, s0_ref)

    def pair_body(jj, carry, biased):
        j = first + 2 * jj
        scores(j + 1, s1_ref)
        softmax_pv(j, s0_ref, biased)
        scores(jnp.minimum(j + 2, n_tiles - 1), s0_ref)
        softmax_pv(j + 1, s1_ref, biased)
        return carry

    lax.fori_loop(0, n_far_pairs, functools.partial(pair_body, biased=False), 0)
    lax.fori_loop(n_far_pairs, n_pairs, functools.partial(pair_body, biased=True), 0)

    for h in heads:
        inv_l = 1.0 / acc_ref[h, :, LANES:2 * LANES]
        o1 = acc_ref[h, 0:tq, 0:LANES] * inv_l[0:tq]
        o2 = acc_ref[h, tq:2 * tq, 0:LANES] * inv_l[tq:2 * tq]
        o = o1 - lam_ref[0] * o2
        ms = jnp.mean(o * o, axis=-1, keepdims=True)
        o_ref[:, h * LANES:(h + 1) * LANES] = (o * lax.rsqrt(ms + EPS) * g_ref[...] * out_scale).astype(o_ref.dtype)


def _t5_bucket(dist):
    n = jnp.maximum(dist, 0)
    max_exact = REL_BUCKETS // 2
    large = max_exact + (jnp.log(jnp.maximum(n, 1).astype(F32) / max_exact)
                         / math.log(REL_MAX_DIST / max_exact) * (REL_BUCKETS - max_exact)).astype(jnp.int32)
    large = jnp.minimum(large, REL_BUCKETS - 1)
    return jnp.where(n < max_exact, n, large)


def _attn_bias_tiles(rel_bias, tq, tk):
    table = rel_bias.astype(F32)
    far = table[REL_BUCKETS - 1]
    r = jnp.arange(tq)[:, None]
    c = jnp.arange(tk)[None, :]

    def lookup(dist):
        bucket = _t5_bucket(dist)
        out = jnp.zeros((ATTN_HEADS,) + dist.shape, F32)
        for b in range(REL_BUCKETS):
            out = jnp.where((bucket == b)[None], table[b][:, None, None], out)
        return out

    tiles = [jnp.zeros((ATTN_HEADS, tq, tk), F32)]
    for rel in range(-ATTN_NEAR + 1, tq // tk):
        dist = r - c - rel * tk
        tiles.append(jnp.where((dist >= 0)[None], (lookup(dist) - far[:, None, None]) * LOG2E, -1e30))
    return jnp.stack(tiles, axis=1)


def _diff_attention(proj, bias, lam, subln_g, lam_init, batch, seq):
    t = proj.shape[0]
    tq = min(ATTN_TQ, seq)
    tk = min(ATTN_TK, seq)
    ratio = tq // tk
    hps = ATTN_HPS
    assert (ATTN_NEAR - 1) * tk >= REL_MAX_DIST and ratio * tk == tq
    assert ATTN_HEADS % hps == 0 and ATTN_COL_BLOCK % hps == 0
    nq = seq // tq
    kernel = functools.partial(_attn_kernel, tq=tq, tk=tk, out_scale=1.0 - lam_init)
    qb, kb, vb = (ATTN_COL_BLOCK // hps, (ATTN_COL_BLOCK + ATTN_HEADS) // hps,
                  (ATTN_COL_BLOCK + 2 * ATTN_HEADS) // hps)
    w = hps * LANES
    return pl.pallas_call(
        kernel,
        grid=(batch, ATTN_HEADS // hps, nq),
        in_specs=[
            pl.BlockSpec(memory_space=pltpu.SMEM),
            pl.BlockSpec((tq, w), lambda b, h, i: (b * nq + i, qb + h)),
            pl.BlockSpec((seq, w), lambda b, h, i: (b, kb + h)),
            pl.BlockSpec((seq, w), lambda b, h, i: (b, vb + h)),
            pl.BlockSpec((hps,) + bias.shape[1:], lambda b, h, i: (h, 0, 0, 0)),
            pl.BlockSpec((1, LANES), lambda b, h, i: (0, 0)),
        ],
        out_specs=pl.BlockSpec((tq, w), lambda b, h, i: (b * nq + i, h)),
        out_shape=jax.ShapeDtypeStruct((t, ATTN_WIDTH), BF16),
        scratch_shapes=[
            pltpu.VMEM((hps, seq, LANES), BF16),
            pltpu.VMEM((hps, seq, 2 * LANES), BF16),
            pltpu.VMEM((hps, 2 * tq, LANES), BF16),
            pltpu.VMEM((hps, 2 * tq, LANES), F32),
            pltpu.VMEM((hps, 2 * tq, LANES), F32),
            pltpu.VMEM((hps, 2 * tq, 2 * LANES), F32),
            pltpu.VMEM((hps, 2 * tq, tk), F32),
            pltpu.VMEM((hps, 2 * tq, tk), F32),
            pltpu.VMEM((hps, 2 * tq, tk), BF16),
        ],
        compiler_params=_params(("parallel", "parallel", "arbitrary"), VMEM_LIMIT),
        name="diff_attn",
    )(lam.reshape(1), proj, proj, proj, bias, subln_g.reshape(1, LANES))


def _s5_kernel(u_ref, perm_ref, permt_ref, wb_ref, wc_ref, pw_ref, d_ref, wglu_ref, o_ref,
               xr_ref, xi_ref, cr_ref, ci_ref, kr_ref, ki_ref, *, tb):
    @pl.when(pl.program_id(1) == 0)
    def _():
        cr_ref[...] = jnp.zeros(cr_ref.shape, F32)
        ci_ref[...] = jnp.zeros(ci_ref.shape, F32)

    un = u_ref[...]
    uh = un.astype(BF16)
    ul = (un - uh.astype(F32)).astype(BF16)
    up = jnp.dot(perm_ref[...], jnp.concatenate([uh, ul], axis=1), preferred_element_type=F32)
    ub = up[:, :S5_WIDTH].astype(BF16)
    u = up[:, :S5_WIDTH] + up[:, S5_WIDTH:]
    nblk = S5_WIDTH // S5_BLK
    sw = S5_LANES // nblk
    tpb = sw // LANES
    for i in range(nblk):
        bu = jnp.dot(ub[:, i * S5_BLK:(i + 1) * S5_BLK], wb_ref[i], preferred_element_type=F32)
        for k in range(tpb):
            xr_ref[i * tpb + k] = bu[:, k * LANES:(k + 1) * LANES]
            xi_ref[i * tpb + k] = bu[:, sw + k * LANES:sw + (k + 1) * LANES]

    sl = tb // SUBLANES
    for c in range(S5_LANES // S5_CHUNK):
        tiles = range(c * S5_CHUNK // LANES, (c + 1) * S5_CHUNK // LANES)
        lss = [slice(n * LANES, (n + 1) * LANES) for n in tiles]
        a = [(pw_ref[0, 0, :, ls], pw_ref[1, 0, :, ls]) for ls in lss]

        def local_step(t, carry, tiles=tiles, a=a):
            rows = pl.ds(pl.multiple_of(t * SUBLANES, SUBLANES), SUBLANES)
            out = []
            for n, (ar, ai), (pr, pi) in zip(tiles, a, carry):
                xr = ar * pr - ai * pi + xr_ref[n, rows, :]
                xi = ar * pi + ai * pr + xi_ref[n, rows, :]
                xr_ref[n, rows, :] = xr
                xi_ref[n, rows, :] = xi
                out.append((xr, xi))
            return tuple(out)

        zero = jnp.zeros((SUBLANES, LANES), F32)
        ends = lax.fori_loop(0, sl, local_step, tuple((zero, zero) for _ in tiles), unroll=2)

        for ls, (er, ei) in zip(lss, ends):
            lr, li = pw_ref[0, sl - 1, 0:1, ls], pw_ref[1, sl - 1, 0:1, ls]
            kr, ki = cr_ref[:, ls], ci_ref[:, ls]
            for sub in range(SUBLANES):
                kr_ref[sub:sub + 1, ls] = kr
                ki_ref[sub:sub + 1, ls] = ki
                kr, ki = (er[sub:sub + 1, :] + lr * kr - li * ki, ei[sub:sub + 1, :] + lr * ki + li * kr)
            cr_ref[:, ls] = kr
            ci_ref[:, ls] = ki

        k8 = [(kr_ref[:, ls], ki_ref[:, ls]) for ls in lss]

        def fix_step(t, carry, tiles=tiles, lss=lss, k8=k8):
            rows = pl.ds(pl.multiple_of(t * SUBLANES, SUBLANES), SUBLANES)
            for n, ls, (kr8, ki8) in zip(tiles, lss, k8):
                pr = pw_ref[0, t, :, ls]
                pi = pw_ref[1, t, :, ls]
                xr_ref[n, rows, :] = xr_ref[n, rows, :] + pr * kr8 - pi * ki8
                xi_ref[n, rows, :] = xi_ref[n, rows, :] + pr * ki8 + pi * kr8
            return carry

        lax.fori_loop(0, sl, fix_step, 0, unroll=2)

    ys = []
    for i in range(nblk):
        xc = jnp.concatenate([xr_ref[i * tpb + k] for k in range(tpb)]
                             + [xi_ref[i * tpb + k] for k in range(tpb)], axis=1)
        ys.append(jnp.dot(xc.astype(BF16), wc_ref[i], preferred_element_type=F32))
    y = jnp.concatenate(ys, axis=1) + d_ref[...] * u
    z = 0.5 * y * (1.0 + jnp.tanh(math.sqrt(2.0 / math.pi) * (y + 0.044715 * (y * y * y))))
    gate = jax.nn.sigmoid(jnp.dot(z.astype(BF16), wglu_ref[...], preferred_element_type=F32))
    o_ref[...] = jnp.dot(permt_ref[...], (z * gate).astype(BF16),
                         preferred_element_type=F32).astype(o_ref.dtype)


def _s5_prep(a_re, a_im, log_dt, b_re, b_im, c_re, c_im, steps):
    lr = jnp.minimum(a_re.astype(F32), -1e-4)
    li = a_im.astype(F32)
    dt = jnp.exp(log_dt.astype(F32))[:, None]
    mag = jnp.exp(lr * dt)
    ab_re = mag * jnp.cos(li * dt)
    ab_im = mag * jnp.sin(li * dt)
    den = lr * lr + li * li
    nr = ab_re - 1.0
    ni = ab_im
    f_re = (nr * lr + ni * li) / den
    f_im = (ni * lr - nr * li) / den
    br = b_re.astype(F32)
    bi = b_im.astype(F32)
    bb_re = f_re[..., None] * br - f_im[..., None] * bi
    bb_im = f_re[..., None] * bi + f_im[..., None] * br

    nblk = S5_WIDTH // S5_BLK
    gpb = S5_GROUPS // nblk
    eye = jnp.eye(gpb, dtype=F32)

    def bmat(bb):
        bb = bb.reshape(nblk, gpb, S5_STATE, S5_GROUP_CH)
        m = jnp.einsum('ignc,gh->igchn', bb, eye)
        return m.reshape(nblk, gpb * S5_GROUP_CH, gpb * S5_STATE)

    def cmat(cc):
        cc = cc.astype(F32).reshape(nblk, gpb, S5_GROUP_CH, S5_STATE)
        m = jnp.einsum('igcn,gh->ignhc', cc, eye)
        return m.reshape(nblk, gpb * S5_STATE, gpb * S5_GROUP_CH)

    w_b = jnp.concatenate([bmat(bb_re), bmat(bb_im)], axis=2).astype(BF16)
    w_c = jnp.concatenate([cmat(c_re), -cmat(c_im)], axis=1).astype(BF16)

    j = jnp.arange(1, steps + 1, dtype=F32)[:, None, None]
    m = jnp.exp(lr * dt * j)
    pows = jnp.stack([(m * jnp.cos(li * dt * j)).reshape(steps, 1, -1),
                      (m * jnp.sin(li * dt * j)).reshape(steps, 1, -1)])
    pows = jnp.broadcast_to(pows, (2, steps, SUBLANES, S5_LANES))
    return w_b, w_c, pows


def _s5_mixer(proj, w_b, w_c, tabs, d_skip, w_glu, batch, seq):
    t = proj.shape[0]
    tb = min(S5_TB, seq)
    nt = seq // tb
    sl = tb // SUBLANES
    r = np.arange(tb)
    perm = np.zeros((tb, tb), np.float32)
    perm[r, (r % SUBLANES) * sl + r // SUBLANES] = 1.0
    perm_t = jnp.asarray(perm.T, BF16)
    perm = jnp.asarray(perm, BF16)
    return pl.pallas_call(
        functools.partial(_s5_kernel, tb=tb),
        grid=(batch, nt),
        in_specs=[
            pl.BlockSpec((tb, S5_WIDTH), lambda b, i: (b * nt + i, 0)),
            pl.BlockSpec((tb, tb), lambda b, i: (0, 0)),
            pl.BlockSpec((tb, tb), lambda b, i: (0, 0)),
            pl.BlockSpec(w_b.shape, lambda b, i: (0, 0, 0)),
            pl.BlockSpec(w_c.shape, lambda b, i: (0, 0, 0)),
            pl.BlockSpec(tabs.shape, lambda b, i: (0, 0, 0, 0)),
            pl.BlockSpec((1, S5_WIDTH), lambda b, i: (0, 0)),
            pl.BlockSpec((S5_WIDTH, S5_WIDTH), lambda b, i: (0, 0)),
        ],
        out_specs=pl.BlockSpec((tb, S5_WIDTH), lambda b, i: (b * nt + i, 0)),
        out_shape=jax.ShapeDtypeStruct((t, S5_WIDTH), BF16),
        scratch_shapes=[
            pltpu.VMEM((S5_LANES // LANES, tb, LANES), F32),
            pltpu.VMEM((S5_LANES // LANES, tb, LANES), F32),
            pltpu.VMEM((1, S5_LANES), F32),
            pltpu.VMEM((1, S5_LANES), F32),
            pltpu.VMEM((SUBLANES, S5_LANES), F32),
            pltpu.VMEM((SUBLANES, S5_LANES), F32),
        ],
        compiler_params=_params(("parallel", "arbitrary"), VMEM_LIMIT),
        name="s5_mixer",
    )(proj, perm, perm_t, w_b, w_c, tabs, d_skip.reshape(1, S5_WIDTH), w_glu)


def _hgrn_consts(c):
    t = np.arange(c)[:, None]
    u = np.arange(c)[None, :]
    mats = [u <= t]
    masks = [np.eye(c, dtype=bool)]
    h = 1
    while h < c:
        mid = (t // (2 * h)) * 2 * h + h
        if 2 * h < SUBLANES:
            mats.append(np.where(t >= mid, (u >= mid) & (u <= t), (u > t) & (u < mid)))
        masks.append(((u // (2 * h)) == (t // (2 * h))) & (t >= mid) & (u < mid))
        h *= 2
    return (np.concatenate(mats, axis=0).astype(np.float32), np.stack(masks).astype(np.float32))


def _hgrn_kernel(q_ref, f_ref, i_ref, g_ref, lb_ref, ng_ref, mall_ref, mask_ref, o_ref,
                 e_ref, st_ref, *, c):
    @pl.when(pl.program_id(1) == 0)
    def _():
        st_ref[...] = jnp.zeros(st_ref.shape, F32)

    lb = lb_ref[...]
    fl = f_ref[...]
    la = jnp.log(lb)
    lbb = jnp.log1p(-lb) + (jnp.minimum(fl, 0.0) - jnp.log1p(jnp.exp(-jnp.abs(fl))))
    logf = jnp.maximum(la, lbb) + jnp.log1p(jnp.exp(-jnp.abs(la - lbb)))
    hi = logf.astype(BF16)
    lo = (logf - hi.astype(F32)).astype(BF16)
    e2 = jnp.dot(mall_ref[...], jnp.concatenate([hi, lo], axis=1), preferred_element_type=F32)
    es = e2[:, :HGRN_WIDTH] + e2[:, HGRN_WIDTH:]
    n_levels = mask_ref.shape[0] - 1
    n_small = mall_ref.shape[0] // c - 1
    gall = es[0:c]
    e_ref[0:c, :] = gall
    e_ref[c:2 * c, :] = gall[c - 1:c, :] - gall
    e_ref[2 * c:(2 + n_small) * c, :] = es[c:]
    for lv in range(n_small, n_levels):
        blk = 2 << lv
        g3 = gall.reshape(c // blk, blk, HGRN_WIDTH)
        gm = jnp.broadcast_to(g3[:, blk // 2 - 1:blk // 2, :], g3.shape)
        e_ref[(2 + lv) * c:(3 + lv) * c, :] = (-jnp.abs(g3 - gm)).reshape(c, HGRN_WIDTH)

    nt = (((1,), (1,)), ((), ()))
    for hd in range(HGRN_HEADS):
        ls = slice(hd * HGRN_DIM, (hd + 1) * HGRN_DIM)
        qr = q_ref[:, ls]
        q = qr * jax.nn.sigmoid(qr) * (HGRN_DIM ** -0.5)
        k = (1.0 - lb[:, ls]) * jax.nn.sigmoid(-fl[:, ls])
        vb = i_ref[:, ls].astype(BF16)
        p = lax.dot_general(q.astype(BF16), k.astype(BF16), nt, preferred_element_type=F32) * mask_ref[0]
        for lv in range(n_levels):
            fac = jnp.exp(e_ref[(2 + lv) * c:(3 + lv) * c, ls])
            s = lax.dot_general((q * fac).astype(BF16), (k * fac).astype(BF16), nt,
                                preferred_element_type=F32)
            p = p + s * mask_ref[lv + 1]
        o = jnp.dot(p.astype(BF16), vb, preferred_element_type=F32)
        gcum = e_ref[0:c, ls]
        st = st_ref[hd]
        o = o + lax.dot_general((q * jnp.exp(gcum)).astype(BF16), st.astype(BF16), nt,
                                preferred_element_type=F32)
        kd = (k * jnp.exp(e_ref[c:2 * c, ls])).astype(BF16)
        st_ref[hd] = st * jnp.exp(gcum[c - 1:c, :]) + lax.dot_general(
            vb, kd, (((0,), (0,)), ((), ())), preferred_element_type=F32)
        ms = jnp.mean(o * o, axis=-1, keepdims=True)
        o = o * lax.rsqrt(ms + EPS) * ng_ref[...] * jax.nn.sigmoid(g_ref[:, ls])
        o_ref[:, ls] = o.astype(o_ref.dtype)


def _hgrn_mixer(proj, lb, norm_g, batch, seq):
    t = proj.shape[0]
    c = min(HGRN_C, seq)
    nc = seq // c
    m_all, masks = _hgrn_consts(c)
    m_all = jnp.asarray(m_all, BF16)
    masks = jnp.asarray(masks, F32)

    def col(j):
        return pl.BlockSpec((c, HGRN_WIDTH), lambda b, i: (b * nc + i, j))

    return pl.pallas_call(
        functools.partial(_hgrn_kernel, c=c),
        grid=(batch, nc),
        in_specs=[
            col(1), col(2), col(3), col(4),
            pl.BlockSpec((1, HGRN_WIDTH), lambda b, i: (0, 0)),
            pl.BlockSpec((1, HGRN_DIM), lambda b, i: (0, 0)),
            pl.BlockSpec(m_all.shape, lambda b, i: (0, 0)),
            pl.BlockSpec(masks.shape, lambda b, i: (0, 0, 0)),
        ],
        out_specs=pl.BlockSpec((c, HGRN_WIDTH), lambda b, i: (b * nc + i, 0)),
        out_shape=jax.ShapeDtypeStruct((t, HGRN_WIDTH), BF16),
        scratch_shapes=[
            pltpu.VMEM(((masks.shape[0] + 1) * c, HGRN_WIDTH), F32),
            pltpu.VMEM((HGRN_HEADS, HGRN_DIM, HGRN_DIM), F32),
        ],
        compiler_params=_params(("parallel", "arbitrary"), VMEM_LIMIT),
        name="hgrn2_mixer",
    )(proj, proj, proj, proj, lb.reshape(1, HGRN_WIDTH), norm_g.reshape(1, HGRN_DIM), m_all, masks)


def _out_proj_kernel(h_ref, a_ref, s_ref, r_ref, w_ref, gn_ref, rh_ref, rl_ref,
                     ho_ref, hn_ref, eid_ref, gate_ref):
    s_lo, r_lo = ATTN_WIDTH, ATTN_WIDTH + S5_WIDTH
    acc = jnp.dot(a_ref[...], w_ref[0:s_lo, :], preferred_element_type=F32)
    acc = acc + jnp.dot(s_ref[...], w_ref[s_lo:r_lo, :], preferred_element_type=F32)
    acc = acc + jnp.dot(r_ref[...], w_ref[r_lo:, :], preferred_element_type=F32)
    h = h_ref[...] + acc
    ho_ref[...] = h
    ms = jnp.mean(h * h, axis=-1, keepdims=True)
    hn = h * lax.rsqrt(ms + EPS) * gn_ref[...]
    half = hn.shape[1] // 2
    hn_ref[...] = _pack_bf16_pair(hn[:, :half], hn[:, half:])

    hh = hn.astype(BF16)
    hl = (hn - hh.astype(F32)).astype(BF16)
    logits = (jnp.dot(hh, rh_ref[...], preferred_element_type=F32)
              + jnp.dot(hh, rl_ref[...], preferred_element_type=F32)
              + jnp.dot(hl, rh_ref[...], preferred_element_type=F32))
    lane = lax.broadcasted_iota(jnp.int32, logits.shape, 1).astype(F32)
    neg = -jnp.inf
    big = 1e9
    is_group = jnp.where(lane >= N_EXPERTS, jnp.where(lane < N_EXPERTS + N_GROUPS, 1.0, 0.0), 0.0)
    gl = jnp.where(is_group > 0, logits, neg)
    gmax = jnp.max(gl, axis=-1, keepdims=True)
    g_lane = jnp.min(jnp.where(gl == gmax, lane, big), axis=-1, keepdims=True)
    p_g = 1.0 / jnp.sum(jnp.exp(gl - gmax), axis=-1, keepdims=True)
    lo_lane = (g_lane - N_EXPERTS) * EXPERTS_PER_GROUP
    in_group = jnp.where(lane >= lo_lane, jnp.where(lane < lo_lane + EXPERTS_PER_GROUP, 1.0, 0.0), 0.0)
    el = jnp.where(in_group > 0, logits, neg)
    t1 = jnp.max(el, axis=-1, keepdims=True)
    i1 = jnp.min(jnp.where(el == t1, lane, big), axis=-1, keepdims=True)
    el2 = jnp.where(lane == i1, neg, el)
    t2 = jnp.max(el2, axis=-1, keepdims=True)
    i2 = jnp.min(jnp.where(el2 == t2, lane, big), axis=-1, keepdims=True)
    e21 = jnp.exp(t2 - t1)
    g1 = p_g / (1.0 + e21)
    g2 = p_g * e21 / (1.0 + e21)
    eid_ref[...] = jnp.where(lane == 0, i1, jnp.where(lane == 1, i2, 0.0)).astype(jnp.int32)
    gate_ref[...] = jnp.where(lane == 0, g1, jnp.where(lane == 1, g2, 0.0))


def _out_proj(h, attn, s5, hg, w_out, gn, r_hi, r_lo):
    t, d = h.shape
    tm = min(OUT_TM, t)

    def rows(w):
        return pl.BlockSpec((tm, w), lambda i: (i, 0))

    def full(a):
        return pl.BlockSpec(a.shape, lambda i: (0,) * a.ndim)

    return pl.pallas_call(
        _out_proj_kernel,
        grid=(t // tm,),
        in_specs=[rows(d), rows(ATTN_WIDTH), rows(S5_WIDTH), rows(HGRN_WIDTH),
                  full(w_out), pl.BlockSpec((1, d), lambda i: (0, 0)),
                  full(r_hi), full(r_lo)],
        out_specs=[rows(d), rows(d // 2), rows(LANES), rows(LANES)],
        out_shape=[jax.ShapeDtypeStruct((t, d), F32), jax.ShapeDtypeStruct((t, d // 2), jnp.uint32),
                   jax.ShapeDtypeStruct((t, LANES), jnp.int32), jax.ShapeDtypeStruct((t, LANES), F32)],
        compiler_params=_params(("parallel",), VMEM_LIMIT),
        name="out_proj_router",
    )(h, attn, s5, hg, w_out, gn.reshape(1, d), r_hi, r_lo)


def _moe_plan(eid, tm):
    flat = eid.reshape(-1)
    n_slots = flat.shape[0]
    order = jnp.argsort(flat, stable=True).astype(jnp.int32)
    sorted_pos = jnp.argsort(order).astype(jnp.int32)
    slot_start = jnp.searchsorted(jnp.take(flat, order), jnp.arange(N_EXPERTS + 1, dtype=jnp.int32),
                                  side="left").astype(jnp.int32)
    counts = slot_start[1:] - slot_start[:-1]
    slot_start = slot_start[:-1]
    tiles = (counts + tm - 1) // tm
    tile_end = jnp.cumsum(tiles)
    tile_start = tile_end - tiles
    pos = jnp.take(tile_start * tm - slot_start, flat) + sorted_pos
    tok_sorted = order // TOP_K
    tok_sorted = jnp.pad(tok_sorted, (0, tm))
    nt_max = (n_slots + N_EXPERTS * (tm - 1)) // tm
    n_used = tile_end[-1]
    j = jnp.minimum(jnp.arange(nt_max, dtype=jnp.int32), n_used - 1)
    tile_expert = jnp.sum((j[:, None] >= tile_end[None, :]).astype(jnp.int32), axis=1)
    onehot_t = (tile_expert[:, None] == jnp.arange(N_EXPERTS, dtype=jnp.int32)[None, :]).astype(jnp.int32)
    tile_slot0 = jnp.sum(onehot_t * (slot_start - tile_start * tm)[None, :], axis=1) + j * tm
    used = (tiles > 0).astype(jnp.int32)
    ordinal = jnp.cumsum(used) - used
    ids = jnp.arange(N_EXPERTS, dtype=jnp.int32)
    later = jnp.where((ids[None, :] > ids[:, None]) & (used[None, :] > 0), ids[None, :], N_EXPERTS)
    nxt = jnp.min(later, axis=1)
    nxt = jnp.where(nxt >= N_EXPERTS, -1, nxt)
    tile_wslot = jnp.sum(onehot_t * (ordinal % 2)[None, :], axis=1)
    tile_next = jnp.sum(onehot_t * nxt[None, :], axis=1)
    return dict(pos=pos.astype(jnp.int32), tok_sorted=tok_sorted, tile_expert=tile_expert.astype(jnp.int32),
                tile_slot0=tile_slot0.astype(jnp.int32), tile_wslot=tile_wslot.astype(jnp.int32),
                tile_next=tile_next.astype(jnp.int32), n_used=n_used.reshape(1).astype(jnp.int32),
                nt_max=nt_max)


def _expert_kernel(tok_ref, te_ref, s0_ref, ws_ref, ne_ref, nu_ref, hn_ref, wg_ref, wu_ref, wd_ref, y_ref,
                   xa_ref, xb_ref, wgf_ref, wuf_ref, wdf_ref, wgb_ref, wub_ref, wdb_ref, sem, wsem,
                   *, tm, layer):
    j = pl.program_id(0)
    n_used = nu_ref[0]
    bufs = (xa_ref, xb_ref)

    def weight_copies(expert, slot):
        return [pltpu.make_async_copy(src.at[layer, expert], dst.at[slot], wsem.at[slot])
                for src, dst in ((wg_ref, wgf_ref), (wu_ref, wuf_ref), (wd_ref, wdf_ref))]

    def row_copy(buf, r, tok):
        return pltpu.make_async_copy(hn_ref.at[pl.ds(tok, 1)], bufs[buf].at[pl.ds(r, 1)], sem.at[buf])

    def wait_gather(buf):
        def wait(r, carry):
            row_copy(buf, r, 0).wait()
            return carry

        lax.fori_loop(0, tm, wait, 0, unroll=DMA_UNROLL)

    @pl.when(j == 0)
    def _():
        for cp in weight_copies(te_ref[0], ws_ref[0]):
            cp.start(priority=WEIGHT_DMA_PRIORITY)
        base = s0_ref[0]

        def issue(r, carry):
            row_copy(0, r, tok_ref[base + r]).start()
            return carry

        lax.fori_loop(0, tm, issue, 0, unroll=DMA_UNROLL)

    def tile(buf):
        wait_gather(buf)
        prev = te_ref[jnp.maximum(j - 1, 0)]

        @pl.when(jnp.logical_or(j == 0, te_ref[j] != prev))
        def _():
            slot = ws_ref[j]

            @pl.when(ne_ref[j] >= 0)
            def _():
                for cp in weight_copies(ne_ref[j], 1 - slot):
                    cp.start(priority=WEIGHT_DMA_PRIORITY)

            for cp in weight_copies(te_ref[j], slot):
                cp.wait()
            wgb_ref[...] = wgf_ref[slot].astype(BF16)
            wub_ref[...] = wuf_ref[slot].astype(BF16)
            wdb_ref[...] = wdf_ref[slot].astype(BF16)

        nbase = s0_ref[jnp.minimum(j + 1, n_used - 1)]
        for r in range(tm):
            row_copy(1 - buf, r, tok_ref[nbase + r]).start()

        x_lo, x_hi = _unpack_bf16_pair(bufs[buf][...])
        x = jnp.concatenate([x_lo.astype(BF16), x_hi.astype(BF16)], axis=1)
        g = jnp.dot(x, wgb_ref[...], preferred_element_type=F32)
        u = jnp.dot(x, wub_ref[...], preferred_element_type=F32)
        hmid = (g * jax.nn.sigmoid(g) * u).astype(BF16)
        y = jnp.dot(hmid, wdb_ref[...], preferred_element_type=F32)
        half = y.shape[1] // 2
        y_ref[...] = _pack_bf16_pair(y[:, :half], y[:, half:])

        @pl.when(j == n_used - 1)
        def _():
            wait_gather(1 - buf)

    for parity in range(2):
        @pl.when(jnp.logical_and(j < n_used, lax.rem(j, 2) == parity))
        def _(parity=parity):
            tile(parity)

    @pl.when(j >= n_used)
    def _():
        y_ref[...] = jnp.zeros(y_ref.shape, y_ref.dtype)


def _moe_experts(hn, plan, w_gate, w_up, w_down, layer, tm):
    t, dp = hn.shape
    d, de = w_gate.shape[2], w_gate.shape[3]
    nt_max = plan["nt_max"]
    grid_spec = pltpu.PrefetchScalarGridSpec(
        num_scalar_prefetch=6,
        grid=(nt_max,),
        in_specs=[pl.BlockSpec(memory_space=pl.ANY)] * 4,
        out_specs=pl.BlockSpec((tm, dp), lambda j, *_: (j, 0)),
        scratch_shapes=[pltpu.VMEM((tm, dp), jnp.uint32), pltpu.VMEM((tm, dp), jnp.uint32),
                        pltpu.VMEM((2, d, de), F32), pltpu.VMEM((2, d, de), F32), pltpu.VMEM((2, de, d), F32),
                        pltpu.VMEM((d, de), BF16), pltpu.VMEM((d, de), BF16), pltpu.VMEM((de, d), BF16),
                        pltpu.SemaphoreType.DMA((2,)), pltpu.SemaphoreType.DMA((2,))],
    )
    return pl.pallas_call(
        functools.partial(_expert_kernel, tm=tm, layer=layer),
        grid_spec=grid_spec,
        out_shape=jax.ShapeDtypeStruct((nt_max * tm, dp), jnp.uint32),
        compiler_params=_params(("arbitrary",), VMEM_LIMIT),
        name="moe_experts",
    )(plan["tok_sorted"], plan["tile_expert"], plan["tile_slot0"], plan["tile_wslot"], plan["tile_next"],
      plan["n_used"], hn, w_gate, w_up, w_down)


def _combine_kernel(pos_ref, h_ref, gate_ref, gn_ref, y_ref, o_ref, ba_ref, bb_ref, sem, *, tm, final_norm):
    i = pl.program_id(0)
    n = pl.num_programs(0)
    bufs = (ba_ref, bb_ref)

    def row_copy(b, k, r, src_row):
        return pltpu.make_async_copy(y_ref.at[pl.ds(src_row, 1)], bufs[b].at[k, pl.ds(r, 1)], sem.at[b])

    def wait_gather(b):
        def wait(r, carry):
            for k in range(TOP_K):
                row_copy(b, k, r, 0).wait()
            return carry

        lax.fori_loop(0, tm, wait, 0, unroll=DMA_UNROLL)

    @pl.when(i == 0)
    def _():
        def issue(r, carry):
            for k in range(TOP_K):
                row_copy(0, k, r, pos_ref[r * TOP_K + k]).start()
            return carry

        lax.fori_loop(0, tm, issue, 0, unroll=DMA_UNROLL)

    def tile(b):
        wait_gather(b)
        base = jnp.minimum(i + 1, n - 1) * (tm * TOP_K)
        for r in range(tm):
            for k in range(TOP_K):
                row_copy(1 - b, k, r, pos_ref[base + r * TOP_K + k]).start()

        gates = gate_ref[...]
        y0_lo, y0_hi = _unpack_bf16_pair(bufs[b][0])
        y1_lo, y1_hi = _unpack_bf16_pair(bufs[b][1])
        g0, g1 = gates[:, 0:1], gates[:, 1:2]
        h = h_ref[...] + jnp.concatenate([g0 * y0_lo + g1 * y1_lo, g0 * y0_hi + g1 * y1_hi], axis=1)
        if final_norm:
            ms = jnp.mean(h * h, axis=-1, keepdims=True)
            h = h * lax.rsqrt(ms + EPS) * gn_ref[...]
        o_ref[...] = h

        @pl.when(i == n - 1)
        def _():
            wait_gather(1 - b)

    for parity in range(2):
        @pl.when(lax.rem(i, 2) == parity)
        def _(parity=parity):
            tile(parity)


def _moe_combine(h, gates, ys, pos, final_g, final_norm):
    t, d = h.shape
    tm = min(COMB_TM, t)
    grid_spec = pltpu.PrefetchScalarGridSpec(
        num_scalar_prefetch=1,
        grid=(t // tm,),
        in_specs=[
            pl.BlockSpec((tm, d), lambda i, p: (i, 0)),
            pl.BlockSpec((tm, LANES), lambda i, p: (i, 0)),
            pl.BlockSpec((1, d), lambda i, p: (0, 0)),
            pl.BlockSpec(memory_space=pl.ANY),
        ],
        out_specs=pl.BlockSpec((tm, d), lambda i, p: (i, 0)),
        scratch_shapes=[pltpu.VMEM((TOP_K, tm, d // 2), jnp.uint32), pltpu.VMEM((TOP_K, tm, d // 2), jnp.uint32),
                        pltpu.SemaphoreType.DMA((2,))],
    )
    return pl.pallas_call(
        functools.partial(_combine_kernel, tm=tm, final_norm=final_norm),
        grid_spec=grid_spec,
        out_shape=jax.ShapeDtypeStruct((t, d), F32),
        compiler_params=_params(("arbitrary",), VMEM_LIMIT),
        name="moe_combine",
    )(pos, h, gates, final_g.reshape(1, d), ys)


def _permute_w_in(w):
    a = 3 * ATTN_WIDTH
    pad = jnp.zeros((w.shape[0], IN_COLS_PAD - IN_COLS), w.dtype)
    return jnp.concatenate([w[:, a:], pad, w[:, :a]], axis=1).astype(BF16)


def kernel(x, w_in, w_out, mix_norm_g, ffn_norm_g, rel_bias, diff_lambda, attn_subln_g, s5_a_re, s5_a_im, s5_log_dt, s5_b_re, s5_b_im, s5_c_re, s5_c_im, s5_d, s5_w_glu, hgrn_lb_logits, hgrn_norm_g, moe_w_group, moe_w_router, moe_w_gate, moe_w_up, moe_w_down, final_norm_g):
    batch, seq, d = x.shape
    depth = w_in.shape[0]
    t = batch * seq
    h = x.reshape(t, d)

    lb_cum = jnp.cumsum(jax.nn.softmax(hgrn_lb_logits.astype(F32), axis=0), axis=0)
    lb_all = lb_cum - lb_cum[0:1]
    attn_bias = _attn_bias_tiles(rel_bias, min(ATTN_TQ, seq), min(ATTN_TK, seq))

    for l in range(depth):
        proj = _in_proj(h, mix_norm_g[l], _permute_w_in(w_in[l]))

        lam_init = 0.8 - 0.6 * math.exp(-0.3 * l)
        lv = diff_lambda[l].astype(F32)
        lam = jnp.exp(jnp.sum(lv[0] * lv[1])) - jnp.exp(jnp.sum(lv[2] * lv[3])) + lam_init
        attn = _diff_attention(proj, attn_bias, lam, attn_subln_g[l], lam_init, batch, seq)

        w_b, w_c, tabs = _s5_prep(s5_a_re[l], s5_a_im[l], s5_log_dt[l], s5_b_re[l], s5_b_im[l],
                                  s5_c_re[l], s5_c_im[l], min(S5_TB, seq) // SUBLANES)
        s5 = _s5_mixer(proj, w_b, w_c, tabs, s5_d[l], s5_w_glu[l].astype(BF16), batch, seq)

        hg = _hgrn_mixer(proj, lb_all[l], hgrn_norm_g[l], batch, seq)

        wo = w_out[l].astype(BF16)
        w_r = jnp.concatenate([moe_w_router[l], moe_w_group[l]], axis=1).astype(F32)
        w_r = jnp.pad(w_r, ((0, 0), (0, LANES - w_r.shape[1])))
        r_hi = w_r.astype(BF16)
        r_lo = (w_r - r_hi.astype(F32)).astype(BF16)
        h, hn, eid, gates = _out_proj(h, attn, s5, hg, wo, ffn_norm_g[l], r_hi, r_lo)

        plan = _moe_plan(eid[:, :TOP_K], MOE_TM)
        ys = _moe_experts(hn, plan, moe_w_gate, moe_w_up, moe_w_down, l, MOE_TM)
        h = _moe_combine(h, gates, ys, plan["pos"], final_norm_g, final_norm=(l == depth - 1))

    return h.reshape(batch, seq, d)
```

```python
import functools
import math

import numpy as np
import jax
import jax.numpy as jnp
from jax import lax
from jax.experimental import pallas as pl
from jax.experimental.pallas import tpu as pltpu

F32 = jnp.float32
BF16 = jnp.bfloat16
EPS = 1e-6
LOG2E = 1.4426950408889634

D_MODEL = 2048
ATTN_HEADS = 6
ATTN_QK_DIM = 64
ATTN_V_DIM = 128
ATTN_WIDTH = ATTN_HEADS * ATTN_V_DIM
REL_BUCKETS = 32
REL_MAX_DIST = 128
S5_GROUPS = 40
S5_GROUP_CH = 16
S5_STATE = 64
S5_WIDTH = S5_GROUPS * S5_GROUP_CH
S5_LANES = S5_GROUPS * S5_STATE
HGRN_HEADS = 5
HGRN_DIM = 128
HGRN_WIDTH = HGRN_HEADS * HGRN_DIM
N_GROUPS = 4
EXPERTS_PER_GROUP = 8
N_EXPERTS = N_GROUPS * EXPERTS_PER_GROUP
TOP_K = 2
D_EXPERT = 512

LANES = 128
SUBLANES = 8

IN_COLS = 3 * ATTN_WIDTH + S5_WIDTH + 4 * HGRN_WIDTH
PROJ_TN = 256
IN_COLS_PAD = -(-IN_COLS // PROJ_TN) * PROJ_TN
ATTN_COL_BLOCK = (IN_COLS_PAD - 3 * ATTN_WIDTH) // LANES

PROJ_TM = 2048
ATTN_TQ = 512
ATTN_TK = 512
ATTN_NEAR = 2
ATTN_ROWS = 32
ATTN_HPS = 1
S5_TB = 256
S5_CHUNK = 640
S5_BLK = 128
HGRN_C = 128
OUT_TM = 256
MOE_TM = 256
COMB_TM = 128
DMA_UNROLL = 8
WEIGHT_DMA_PRIORITY = 1

VMEM_LIMIT = 56 * 1024 * 1024


def _params(sem, vmem=None):
    return pltpu.CompilerParams(dimension_semantics=sem, vmem_limit_bytes=vmem)


def _pack_bf16_pair(lo, hi):
    lo_bits = lax.bitcast_convert_type(lo.astype(BF16).astype(F32), jnp.uint32) >> 16
    hi_bits = lax.bitcast_convert_type(hi.astype(BF16).astype(F32), jnp.uint32) & jnp.uint32(0xFFFF0000)
    return lo_bits | hi_bits


def _unpack_bf16_pair(w):
    return (lax.bitcast_convert_type(w << 16, F32),
            lax.bitcast_convert_type(w & jnp.uint32(0xFFFF0000), F32))


def _in_proj_kernel(x_ref, g_ref, w_ref, wt_ref, o_ref, xn_ref, *, tail_tile):
    j = pl.program_id(1)

    @pl.when(j == 0)
    def _():
        x = x_ref[...]
        ms = jnp.mean(x * x, axis=-1, keepdims=True)
        xn_ref[...] = (x * lax.rsqrt(ms + EPS) * g_ref[...]).astype(BF16)

    w = jnp.where(j == tail_tile, wt_ref[...], w_ref[0])
    o_ref[...] = jnp.dot(xn_ref[...], w.astype(BF16), preferred_element_type=F32)


def _in_proj(h, g, w_all, layer):
    t, d = h.shape
    tm = min(PROJ_TM, t)
    tn = PROJ_TN
    a = 3 * ATTN_WIDTH
    rest = IN_COLS - a
    assert a % tn == 0 and IN_COLS_PAD - IN_COLS == tn - rest % tn
    rest_full, a_tiles = rest // tn, a // tn
    tail_tile = rest_full
    w_tail = jnp.pad(w_all[layer][:, a + rest_full * tn:], ((0, 0), (0, IN_COLS_PAD - IN_COLS)))

    def w_map(i, j):
        col = jnp.where(j < tail_tile, a_tiles + j, jnp.where(j == tail_tile, 0, j - tail_tile - 1))
        return (layer, 0, col)

    return pl.pallas_call(
        functools.partial(_in_proj_kernel, tail_tile=tail_tile),
        grid=(t // tm, IN_COLS_PAD // tn),
        in_specs=[
            pl.BlockSpec((tm, d), lambda i, j: (i, 0)),
            pl.BlockSpec((1, d), lambda i, j: (0, 0)),
            pl.BlockSpec((1, d, tn), w_map),
            pl.BlockSpec((d, tn), lambda i, j: (0, 0)),
        ],
        out_specs=pl.BlockSpec((tm, tn), lambda i, j: (i, j)),
        out_shape=jax.ShapeDtypeStruct((t, IN_COLS_PAD), F32),
        scratch_shapes=[pltpu.VMEM((tm, d), BF16)],
        compiler_params=_params(("parallel", "arbitrary"), VMEM_LIMIT),
        name="in_proj",
    )(h, g.reshape(1, d), w_all, w_tail)


def _attn_kernel(lam_ref, q_ref, k_ref, v_ref, bias_ref, g_ref, o_ref,
                 kb_ref, vb_ref, qz_ref, m_ref, a_ref, acc_ref, s0_ref, s1_ref, p_ref, *, tq, tk, out_scale):
    qi = pl.program_id(2)
    ratio = tq // tk
    heads = range(ATTN_HPS)

    @pl.when(qi == 0)
    def _():
        for h in heads:
            hl = slice(h * LANES, (h + 1) * LANES)
            kb_ref[h] = k_ref[:, hl].astype(BF16)
            vb_ref[h, :, 0:LANES] = v_ref[:, hl].astype(BF16)
            vb_ref[h, :, LANES:2 * LANES] = jnp.ones((vb_ref.shape[1], LANES), BF16)

    for h in heads:
        q = q_ref[:, h * LANES:(h + 1) * LANES] * (ATTN_QK_DIM ** -0.5 * LOG2E)
        lane = lax.broadcasted_iota(jnp.int32, q.shape, 1)
        qz_ref[h, 0:tq, :] = jnp.where(lane < ATTN_QK_DIM, q, 0.0).astype(BF16)
        qz_ref[h, tq:2 * tq, :] = jnp.where(lane >= ATTN_QK_DIM, q, 0.0).astype(BF16)
    m_ref[...] = jnp.full(m_ref.shape, -jnp.inf, F32)
    acc_ref[...] = jnp.zeros(acc_ref.shape, F32)

    n_col = tk // LANES
    groups = [slice(g * ATTN_ROWS, (g + 1) * ATTN_ROWS) for g in range(2 * tq // ATTN_ROWS)]

    def scores(j, s_ref):
        for h in heads:
            k = kb_ref[h, pl.ds(pl.multiple_of(j * tk, tk), tk), :]
            s_ref[h] = lax.dot_general(qz_ref[h], k, (((1,), (1,)), ((), ())), preferred_element_type=F32)

    def softmax_pv(j, s_ref, biased):
        if biased:
            kind = jnp.clip(j - ratio * qi + ATTN_NEAR, 0, ATTN_NEAR + ratio - 1)
        for h in heads:
            for rows in groups:
                cols = [s_ref[h, rows, i * LANES:(i + 1) * LANES] for i in range(n_col)]
                if biased:
                    b0 = rows.start % tq
                    cols = [x + bias_ref[h, kind, b0:b0 + ATTN_ROWS, i * LANES:(i + 1) * LANES]
                            for i, x in enumerate(cols)]
                    for i, x in enumerate(cols):
                        s_ref[h, rows, i * LANES:(i + 1) * LANES] = x
                mx = functools.reduce(jnp.maximum, cols)
                m_old = m_ref[h, rows, :]
                m_new = jnp.maximum(m_old, jnp.max(mx, axis=-1, keepdims=True))
                m_ref[h, rows, :] = m_new
                a_ref[h, rows, :] = jnp.exp2(m_old - m_new)
        for h in heads:
            for rows in groups:
                m_new = m_ref[h, rows, :]
                for i in range(n_col):
                    p_ref[h, rows, i * LANES:(i + 1) * LANES] = jnp.exp2(
                        s_ref[h, rows, i * LANES:(i + 1) * LANES] - m_new).astype(BF16)
        for h in heads:
            v = vb_ref[h, pl.ds(pl.multiple_of(j * tk, tk), tk), :]
            alpha = a_ref[h]
            acc_ref[h] = (jnp.concatenate([alpha, alpha], axis=1) * acc_ref[h]
                          + jnp.dot(p_ref[h], v, preferred_element_type=F32))

    n_tiles = ratio * (qi + 1)
    first = lax.rem(n_tiles, 2)
    n_far = jnp.maximum(ratio * qi - ATTN_NEAR + 1, 0)
    n_far_pairs = jnp.maximum(n_far - first, 0) // 2
    n_pairs = (n_tiles - first) // 2

    @pl.when(first == 1)
    def _():
        scores(0, s1_ref)
        scores(jnp.minimum(1, n_tiles - 1), s0_ref)
        softmax_pv(0, s1_ref, True)

    @pl.when(first == 0)
    def _():
        scores(0, s0_ref)

    def pair_body(jj, carry, biased):
        j = first + 2 * jj
        scores(j + 1, s1_ref)
        softmax_pv(j, s0_ref, biased)
        scores(jnp.minimum(j + 2, n_tiles - 1), s0_ref)
        softmax_pv(j + 1, s1_ref, biased)
        return carry

    lax.fori_loop(0, n_far_pairs, functools.partial(pair_body, biased=False), 0)
    lax.fori_loop(n_far_pairs, n_pairs, functools.partial(pair_body, biased=True), 0)

    for h in heads:
        inv_l = 1.0 / acc_ref[h, :, LANES:2 * LANES]
        o1 = acc_ref[h, 0:tq, 0:LANES] * inv_l[0:tq]
        o2 = acc_ref[h, tq:2 * tq, 0:LANES] * inv_l[tq:2 * tq]
        o = o1 - lam_ref[0] * o2
        ms = jnp.mean(o * o, axis=-1, keepdims=True)
        o_ref[:, h * LANES:(h + 1) * LANES] = (o * lax.rsqrt(ms + EPS) * g_ref[...] * out_scale).astype(o_ref.dtype)


def _t5_bucket(dist):
    n = jnp.maximum(dist, 0)
    max_exact = REL_BUCKETS // 2
    large = max_exact + (jnp.log(jnp.maximum(n, 1).astype(F32) / max_exact)
                         / math.log(REL_MAX_DIST / max_exact) * (REL_BUCKETS - max_exact)).astype(jnp.int32)
    large = jnp.minimum(large, REL_BUCKETS - 1)
    return jnp.where(n < max_exact, n, large)


def _attn_bias_tiles(rel_bias, tq, tk):
    table = rel_bias.astype(F32)
    far = table[REL_BUCKETS - 1]
    r = jnp.arange(tq)[:, None]
    c = jnp.arange(tk)[None, :]

    def lookup(dist):
        bucket = _t5_bucket(dist)
        out = jnp.zeros((ATTN_HEADS,) + dist.shape, F32)
        for b in range(REL_BUCKETS):
            out = jnp.where((bucket == b)[None], table[b][:, None, None], out)
        return out

    tiles = [jnp.zeros((ATTN_HEADS, tq, tk), F32)]
    for rel in range(-ATTN_NEAR + 1, tq // tk):
        dist = r - c - rel * tk
        tiles.append(jnp.where((dist >= 0)[None], (lookup(dist) - far[:, None, None]) * LOG2E, -1e30))
    return jnp.stack(tiles, axis=1)


def _diff_attention(proj, bias, lam, subln_g, lam_init, batch, seq):
    t = proj.shape[0]
    tq = min(ATTN_TQ, seq)
    tk = min(ATTN_TK, seq)
    ratio = tq // tk
    hps = ATTN_HPS
    assert (ATTN_NEAR - 1) * tk >= REL_MAX_DIST and ratio * tk == tq
    assert ATTN_HEADS % hps == 0 and ATTN_COL_BLOCK % hps == 0
    nq = seq // tq
    kernel = functools.partial(_attn_kernel, tq=tq, tk=tk, out_scale=1.0 - lam_init)
    qb, kb, vb = (ATTN_COL_BLOCK // hps, (ATTN_COL_BLOCK + ATTN_HEADS) // hps,
                  (ATTN_COL_BLOCK + 2 * ATTN_HEADS) // hps)
    w = hps * LANES
    return pl.pallas_call(
        kernel,
        grid=(batch, ATTN_HEADS // hps, nq),
        in_specs=[
            pl.BlockSpec(memory_space=pltpu.SMEM),
            pl.BlockSpec((tq, w), lambda b, h, i: (b * nq + i, qb + h)),
            pl.BlockSpec((seq, w), lambda b, h, i: (b, kb + h)),
            pl.BlockSpec((seq, w), lambda b, h, i: (b, vb + h)),
            pl.BlockSpec((hps,) + bias.shape[1:], lambda b, h, i: (h, 0, 0, 0)),
            pl.BlockSpec((1, LANES), lambda b, h, i: (0, 0)),
        ],
        out_specs=pl.BlockSpec((tq, w), lambda b, h, i: (b * nq + i, h)),
        out_shape=jax.ShapeDtypeStruct((t, ATTN_WIDTH), BF16),
        scratch_shapes=[
            pltpu.VMEM((hps, seq, LANES), BF16),
            pltpu.VMEM((hps, seq, 2 * LANES), BF16),
            pltpu.VMEM((hps, 2 * tq, LANES), BF16),
            pltpu.VMEM((hps, 2 * tq, LANES), F32),
            pltpu.VMEM((hps, 2 * tq, LANES), F32),
            pltpu.VMEM((hps, 2 * tq, 2 * LANES), F32),
            pltpu.VMEM((hps, 2 * tq, tk), F32),
            pltpu.VMEM((hps, 2 * tq, tk), F32),
            pltpu.VMEM((hps, 2 * tq, tk), BF16),
        ],
        compiler_params=_params(("parallel", "parallel", "arbitrary"), VMEM_LIMIT),
        name="diff_attn",
    )(lam.reshape(1), proj, proj, proj, bias, subln_g.reshape(1, LANES))


def _s5_kernel(u_ref, perm_ref, permt_ref, wb_ref, wc_ref, pw_ref, d_ref, wglu_ref, o_ref,
               xr_ref, xi_ref, cr_ref, ci_ref, kr_ref, ki_ref, *, tb):
    @pl.when(pl.program_id(1) == 0)
    def _():
        cr_ref[...] = jnp.zeros(cr_ref.shape, F32)
        ci_ref[...] = jnp.zeros(ci_ref.shape, F32)

    un = u_ref[...]
    uh = un.astype(BF16)
    ul = (un - uh.astype(F32)).astype(BF16)
    up = jnp.dot(perm_ref[...], jnp.concatenate([uh, ul], axis=1), preferred_element_type=F32)
    ub = up[:, :S5_WIDTH].astype(BF16)
    u = up[:, :S5_WIDTH] + up[:, S5_WIDTH:]
    nblk = S5_WIDTH // S5_BLK
    sw = S5_LANES // nblk
    tpb = sw // LANES
    for i in range(nblk):
        bu = jnp.dot(ub[:, i * S5_BLK:(i + 1) * S5_BLK], wb_ref[i], preferred_element_type=F32)
        for k in range(tpb):
            xr_ref[i * tpb + k] = bu[:, k * LANES:(k + 1) * LANES]
            xi_ref[i * tpb + k] = bu[:, sw + k * LANES:sw + (k + 1) * LANES]

    sl = tb // SUBLANES
    for c in range(S5_LANES // S5_CHUNK):
        tiles = range(c * S5_CHUNK // LANES, (c + 1) * S5_CHUNK // LANES)
        lss = [slice(n * LANES, (n + 1) * LANES) for n in tiles]
        a = [(pw_ref[0, 0, :, ls], pw_ref[1, 0, :, ls]) for ls in lss]

        def local_step(t, carry, tiles=tiles, a=a):
            rows = pl.ds(pl.multiple_of(t * SUBLANES, SUBLANES), SUBLANES)
            out = []
            for n, (ar, ai), (pr, pi) in zip(tiles, a, carry):
                xr = ar * pr - ai * pi + xr_ref[n, rows, :]
                xi = ar * pi + ai * pr + xi_ref[n, rows, :]
                xr_ref[n, rows, :] = xr
                xi_ref[n, rows, :] = xi
                out.append((xr, xi))
            return tuple(out)

        zero = jnp.zeros((SUBLANES, LANES), F32)
        ends = lax.fori_loop(0, sl, local_step, tuple((zero, zero) for _ in tiles), unroll=2)

        for ls, (er, ei) in zip(lss, ends):
            lr, li = pw_ref[0, sl - 1, 0:1, ls], pw_ref[1, sl - 1, 0:1, ls]
            kr, ki = cr_ref[:, ls], ci_ref[:, ls]
            for sub in range(SUBLANES):
                kr_ref[sub:sub + 1, ls] = kr
                ki_ref[sub:sub + 1, ls] = ki
                kr, ki = (er[sub:sub + 1, :] + lr * kr - li * ki, ei[sub:sub + 1, :] + lr * ki + li * kr)
            cr_ref[:, ls] = kr
            ci_ref[:, ls] = ki

        k8 = [(kr_ref[:, ls], ki_ref[:, ls]) for ls in lss]

        def fix_step(t, carry, tiles=tiles, lss=lss, k8=k8):
            rows = pl.ds(pl.multiple_of(t * SUBLANES, SUBLANES), SUBLANES)
            for n, ls, (kr8, ki8) in zip(tiles, lss, k8):
                pr = pw_ref[0, t, :, ls]
                pi = pw_ref[1, t, :, ls]
                xr_ref[n, rows, :] = xr_ref[n, rows, :] + pr * kr8 - pi * ki8
                xi_ref[n, rows, :] = xi_ref[n, rows, :] + pr * ki8 + pi * kr8
            return carry

        lax.fori_loop(0, sl, fix_step, 0, unroll=2)

    ys = []
    for i in range(nblk):
        xc = jnp.concatenate([xr_ref[i * tpb + k] for k in range(tpb)]
                             + [xi_ref[i * tpb + k] for k in range(tpb)], axis=1)
        ys.append(jnp.dot(xc.astype(BF16), wc_ref[i], preferred_element_type=F32))
    y = jnp.concatenate(ys, axis=1) + d_ref[...] * u
    z = 0.5 * y * (1.0 + jnp.tanh(math.sqrt(2.0 / math.pi) * (y + 0.044715 * (y * y * y))))
    gate = jax.nn.sigmoid(jnp.dot(z.astype(BF16), wglu_ref[...].astype(BF16), preferred_element_type=F32))
    o_ref[...] = jnp.dot(permt_ref[...], (z * gate).astype(BF16),
                         preferred_element_type=F32).astype(o_ref.dtype)


def _s5_prep(a_re, a_im, log_dt, b_re, b_im, c_re, c_im, steps):
    lr = jnp.minimum(a_re.astype(F32), -1e-4)
    li = a_im.astype(F32)
    dt = jnp.exp(log_dt.astype(F32))[:, None]
    mag = jnp.exp(lr * dt)
    ab_re = mag * jnp.cos(li * dt)
    ab_im = mag * jnp.sin(li * dt)
    den = lr * lr + li * li
    nr = ab_re - 1.0
    ni = ab_im
    f_re = (nr * lr + ni * li) / den
    f_im = (ni * lr - nr * li) / den
    br = b_re.astype(F32)
    bi = b_im.astype(F32)
    bb_re = f_re[..., None] * br - f_im[..., None] * bi
    bb_im = f_re[..., None] * bi + f_im[..., None] * br

    nblk = S5_WIDTH // S5_BLK
    gpb = S5_GROUPS // nblk
    eye = jnp.eye(gpb, dtype=F32)

    def bmat(bb):
        bb = bb.reshape(nblk, gpb, S5_STATE, S5_GROUP_CH)
        m = jnp.einsum('ignc,gh->igchn', bb, eye)
        return m.reshape(nblk, gpb * S5_GROUP_CH, gpb * S5_STATE)

    def cmat(cc):
        cc = cc.astype(F32).reshape(nblk, gpb, S5_GROUP_CH, S5_STATE)
        m = jnp.einsum('igcn,gh->ignhc', cc, eye)
        return m.reshape(nblk, gpb * S5_STATE, gpb * S5_GROUP_CH)

    w_b = jnp.concatenate([bmat(bb_re), bmat(bb_im)], axis=2).astype(BF16)
    w_c = jnp.concatenate([cmat(c_re), -cmat(c_im)], axis=1).astype(BF16)

    j = jnp.arange(1, steps + 1, dtype=F32)[:, None, None]
    m = jnp.exp(lr * dt * j)
    pows = jnp.stack([(m * jnp.cos(li * dt * j)).reshape(steps, 1, -1),
                      (m * jnp.sin(li * dt * j)).reshape(steps, 1, -1)])
    pows = jnp.broadcast_to(pows, (2, steps, SUBLANES, S5_LANES))
    return w_b, w_c, pows


def _s5_mixer(proj, w_b, w_c, tabs, d_skip, w_glu, batch, seq):
    t = proj.shape[0]
    tb = min(S5_TB, seq)
    nt = seq // tb
    sl = tb // SUBLANES
    r = np.arange(tb)
    perm = np.zeros((tb, tb), np.float32)
    perm[r, (r % SUBLANES) * sl + r // SUBLANES] = 1.0
    perm_t = jnp.asarray(perm.T, BF16)
    perm = jnp.asarray(perm, BF16)
    return pl.pallas_call(
        functools.partial(_s5_kernel, tb=tb),
        grid=(batch, nt),
        in_specs=[
            pl.BlockSpec((tb, S5_WIDTH), lambda b, i: (b * nt + i, 0)),
            pl.BlockSpec((tb, tb), lambda b, i: (0, 0)),
            pl.BlockSpec((tb, tb), lambda b, i: (0, 0)),
            pl.BlockSpec(w_b.shape, lambda b, i: (0, 0, 0)),
            pl.BlockSpec(w_c.shape, lambda b, i: (0, 0, 0)),
            pl.BlockSpec(tabs.shape, lambda b, i: (0, 0, 0, 0)),
            pl.BlockSpec((1, S5_WIDTH), lambda b, i: (0, 0)),
            pl.BlockSpec((S5_WIDTH, S5_WIDTH), lambda b, i: (0, 0)),
        ],
        out_specs=pl.BlockSpec((tb, S5_WIDTH), lambda b, i: (b * nt + i, 0)),
        out_shape=jax.ShapeDtypeStruct((t, S5_WIDTH), BF16),
        scratch_shapes=[
            pltpu.VMEM((S5_LANES // LANES, tb, LANES), F32),
            pltpu.VMEM((S5_LANES // LANES, tb, LANES), F32),
            pltpu.VMEM((1, S5_LANES), F32),
            pltpu.VMEM((1, S5_LANES), F32),
            pltpu.VMEM((SUBLANES, S5_LANES), F32),
            pltpu.VMEM((SUBLANES, S5_LANES), F32),
        ],
        compiler_params=_params(("parallel", "arbitrary"), VMEM_LIMIT),
        name="s5_mixer",
    )(proj, perm, perm_t, w_b, w_c, tabs, d_skip.reshape(1, S5_WIDTH), w_glu)


def _hgrn_consts(c):
    t = np.arange(c)[:, None]
    u = np.arange(c)[None, :]
    mats = [u <= t]
    masks = [np.eye(c, dtype=bool)]
    h = 1
    while h < c:
        mid = (t // (2 * h)) * 2 * h + h
        if 2 * h < SUBLANES:
            mats.append(np.where(t >= mid, (u >= mid) & (u <= t), (u > t) & (u < mid)))
        masks.append(((u // (2 * h)) == (t // (2 * h))) & (t >= mid) & (u < mid))
        h *= 2
    return (np.concatenate(mats, axis=0).astype(np.float32), np.stack(masks).astype(np.float32))


def _hgrn_kernel(q_ref, f_ref, i_ref, g_ref, lb_ref, ng_ref, mall_ref, mask_ref, o_ref,
                 e_ref, st_ref, *, c):
    @pl.when(pl.program_id(1) == 0)
    def _():
        st_ref[...] = jnp.zeros(st_ref.shape, F32)

    lb = lb_ref[...]
    fl = f_ref[...]
    la = jnp.log(lb)
    lbb = jnp.log1p(-lb) + (jnp.minimum(fl, 0.0) - jnp.log1p(jnp.exp(-jnp.abs(fl))))
    logf = jnp.maximum(la, lbb) + jnp.log1p(jnp.exp(-jnp.abs(la - lbb)))
    hi = logf.astype(BF16)
    lo = (logf - hi.astype(F32)).astype(BF16)
    e2 = jnp.dot(mall_ref[...], jnp.concatenate([hi, lo], axis=1), preferred_element_type=F32)
    es = e2[:, :HGRN_WIDTH] + e2[:, HGRN_WIDTH:]
    n_levels = mask_ref.shape[0] - 1
    n_small = mall_ref.shape[0] // c - 1
    gall = es[0:c]
    e_ref[0:c, :] = gall
    e_ref[c:2 * c, :] = gall[c - 1:c, :] - gall
    e_ref[2 * c:(2 + n_small) * c, :] = es[c:]
    for lv in range(n_small, n_levels):
        blk = 2 << lv
        g3 = gall.reshape(c // blk, blk, HGRN_WIDTH)
        gm = jnp.broadcast_to(g3[:, blk // 2 - 1:blk // 2, :], g3.shape)
        e_ref[(2 + lv) * c:(3 + lv) * c, :] = (-jnp.abs(g3 - gm)).reshape(c, HGRN_WIDTH)

    nt = (((1,), (1,)), ((), ()))
    for hd in range(HGRN_HEADS):
        ls = slice(hd * HGRN_DIM, (hd + 1) * HGRN_DIM)
        qr = q_ref[:, ls]
        q = qr * jax.nn.sigmoid(qr) * (HGRN_DIM ** -0.5)
        k = (1.0 - lb[:, ls]) * jax.nn.sigmoid(-fl[:, ls])
        vb = i_ref[:, ls].astype(BF16)
        p = lax.dot_general(q.astype(BF16), k.astype(BF16), nt, preferred_element_type=F32) * mask_ref[0]
        for lv in range(n_levels):
            fac = jnp.exp(e_ref[(2 + lv) * c:(3 + lv) * c, ls])
            s = lax.dot_general((q * fac).astype(BF16), (k * fac).astype(BF16), nt,
                                preferred_element_type=F32)
            p = p + s * mask_ref[lv + 1]
        o = jnp.dot(p.astype(BF16), vb, preferred_element_type=F32)
        gcum = e_ref[0:c, ls]
        st = st_ref[hd]
        o = o + lax.dot_general((q * jnp.exp(gcum)).astype(BF16), st.astype(BF16), nt,
                                preferred_element_type=F32)
        kd = (k * jnp.exp(e_ref[c:2 * c, ls])).astype(BF16)
        st_ref[hd] = st * jnp.exp(gcum[c - 1:c, :]) + lax.dot_general(
            vb, kd, (((0,), (0,)), ((), ())), preferred_element_type=F32)
        ms = jnp.mean(o * o, axis=-1, keepdims=True)
        o = o * lax.rsqrt(ms + EPS) * ng_ref[...] * jax.nn.sigmoid(g_ref[:, ls])
        o_ref[:, ls] = o.astype(o_ref.dtype)


def _hgrn_mixer(proj, lb, norm_g, batch, seq):
    t = proj.shape[0]
    c = min(HGRN_C, seq)
    nc = seq // c
    m_all, masks = _hgrn_consts(c)
    m_all = jnp.asarray(m_all, BF16)
    masks = jnp.asarray(masks, F32)

    def col(j):
        return pl.BlockSpec((c, HGRN_WIDTH), lambda b, i: (b * nc + i, j))

    return pl.pallas_call(
        functools.partial(_hgrn_kernel, c=c),
        grid=(batch, nc),
        in_specs=[
            col(1), col(2), col(3), col(4),
            pl.BlockSpec((1, HGRN_WIDTH), lambda b, i: (0, 0)),
            pl.BlockSpec((1, HGRN_DIM), lambda b, i: (0, 0)),
            pl.BlockSpec(m_all.shape, lambda b, i: (0, 0)),
            pl.BlockSpec(masks.shape, lambda b, i: (0, 0, 0)),
        ],
        out_specs=pl.BlockSpec((c, HGRN_WIDTH), lambda b, i: (b * nc + i, 0)),
        out_shape=jax.ShapeDtypeStruct((t, HGRN_WIDTH), BF16),
        scratch_shapes=[
            pltpu.VMEM(((masks.shape[0] + 1) * c, HGRN_WIDTH), F32),
            pltpu.VMEM((HGRN_HEADS, HGRN_DIM, HGRN_DIM), F32),
        ],
        compiler_params=_params(("parallel", "arbitrary"), VMEM_LIMIT),
        name="hgrn2_mixer",
    )(proj, proj, proj, proj, lb.reshape(1, HGRN_WIDTH), norm_g.reshape(1, HGRN_DIM), m_all, masks)


def _out_proj_kernel(h_ref, a_ref, s_ref, r_ref, w_ref, gn_ref, rh_ref, rl_ref,
                     ho_ref, hn_ref, eid_ref, gate_ref):
    s_lo, r_lo = ATTN_WIDTH, ATTN_WIDTH + S5_WIDTH
    acc = jnp.dot(a_ref[...], w_ref[0:s_lo, :], preferred_element_type=F32)
    acc = acc + jnp.dot(s_ref[...], w_ref[s_lo:r_lo, :], preferred_element_type=F32)
    acc = acc + jnp.dot(r_ref[...], w_ref[r_lo:, :], preferred_element_type=F32)
    h = h_ref[...] + acc
    ho_ref[...] = h
    ms = jnp.mean(h * h, axis=-1, keepdims=True)
    hn = h * lax.rsqrt(ms + EPS) * gn_ref[...]
    half = hn.shape[1] // 2
    hn_ref[...] = _pack_bf16_pair(hn[:, :half], hn[:, half:])

    hh = hn.astype(BF16)
    hl = (hn - hh.astype(F32)).astype(BF16)
    logits = (jnp.dot(hh, rh_ref[...], preferred_element_type=F32)
              + jnp.dot(hh, rl_ref[...], preferred_element_type=F32)
              + jnp.dot(hl, rh_ref[...], preferred_element_type=F32))
    lane = lax.broadcasted_iota(jnp.int32, logits.shape, 1).astype(F32)
    neg = -jnp.inf
    big = 1e9
    is_group = jnp.where(lane >= N_EXPERTS, jnp.where(lane < N_EXPERTS + N_GROUPS, 1.0, 0.0), 0.0)
    gl = jnp.where(is_group > 0, logits, neg)
    gmax = jnp.max(gl, axis=-1, keepdims=True)
    g_lane = jnp.min(jnp.where(gl == gmax, lane, big), axis=-1, keepdims=True)
    p_g = 1.0 / jnp.sum(jnp.exp(gl - gmax), axis=-1, keepdims=True)
    lo_lane = (g_lane - N_EXPERTS) * EXPERTS_PER_GROUP
    in_group = jnp.where(lane >= lo_lane, jnp.where(lane < lo_lane + EXPERTS_PER_GROUP, 1.0, 0.0), 0.0)
    el = jnp.where(in_group > 0, logits, neg)
    t1 = jnp.max(el, axis=-1, keepdims=True)
    i1 = jnp.min(jnp.where(el == t1, lane, big), axis=-1, keepdims=True)
    el2 = jnp.where(lane == i1, neg, el)
    t2 = jnp.max(el2, axis=-1, keepdims=True)
    i2 = jnp.min(jnp.where(el2 == t2, lane, big), axis=-1, keepdims=True)
    e21 = jnp.exp(t2 - t1)
    g1 = p_g / (1.0 + e21)
    g2 = p_g * e21 / (1.0 + e21)
    eid_ref[...] = jnp.where(lane == 0, i1, jnp.where(lane == 1, i2, 0.0)).astype(jnp.int32)
    gate_ref[...] = jnp.where(lane == 0, g1, jnp.where(lane == 1, g2, 0.0))


def _out_proj(h, attn, s5, hg, w_out, gn, r_hi, r_lo):
    t, d = h.shape
    tm = min(OUT_TM, t)

    def rows(w):
        return pl.BlockSpec((tm, w), lambda i: (i, 0))

    def full(a):
        return pl.BlockSpec(a.shape, lambda i: (0,) * a.ndim)

    return pl.pallas_call(
        _out_proj_kernel,
        grid=(t // tm,),
        in_specs=[rows(d), rows(ATTN_WIDTH), rows(S5_WIDTH), rows(HGRN_WIDTH),
                  full(w_out), pl.BlockSpec((1, d), lambda i: (0, 0)),
                  full(r_hi), full(r_lo)],
        out_specs=[rows(d), rows(d // 2), rows(LANES), rows(LANES)],
        out_shape=[jax.ShapeDtypeStruct((t, d), F32), jax.ShapeDtypeStruct((t, d // 2), jnp.uint32),
                   jax.ShapeDtypeStruct((t, LANES), jnp.int32), jax.ShapeDtypeStruct((t, LANES), F32)],
        compiler_params=_params(("parallel",), VMEM_LIMIT),
        name="out_proj_router",
    )(h, attn, s5, hg, w_out, gn.reshape(1, d), r_hi, r_lo)


def _moe_plan(eid, tm):
    flat = eid.reshape(-1)
    n_slots = flat.shape[0]
    onehot = (flat[:, None] == jnp.arange(N_EXPERTS, dtype=jnp.int32)[None, :]).astype(jnp.int32)
    csum = jnp.cumsum(onehot, axis=0)
    rank = jnp.sum(onehot * csum, axis=1) - 1
    counts = csum[-1]
    tiles = (counts + tm - 1) // tm
    tile_end = jnp.cumsum(tiles)
    tile_start = tile_end - tiles
    slot_start = jnp.cumsum(counts) - counts
    pos = jnp.sum(onehot * tile_start[None, :], axis=1) * tm + rank
    tok_sorted = jnp.argsort(flat, stable=True).astype(jnp.int32) // TOP_K
    tok_sorted = jnp.pad(tok_sorted, (0, tm))
    nt_max = (n_slots + N_EXPERTS * (tm - 1)) // tm
    n_used = tile_end[-1]
    j = jnp.minimum(jnp.arange(nt_max, dtype=jnp.int32), n_used - 1)
    tile_expert = jnp.sum((j[:, None] >= tile_end[None, :]).astype(jnp.int32), axis=1)
    onehot_t = (tile_expert[:, None] == jnp.arange(N_EXPERTS, dtype=jnp.int32)[None, :]).astype(jnp.int32)
    tile_slot0 = jnp.sum(onehot_t * (slot_start - tile_start * tm)[None, :], axis=1) + j * tm
    used = (tiles > 0).astype(jnp.int32)
    ordinal = jnp.cumsum(used) - used
    ids = jnp.arange(N_EXPERTS, dtype=jnp.int32)
    later = jnp.where((ids[None, :] > ids[:, None]) & (used[None, :] > 0), ids[None, :], N_EXPERTS)
    nxt = jnp.min(later, axis=1)
    nxt = jnp.where(nxt >= N_EXPERTS, -1, nxt)
    tile_wslot = jnp.sum(onehot_t * (ordinal % 2)[None, :], axis=1)
    tile_next = jnp.sum(onehot_t * nxt[None, :], axis=1)
    return dict(pos=pos.astype(jnp.int32), tok_sorted=tok_sorted, tile_expert=tile_expert.astype(jnp.int32),
                tile_slot0=tile_slot0.astype(jnp.int32), tile_wslot=tile_wslot.astype(jnp.int32),
                tile_next=tile_next.astype(jnp.int32), n_used=n_used.reshape(1).astype(jnp.int32),
                nt_max=nt_max)


def _expert_kernel(tok_ref, te_ref, s0_ref, ws_ref, ne_ref, nu_ref, hn_ref, wg_ref, wu_ref, wd_ref, y_ref,
                   xa_ref, xb_ref, wgf_ref, wuf_ref, wdf_ref, wgb_ref, wub_ref, wdb_ref, sem, wsem,
                   *, tm, layer):
    j = pl.program_id(0)
    n_used = nu_ref[0]
    bufs = (xa_ref, xb_ref)

    def weight_copies(expert, slot):
        return [pltpu.make_async_copy(src.at[layer, expert], dst.at[slot], wsem.at[slot])
                for src, dst in ((wg_ref, wgf_ref), (wu_ref, wuf_ref), (wd_ref, wdf_ref))]

    def row_copy(buf, r, tok):
        return pltpu.make_async_copy(hn_ref.at[pl.ds(tok, 1)], bufs[buf].at[pl.ds(r, 1)], sem.at[buf])

    def wait_gather(buf):
        def wait(r, carry):
            row_copy(buf, r, 0).wait()
            return carry

        lax.fori_loop(0, tm, wait, 0, unroll=DMA_UNROLL)

    @pl.when(j == 0)
    def _():
        for cp in weight_copies(te_ref[0], ws_ref[0]):
            cp.start(priority=WEIGHT_DMA_PRIORITY)
        base = s0_ref[0]

        def issue(r, carry):
            row_copy(0, r, tok_ref[base + r]).start()
            return carry

        lax.fori_loop(0, tm, issue, 0, unroll=DMA_UNROLL)

    def tile(buf):
        wait_gather(buf)
        prev = te_ref[jnp.maximum(j - 1, 0)]

        @pl.when(jnp.logical_or(j == 0, te_ref[j] != prev))
        def _():
            slot = ws_ref[j]

            @pl.when(ne_ref[j] >= 0)
            def _():
                for cp in weight_copies(ne_ref[j], 1 - slot):
                    cp.start(priority=WEIGHT_DMA_PRIORITY)

            for cp in weight_copies(te_ref[j], slot):
                cp.wait()
            wgb_ref[...] = wgf_ref[slot].astype(BF16)
            wub_ref[...] = wuf_ref[slot].astype(BF16)
            wdb_ref[...] = wdf_ref[slot].astype(BF16)

        nbase = s0_ref[jnp.minimum(j + 1, n_used - 1)]
        for r in range(tm):
            row_copy(1 - buf, r, tok_ref[nbase + r]).start()

        x_lo, x_hi = _unpack_bf16_pair(bufs[buf][...])
        x = jnp.concatenate([x_lo.astype(BF16), x_hi.astype(BF16)], axis=1)
        g = jnp.dot(x, wgb_ref[...], preferred_element_type=F32)
        u = jnp.dot(x, wub_ref[...], preferred_element_type=F32)
        hmid = (g * jax.nn.sigmoid(g) * u).astype(BF16)
        y = jnp.dot(hmid, wdb_ref[...], preferred_element_type=F32)
        half = y.shape[1] // 2
        y_ref[...] = _pack_bf16_pair(y[:, :half], y[:, half:])

        @pl.when(j == n_used - 1)
        def _():
            wait_gather(1 - buf)

    for parity in range(2):
        @pl.when(jnp.logical_and(j < n_used, lax.rem(j, 2) == parity))
        def _(parity=parity):
            tile(parity)

    @pl.when(j >= n_used)
    def _():
        y_ref[...] = jnp.zeros(y_ref.shape, y_ref.dtype)


def _moe_experts(hn, plan, w_gate, w_up, w_down, layer, tm):
    t, dp = hn.shape
    d, de = w_gate.shape[2], w_gate.shape[3]
    nt_max = plan["nt_max"]
    grid_spec = pltpu.PrefetchScalarGridSpec(
        num_scalar_prefetch=6,
        grid=(nt_max,),
        in_specs=[pl.BlockSpec(memory_space=pl.ANY)] * 4,
        out_specs=pl.BlockSpec((tm, dp), lambda j, *_: (j, 0)),
        scratch_shapes=[pltpu.VMEM((tm, dp), jnp.uint32), pltpu.VMEM((tm, dp), jnp.uint32),
                        pltpu.VMEM((2, d, de), F32), pltpu.VMEM((2, d, de), F32), pltpu.VMEM((2, de, d), F32),
                        pltpu.VMEM((d, de), BF16), pltpu.VMEM((d, de), BF16), pltpu.VMEM((de, d), BF16),
                        pltpu.SemaphoreType.DMA((2,)), pltpu.SemaphoreType.DMA((2,))],
    )
    return pl.pallas_call(
        functools.partial(_expert_kernel, tm=tm, layer=layer),
        grid_spec=grid_spec,
        out_shape=jax.ShapeDtypeStruct((nt_max * tm, dp), jnp.uint32),
        compiler_params=_params(("arbitrary",), VMEM_LIMIT),
        name="moe_experts",
    )(plan["tok_sorted"], plan["tile_expert"], plan["tile_slot0"], plan["tile_wslot"], plan["tile_next"],
      plan["n_used"], hn, w_gate, w_up, w_down)


def _combine_kernel(pos_ref, h_ref, gate_ref, gn_ref, y_ref, o_ref, ba_ref, bb_ref, sem, *, tm, final_norm):
    i = pl.program_id(0)
    n = pl.num_programs(0)
    bufs = (ba_ref, bb_ref)

    def row_copy(b, k, r, src_row):
        return pltpu.make_async_copy(y_ref.at[pl.ds(src_row, 1)], bufs[b].at[k, pl.ds(r, 1)], sem.at[b])

    def wait_gather(b):
        def wait(r, carry):
            for k in range(TOP_K):
                row_copy(b, k, r, 0).wait()
            return carry

        lax.fori_loop(0, tm, wait, 0, unroll=DMA_UNROLL)

    @pl.when(i == 0)
    def _():
        def issue(r, carry):
            for k in range(TOP_K):
                row_copy(0, k, r, pos_ref[r * TOP_K + k]).start()
            return carry

        lax.fori_loop(0, tm, issue, 0, unroll=DMA_UNROLL)

    def tile(b):
        wait_gather(b)
        base = jnp.minimum(i + 1, n - 1) * (tm * TOP_K)
        for r in range(tm):
            for k in range(TOP_K):
                row_copy(1 - b, k, r, pos_ref[base + r * TOP_K + k]).start()

        gates = gate_ref[...]
        y0_lo, y0_hi = _unpack_bf16_pair(bufs[b][0])
        y1_lo, y1_hi = _unpack_bf16_pair(bufs[b][1])
        g0, g1 = gates[:, 0:1], gates[:, 1:2]
        h = h_ref[...] + jnp.concatenate([g0 * y0_lo + g1 * y1_lo, g0 * y0_hi + g1 * y1_hi], axis=1)
        if final_norm:
            ms = jnp.mean(h * h, axis=-1, keepdims=True)
            h = h * lax.rsqrt(ms + EPS) * gn_ref[...]
        o_ref[...] = h

        @pl.when(i == n - 1)
        def _():
            wait_gather(1 - b)

    for parity in range(2):
        @pl.when(lax.rem(i, 2) == parity)
        def _(parity=parity):
            tile(parity)


def _moe_combine(h, gates, ys, pos, final_g, final_norm):
    t, d = h.shape
    tm = min(COMB_TM, t)
    grid_spec = pltpu.PrefetchScalarGridSpec(
        num_scalar_prefetch=1,
        grid=(t // tm,),
        in_specs=[
            pl.BlockSpec((tm, d), lambda i, p: (i, 0)),
            pl.BlockSpec((tm, LANES), lambda i, p: (i, 0)),
            pl.BlockSpec((1, d), lambda i, p: (0, 0)),
            pl.BlockSpec(memory_space=pl.ANY),
        ],
        out_specs=pl.BlockSpec((tm, d), lambda i, p: (i, 0)),
        scratch_shapes=[pltpu.VMEM((TOP_K, tm, d // 2), jnp.uint32), pltpu.VMEM((TOP_K, tm, d // 2), jnp.uint32),
                        pltpu.SemaphoreType.DMA((2,))],
    )
    return pl.pallas_call(
        functools.partial(_combine_kernel, tm=tm, final_norm=final_norm),
        grid_spec=grid_spec,
        out_shape=jax.ShapeDtypeStruct((t, d), F32),
        compiler_params=_params(("arbitrary",), VMEM_LIMIT),
        name="moe_combine",
    )(pos, h, gates, final_g.reshape(1, d), ys)


def kernel(x, w_in, w_out, mix_norm_g, ffn_norm_g, rel_bias, diff_lambda, attn_subln_g, s5_a_re, s5_a_im, s5_log_dt, s5_b_re, s5_b_im, s5_c_re, s5_c_im, s5_d, s5_w_glu, hgrn_lb_logits, hgrn_norm_g, moe_w_group, moe_w_router, moe_w_gate, moe_w_up, moe_w_down, final_norm_g):
    batch, seq, d = x.shape
    depth = w_in.shape[0]
    t = batch * seq
    h = x.reshape(t, d)

    lb_cum = jnp.cumsum(jax.nn.softmax(hgrn_lb_logits.astype(F32), axis=0), axis=0)
    lb_all = lb_cum - lb_cum[0:1]
    attn_bias = _attn_bias_tiles(rel_bias, min(ATTN_TQ, seq), min(ATTN_TK, seq))

    for l in range(depth):
        proj = _in_proj(h, mix_norm_g[l], w_in, l)

        lam_init = 0.8 - 0.6 * math.exp(-0.3 * l)
        lv = diff_lambda[l].astype(F32)
        lam = jnp.exp(jnp.sum(lv[0] * lv[1])) - jnp.exp(jnp.sum(lv[2] * lv[3])) + lam_init
        attn = _diff_attention(proj, attn_bias, lam, attn_subln_g[l], lam_init, batch, seq)

        w_b, w_c, tabs = _s5_prep(s5_a_re[l], s5_a_im[l], s5_log_dt[l], s5_b_re[l], s5_b_im[l],
                                  s5_c_re[l], s5_c_im[l], min(S5_TB, seq) // SUBLANES)
        s5 = _s5_mixer(proj, w_b, w_c, tabs, s5_d[l], s5_w_glu[l], batch, seq)

        hg = _hgrn_mixer(proj, lb_all[l], hgrn_norm_g[l], batch, seq)

        wo = w_out[l].astype(BF16)
        w_r = jnp.concatenate([moe_w_router[l], moe_w_group[l]], axis=1).astype(F32)
        w_r = jnp.pad(w_r, ((0, 0), (0, LANES - w_r.shape[1])))
        r_hi = w_r.astype(BF16)
        r_lo = (w_r - r_hi.astype(F32)).astype(BF16)
        h, hn, eid, gates = _out_proj(h, attn, s5, hg, wo, ffn_norm_g[l], r_hi, r_lo)

        plan = _moe_plan(eid[:, :TOP_K], MOE_TM)
        ys = _moe_experts(hn, plan, moe_w_gate, moe_w_up, moe_w_down, l, MOE_TM)
        h = _moe_combine(h, gates, ys, plan["pos"], final_norm_g, final_norm=(l == depth - 1))

    return h.reshape(batch, seq, d)
```

```python
import functools
import math

import numpy as np
import jax
import jax.numpy as jnp
from jax import lax
from jax.experimental import pallas as pl
from jax.experimental.pallas import tpu as pltpu

F32 = jnp.float32
BF16 = jnp.bfloat16
EPS = 1e-6
LOG2E = 1.4426950408889634

D_MODEL = 2048
ATTN_HEADS = 6
ATTN_QK_DIM = 64
ATTN_V_DIM = 128
ATTN_WIDTH = ATTN_HEADS * ATTN_V_DIM
REL_BUCKETS = 32
REL_MAX_DIST = 128
S5_GROUPS = 40
S5_GROUP_CH = 16
S5_STATE = 64
S5_WIDTH = S5_GROUPS * S5_GROUP_CH
S5_LANES = S5_GROUPS * S5_STATE
HGRN_HEADS = 5
HGRN_DIM = 128
HGRN_WIDTH = HGRN_HEADS * HGRN_DIM
N_GROUPS = 4
EXPERTS_PER_GROUP = 8
N_EXPERTS = N_GROUPS * EXPERTS_PER_GROUP
TOP_K = 2
D_EXPERT = 512

LANES = 128
SUBLANES = 8

IN_COLS = 3 * ATTN_WIDTH + S5_WIDTH + 4 * HGRN_WIDTH
PROJ_TN = 256
IN_COLS_PAD = -(-IN_COLS // PROJ_TN) * PROJ_TN
ATTN_COL_BLOCK = (IN_COLS_PAD - 3 * ATTN_WIDTH) // LANES

PROJ_TM = 2048
ATTN_TQ = 512
ATTN_TK = 512
ATTN_NEAR = 2
ATTN_ROWS = 32
ATTN_HPS = 1
S5_TB = 256
S5_CHUNK = 640
S5_BLK = 128
HGRN_C = 128
OUT_TM = 256
MOE_TM = 256
COMB_TM = 128
DMA_UNROLL = 8
WEIGHT_DMA_PRIORITY = 1

VMEM_LIMIT = 56 * 1024 * 1024


def _params(sem, vmem=None):
    return pltpu.CompilerParams(dimension_semantics=sem, vmem_limit_bytes=vmem)


def _pack_bf16_pair(lo, hi):
    lo_bits = lax.bitcast_convert_type(lo.astype(BF16).astype(F32), jnp.uint32) >> 16
    hi_bits = lax.bitcast_convert_type(hi.astype(BF16).astype(F32), jnp.uint32) & jnp.uint32(0xFFFF0000)
    return lo_bits | hi_bits


def _unpack_bf16_pair(w):
    return (lax.bitcast_convert_type(w << 16, F32),
            lax.bitcast_convert_type(w & jnp.uint32(0xFFFF0000), F32))


def _in_proj_kernel(x_ref, g_ref, w_ref, wt_ref, o_ref, xn_ref, *, tail_tile):
    j = pl.program_id(1)

    @pl.when(j == 0)
    def _():
        x = x_ref[...]
        ms = jnp.mean(x * x, axis=-1, keepdims=True)
        xn_ref[...] = (x * lax.rsqrt(ms + EPS) * g_ref[...]).astype(BF16)

    w = jnp.where(j == tail_tile, wt_ref[...], w_ref[0])
    o_ref[...] = jnp.dot(xn_ref[...], w.astype(BF16), preferred_element_type=F32)


def _in_proj(h, g, w_all, layer):
    t, d = h.shape
    tm = min(PROJ_TM, t)
    tn = PROJ_TN
    a = 3 * ATTN_WIDTH
    rest = IN_COLS - a
    assert a % tn == 0 and IN_COLS_PAD - IN_COLS == tn - rest % tn
    rest_full, a_tiles = rest // tn, a // tn
    tail_tile = rest_full
    w_tail = lax.slice(w_all, (layer, 0, a + rest_full * tn), (layer + 1, d, IN_COLS))[0]
    w_tail = jnp.pad(w_tail, ((0, 0), (0, IN_COLS_PAD - IN_COLS)))

    def w_map(i, j):
        col = jnp.where(j < tail_tile, a_tiles + j, jnp.where(j == tail_tile, 0, j - tail_tile - 1))
        return (layer, 0, col)

    return pl.pallas_call(
        functools.partial(_in_proj_kernel, tail_tile=tail_tile),
        grid=(t // tm, IN_COLS_PAD // tn),
        in_specs=[
            pl.BlockSpec((tm, d), lambda i, j: (i, 0)),
            pl.BlockSpec((1, d), lambda i, j: (0, 0)),
            pl.BlockSpec((1, d, tn), w_map),
            pl.BlockSpec((d, tn), lambda i, j: (0, 0)),
        ],
        out_specs=pl.BlockSpec((tm, tn), lambda i, j: (i, j)),
        out_shape=jax.ShapeDtypeStruct((t, IN_COLS_PAD), F32),
        scratch_shapes=[pltpu.VMEM((tm, d), BF16)],
        compiler_params=_params(("parallel", "arbitrary"), VMEM_LIMIT),
        name="in_proj",
    )(h, g.reshape(1, d), w_all, w_tail)


def _attn_kernel(lam_ref, q_ref, k_ref, v_ref, bias_ref, g_ref, o_ref,
                 kb_ref, vb_ref, qz_ref, m_ref, a_ref, acc_ref, s0_ref, s1_ref, p_ref, *, tq, tk, out_scale):
    qi = pl.program_id(2)
    ratio = tq // tk
    heads = range(ATTN_HPS)

    @pl.when(qi == 0)
    def _():
        for h in heads:
            hl = slice(h * LANES, (h + 1) * LANES)
            kb_ref[h] = k_ref[:, hl].astype(BF16)
            vb_ref[h, :, 0:LANES] = v_ref[:, hl].astype(BF16)
            vb_ref[h, :, LANES:2 * LANES] = jnp.ones((vb_ref.shape[1], LANES), BF16)

    for h in heads:
        q = q_ref[:, h * LANES:(h + 1) * LANES] * (ATTN_QK_DIM ** -0.5 * LOG2E)
        lane = lax.broadcasted_iota(jnp.int32, q.shape, 1)
        qz_ref[h, 0:tq, :] = jnp.where(lane < ATTN_QK_DIM, q, 0.0).astype(BF16)
        qz_ref[h, tq:2 * tq, :] = jnp.where(lane >= ATTN_QK_DIM, q, 0.0).astype(BF16)
    m_ref[...] = jnp.full(m_ref.shape, -jnp.inf, F32)
    acc_ref[...] = jnp.zeros(acc_ref.shape, F32)

    n_col = tk // LANES
    groups = [slice(g * ATTN_ROWS, (g + 1) * ATTN_ROWS) for g in range(2 * tq // ATTN_ROWS)]

    def scores(j, s_ref):
        for h in heads:
            k = kb_ref[h, pl.ds(pl.multiple_of(j * tk, tk), tk), :]
            s_ref[h] = lax.dot_general(qz_ref[h], k, (((1,), (1,)), ((), ())), preferred_element_type=F32)

    def softmax_pv(j, s_ref, biased):
        if biased:
            kind = jnp.clip(j - ratio * qi + ATTN_NEAR, 0, ATTN_NEAR + ratio - 1)
        for h in heads:
            for rows in groups:
                cols = [s_ref[h, rows, i * LANES:(i + 1) * LANES] for i in range(n_col)]
                if biased:
                    b0 = rows.start % tq
                    cols = [x + bias_ref[h, kind, b0:b0 + ATTN_ROWS, i * LANES:(i + 1) * LANES]
                            for i, x in enumerate(cols)]
                    for i, x in enumerate(cols):
                        s_ref[h, rows, i * LANES:(i + 1) * LANES] = x
                mx = functools.reduce(jnp.maximum, cols)
                m_old = m_ref[h, rows, :]
                m_new = jnp.maximum(m_old, jnp.max(mx, axis=-1, keepdims=True))
                m_ref[h, rows, :] = m_new
                a_ref[h, rows, :] = jnp.exp2(m_old - m_new)
        for h in heads:
            for rows in groups:
                m_new = m_ref[h, rows, :]
                for i in range(n_col):
                    p_ref[h, rows, i * LANES:(i + 1) * LANES] = jnp.exp2(
                        s_ref[h, rows, i * LANES:(i + 1) * LANES] - m_new).astype(BF16)
        for h in heads:
            v = vb_ref[h, pl.ds(pl.multiple_of(j * tk, tk), tk), :]
            alpha = a_ref[h]
            acc_ref[h] = (jnp.concatenate([alpha, alpha], axis=1) * acc_ref[h]
                          + jnp.dot(p_ref[h], v, preferred_element_type=F32))

    n_tiles = ratio * (qi + 1)
    first = lax.rem(n_tiles, 2)
    n_far = jnp.maximum(ratio * qi - ATTN_NEAR + 1, 0)
    n_far_pairs = jnp.maximum(n_far - first, 0) // 2
    n_pairs = (n_tiles - first) // 2

    @pl.when(first == 1)
    def _():
        scores(0, s1_ref)
        scores(jnp.minimum(1, n_tiles - 1), s0_ref)
        softmax_pv(0, s1_ref, True)

    @pl.when(first == 0)
    def _():
        scores(0, s0_ref)

    def pair_body(jj, carry, biased):
        j = first + 2 * jj
        scores(j + 1, s1_ref)
        softmax_pv(j, s0_ref, biased)
        scores(jnp.minimum(j + 2, n_tiles - 1), s0_ref)
        softmax_pv(j + 1, s1_ref, biased)
        return carry

    lax.fori_loop(0, n_far_pairs, functools.partial(pair_body, biased=False), 0)
    lax.fori_loop(n_far_pairs, n_pairs, functools.partial(pair_body, biased=True), 0)

    for h in heads:
        inv_l = 1.0 / acc_ref[h, :, LANES:2 * LANES]
        o1 = acc_ref[h, 0:tq, 0:LANES] * inv_l[0:tq]
        o2 = acc_ref[h, tq:2 * tq, 0:LANES] * inv_l[tq:2 * tq]
        o = o1 - lam_ref[0] * o2
        ms = jnp.mean(o * o, axis=-1, keepdims=True)
        o_ref[:, h * LANES:(h + 1) * LANES] = (o * lax.rsqrt(ms + EPS) * g_ref[...] * out_scale).astype(o_ref.dtype)


def _t5_bucket(dist):
    n = jnp.maximum(dist, 0)
    max_exact = REL_BUCKETS // 2
    large = max_exact + (jnp.log(jnp.maximum(n, 1).astype(F32) / max_exact)
                         / math.log(REL_MAX_DIST / max_exact) * (REL_BUCKETS - max_exact)).astype(jnp.int32)
    large = jnp.minimum(large, REL_BUCKETS - 1)
    return jnp.where(n < max_exact, n, large)


def _attn_bias_tiles(rel_bias, tq, tk):
    table = rel_bias.astype(F32)
    far = table[REL_BUCKETS - 1]
    r = jnp.arange(tq)[:, None]
    c = jnp.arange(tk)[None, :]

    def lookup(dist):
        bucket = _t5_bucket(dist)
        out = jnp.zeros((ATTN_HEADS,) + dist.shape, F32)
        for b in range(REL_BUCKETS):
            out = jnp.where((bucket == b)[None], table[b][:, None, None], out)
        return out

    tiles = [jnp.zeros((ATTN_HEADS, tq, tk), F32)]
    for rel in range(-ATTN_NEAR + 1, tq // tk):
        dist = r - c - rel * tk
        tiles.append(jnp.where((dist >= 0)[None], (lookup(dist) - far[:, None, None]) * LOG2E, -1e30))
    return jnp.stack(tiles, axis=1)


def _diff_attention(proj, bias, lam, subln_g, lam_init, batch, seq):
    t = proj.shape[0]
    tq = min(ATTN_TQ, seq)
    tk = min(ATTN_TK, seq)
    ratio = tq // tk
    hps = ATTN_HPS
    assert (ATTN_NEAR - 1) * tk >= REL_MAX_DIST and ratio * tk == tq
    assert ATTN_HEADS % hps == 0 and ATTN_COL_BLOCK % hps == 0
    nq = seq // tq
    kernel = functools.partial(_attn_kernel, tq=tq, tk=tk, out_scale=1.0 - lam_init)
    qb, kb, vb = (ATTN_COL_BLOCK // hps, (ATTN_COL_BLOCK + ATTN_HEADS) // hps,
                  (ATTN_COL_BLOCK + 2 * ATTN_HEADS) // hps)
    w = hps * LANES
    return pl.pallas_call(
        kernel,
        grid=(batch, ATTN_HEADS // hps, nq),
        in_specs=[
            pl.BlockSpec(memory_space=pltpu.SMEM),
            pl.BlockSpec((tq, w), lambda b, h, i: (b * nq + i, qb + h)),
            pl.BlockSpec((seq, w), lambda b, h, i: (b, kb + h)),
            pl.BlockSpec((seq, w), lambda b, h, i: (b, vb + h)),
            pl.BlockSpec((hps,) + bias.shape[1:], lambda b, h, i: (h, 0, 0, 0)),
            pl.BlockSpec((1, LANES), lambda b, h, i: (0, 0)),
        ],
        out_specs=pl.BlockSpec((tq, w), lambda b, h, i: (b * nq + i, h)),
        out_shape=jax.ShapeDtypeStruct((t, ATTN_WIDTH), BF16),
        scratch_shapes=[
            pltpu.VMEM((hps, seq, LANES), BF16),
            pltpu.VMEM((hps, seq, 2 * LANES), BF16),
            pltpu.VMEM((hps, 2 * tq, LANES), BF16),
            pltpu.VMEM((hps, 2 * tq, LANES), F32),
            pltpu.VMEM((hps, 2 * tq, LANES), F32),
            pltpu.VMEM((hps, 2 * tq, 2 * LANES), F32),
            pltpu.VMEM((hps, 2 * tq, tk), F32),
            pltpu.VMEM((hps, 2 * tq, tk), F32),
            pltpu.VMEM((hps, 2 * tq, tk), BF16),
        ],
        compiler_params=_params(("parallel", "parallel", "arbitrary"), VMEM_LIMIT),
        name="diff_attn",
    )(lam.reshape(1), proj, proj, proj, bias, subln_g.reshape(1, LANES))


def _s5_kernel(u_ref, perm_ref, permt_ref, wb_ref, wc_ref, pw_ref, d_ref, wglu_ref, o_ref,
               xr_ref, xi_ref, cr_ref, ci_ref, kr_ref, ki_ref, *, tb):
    @pl.when(pl.program_id(1) == 0)
    def _():
        cr_ref[...] = jnp.zeros(cr_ref.shape, F32)
        ci_ref[...] = jnp.zeros(ci_ref.shape, F32)

    un = u_ref[...]
    uh = un.astype(BF16)
    ul = (un - uh.astype(F32)).astype(BF16)
    up = jnp.dot(perm_ref[...], jnp.concatenate([uh, ul], axis=1), preferred_element_type=F32)
    ub = up[:, :S5_WIDTH].astype(BF16)
    u = up[:, :S5_WIDTH] + up[:, S5_WIDTH:]
    nblk = S5_WIDTH // S5_BLK
    sw = S5_LANES // nblk
    tpb = sw // LANES
    for i in range(nblk):
        bu = jnp.dot(ub[:, i * S5_BLK:(i + 1) * S5_BLK], wb_ref[i], preferred_element_type=F32)
        for k in range(tpb):
            xr_ref[i * tpb + k] = bu[:, k * LANES:(k + 1) * LANES]
            xi_ref[i * tpb + k] = bu[:, sw + k * LANES:sw + (k + 1) * LANES]

    sl = tb // SUBLANES
    for c in range(S5_LANES // S5_CHUNK):
        tiles = range(c * S5_CHUNK // LANES, (c + 1) * S5_CHUNK // LANES)
        lss = [slice(n * LANES, (n + 1) * LANES) for n in tiles]
        a = [(pw_ref[0, 0, :, ls], pw_ref[1, 0, :, ls]) for ls in lss]

        def local_step(t, carry, tiles=tiles, a=a):
            rows = pl.ds(pl.multiple_of(t * SUBLANES, SUBLANES), SUBLANES)
            out = []
            for n, (ar, ai), (pr, pi) in zip(tiles, a, carry):
                xr = ar * pr - ai * pi + xr_ref[n, rows, :]
                xi = ar * pi + ai * pr + xi_ref[n, rows, :]
                xr_ref[n, rows, :] = xr
                xi_ref[n, rows, :] = xi
                out.append((xr, xi))
            return tuple(out)

        zero = jnp.zeros((SUBLANES, LANES), F32)
        ends = lax.fori_loop(0, sl, local_step, tuple((zero, zero) for _ in tiles), unroll=2)

        for ls, (er, ei) in zip(lss, ends):
            lr, li = pw_ref[0, sl - 1, 0:1, ls], pw_ref[1, sl - 1, 0:1, ls]
            kr, ki = cr_ref[:, ls], ci_ref[:, ls]
            for sub in range(SUBLANES):
                kr_ref[sub:sub + 1, ls] = kr
                ki_ref[sub:sub + 1, ls] = ki
                kr, ki = (er[sub:sub + 1, :] + lr * kr - li * ki, ei[sub:sub + 1, :] + lr * ki + li * kr)
            cr_ref[:, ls] = kr
            ci_ref[:, ls] = ki

        k8 = [(kr_ref[:, ls], ki_ref[:, ls]) for ls in lss]

        def fix_step(t, carry, tiles=tiles, lss=lss, k8=k8):
            rows = pl.ds(pl.multiple_of(t * SUBLANES, SUBLANES), SUBLANES)
            for n, ls, (kr8, ki8) in zip(tiles, lss, k8):
                pr = pw_ref[0, t, :, ls]
                pi = pw_ref[1, t, :, ls]
                xr_ref[n, rows, :] = xr_ref[n, rows, :] + pr * kr8 - pi * ki8
                xi_ref[n, rows, :] = xi_ref[n, rows, :] + pr * ki8 + pi * kr8
            return carry

        lax.fori_loop(0, sl, fix_step, 0, unroll=2)

    ys = []
    for i in range(nblk):
        xc = jnp.concatenate([xr_ref[i * tpb + k] for k in range(tpb)]
                             + [xi_ref[i * tpb + k] for k in range(tpb)], axis=1)
        ys.append(jnp.dot(xc.astype(BF16), wc_ref[i], preferred_element_type=F32))
    y = jnp.concatenate(ys, axis=1) + d_ref[...] * u
    z = 0.5 * y * (1.0 + jnp.tanh(math.sqrt(2.0 / math.pi) * (y + 0.044715 * (y * y * y))))
    gate = jax.nn.sigmoid(jnp.dot(z.astype(BF16), wglu_ref[...].astype(BF16), preferred_element_type=F32))
    o_ref[...] = jnp.dot(permt_ref[...], (z * gate).astype(BF16),
                         preferred_element_type=F32).astype(o_ref.dtype)


def _s5_prep(a_re, a_im, log_dt, b_re, b_im, c_re, c_im, steps):
    lr = jnp.minimum(a_re.astype(F32), -1e-4)
    li = a_im.astype(F32)
    dt = jnp.exp(log_dt.astype(F32))[:, None]
    mag = jnp.exp(lr * dt)
    ab_re = mag * jnp.cos(li * dt)
    ab_im = mag * jnp.sin(li * dt)
    den = lr * lr + li * li
    nr = ab_re - 1.0
    ni = ab_im
    f_re = (nr * lr + ni * li) / den
    f_im = (ni * lr - nr * li) / den
    br = b_re.astype(F32)
    bi = b_im.astype(F32)
    bb_re = f_re[..., None] * br - f_im[..., None] * bi
    bb_im = f_re[..., None] * bi + f_im[..., None] * br

    nblk = S5_WIDTH // S5_BLK
    gpb = S5_GROUPS // nblk
    eye = jnp.eye(gpb, dtype=F32)

    def bmat(bb):
        bb = bb.reshape(nblk, gpb, S5_STATE, S5_GROUP_CH)
        m = jnp.einsum('ignc,gh->igchn', bb, eye)
        return m.reshape(nblk, gpb * S5_GROUP_CH, gpb * S5_STATE)

    def cmat(cc):
        cc = cc.astype(F32).reshape(nblk, gpb, S5_GROUP_CH, S5_STATE)
        m = jnp.einsum('igcn,gh->ignhc', cc, eye)
        return m.reshape(nblk, gpb * S5_STATE, gpb * S5_GROUP_CH)

    w_b = jnp.concatenate([bmat(bb_re), bmat(bb_im)], axis=2).astype(BF16)
    w_c = jnp.concatenate([cmat(c_re), -cmat(c_im)], axis=1).astype(BF16)

    j = jnp.arange(1, steps + 1, dtype=F32)[:, None, None]
    m = jnp.exp(lr * dt * j)
    pows = jnp.stack([(m * jnp.cos(li * dt * j)).reshape(steps, 1, -1),
                      (m * jnp.sin(li * dt * j)).reshape(steps, 1, -1)])
    pows = jnp.broadcast_to(pows, (2, steps, SUBLANES, S5_LANES))
    return w_b, w_c, pows


def _s5_mixer(proj, w_b, w_c, tabs, d_skip, w_glu, batch, seq):
    t = proj.shape[0]
    tb = min(S5_TB, seq)
    nt = seq // tb
    sl = tb // SUBLANES
    r = np.arange(tb)
    perm = np.zeros((tb, tb), np.float32)
    perm[r, (r % SUBLANES) * sl + r // SUBLANES] = 1.0
    perm_t = jnp.asarray(perm.T, BF16)
    perm = jnp.asarray(perm, BF16)
    return pl.pallas_call(
        functools.partial(_s5_kernel, tb=tb),
        grid=(batch, nt),
        in_specs=[
            pl.BlockSpec((tb, S5_WIDTH), lambda b, i: (b * nt + i, 0)),
            pl.BlockSpec((tb, tb), lambda b, i: (0, 0)),
            pl.BlockSpec((tb, tb), lambda b, i: (0, 0)),
            pl.BlockSpec(w_b.shape, lambda b, i: (0, 0, 0)),
            pl.BlockSpec(w_c.shape, lambda b, i: (0, 0, 0)),
            pl.BlockSpec(tabs.shape, lambda b, i: (0, 0, 0, 0)),
            pl.BlockSpec((1, S5_WIDTH), lambda b, i: (0, 0)),
            pl.BlockSpec((S5_WIDTH, S5_WIDTH), lambda b, i: (0, 0)),
        ],
        out_specs=pl.BlockSpec((tb, S5_WIDTH), lambda b, i: (b * nt + i, 0)),
        out_shape=jax.ShapeDtypeStruct((t, S5_WIDTH), BF16),
        scratch_shapes=[
            pltpu.VMEM((S5_LANES // LANES, tb, LANES), F32),
            pltpu.VMEM((S5_LANES // LANES, tb, LANES), F32),
            pltpu.VMEM((1, S5_LANES), F32),
            pltpu.VMEM((1, S5_LANES), F32),
            pltpu.VMEM((SUBLANES, S5_LANES), F32),
            pltpu.VMEM((SUBLANES, S5_LANES), F32),
        ],
        compiler_params=_params(("parallel", "arbitrary"), VMEM_LIMIT),
        name="s5_mixer",
    )(proj, perm, perm_t, w_b, w_c, tabs, d_skip.reshape(1, S5_WIDTH), w_glu)


def _hgrn_consts(c):
    t = np.arange(c)[:, None]
    u = np.arange(c)[None, :]
    mats = [u <= t]
    masks = [np.eye(c, dtype=bool)]
    h = 1
    while h < c:
        mid = (t // (2 * h)) * 2 * h + h
        if 2 * h < SUBLANES:
            mats.append(np.where(t >= mid, (u >= mid) & (u <= t), (u > t) & (u < mid)))
        masks.append(((u // (2 * h)) == (t // (2 * h))) & (t >= mid) & (u < mid))
        h *= 2
    return (np.concatenate(mats, axis=0).astype(np.float32), np.stack(masks).astype(np.float32))


def _hgrn_kernel(q_ref, f_ref, i_ref, g_ref, lb_ref, ng_ref, mall_ref, mask_ref, o_ref,
                 e_ref, st_ref, *, c):
    @pl.when(pl.program_id(1) == 0)
    def _():
        st_ref[...] = jnp.zeros(st_ref.shape, F32)

    lb = lb_ref[...]
    fl = f_ref[...]
    la = jnp.log(lb)
    lbb = jnp.log1p(-lb) + (jnp.minimum(fl, 0.0) - jnp.log1p(jnp.exp(-jnp.abs(fl))))
    logf = jnp.maximum(la, lbb) + jnp.log1p(jnp.exp(-jnp.abs(la - lbb)))
    hi = logf.astype(BF16)
    lo = (logf - hi.astype(F32)).astype(BF16)
    e2 = jnp.dot(mall_ref[...], jnp.concatenate([hi, lo], axis=1), preferred_element_type=F32)
    es = e2[:, :HGRN_WIDTH] + e2[:, HGRN_WIDTH:]
    n_levels = mask_ref.shape[0] - 1
    n_small = mall_ref.shape[0] // c - 1
    gall = es[0:c]
    e_ref[0:c, :] = gall
    e_ref[c:2 * c, :] = gall[c - 1:c, :] - gall
    e_ref[2 * c:(2 + n_small) * c, :] = es[c:]
    for lv in range(n_small, n_levels):
        blk = 2 << lv
        g3 = gall.reshape(c // blk, blk, HGRN_WIDTH)
        gm = jnp.broadcast_to(g3[:, blk // 2 - 1:blk // 2, :], g3.shape)
        e_ref[(2 + lv) * c:(3 + lv) * c, :] = (-jnp.abs(g3 - gm)).reshape(c, HGRN_WIDTH)

    nt = (((1,), (1,)), ((), ()))
    for hd in range(HGRN_HEADS):
        ls = slice(hd * HGRN_DIM, (hd + 1) * HGRN_DIM)
        qr = q_ref[:, ls]
        q = qr * jax.nn.sigmoid(qr) * (HGRN_DIM ** -0.5)
        k = (1.0 - lb[:, ls]) * jax.nn.sigmoid(-fl[:, ls])
        vb = i_ref[:, ls].astype(BF16)
        p = lax.dot_general(q.astype(BF16), k.astype(BF16), nt, preferred_element_type=F32) * mask_ref[0]
        for lv in range(n_levels):
            fac = jnp.exp(e_ref[(2 + lv) * c:(3 + lv) * c, ls])
            s = lax.dot_general((q * fac).astype(BF16), (k * fac).astype(BF16), nt,
                                preferred_element_type=F32)
            p = p + s * mask_ref[lv + 1]
        o = jnp.dot(p.astype(BF16), vb, preferred_element_type=F32)
        gcum = e_ref[0:c, ls]
        st = st_ref[hd]
        o = o + lax.dot_general((q * jnp.exp(gcum)).astype(BF16), st.astype(BF16), nt,
                                preferred_element_type=F32)
        kd = (k * jnp.exp(e_ref[c:2 * c, ls])).astype(BF16)
        st_ref[hd] = st * jnp.exp(gcum[c - 1:c, :]) + lax.dot_general(
            vb, kd, (((0,), (0,)), ((), ())), preferred_element_type=F32)
        ms = jnp.mean(o * o, axis=-1, keepdims=True)
        o = o * lax.rsqrt(ms + EPS) * ng_ref[...] * jax.nn.sigmoid(g_ref[:, ls])
        o_ref[:, ls] = o.astype(o_ref.dtype)


def _hgrn_mixer(proj, lb, norm_g, batch, seq):
    t = proj.shape[0]
    c = min(HGRN_C, seq)
    nc = seq // c
    m_all, masks = _hgrn_consts(c)
    m_all = jnp.asarray(m_all, BF16)
    masks = jnp.asarray(masks, F32)

    def col(j):
        return pl.BlockSpec((c, HGRN_WIDTH), lambda b, i: (b * nc + i, j))

    return pl.pallas_call(
        functools.partial(_hgrn_kernel, c=c),
        grid=(batch, nc),
        in_specs=[
            col(1), col(2), col(3), col(4),
            pl.BlockSpec((1, HGRN_WIDTH), lambda b, i: (0, 0)),
            pl.BlockSpec((1, HGRN_DIM), lambda b, i: (0, 0)),
            pl.BlockSpec(m_all.shape, lambda b, i: (0, 0)),
            pl.BlockSpec(masks.shape, lambda b, i: (0, 0, 0)),
        ],
        out_specs=pl.BlockSpec((c, HGRN_WIDTH), lambda b, i: (b * nc + i, 0)),
        out_shape=jax.ShapeDtypeStruct((t, HGRN_WIDTH), BF16),
        scratch_shapes=[
            pltpu.VMEM(((masks.shape[0] + 1) * c, HGRN_WIDTH), F32),
            pltpu.VMEM((HGRN_HEADS, HGRN_DIM, HGRN_DIM), F32),
        ],
        compiler_params=_params(("parallel", "arbitrary"), VMEM_LIMIT),
        name="hgrn2_mixer",
    )(proj, proj, proj, proj, lb.reshape(1, HGRN_WIDTH), norm_g.reshape(1, HGRN_DIM), m_all, masks)


def _out_proj_kernel(h_ref, a_ref, s_ref, r_ref, w_ref, gn_ref, rh_ref, rl_ref,
                     ho_ref, hn_ref, eid_ref, gate_ref):
    s_lo, r_lo = ATTN_WIDTH, ATTN_WIDTH + S5_WIDTH
    acc = jnp.dot(a_ref[...], w_ref[0:s_lo, :], preferred_element_type=F32)
    acc = acc + jnp.dot(s_ref[...], w_ref[s_lo:r_lo, :], preferred_element_type=F32)
    acc = acc + jnp.dot(r_ref[...], w_ref[r_lo:, :], preferred_element_type=F32)
    h = h_ref[...] + acc
    ho_ref[...] = h
    ms = jnp.mean(h * h, axis=-1, keepdims=True)
    hn = h * lax.rsqrt(ms + EPS) * gn_ref[...]
    half = hn.shape[1] // 2
    hn_ref[...] = _pack_bf16_pair(hn[:, :half], hn[:, half:])

    hh = hn.astype(BF16)
    hl = (hn - hh.astype(F32)).astype(BF16)
    logits = (jnp.dot(hh, rh_ref[...], preferred_element_type=F32)
              + jnp.dot(hh, rl_ref[...], preferred_element_type=F32)
              + jnp.dot(hl, rh_ref[...], preferred_element_type=F32))
    lane = lax.broadcasted_iota(jnp.int32, logits.shape, 1).astype(F32)
    neg = -jnp.inf
    big = 1e9
    is_group = jnp.where(lane >= N_EXPERTS, jnp.where(lane < N_EXPERTS + N_GROUPS, 1.0, 0.0), 0.0)
    gl = jnp.where(is_group > 0, logits, neg)
    gmax = jnp.max(gl, axis=-1, keepdims=True)
    g_lane = jnp.min(jnp.where(gl == gmax, lane, big), axis=-1, keepdims=True)
    p_g = 1.0 / jnp.sum(jnp.exp(gl - gmax), axis=-1, keepdims=True)
    lo_lane = (g_lane - N_EXPERTS) * EXPERTS_PER_GROUP
    in_group = jnp.where(lane >= lo_lane, jnp.where(lane < lo_lane + EXPERTS_PER_GROUP, 1.0, 0.0), 0.0)
    el = jnp.where(in_group > 0, logits, neg)
    t1 = jnp.max(el, axis=-1, keepdims=True)
    i1 = jnp.min(jnp.where(el == t1, lane, big), axis=-1, keepdims=True)
    el2 = jnp.where(lane == i1, neg, el)
    t2 = jnp.max(el2, axis=-1, keepdims=True)
    i2 = jnp.min(jnp.where(el2 == t2, lane, big), axis=-1, keepdims=True)
    e21 = jnp.exp(t2 - t1)
    g1 = p_g / (1.0 + e21)
    g2 = p_g * e21 / (1.0 + e21)
    eid_ref[...] = jnp.where(lane == 0, i1, jnp.where(lane == 1, i2, 0.0)).astype(jnp.int32)
    gate_ref[...] = jnp.where(lane == 0, g1, jnp.where(lane == 1, g2, 0.0))


def _out_proj(h, attn, s5, hg, w_out, gn, r_hi, r_lo):
    t, d = h.shape
    tm = min(OUT_TM, t)

    def rows(w):
        return pl.BlockSpec((tm, w), lambda i: (i, 0))

    def full(a):
        return pl.BlockSpec(a.shape, lambda i: (0,) * a.ndim)

    return pl.pallas_call(
        _out_proj_kernel,
        grid=(t // tm,),
        in_specs=[rows(d), rows(ATTN_WIDTH), rows(S5_WIDTH), rows(HGRN_WIDTH),
                  full(w_out), pl.BlockSpec((1, d), lambda i: (0, 0)),
                  full(r_hi), full(r_lo)],
        out_specs=[rows(d), rows(d // 2), rows(LANES), rows(LANES)],
        out_shape=[jax.ShapeDtypeStruct((t, d), F32), jax.ShapeDtypeStruct((t, d // 2), jnp.uint32),
                   jax.ShapeDtypeStruct((t, LANES), jnp.int32), jax.ShapeDtypeStruct((t, LANES), F32)],
        compiler_params=_params(("parallel",), VMEM_LIMIT),
        name="out_proj_router",
    )(h, attn, s5, hg, w_out, gn.reshape(1, d), r_hi, r_lo)


def _moe_plan(eid, tm):
    flat = eid.reshape(-1)
    n_slots = flat.shape[0]
    onehot = (flat[:, None] == jnp.arange(N_EXPERTS, dtype=jnp.int32)[None, :]).astype(jnp.int32)
    csum = jnp.cumsum(onehot, axis=0)
    rank = jnp.sum(onehot * csum, axis=1) - 1
    counts = csum[-1]
    tiles = (counts + tm - 1) // tm
    tile_end = jnp.cumsum(tiles)
    tile_start = tile_end - tiles
    slot_start = jnp.cumsum(counts) - counts
    pos = jnp.sum(onehot * tile_start[None, :], axis=1) * tm + rank
    tok_sorted = jnp.argsort(flat, stable=True).astype(jnp.int32) // TOP_K
    tok_sorted = jnp.pad(tok_sorted, (0, tm))
    nt_max = (n_slots + N_EXPERTS * (tm - 1)) // tm
    n_used = tile_end[-1]
    j = jnp.minimum(jnp.arange(nt_max, dtype=jnp.int32), n_used - 1)
    tile_expert = jnp.sum((j[:, None] >= tile_end[None, :]).astype(jnp.int32), axis=1)
    onehot_t = (tile_expert[:, None] == jnp.arange(N_EXPERTS, dtype=jnp.int32)[None, :]).astype(jnp.int32)
    tile_slot0 = jnp.sum(onehot_t * (slot_start - tile_start * tm)[None, :], axis=1) + j * tm
    used = (tiles > 0).astype(jnp.int32)
    ordinal = jnp.cumsum(used) - used
    ids = jnp.arange(N_EXPERTS, dtype=jnp.int32)
    later = jnp.where((ids[None, :] > ids[:, None]) & (used[None, :] > 0), ids[None, :], N_EXPERTS)
    nxt = jnp.min(later, axis=1)
    nxt = jnp.where(nxt >= N_EXPERTS, -1, nxt)
    tile_wslot = jnp.sum(onehot_t * (ordinal % 2)[None, :], axis=1)
    tile_next = jnp.sum(onehot_t * nxt[None, :], axis=1)
    return dict(pos=pos.astype(jnp.int32), tok_sorted=tok_sorted, tile_expert=tile_expert.astype(jnp.int32),
                tile_slot0=tile_slot0.astype(jnp.int32), tile_wslot=tile_wslot.astype(jnp.int32),
                tile_next=tile_next.astype(jnp.int32), n_used=n_used.reshape(1).astype(jnp.int32),
                nt_max=nt_max)


def _expert_kernel(tok_ref, te_ref, s0_ref, ws_ref, ne_ref, nu_ref, hn_ref, wg_ref, wu_ref, wd_ref, y_ref,
                   xa_ref, xb_ref, wgf_ref, wuf_ref, wdf_ref, wgb_ref, wub_ref, wdb_ref, sem, wsem,
                   *, tm, layer):
    j = pl.program_id(0)
    n_used = nu_ref[0]
    bufs = (xa_ref, xb_ref)

    def weight_copies(expert, slot):
        return [pltpu.make_async_copy(src.at[layer, expert], dst.at[slot], wsem.at[slot])
                for src, dst in ((wg_ref, wgf_ref), (wu_ref, wuf_ref), (wd_ref, wdf_ref))]

    def row_copy(buf, r, tok):
        return pltpu.make_async_copy(hn_ref.at[pl.ds(tok, 1)], bufs[buf].at[pl.ds(r, 1)], sem.at[buf])

    def wait_gather(buf):
        def wait(r, carry):
            row_copy(buf, r, 0).wait()
            return carry

        lax.fori_loop(0, tm, wait, 0, unroll=DMA_UNROLL)

    @pl.when(j == 0)
    def _():
        for cp in weight_copies(te_ref[0], ws_ref[0]):
            cp.start(priority=WEIGHT_DMA_PRIORITY)
        base = s0_ref[0]

        def issue(r, carry):
            row_copy(0, r, tok_ref[base + r]).start()
            return carry

        lax.fori_loop(0, tm, issue, 0, unroll=DMA_UNROLL)

    def tile(buf):
        wait_gather(buf)
        prev = te_ref[jnp.maximum(j - 1, 0)]

        @pl.when(jnp.logical_or(j == 0, te_ref[j] != prev))
        def _():
            slot = ws_ref[j]

            @pl.when(ne_ref[j] >= 0)
            def _():
                for cp in weight_copies(ne_ref[j], 1 - slot):
                    cp.start(priority=WEIGHT_DMA_PRIORITY)

            for cp in weight_copies(te_ref[j], slot):
                cp.wait()
            wgb_ref[...] = wgf_ref[slot].astype(BF16)
            wub_ref[...] = wuf_ref[slot].astype(BF16)
            wdb_ref[...] = wdf_ref[slot].astype(BF16)

        nbase = s0_ref[jnp.minimum(j + 1, n_used - 1)]
        for r in range(tm):
            row_copy(1 - buf, r, tok_ref[nbase + r]).start()

        x_lo, x_hi = _unpack_bf16_pair(bufs[buf][...])
        x = jnp.concatenate([x_lo.astype(BF16), x_hi.astype(BF16)], axis=1)
        g = jnp.dot(x, wgb_ref[...], preferred_element_type=F32)
        u = jnp.dot(x, wub_ref[...], preferred_element_type=F32)
        hmid = (g * jax.nn.sigmoid(g) * u).astype(BF16)
        y = jnp.dot(hmid, wdb_ref[...], preferred_element_type=F32)
        half = y.shape[1] // 2
        y_ref[...] = _pack_bf16_pair(y[:, :half], y[:, half:])

        @pl.when(j == n_used - 1)
        def _():
            wait_gather(1 - buf)

    for parity in range(2):
        @pl.when(jnp.logical_and(j < n_used, lax.rem(j, 2) == parity))
        def _(parity=parity):
            tile(parity)

    @pl.when(j >= n_used)
    def _():
        y_ref[...] = jnp.zeros(y_ref.shape, y_ref.dtype)


def _moe_experts(hn, plan, w_gate, w_up, w_down, layer, tm):
    t, dp = hn.shape
    d, de = w_gate.shape[2], w_gate.shape[3]
    nt_max = plan["nt_max"]
    grid_spec = pltpu.PrefetchScalarGridSpec(
        num_scalar_prefetch=6,
        grid=(nt_max,),
        in_specs=[pl.BlockSpec(memory_space=pl.ANY)] * 4,
        out_specs=pl.BlockSpec((tm, dp), lambda j, *_: (j, 0)),
        scratch_shapes=[pltpu.VMEM((tm, dp), jnp.uint32), pltpu.VMEM((tm, dp), jnp.uint32),
                        pltpu.VMEM((2, d, de), F32), pltpu.VMEM((2, d, de), F32), pltpu.VMEM((2, de, d), F32),
                        pltpu.VMEM((d, de), BF16), pltpu.VMEM((d, de), BF16), pltpu.VMEM((de, d), BF16),
                        pltpu.SemaphoreType.DMA((2,)), pltpu.SemaphoreType.DMA((2,))],
    )
    return pl.pallas_call(
        functools.partial(_expert_kernel, tm=tm, layer=layer),
        grid_spec=grid_spec,
        out_shape=jax.ShapeDtypeStruct((nt_max * tm, dp), jnp.uint32),
        compiler_params=_params(("arbitrary",), VMEM_LIMIT),
        name="moe_experts",
    )(plan["tok_sorted"], plan["tile_expert"], plan["tile_slot0"], plan["tile_wslot"], plan["tile_next"],
      plan["n_used"], hn, w_gate, w_up, w_down)


def _combine_kernel(pos_ref, h_ref, gate_ref, gn_ref, y_ref, o_ref, ba_ref, bb_ref, sem, *, tm, final_norm):
    i = pl.program_id(0)
    n = pl.num_programs(0)
    bufs = (ba_ref, bb_ref)

    def row_copy(b, k, r, src_row):
        return pltpu.make_async_copy(y_ref.at[pl.ds(src_row, 1)], bufs[b].at[k, pl.ds(r, 1)], sem.at[b])

    def wait_gather(b):
        def wait(r, carry):
            for k in range(TOP_K):
                row_copy(b, k, r, 0).wait()
            return carry

        lax.fori_loop(0, tm, wait, 0, unroll=DMA_UNROLL)

    @pl.when(i == 0)
    def _():
        def issue(r, carry):
            for k in range(TOP_K):
                row_copy(0, k, r, pos_ref[r * TOP_K + k]).start()
            return carry

        lax.fori_loop(0, tm, issue, 0, unroll=DMA_UNROLL)

    def tile(b):
        wait_gather(b)
        base = jnp.minimum(i + 1, n - 1) * (tm * TOP_K)
        for r in range(tm):
            for k in range(TOP_K):
                row_copy(1 - b, k, r, pos_ref[base + r * TOP_K + k]).start()

        gates = gate_ref[...]
        y0_lo, y0_hi = _unpack_bf16_pair(bufs[b][0])
        y1_lo, y1_hi = _unpack_bf16_pair(bufs[b][1])
        g0, g1 = gates[:, 0:1], gates[:, 1:2]
        h = h_ref[...] + jnp.concatenate([g0 * y0_lo + g1 * y1_lo, g0 * y0_hi + g1 * y1_hi], axis=1)
        if final_norm:
            ms = jnp.mean(h * h, axis=-1, keepdims=True)
            h = h * lax.rsqrt(ms + EPS) * gn_ref[...]
        o_ref[...] = h

        @pl.when(i == n - 1)
        def _():
            wait_gather(1 - b)

    for parity in range(2):
        @pl.when(lax.rem(i, 2) == parity)
        def _(parity=parity):
            tile(parity)


def _moe_combine(h, gates, ys, pos, final_g, final_norm):
    t, d = h.shape
    tm = min(COMB_TM, t)
    grid_spec = pltpu.PrefetchScalarGridSpec(
        num_scalar_prefetch=1,
        grid=(t // tm,),
        in_specs=[
            pl.BlockSpec((tm, d), lambda i, p: (i, 0)),
            pl.BlockSpec((tm, LANES), lambda i, p: (i, 0)),
            pl.BlockSpec((1, d), lambda i, p: (0, 0)),
            pl.BlockSpec(memory_space=pl.ANY),
        ],
        out_specs=pl.BlockSpec((tm, d), lambda i, p: (i, 0)),
        scratch_shapes=[pltpu.VMEM((TOP_K, tm, d // 2), jnp.uint32), pltpu.VMEM((TOP_K, tm, d // 2), jnp.uint32),
                        pltpu.SemaphoreType.DMA((2,))],
    )
    return pl.pallas_call(
        functools.partial(_combine_kernel, tm=tm, final_norm=final_norm),
        grid_spec=grid_spec,
        out_shape=jax.ShapeDtypeStruct((t, d), F32),
        compiler_params=_params(("arbitrary",), VMEM_LIMIT),
        name="moe_combine",
    )(pos, h, gates, final_g.reshape(1, d), ys)


def kernel(x, w_in, w_out, mix_norm_g, ffn_norm_g, rel_bias, diff_lambda, attn_subln_g, s5_a_re, s5_a_im, s5_log_dt, s5_b_re, s5_b_im, s5_c_re, s5_c_im, s5_d, s5_w_glu, hgrn_lb_logits, hgrn_norm_g, moe_w_group, moe_w_router, moe_w_gate, moe_w_up, moe_w_down, final_norm_g):
    batch, seq, d = x.shape
    depth = w_in.shape[0]
    t = batch * seq
    h = x.reshape(t, d)

    lb_cum = jnp.cumsum(jax.nn.softmax(hgrn_lb_logits.astype(F32), axis=0), axis=0)
    lb_all = lb_cum - lb_cum[0:1]
    attn_bias = _attn_bias_tiles(rel_bias, min(ATTN_TQ, seq), min(ATTN_TK, seq))

    for l in range(depth):
        proj = _in_proj(h, mix_norm_g[l], w_in, l)

        lam_init = 0.8 - 0.6 * math.exp(-0.3 * l)
        lv = diff_lambda[l].astype(F32)
        lam = jnp.exp(jnp.sum(lv[0] * lv[1])) - jnp.exp(jnp.sum(lv[2] * lv[3])) + lam_init
        attn = _diff_attention(proj, attn_bias, lam, attn_subln_g[l], lam_init, batch, seq)

        w_b, w_c, tabs = _s5_prep(s5_a_re[l], s5_a_im[l], s5_log_dt[l], s5_b_re[l], s5_b_im[l],
                                  s5_c_re[l], s5_c_im[l], min(S5_TB, seq) // SUBLANES)
        s5 = _s5_mixer(proj, w_b, w_c, tabs, s5_d[l], s5_w_glu[l], batch, seq)

        hg = _hgrn_mixer(proj, lb_all[l], hgrn_norm_g[l], batch, seq)

        wo = w_out[l].astype(BF16)
        w_r = jnp.concatenate([moe_w_router[l], moe_w_group[l]], axis=1).astype(F32)
        w_r = jnp.pad(w_r, ((0, 0), (0, LANES - w_r.shape[1])))
        r_hi = w_r.astype(BF16)
        r_lo = (w_r - r_hi.astype(F32)).astype(BF16)
        h, hn, eid, gates = _out_proj(h, attn, s5, hg, wo, ffn_norm_g[l], r_hi, r_lo)

        plan = _moe_plan(eid[:, :TOP_K], MOE_TM)
        ys = _moe_experts(hn, plan, moe_w_gate, moe_w_up, moe_w_down, l, MOE_TM)
        h = _moe_combine(h, gates, ys, plan["pos"], final_norm_g, final_norm=(l == depth - 1))

    return h.reshape(batch, seq, d)
```

```python
import functools
import math

import numpy as np
import jax
import jax.numpy as jnp
from jax import lax
from jax.experimental import pallas as pl
from jax.experimental.pallas import tpu as pltpu

F32 = jnp.float32
BF16 = jnp.bfloat16
EPS = 1e-6
LOG2E = 1.4426950408889634

D_MODEL = 2048
ATTN_HEADS = 6
ATTN_QK_DIM = 64
ATTN_V_DIM = 128
ATTN_WIDTH = ATTN_HEADS * ATTN_V_DIM
REL_BUCKETS = 32
REL_MAX_DIST = 128
S5_GROUPS = 40
S5_GROUP_CH = 16
S5_STATE = 64
S5_WIDTH = S5_GROUPS * S5_GROUP_CH
S5_LANES = S5_GROUPS * S5_STATE
HGRN_HEADS = 5
HGRN_DIM = 128
HGRN_WIDTH = HGRN_HEADS * HGRN_DIM
N_GROUPS = 4
EXPERTS_PER_GROUP = 8
N_EXPERTS = N_GROUPS * EXPERTS_PER_GROUP
TOP_K = 2
D_EXPERT = 512

LANES = 128
SUBLANES = 8

IN_COLS = 3 * ATTN_WIDTH + S5_WIDTH + 4 * HGRN_WIDTH
PROJ_TN = 256
IN_COLS_PAD = -(-IN_COLS // PROJ_TN) * PROJ_TN
ATTN_COL_BLOCK = (IN_COLS_PAD - 3 * ATTN_WIDTH) // LANES

PROJ_TM = 2048
ATTN_TQ = 512
ATTN_TK = 512
ATTN_NEAR = 2
ATTN_ROWS = 32
ATTN_HPS = 1
S5_TB = 256
S5_CHUNK = 640
S5_BLK = 128
HGRN_C = 128
OUT_TM = 256
MOE_TM = 256
COMB_TM = 128
DMA_UNROLL = 8
WEIGHT_DMA_PRIORITY = 1

VMEM_LIMIT = 56 * 1024 * 1024


def _params(sem, vmem=None, flags=None):
    return pltpu.CompilerParams(dimension_semantics=sem, vmem_limit_bytes=vmem, flags=flags)


def _pack_bf16_pair(lo, hi):
    lo_bits = lax.bitcast_convert_type(lo.astype(BF16).astype(F32), jnp.uint32) >> 16
    hi_bits = lax.bitcast_convert_type(hi.astype(BF16).astype(F32), jnp.uint32) & jnp.uint32(0xFFFF0000)
    return lo_bits | hi_bits


def _unpack_bf16_pair(w):
    return (lax.bitcast_convert_type(w << 16, F32),
            lax.bitcast_convert_type(w & jnp.uint32(0xFFFF0000), F32))


def _in_proj_kernel(x_ref, g_ref, w_ref, wt_ref, o_ref, xn_ref, *, tail_tile):
    j = pl.program_id(1)

    @pl.when(j == 0)
    def _():
        x = x_ref[...]
        ms = jnp.mean(x * x, axis=-1, keepdims=True)
        xn_ref[...] = (x * lax.rsqrt(ms + EPS) * g_ref[...]).astype(BF16)

    w = jnp.where(j == tail_tile, wt_ref[...], w_ref[0])
    o_ref[...] = jnp.dot(xn_ref[...], w.astype(BF16), preferred_element_type=F32)


def _in_proj(h, g, w_all, layer):
    t, d = h.shape
    tm = min(PROJ_TM, t)
    tn = PROJ_TN
    a = 3 * ATTN_WIDTH
    rest = IN_COLS - a
    assert a % tn == 0 and IN_COLS_PAD - IN_COLS == tn - rest % tn
    rest_full, a_tiles = rest // tn, a // tn
    tail_tile = rest_full
    w_tail = lax.slice(w_all, (layer, 0, a + rest_full * tn), (layer + 1, d, IN_COLS))[0]
    w_tail = jnp.pad(w_tail, ((0, 0), (0, IN_COLS_PAD - IN_COLS)))

    def w_map(i, j):
        col = jnp.where(j < tail_tile, a_tiles + j, jnp.where(j == tail_tile, 0, j - tail_tile - 1))
        return (layer, 0, col)

    return pl.pallas_call(
        functools.partial(_in_proj_kernel, tail_tile=tail_tile),
        grid=(t // tm, IN_COLS_PAD // tn),
        in_specs=[
            pl.BlockSpec((tm, d), lambda i, j: (i, 0)),
            pl.BlockSpec((1, d), lambda i, j: (0, 0)),
            pl.BlockSpec((1, d, tn), w_map),
            pl.BlockSpec((d, tn), lambda i, j: (0, 0)),
        ],
        out_specs=pl.BlockSpec((tm, tn), lambda i, j: (i, j)),
        out_shape=jax.ShapeDtypeStruct((t, IN_COLS_PAD), F32),
        scratch_shapes=[pltpu.VMEM((tm, d), BF16)],
        compiler_params=_params(("parallel", "arbitrary"), VMEM_LIMIT),
        name="in_proj",
    )(h, g.reshape(1, d), w_all, w_tail)


def _attn_kernel(lam_ref, q_ref, k_ref, v_ref, bias_ref, g_ref, o_ref,
                 kb_ref, vb_ref, qz_ref, m_ref, a_ref, acc_ref, s0_ref, s1_ref, p_ref, *, tq, tk, out_scale):
    qi = pl.program_id(2)
    ratio = tq // tk
    heads = range(ATTN_HPS)

    @pl.when(qi == 0)
    def _():
        for h in heads:
            hl = slice(h * LANES, (h + 1) * LANES)
            kb_ref[h] = k_ref[:, hl].astype(BF16)
            vb_ref[h, :, 0:LANES] = v_ref[:, hl].astype(BF16)
            vb_ref[h, :, LANES:2 * LANES] = jnp.ones((vb_ref.shape[1], LANES), BF16)

    for h in heads:
        q = q_ref[:, h * LANES:(h + 1) * LANES] * (ATTN_QK_DIM ** -0.5 * LOG2E)
        lane = lax.broadcasted_iota(jnp.int32, q.shape, 1)
        qz_ref[h, 0:tq, :] = jnp.where(lane < ATTN_QK_DIM, q, 0.0).astype(BF16)
        qz_ref[h, tq:2 * tq, :] = jnp.where(lane >= ATTN_QK_DIM, q, 0.0).astype(BF16)
    m_ref[...] = jnp.full(m_ref.shape, -jnp.inf, F32)
    acc_ref[...] = jnp.zeros(acc_ref.shape, F32)

    n_col = tk // LANES
    groups = [slice(g * ATTN_ROWS, (g + 1) * ATTN_ROWS) for g in range(2 * tq // ATTN_ROWS)]

    def scores(j, s_ref):
        for h in heads:
            k = kb_ref[h, pl.ds(pl.multiple_of(j * tk, tk), tk), :]
            s_ref[h] = lax.dot_general(qz_ref[h], k, (((1,), (1,)), ((), ())), preferred_element_type=F32)

    def softmax_pv(j, s_ref, biased):
        if biased:
            kind = jnp.clip(j - ratio * qi + ATTN_NEAR, 0, ATTN_NEAR + ratio - 1)
        for h in heads:
            for rows in groups:
                cols = [s_ref[h, rows, i * LANES:(i + 1) * LANES] for i in range(n_col)]
                if biased:
                    b0 = rows.start % tq
                    cols = [x + bias_ref[h, kind, b0:b0 + ATTN_ROWS, i * LANES:(i + 1) * LANES]
                            for i, x in enumerate(cols)]
                    for i, x in enumerate(cols):
                        s_ref[h, rows, i * LANES:(i + 1) * LANES] = x
                mx = functools.reduce(jnp.maximum, cols)
                m_old = m_ref[h, rows, :]
                m_new = jnp.maximum(m_old, jnp.max(mx, axis=-1, keepdims=True))
                m_ref[h, rows, :] = m_new
                a_ref[h, rows, :] = jnp.exp2(m_old - m_new)
        for h in heads:
            for rows in groups:
                m_new = m_ref[h, rows, :]
                for i in range(n_col):
                    p_ref[h, rows, i * LANES:(i + 1) * LANES] = jnp.exp2(
                        s_ref[h, rows, i * LANES:(i + 1) * LANES] - m_new).astype(BF16)
        for h in heads:
            v = vb_ref[h, pl.ds(pl.multiple_of(j * tk, tk), tk), :]
            alpha = a_ref[h]
            acc_ref[h] = (jnp.concatenate([alpha, alpha], axis=1) * acc_ref[h]
                          + jnp.dot(p_ref[h], v, preferred_element_type=F32))

    n_tiles = ratio * (qi + 1)
    first = lax.rem(n_tiles, 2)
    n_far = jnp.maximum(ratio * qi - ATTN_NEAR + 1, 0)
    n_far_pairs = jnp.maximum(n_far - first, 0) // 2
    n_pairs = (n_tiles - first) // 2

    @pl.when(first == 1)
    def _():
        scores(0, s1_ref)
        scores(jnp.minimum(1, n_tiles - 1), s0_ref)
        softmax_pv(0, s1_ref, True)

    @pl.when(first == 0)
    def _():
        scores(0, s0_ref)

    def pair_body(jj, carry, biased):
        j = first + 2 * jj
        scores(j + 1, s1_ref)
        softmax_pv(j, s0_ref, biased)
        scores(jnp.minimum(j + 2, n_tiles - 1), s0_ref)
        softmax_pv(j + 1, s1_ref, biased)
        return carry

    lax.fori_loop(0, n_far_pairs, functools.partial(pair_body, biased=False), 0)
    lax.fori_loop(n_far_pairs, n_pairs, functools.partial(pair_body, biased=True), 0)

    for h in heads:
        inv_l = 1.0 / acc_ref[h, :, LANES:2 * LANES]
        o1 = acc_ref[h, 0:tq, 0:LANES] * inv_l[0:tq]
        o2 = acc_ref[h, tq:2 * tq, 0:LANES] * inv_l[tq:2 * tq]
        o = o1 - lam_ref[0] * o2
        ms = jnp.mean(o * o, axis=-1, keepdims=True)
        o_ref[:, h * LANES:(h + 1) * LANES] = (o * lax.rsqrt(ms + EPS) * g_ref[...] * out_scale).astype(o_ref.dtype)


def _t5_bucket(dist):
    n = jnp.maximum(dist, 0)
    max_exact = REL_BUCKETS // 2
    large = max_exact + (jnp.log(jnp.maximum(n, 1).astype(F32) / max_exact)
                         / math.log(REL_MAX_DIST / max_exact) * (REL_BUCKETS - max_exact)).astype(jnp.int32)
    large = jnp.minimum(large, REL_BUCKETS - 1)
    return jnp.where(n < max_exact, n, large)


def _attn_bias_tiles(rel_bias, tq, tk):
    table = rel_bias.astype(F32)
    far = table[REL_BUCKETS - 1]
    r = jnp.arange(tq)[:, None]
    c = jnp.arange(tk)[None, :]

    def lookup(dist):
        bucket = _t5_bucket(dist)
        out = jnp.zeros((ATTN_HEADS,) + dist.shape, F32)
        for b in range(REL_BUCKETS):
            out = jnp.where((bucket == b)[None], table[b][:, None, None], out)
        return out

    tiles = [jnp.zeros((ATTN_HEADS, tq, tk), F32)]
    for rel in range(-ATTN_NEAR + 1, tq // tk):
        dist = r - c - rel * tk
        tiles.append(jnp.where((dist >= 0)[None], (lookup(dist) - far[:, None, None]) * LOG2E, -1e30))
    return jnp.stack(tiles, axis=1)


def _diff_attention(proj, bias, lam, subln_g, lam_init, batch, seq):
    t = proj.shape[0]
    tq = min(ATTN_TQ, seq)
    tk = min(ATTN_TK, seq)
    ratio = tq // tk
    hps = ATTN_HPS
    assert (ATTN_NEAR - 1) * tk >= REL_MAX_DIST and ratio * tk == tq
    assert ATTN_HEADS % hps == 0 and ATTN_COL_BLOCK % hps == 0
    nq = seq // tq
    kernel = functools.partial(_attn_kernel, tq=tq, tk=tk, out_scale=1.0 - lam_init)
    qb, kb, vb = (ATTN_COL_BLOCK // hps, (ATTN_COL_BLOCK + ATTN_HEADS) // hps,
                  (ATTN_COL_BLOCK + 2 * ATTN_HEADS) // hps)
    w = hps * LANES
    return pl.pallas_call(
        kernel,
        grid=(batch, ATTN_HEADS // hps, nq),
        in_specs=[
            pl.BlockSpec(memory_space=pltpu.SMEM),
            pl.BlockSpec((tq, w), lambda b, h, i: (b * nq + i, qb + h)),
            pl.BlockSpec((seq, w), lambda b, h, i: (b, kb + h)),
            pl.BlockSpec((seq, w), lambda b, h, i: (b, vb + h)),
            pl.BlockSpec((hps,) + bias.shape[1:], lambda b, h, i: (h, 0, 0, 0)),
            pl.BlockSpec((1, LANES), lambda b, h, i: (0, 0)),
        ],
        out_specs=pl.BlockSpec((tq, w), lambda b, h, i: (b * nq + i, h)),
        out_shape=jax.ShapeDtypeStruct((t, ATTN_WIDTH), BF16),
        scratch_shapes=[
            pltpu.VMEM((hps, seq, LANES), BF16),
            pltpu.VMEM((hps, seq, 2 * LANES), BF16),
            pltpu.VMEM((hps, 2 * tq, LANES), BF16),
            pltpu.VMEM((hps, 2 * tq, LANES), F32),
            pltpu.VMEM((hps, 2 * tq, LANES), F32),
            pltpu.VMEM((hps, 2 * tq, 2 * LANES), F32),
            pltpu.VMEM((hps, 2 * tq, tk), F32),
            pltpu.VMEM((hps, 2 * tq, tk), F32),
            pltpu.VMEM((hps, 2 * tq, tk), BF16),
        ],
        compiler_params=_params(("parallel", "parallel", "arbitrary"), VMEM_LIMIT),
        name="diff_attn",
    )(lam.reshape(1), proj, proj, proj, bias, subln_g.reshape(1, LANES))


def _s5_kernel(u_ref, perm_ref, permt_ref, wb_ref, wc_ref, pw_ref, d_ref, wglu_ref, o_ref,
               xr_ref, xi_ref, cr_ref, ci_ref, kr_ref, ki_ref, *, tb):
    @pl.when(pl.program_id(1) == 0)
    def _():
        cr_ref[...] = jnp.zeros(cr_ref.shape, F32)
        ci_ref[...] = jnp.zeros(ci_ref.shape, F32)

    un = u_ref[...]
    uh = un.astype(BF16)
    ul = (un - uh.astype(F32)).astype(BF16)
    up = jnp.dot(perm_ref[...], jnp.concatenate([uh, ul], axis=1), preferred_element_type=F32)
    ub = up[:, :S5_WIDTH].astype(BF16)
    u = up[:, :S5_WIDTH] + up[:, S5_WIDTH:]
    nblk = S5_WIDTH // S5_BLK
    sw = S5_LANES // nblk
    tpb = sw // LANES
    for i in range(nblk):
        bu = jnp.dot(ub[:, i * S5_BLK:(i + 1) * S5_BLK], wb_ref[i], preferred_element_type=F32)
        for k in range(tpb):
            xr_ref[i * tpb + k] = bu[:, k * LANES:(k + 1) * LANES]
            xi_ref[i * tpb + k] = bu[:, sw + k * LANES:sw + (k + 1) * LANES]

    sl = tb // SUBLANES
    for c in range(S5_LANES // S5_CHUNK):
        tiles = range(c * S5_CHUNK // LANES, (c + 1) * S5_CHUNK // LANES)
        lss = [slice(n * LANES, (n + 1) * LANES) for n in tiles]
        a = [(pw_ref[0, 0, :, ls], pw_ref[1, 0, :, ls]) for ls in lss]

        def local_step(t, carry, tiles=tiles, a=a):
            rows = pl.ds(pl.multiple_of(t * SUBLANES, SUBLANES), SUBLANES)
            out = []
            for n, (ar, ai), (pr, pi) in zip(tiles, a, carry):
                xr = ar * pr - ai * pi + xr_ref[n, rows, :]
                xi = ar * pi + ai * pr + xi_ref[n, rows, :]
                xr_ref[n, rows, :] = xr
                xi_ref[n, rows, :] = xi
                out.append((xr, xi))
            return tuple(out)

        zero = jnp.zeros((SUBLANES, LANES), F32)
        ends = lax.fori_loop(0, sl, local_step, tuple((zero, zero) for _ in tiles), unroll=2)

        for ls, (er, ei) in zip(lss, ends):
            lr, li = pw_ref[0, sl - 1, 0:1, ls], pw_ref[1, sl - 1, 0:1, ls]
            kr, ki = cr_ref[:, ls], ci_ref[:, ls]
            for sub in range(SUBLANES):
                kr_ref[sub:sub + 1, ls] = kr
                ki_ref[sub:sub + 1, ls] = ki
                kr, ki = (er[sub:sub + 1, :] + lr * kr - li * ki, ei[sub:sub + 1, :] + lr * ki + li * kr)
            cr_ref[:, ls] = kr
            ci_ref[:, ls] = ki

        k8 = [(kr_ref[:, ls], ki_ref[:, ls]) for ls in lss]

        def fix_step(t, carry, tiles=tiles, lss=lss, k8=k8):
            rows = pl.ds(pl.multiple_of(t * SUBLANES, SUBLANES), SUBLANES)
            for n, ls, (kr8, ki8) in zip(tiles, lss, k8):
                pr = pw_ref[0, t, :, ls]
                pi = pw_ref[1, t, :, ls]
                xr_ref[n, rows, :] = xr_ref[n, rows, :] + pr * kr8 - pi * ki8
                xi_ref[n, rows, :] = xi_ref[n, rows, :] + pr * ki8 + pi * kr8
            return carry

        lax.fori_loop(0, sl, fix_step, 0, unroll=2)

    ys = []
    for i in range(nblk):
        xc = jnp.concatenate([xr_ref[i * tpb + k] for k in range(tpb)]
                             + [xi_ref[i * tpb + k] for k in range(tpb)], axis=1)
        ys.append(jnp.dot(xc.astype(BF16), wc_ref[i], preferred_element_type=F32))
    y = jnp.concatenate(ys, axis=1) + d_ref[...] * u
    z = 0.5 * y * (1.0 + jnp.tanh(math.sqrt(2.0 / math.pi) * (y + 0.044715 * (y * y * y))))
    gate = jax.nn.sigmoid(jnp.dot(z.astype(BF16), wglu_ref[...].astype(BF16), preferred_element_type=F32))
    o_ref[...] = jnp.dot(permt_ref[...], (z * gate).astype(BF16),
                         preferred_element_type=F32).astype(o_ref.dtype)


def _s5_prep(a_re, a_im, log_dt, b_re, b_im, c_re, c_im, steps):
    lr = jnp.minimum(a_re.astype(F32), -1e-4)
    li = a_im.astype(F32)
    dt = jnp.exp(log_dt.astype(F32))[:, None]
    mag = jnp.exp(lr * dt)
    ab_re = mag * jnp.cos(li * dt)
    ab_im = mag * jnp.sin(li * dt)
    den = lr * lr + li * li
    nr = ab_re - 1.0
    ni = ab_im
    f_re = (nr * lr + ni * li) / den
    f_im = (ni * lr - nr * li) / den
    br = b_re.astype(F32)
    bi = b_im.astype(F32)
    bb_re = f_re[..., None] * br - f_im[..., None] * bi
    bb_im = f_re[..., None] * bi + f_im[..., None] * br

    nblk = S5_WIDTH // S5_BLK
    gpb = S5_GROUPS // nblk
    eye = jnp.eye(gpb, dtype=F32)

    def bmat(bb):
        bb = bb.reshape(nblk, gpb, S5_STATE, S5_GROUP_CH)
        m = jnp.einsum('ignc,gh->igchn', bb, eye)
        return m.reshape(nblk, gpb * S5_GROUP_CH, gpb * S5_STATE)

    def cmat(cc):
        cc = cc.astype(F32).reshape(nblk, gpb, S5_GROUP_CH, S5_STATE)
        m = jnp.einsum('igcn,gh->ignhc', cc, eye)
        return m.reshape(nblk, gpb * S5_STATE, gpb * S5_GROUP_CH)

    w_b = jnp.concatenate([bmat(bb_re), bmat(bb_im)], axis=2).astype(BF16)
    w_c = jnp.concatenate([cmat(c_re), -cmat(c_im)], axis=1).astype(BF16)

    j = jnp.arange(1, steps + 1, dtype=F32)[:, None, None]
    m = jnp.exp(lr * dt * j)
    pows = jnp.stack([(m * jnp.cos(li * dt * j)).reshape(steps, 1, -1),
                      (m * jnp.sin(li * dt * j)).reshape(steps, 1, -1)])
    pows = jnp.broadcast_to(pows, (2, steps, SUBLANES, S5_LANES))
    return w_b, w_c, pows


def _s5_mixer(proj, w_b, w_c, tabs, d_skip, w_glu, batch, seq):
    t = proj.shape[0]
    tb = min(S5_TB, seq)
    nt = seq // tb
    sl = tb // SUBLANES
    r = np.arange(tb)
    perm = np.zeros((tb, tb), np.float32)
    perm[r, (r % SUBLANES) * sl + r // SUBLANES] = 1.0
    perm_t = jnp.asarray(perm.T, BF16)
    perm = jnp.asarray(perm, BF16)
    return pl.pallas_call(
        functools.partial(_s5_kernel, tb=tb),
        grid=(batch, nt),
        in_specs=[
            pl.BlockSpec((tb, S5_WIDTH), lambda b, i: (b * nt + i, 0)),
            pl.BlockSpec((tb, tb), lambda b, i: (0, 0)),
            pl.BlockSpec((tb, tb), lambda b, i: (0, 0)),
            pl.BlockSpec(w_b.shape, lambda b, i: (0, 0, 0)),
            pl.BlockSpec(w_c.shape, lambda b, i: (0, 0, 0)),
            pl.BlockSpec(tabs.shape, lambda b, i: (0, 0, 0, 0)),
            pl.BlockSpec((1, S5_WIDTH), lambda b, i: (0, 0)),
            pl.BlockSpec((S5_WIDTH, S5_WIDTH), lambda b, i: (0, 0)),
        ],
        out_specs=pl.BlockSpec((tb, S5_WIDTH), lambda b, i: (b * nt + i, 0)),
        out_shape=jax.ShapeDtypeStruct((t, S5_WIDTH), BF16),
        scratch_shapes=[
            pltpu.VMEM((S5_LANES // LANES, tb, LANES), F32),
            pltpu.VMEM((S5_LANES // LANES, tb, LANES), F32),
            pltpu.VMEM((1, S5_LANES), F32),
            pltpu.VMEM((1, S5_LANES), F32),
            pltpu.VMEM((SUBLANES, S5_LANES), F32),
            pltpu.VMEM((SUBLANES, S5_LANES), F32),
        ],
        compiler_params=_params(("parallel", "arbitrary"), VMEM_LIMIT),
        name="s5_mixer",
    )(proj, perm, perm_t, w_b, w_c, tabs, d_skip.reshape(1, S5_WIDTH), w_glu)


def _hgrn_consts(c):
    t = np.arange(c)[:, None]
    u = np.arange(c)[None, :]
    mats = [u <= t]
    masks = [np.eye(c, dtype=bool)]
    h = 1
    while h < c:
        mid = (t // (2 * h)) * 2 * h + h
        if 2 * h < SUBLANES:
            mats.append(np.where(t >= mid, (u >= mid) & (u <= t), (u > t) & (u < mid)))
        masks.append(((u // (2 * h)) == (t // (2 * h))) & (t >= mid) & (u < mid))
        h *= 2
    return (np.concatenate(mats, axis=0).astype(np.float32), np.stack(masks).astype(np.float32))


def _hgrn_kernel(q_ref, f_ref, i_ref, g_ref, lb_ref, ng_ref, mall_ref, mask_ref, o_ref,
                 e_ref, st_ref, *, c):
    @pl.when(pl.program_id(1) == 0)
    def _():
        st_ref[...] = jnp.zeros(st_ref.shape, F32)

    lb = lb_ref[...]
    fl = f_ref[...]
    la = jnp.log(lb)
    lbb = jnp.log1p(-lb) + (jnp.minimum(fl, 0.0) - jnp.log1p(jnp.exp(-jnp.abs(fl))))
    logf = jnp.maximum(la, lbb) + jnp.log1p(jnp.exp(-jnp.abs(la - lbb)))
    hi = logf.astype(BF16)
    lo = (logf - hi.astype(F32)).astype(BF16)
    e2 = jnp.dot(mall_ref[...], jnp.concatenate([hi, lo], axis=1), preferred_element_type=F32)
    es = e2[:, :HGRN_WIDTH] + e2[:, HGRN_WIDTH:]
    n_levels = mask_ref.shape[0] - 1
    n_small = mall_ref.shape[0] // c - 1
    gall = es[0:c]
    e_ref[0:c, :] = gall
    e_ref[c:2 * c, :] = gall[c - 1:c, :] - gall
    e_ref[2 * c:(2 + n_small) * c, :] = es[c:]
    for lv in range(n_small, n_levels):
        blk = 2 << lv
        g3 = gall.reshape(c // blk, blk, HGRN_WIDTH)
        gm = jnp.broadcast_to(g3[:, blk // 2 - 1:blk // 2, :], g3.shape)
        e_ref[(2 + lv) * c:(3 + lv) * c, :] = (-jnp.abs(g3 - gm)).reshape(c, HGRN_WIDTH)

    nt = (((1,), (1,)), ((), ()))
    for hd in range(HGRN_HEADS):
        ls = slice(hd * HGRN_DIM, (hd + 1) * HGRN_DIM)
        qr = q_ref[:, ls]
        q = qr * jax.nn.sigmoid(qr) * (HGRN_DIM ** -0.5)
        k = (1.0 - lb[:, ls]) * jax.nn.sigmoid(-fl[:, ls])
        vb = i_ref[:, ls].astype(BF16)
        p = lax.dot_general(q.astype(BF16), k.astype(BF16), nt, preferred_element_type=F32) * mask_ref[0]
        for lv in range(n_levels):
            fac = jnp.exp(e_ref[(2 + lv) * c:(3 + lv) * c, ls])
            s = lax.dot_general((q * fac).astype(BF16), (k * fac).astype(BF16), nt,
                                preferred_element_type=F32)
            p = p + s * mask_ref[lv + 1]
        o = jnp.dot(p.astype(BF16), vb, preferred_element_type=F32)
        gcum = e_ref[0:c, ls]
        st = st_ref[hd]
        o = o + lax.dot_general((q * jnp.exp(gcum)).astype(BF16), st.astype(BF16), nt,
                                preferred_element_type=F32)
        kd = (k * jnp.exp(e_ref[c:2 * c, ls])).astype(BF16)
        st_ref[hd] = st * jnp.exp(gcum[c - 1:c, :]) + lax.dot_general(
            vb, kd, (((0,), (0,)), ((), ())), preferred_element_type=F32)
        ms = jnp.mean(o * o, axis=-1, keepdims=True)
        o = o * lax.rsqrt(ms + EPS) * ng_ref[...] * jax.nn.sigmoid(g_ref[:, ls])
        o_ref[:, ls] = o.astype(o_ref.dtype)


def _hgrn_mixer(proj, lb, norm_g, batch, seq):
    t = proj.shape[0]
    c = min(HGRN_C, seq)
    nc = seq // c
    m_all, masks = _hgrn_consts(c)
    m_all = jnp.asarray(m_all, BF16)
    masks = jnp.asarray(masks, F32)

    def col(j):
        return pl.BlockSpec((c, HGRN_WIDTH), lambda b, i: (b * nc + i, j))

    return pl.pallas_call(
        functools.partial(_hgrn_kernel, c=c),
        grid=(batch, nc),
        in_specs=[
            col(1), col(2), col(3), col(4),
            pl.BlockSpec((1, HGRN_WIDTH), lambda b, i: (0, 0)),
            pl.BlockSpec((1, HGRN_DIM), lambda b, i: (0, 0)),
            pl.BlockSpec(m_all.shape, lambda b, i: (0, 0)),
            pl.BlockSpec(masks.shape, lambda b, i: (0, 0, 0)),
        ],
        out_specs=pl.BlockSpec((c, HGRN_WIDTH), lambda b, i: (b * nc + i, 0)),
        out_shape=jax.ShapeDtypeStruct((t, HGRN_WIDTH), BF16),
        scratch_shapes=[
            pltpu.VMEM(((masks.shape[0] + 1) * c, HGRN_WIDTH), F32),
            pltpu.VMEM((HGRN_HEADS, HGRN_DIM, HGRN_DIM), F32),
        ],
        compiler_params=_params(("parallel", "arbitrary"), VMEM_LIMIT),
        name="hgrn2_mixer",
    )(proj, proj, proj, proj, lb.reshape(1, HGRN_WIDTH), norm_g.reshape(1, HGRN_DIM), m_all, masks)


def _out_proj_kernel(h_ref, a_ref, s_ref, r_ref, w_ref, gn_ref, rh_ref, rl_ref,
                     ho_ref, hn_ref, eid_ref, gate_ref):
    s_lo, r_lo = ATTN_WIDTH, ATTN_WIDTH + S5_WIDTH
    acc = jnp.dot(a_ref[...], w_ref[0:s_lo, :], preferred_element_type=F32)
    acc = acc + jnp.dot(s_ref[...], w_ref[s_lo:r_lo, :], preferred_element_type=F32)
    acc = acc + jnp.dot(r_ref[...], w_ref[r_lo:, :], preferred_element_type=F32)
    h = h_ref[...] + acc
    ho_ref[...] = h
    ms = jnp.mean(h * h, axis=-1, keepdims=True)
    hn = h * lax.rsqrt(ms + EPS) * gn_ref[...]
    half = hn.shape[1] // 2
    hn_ref[...] = _pack_bf16_pair(hn[:, :half], hn[:, half:])

    hh = hn.astype(BF16)
    hl = (hn - hh.astype(F32)).astype(BF16)
    logits = (jnp.dot(hh, rh_ref[...], preferred_element_type=F32)
              + jnp.dot(hh, rl_ref[...], preferred_element_type=F32)
              + jnp.dot(hl, rh_ref[...], preferred_element_type=F32))
    lane = lax.broadcasted_iota(jnp.int32, logits.shape, 1).astype(F32)
    neg = -jnp.inf
    big = 1e9
    is_group = jnp.where(lane >= N_EXPERTS, jnp.where(lane < N_EXPERTS + N_GROUPS, 1.0, 0.0), 0.0)
    gl = jnp.where(is_group > 0, logits, neg)
    gmax = jnp.max(gl, axis=-1, keepdims=True)
    g_lane = jnp.min(jnp.where(gl == gmax, lane, big), axis=-1, keepdims=True)
    p_g = 1.0 / jnp.sum(jnp.exp(gl - gmax), axis=-1, keepdims=True)
    lo_lane = (g_lane - N_EXPERTS) * EXPERTS_PER_GROUP
    in_group = jnp.where(lane >= lo_lane, jnp.where(lane < lo_lane + EXPERTS_PER_GROUP, 1.0, 0.0), 0.0)
    el = jnp.where(in_group > 0, logits, neg)
    t1 = jnp.max(el, axis=-1, keepdims=True)
    i1 = jnp.min(jnp.where(el == t1, lane, big), axis=-1, keepdims=True)
    el2 = jnp.where(lane == i1, neg, el)
    t2 = jnp.max(el2, axis=-1, keepdims=True)
    i2 = jnp.min(jnp.where(el2 == t2, lane, big), axis=-1, keepdims=True)
    e21 = jnp.exp(t2 - t1)
    g1 = p_g / (1.0 + e21)
    g2 = p_g * e21 / (1.0 + e21)
    eid_ref[...] = jnp.where(lane == 0, i1, jnp.where(lane == 1, i2, 0.0)).astype(jnp.int32)
    gate_ref[...] = jnp.where(lane == 0, g1, jnp.where(lane == 1, g2, 0.0))


def _out_proj(h, attn, s5, hg, w_out, gn, r_hi, r_lo):
    t, d = h.shape
    tm = min(OUT_TM, t)

    def rows(w):
        return pl.BlockSpec((tm, w), lambda i: (i, 0))

    def full(a):
        return pl.BlockSpec(a.shape, lambda i: (0,) * a.ndim)

    return pl.pallas_call(
        _out_proj_kernel,
        grid=(t // tm,),
        in_specs=[rows(d), rows(ATTN_WIDTH), rows(S5_WIDTH), rows(HGRN_WIDTH),
                  full(w_out), pl.BlockSpec((1, d), lambda i: (0, 0)),
                  full(r_hi), full(r_lo)],
        out_specs=[rows(d), rows(d // 2), rows(LANES), rows(LANES)],
        out_shape=[jax.ShapeDtypeStruct((t, d), F32), jax.ShapeDtypeStruct((t, d // 2), jnp.uint32),
                   jax.ShapeDtypeStruct((t, LANES), jnp.int32), jax.ShapeDtypeStruct((t, LANES), F32)],
        compiler_params=_params(("parallel",), VMEM_LIMIT),
        name="out_proj_router",
    )(h, attn, s5, hg, w_out, gn.reshape(1, d), r_hi, r_lo)


def _moe_plan(eid, tm):
    flat = eid.reshape(-1)
    n_slots = flat.shape[0]
    onehot = (flat[:, None] == jnp.arange(N_EXPERTS, dtype=jnp.int32)[None, :]).astype(jnp.int32)
    csum = jnp.cumsum(onehot, axis=0)
    rank = jnp.sum(onehot * csum, axis=1) - 1
    counts = csum[-1]
    tiles = (counts + tm - 1) // tm
    tile_end = jnp.cumsum(tiles)
    tile_start = tile_end - tiles
    slot_start = jnp.cumsum(counts) - counts
    pos = jnp.sum(onehot * tile_start[None, :], axis=1) * tm + rank
    tok_sorted = jnp.argsort(flat, stable=True).astype(jnp.int32) // TOP_K
    tok_sorted = jnp.pad(tok_sorted, (0, tm))
    nt_max = (n_slots + N_EXPERTS * (tm - 1)) // tm
    n_used = tile_end[-1]
    j = jnp.minimum(jnp.arange(nt_max, dtype=jnp.int32), n_used - 1)
    tile_expert = jnp.sum((j[:, None] >= tile_end[None, :]).astype(jnp.int32), axis=1)
    onehot_t = (tile_expert[:, None] == jnp.arange(N_EXPERTS, dtype=jnp.int32)[None, :]).astype(jnp.int32)
    tile_slot0 = jnp.sum(onehot_t * (slot_start - tile_start * tm)[None, :], axis=1) + j * tm
    used = (tiles > 0).astype(jnp.int32)
    ordinal = jnp.cumsum(used) - used
    ids = jnp.arange(N_EXPERTS, dtype=jnp.int32)
    later = jnp.where((ids[None, :] > ids[:, None]) & (used[None, :] > 0), ids[None, :], N_EXPERTS)
    nxt = jnp.min(later, axis=1)
    nxt = jnp.where(nxt >= N_EXPERTS, -1, nxt)
    tile_wslot = jnp.sum(onehot_t * (ordinal % 2)[None, :], axis=1)
    tile_next = jnp.sum(onehot_t * nxt[None, :], axis=1)
    return dict(pos=pos.astype(jnp.int32), tok_sorted=tok_sorted, tile_expert=tile_expert.astype(jnp.int32),
                tile_slot0=tile_slot0.astype(jnp.int32), tile_wslot=tile_wslot.astype(jnp.int32),
                tile_next=tile_next.astype(jnp.int32), n_used=n_used.reshape(1).astype(jnp.int32),
                nt_max=nt_max)


def _expert_kernel(tok_ref, te_ref, s0_ref, ws_ref, ne_ref, nu_ref, hn_ref, wg_ref, wu_ref, wd_ref, y_ref,
                   xa_ref, xb_ref, wgf_ref, wuf_ref, wdf_ref, wgb_ref, wub_ref, wdb_ref, sem, wsem,
                   *, tm, layer):
    j = pl.program_id(0)
    n_used = nu_ref[0]
    bufs = (xa_ref, xb_ref)

    def weight_copies(expert, slot):
        return [pltpu.make_async_copy(src.at[layer, expert], dst.at[slot], wsem.at[slot])
                for src, dst in ((wg_ref, wgf_ref), (wu_ref, wuf_ref), (wd_ref, wdf_ref))]

    def row_copy(buf, r, tok):
        return pltpu.make_async_copy(hn_ref.at[pl.ds(tok, 1)], bufs[buf].at[pl.ds(r, 1)], sem.at[buf])

    def wait_gather(buf):
        def wait(r, carry):
            row_copy(buf, r, 0).wait()
            return carry

        lax.fori_loop(0, tm, wait, 0, unroll=DMA_UNROLL)

    @pl.when(j == 0)
    def _():
        for cp in weight_copies(te_ref[0], ws_ref[0]):
            cp.start(priority=WEIGHT_DMA_PRIORITY)
        base = s0_ref[0]

        def issue(r, carry):
            row_copy(0, r, tok_ref[base + r]).start()
            return carry

        lax.fori_loop(0, tm, issue, 0, unroll=DMA_UNROLL)

    def tile(buf):
        wait_gather(buf)
        prev = te_ref[jnp.maximum(j - 1, 0)]

        @pl.when(jnp.logical_or(j == 0, te_ref[j] != prev))
        def _():
            slot = ws_ref[j]

            @pl.when(ne_ref[j] >= 0)
            def _():
                for cp in weight_copies(ne_ref[j], 1 - slot):
                    cp.start(priority=WEIGHT_DMA_PRIORITY)

            for cp in weight_copies(te_ref[j], slot):
                cp.wait()
            wgb_ref[...] = wgf_ref[slot].astype(BF16)
            wub_ref[...] = wuf_ref[slot].astype(BF16)
            wdb_ref[...] = wdf_ref[slot].astype(BF16)

        nbase = s0_ref[jnp.minimum(j + 1, n_used - 1)]
        for r in range(tm):
            row_copy(1 - buf, r, tok_ref[nbase + r]).start(priority=r % 2)

        x_lo, x_hi = _unpack_bf16_pair(bufs[buf][...])
        x = jnp.concatenate([x_lo.astype(BF16), x_hi.astype(BF16)], axis=1)
        g = jnp.dot(x, wgb_ref[...], preferred_element_type=F32)
        u = jnp.dot(x, wub_ref[...], preferred_element_type=F32)
        hmid = (g * jax.nn.sigmoid(g) * u).astype(BF16)
        y = jnp.dot(hmid, wdb_ref[...], preferred_element_type=F32)
        half = y.shape[1] // 2
        y_ref[...] = _pack_bf16_pair(y[:, :half], y[:, half:])

        @pl.when(j == n_used - 1)
        def _():
            wait_gather(1 - buf)

    for parity in range(2):
        @pl.when(jnp.logical_and(j < n_used, lax.rem(j, 2) == parity))
        def _(parity=parity):
            tile(parity)

    @pl.when(j >= n_used)
    def _():
        y_ref[...] = jnp.zeros(y_ref.shape, y_ref.dtype)


def _moe_experts(hn, plan, w_gate, w_up, w_down, layer, tm):
    t, dp = hn.shape
    d, de = w_gate.shape[2], w_gate.shape[3]
    nt_max = plan["nt_max"]
    grid_spec = pltpu.PrefetchScalarGridSpec(
        num_scalar_prefetch=6,
        grid=(nt_max,),
        in_specs=[pl.BlockSpec(memory_space=pl.ANY)] * 4,
        out_specs=pl.BlockSpec((tm, dp), lambda j, *_: (j, 0)),
        scratch_shapes=[pltpu.VMEM((tm, dp), jnp.uint32), pltpu.VMEM((tm, dp), jnp.uint32),
                        pltpu.VMEM((2, d, de), F32), pltpu.VMEM((2, d, de), F32), pltpu.VMEM((2, de, d), F32),
                        pltpu.VMEM((d, de), BF16), pltpu.VMEM((d, de), BF16), pltpu.VMEM((de, d), BF16),
                        pltpu.SemaphoreType.DMA((2,)), pltpu.SemaphoreType.DMA((2,))],
    )
    return pl.pallas_call(
        functools.partial(_expert_kernel, tm=tm, layer=layer),
        grid_spec=grid_spec,
        out_shape=jax.ShapeDtypeStruct((nt_max * tm, dp), jnp.uint32),
        compiler_params=_params(("arbitrary",), VMEM_LIMIT),
        name="moe_experts",
    )(plan["tok_sorted"], plan["tile_expert"], plan["tile_slot0"], plan["tile_wslot"], plan["tile_next"],
      plan["n_used"], hn, w_gate, w_up, w_down)


def _combine_kernel(pos_ref, h_ref, gate_ref, gn_ref, y_ref, o_ref, ba_ref, bb_ref, sem, *, tm, final_norm):
    i = pl.program_id(0)
    n = pl.num_programs(0)
    bufs = (ba_ref, bb_ref)

    def row_copy(b, k, r, src_row):
        return pltpu.make_async_copy(y_ref.at[pl.ds(src_row, 1)], bufs[b].at[k, pl.ds(r, 1)], sem.at[b])

    def wait_gather(b):
        def wait(r, carry):
            for k in range(TOP_K):
                row_copy(b, k, r, 0).wait()
            return carry

        lax.fori_loop(0, tm, wait, 0, unroll=DMA_UNROLL)

    @pl.when(i == 0)
    def _():
        def issue(r, carry):
            for k in range(TOP_K):
                row_copy(0, k, r, pos_ref[r * TOP_K + k]).start()
            return carry

        lax.fori_loop(0, tm, issue, 0, unroll=DMA_UNROLL)

    def tile(b):
        wait_gather(b)
        base = jnp.minimum(i + 1, n - 1) * (tm * TOP_K)
        for r in range(tm):
            for k in range(TOP_K):
                row_copy(1 - b, k, r, pos_ref[base + r * TOP_K + k]).start(priority=(r * TOP_K + k) % 2)

        gates = gate_ref[...]
        y0_lo, y0_hi = _unpack_bf16_pair(bufs[b][0])
        y1_lo, y1_hi = _unpack_bf16_pair(bufs[b][1])
        g0, g1 = gates[:, 0:1], gates[:, 1:2]
        h = h_ref[...] + jnp.concatenate([g0 * y0_lo + g1 * y1_lo, g0 * y0_hi + g1 * y1_hi], axis=1)
        if final_norm:
            ms = jnp.mean(h * h, axis=-1, keepdims=True)
            h = h * lax.rsqrt(ms + EPS) * gn_ref[...]
        o_ref[...] = h

        @pl.when(i == n - 1)
        def _():
            wait_gather(1 - b)

    for parity in range(2):
        @pl.when(lax.rem(i, 2) == parity)
        def _(parity=parity):
            tile(parity)


def _moe_combine(h, gates, ys, pos, final_g, final_norm):
    t, d = h.shape
    tm = min(COMB_TM, t)
    grid_spec = pltpu.PrefetchScalarGridSpec(
        num_scalar_prefetch=1,
        grid=(t // tm,),
        in_specs=[
            pl.BlockSpec((tm, d), lambda i, p: (i, 0)),
            pl.BlockSpec((tm, LANES), lambda i, p: (i, 0)),
            pl.BlockSpec((1, d), lambda i, p: (0, 0)),
            pl.BlockSpec(memory_space=pl.ANY),
        ],
        out_specs=pl.BlockSpec((tm, d), lambda i, p: (i, 0)),
        scratch_shapes=[pltpu.VMEM((TOP_K, tm, d // 2), jnp.uint32), pltpu.VMEM((TOP_K, tm, d // 2), jnp.uint32),
                        pltpu.SemaphoreType.DMA((2,))],
    )
    return pl.pallas_call(
        functools.partial(_combine_kernel, tm=tm, final_norm=final_norm),
        grid_spec=grid_spec,
        out_shape=jax.ShapeDtypeStruct((t, d), F32),
        compiler_params=_params(("arbitrary",), VMEM_LIMIT),
        name="moe_combine",
    )(pos, h, gates, final_g.reshape(1, d), ys)


def kernel(x, w_in, w_out, mix_norm_g, ffn_norm_g, rel_bias, diff_lambda, attn_subln_g, s5_a_re, s5_a_im, s5_log_dt, s5_b_re, s5_b_im, s5_c_re, s5_c_im, s5_d, s5_w_glu, hgrn_lb_logits, hgrn_norm_g, moe_w_group, moe_w_router, moe_w_gate, moe_w_up, moe_w_down, final_norm_g):
    batch, seq, d = x.shape
    depth = w_in.shape[0]
    t = batch * seq
    h = x.reshape(t, d)

    lb_cum = jnp.cumsum(jax.nn.softmax(hgrn_lb_logits.astype(F32), axis=0), axis=0)
    lb_all = lb_cum - lb_cum[0:1]
    attn_bias = _attn_bias_tiles(rel_bias, min(ATTN_TQ, seq), min(ATTN_TK, seq))

    for l in range(depth):
        proj = _in_proj(h, mix_norm_g[l], w_in, l)

        lam_init = 0.8 - 0.6 * math.exp(-0.3 * l)
        lv = diff_lambda[l].astype(F32)
        lam = jnp.exp(jnp.sum(lv[0] * lv[1])) - jnp.exp(jnp.sum(lv[2] * lv[3])) + lam_init
        attn = _diff_attention(proj, attn_bias, lam, attn_subln_g[l], lam_init, batch, seq)

        w_b, w_c, tabs = _s5_prep(s5_a_re[l], s5_a_im[l], s5_log_dt[l], s5_b_re[l], s5_b_im[l],
                                  s5_c_re[l], s5_c_im[l], min(S5_TB, seq) // SUBLANES)
        s5 = _s5_mixer(proj, w_b, w_c, tabs, s5_d[l], s5_w_glu[l], batch, seq)

        hg = _hgrn_mixer(proj, lb_all[l], hgrn_norm_g[l], batch, seq)

        wo = w_out[l].astype(BF16)
        w_r = jnp.concatenate([moe_w_router[l], moe_w_group[l]], axis=1).astype(F32)
        w_r = jnp.pad(w_r, ((0, 0), (0, LANES - w_r.shape[1])))
        r_hi = w_r.astype(BF16)
        r_lo = (w_r - r_hi.astype(F32)).astype(BF16)
        h, hn, eid, gates = _out_proj(h, attn, s5, hg, wo, ffn_norm_g[l], r_hi, r_lo)

        plan = _moe_plan(eid[:, :TOP_K], MOE_TM)
        ys = _moe_experts(hn, plan, moe_w_gate, moe_w_up, moe_w_down, l, MOE_TM)
        h = _moe_combine(h, gates, ys, plan["pos"], final_norm_g, final_norm=(l == depth - 1))

    return h.reshape(batch, seq, d)
```

```python
import functools
import math

import numpy as np
import jax
import jax.numpy as jnp
from jax import lax
from jax.experimental import pallas as pl
from jax.experimental.pallas import tpu as pltpu

F32 = jnp.float32
BF16 = jnp.bfloat16
EPS = 1e-6
LOG2E = 1.4426950408889634

D_MODEL = 2048
ATTN_HEADS = 6
ATTN_QK_DIM = 64
ATTN_V_DIM = 128
ATTN_WIDTH = ATTN_HEADS * ATTN_V_DIM
REL_BUCKETS = 32
REL_MAX_DIST = 128
S5_GROUPS = 40
S5_GROUP_CH = 16
S5_STATE = 64
S5_WIDTH = S5_GROUPS * S5_GROUP_CH
S5_LANES = S5_GROUPS * S5_STATE
HGRN_HEADS = 5
HGRN_DIM = 128
HGRN_WIDTH = HGRN_HEADS * HGRN_DIM
N_GROUPS = 4
EXPERTS_PER_GROUP = 8
N_EXPERTS = N_GROUPS * EXPERTS_PER_GROUP
TOP_K = 2
D_EXPERT = 512

LANES = 128
SUBLANES = 8

IN_COLS = 3 * ATTN_WIDTH + S5_WIDTH + 4 * HGRN_WIDTH
PROJ_TN = 256
IN_COLS_PAD = -(-IN_COLS // PROJ_TN) * PROJ_TN
ATTN_COL_BLOCK = (IN_COLS_PAD - 3 * ATTN_WIDTH) // LANES

PROJ_TM = 2048
ATTN_TQ = 512
ATTN_TK = 512
ATTN_NEAR = 2
ATTN_ROWS = 32
ATTN_HPS = 1
S5_TB = 256
S5_CHUNK = 640
S5_BLK = 128
HGRN_C = 128
OUT_TM = 256
MOE_TM = 256
COMB_TM = 128
DMA_UNROLL = 8
WEIGHT_DMA_PRIORITY = 1

VMEM_LIMIT = 56 * 1024 * 1024


def _params(sem, vmem=None, flags=None):
    return pltpu.CompilerParams(dimension_semantics=sem, vmem_limit_bytes=vmem, flags=flags)


def _pack_bf16_pair(lo, hi):
    lo_bits = lax.bitcast_convert_type(lo.astype(BF16).astype(F32), jnp.uint32) >> 16
    hi_bits = lax.bitcast_convert_type(hi.astype(BF16).astype(F32), jnp.uint32) & jnp.uint32(0xFFFF0000)
    return lo_bits | hi_bits


def _unpack_bf16_pair(w):
    return (lax.bitcast_convert_type(w << 16, F32),
            lax.bitcast_convert_type(w & jnp.uint32(0xFFFF0000), F32))


def _in_proj_kernel(x_ref, g_ref, w_ref, wt_ref, o_ref, xn_ref, *, tail_tile):
    j = pl.program_id(1)

    @pl.when(j == 0)
    def _():
        x = x_ref[...]
        ms = jnp.mean(x * x, axis=-1, keepdims=True)
        xn_ref[...] = (x * lax.rsqrt(ms + EPS) * g_ref[...]).astype(BF16)

    w = jnp.where(j == tail_tile, wt_ref[...], w_ref[0])
    o_ref[...] = jnp.dot(xn_ref[...], w.astype(BF16), preferred_element_type=F32)


def _in_proj(h, g, w_all, layer):
    t, d = h.shape
    tm = min(PROJ_TM, t)
    tn = PROJ_TN
    a = 3 * ATTN_WIDTH
    rest = IN_COLS - a
    assert a % tn == 0 and IN_COLS_PAD - IN_COLS == tn - rest % tn
    rest_full, a_tiles = rest // tn, a // tn
    tail_tile = rest_full
    w_tail = lax.slice(w_all, (layer, 0, a + rest_full * tn), (layer + 1, d, IN_COLS))[0]
    w_tail = jnp.pad(w_tail, ((0, 0), (0, IN_COLS_PAD - IN_COLS)))

    def w_map(i, j):
        col = jnp.where(j < tail_tile, a_tiles + j, jnp.where(j == tail_tile, 0, j - tail_tile - 1))
        return (layer, 0, col)

    return pl.pallas_call(
        functools.partial(_in_proj_kernel, tail_tile=tail_tile),
        grid=(t // tm, IN_COLS_PAD // tn),
        in_specs=[
            pl.BlockSpec((tm, d), lambda i, j: (i, 0)),
            pl.BlockSpec((1, d), lambda i, j: (0, 0)),
            pl.BlockSpec((1, d, tn), w_map),
            pl.BlockSpec((d, tn), lambda i, j: (0, 0)),
        ],
        out_specs=pl.BlockSpec((tm, tn), lambda i, j: (i, j)),
        out_shape=jax.ShapeDtypeStruct((t, IN_COLS_PAD), F32),
        scratch_shapes=[pltpu.VMEM((tm, d), BF16)],
        compiler_params=_params(("parallel", "arbitrary"), VMEM_LIMIT),
        name="in_proj",
    )(h, g.reshape(1, d), w_all, w_tail)


def _attn_kernel(lam_ref, q_ref, k_ref, v_ref, bias_ref, g_ref, o_ref,
                 kb_ref, vb_ref, qz_ref, m_ref, a_ref, acc_ref, s0_ref, s1_ref, p_ref, *, tq, tk, out_scale):
    qi = pl.program_id(2)
    ratio = tq // tk
    heads = range(ATTN_HPS)

    @pl.when(qi == 0)
    def _():
        for h in heads:
            hl = slice(h * LANES, (h + 1) * LANES)
            kb_ref[h] = k_ref[:, hl].astype(BF16)
            vb_ref[h, :, 0:LANES] = v_ref[:, hl].astype(BF16)
            vb_ref[h, :, LANES:2 * LANES] = jnp.ones((vb_ref.shape[1], LANES), BF16)

    for h in heads:
        q = q_ref[:, h * LANES:(h + 1) * LANES] * (ATTN_QK_DIM ** -0.5 * LOG2E)
        lane = lax.broadcasted_iota(jnp.int32, q.shape, 1)
        qz_ref[h, 0:tq, :] = jnp.where(lane < ATTN_QK_DIM, q, 0.0).astype(BF16)
        qz_ref[h, tq:2 * tq, :] = jnp.where(lane >= ATTN_QK_DIM, q, 0.0).astype(BF16)
    m_ref[...] = jnp.full(m_ref.shape, -jnp.inf, F32)
    acc_ref[...] = jnp.zeros(acc_ref.shape, F32)

    n_col = tk // LANES
    groups = [slice(g * ATTN_ROWS, (g + 1) * ATTN_ROWS) for g in range(2 * tq // ATTN_ROWS)]

    def scores(j, s_ref):
        for h in heads:
            k = kb_ref[h, pl.ds(pl.multiple_of(j * tk, tk), tk), :]
            s_ref[h] = lax.dot_general(qz_ref[h], k, (((1,), (1,)), ((), ())), preferred_element_type=F32)

    def softmax_pv(j, s_ref, biased):
        if biased:
            kind = jnp.clip(j - ratio * qi + ATTN_NEAR, 0, ATTN_NEAR + ratio - 1)
        for h in heads:
            for rows in groups:
                cols = [s_ref[h, rows, i * LANES:(i + 1) * LANES] for i in range(n_col)]
                if biased:
                    b0 = rows.start % tq
                    cols = [x + bias_ref[h, kind, b0:b0 + ATTN_ROWS, i * LANES:(i + 1) * LANES]
                            for i, x in enumerate(cols)]
                    for i, x in enumerate(cols):
                        s_ref[h, rows, i * LANES:(i + 1) * LANES] = x
                mx = functools.reduce(jnp.maximum, cols)
                m_old = m_ref[h, rows, :]
                m_new = jnp.maximum(m_old, jnp.max(mx, axis=-1, keepdims=True))
                m_ref[h, rows, :] = m_new
                a_ref[h, rows, :] = jnp.exp2(m_old - m_new)
        for h in heads:
            for rows in groups:
                m_new = m_ref[h, rows, :]
                for i in range(n_col):
                    p_ref[h, rows, i * LANES:(i + 1) * LANES] = jnp.exp2(
                        s_ref[h, rows, i * LANES:(i + 1) * LANES] - m_new).astype(BF16)
        for h in heads:
            v = vb_ref[h, pl.ds(pl.multiple_of(j * tk, tk), tk), :]
            alpha = a_ref[h]
            acc_ref[h] = (jnp.concatenate([alpha, alpha], axis=1) * acc_ref[h]
                          + jnp.dot(p_ref[h], v, preferred_element_type=F32))

    n_tiles = ratio * (qi + 1)
    first = lax.rem(n_tiles, 2)
    n_far = jnp.maximum(ratio * qi - ATTN_NEAR + 1, 0)
    n_far_pairs = jnp.maximum(n_far - first, 0) // 2
    n_pairs = (n_tiles - first) // 2

    @pl.when(first == 1)
    def _():
        scores(0, s1_ref)
        scores(jnp.minimum(1, n_tiles - 1), s0_ref)
        softmax_pv(0, s1_ref, True)

    @pl.when(first == 0)
    def _():
        scores(0, s0_ref)

    def pair_body(jj, carry, biased):
        j = first + 2 * jj
        scores(j + 1, s1_ref)
        softmax_pv(j, s0_ref, biased)
        scores(jnp.minimum(j + 2, n_tiles - 1), s0_ref)
        softmax_pv(j + 1, s1_ref, biased)
        return carry

    lax.fori_loop(0, n_far_pairs, functools.partial(pair_body, biased=False), 0)
    lax.fori_loop(n_far_pairs, n_pairs, functools.partial(pair_body, biased=True), 0)

    for h in heads:
        inv_l = 1.0 / acc_ref[h, :, LANES:2 * LANES]
        o1 = acc_ref[h, 0:tq, 0:LANES] * inv_l[0:tq]
        o2 = acc_ref[h, tq:2 * tq, 0:LANES] * inv_l[tq:2 * tq]
        o = o1 - lam_ref[0] * o2
        ms = jnp.mean(o * o, axis=-1, keepdims=True)
        o_ref[:, h * LANES:(h + 1) * LANES] = (o * lax.rsqrt(ms + EPS) * g_ref[...] * out_scale).astype(o_ref.dtype)


def _t5_bucket(dist):
    n = jnp.maximum(dist, 0)
    max_exact = REL_BUCKETS // 2
    large = max_exact + (jnp.log(jnp.maximum(n, 1).astype(F32) / max_exact)
                         / math.log(REL_MAX_DIST / max_exact) * (REL_BUCKETS - max_exact)).astype(jnp.int32)
    large = jnp.minimum(large, REL_BUCKETS - 1)
    return jnp.where(n < max_exact, n, large)


def _attn_bias_tiles(rel_bias, tq, tk):
    table = rel_bias.astype(F32)
    far = table[REL_BUCKETS - 1]
    r = jnp.arange(tq)[:, None]
    c = jnp.arange(tk)[None, :]

    def lookup(dist):
        bucket = _t5_bucket(dist)
        out = jnp.zeros((ATTN_HEADS,) + dist.shape, F32)
        for b in range(REL_BUCKETS):
            out = jnp.where((bucket == b)[None], table[b][:, None, None], out)
        return out

    tiles = [jnp.zeros((ATTN_HEADS, tq, tk), F32)]
    for rel in range(-ATTN_NEAR + 1, tq // tk):
        dist = r - c - rel * tk
        tiles.append(jnp.where((dist >= 0)[None], (lookup(dist) - far[:, None, None]) * LOG2E, -1e30))
    return jnp.stack(tiles, axis=1)


def _diff_attention(proj, bias, lam, subln_g, lam_init, batch, seq):
    t = proj.shape[0]
    tq = min(ATTN_TQ, seq)
    tk = min(ATTN_TK, seq)
    ratio = tq // tk
    hps = ATTN_HPS
    assert (ATTN_NEAR - 1) * tk >= REL_MAX_DIST and ratio * tk == tq
    assert ATTN_HEADS % hps == 0 and ATTN_COL_BLOCK % hps == 0
    nq = seq // tq
    kernel = functools.partial(_attn_kernel, tq=tq, tk=tk, out_scale=1.0 - lam_init)
    qb, kb, vb = (ATTN_COL_BLOCK // hps, (ATTN_COL_BLOCK + ATTN_HEADS) // hps,
                  (ATTN_COL_BLOCK + 2 * ATTN_HEADS) // hps)
    w = hps * LANES
    return pl.pallas_call(
        kernel,
        grid=(batch, ATTN_HEADS // hps, nq),
        in_specs=[
            pl.BlockSpec(memory_space=pltpu.SMEM),
            pl.BlockSpec((tq, w), lambda b, h, i: (b * nq + i, qb + h)),
            pl.BlockSpec((seq, w), lambda b, h, i: (b, kb + h)),
            pl.BlockSpec((seq, w), lambda b, h, i: (b, vb + h)),
            pl.BlockSpec((hps,) + bias.shape[1:], lambda b, h, i: (h, 0, 0, 0)),
            pl.BlockSpec((1, LANES), lambda b, h, i: (0, 0)),
        ],
        out_specs=pl.BlockSpec((tq, w), lambda b, h, i: (b * nq + i, h)),
        out_shape=jax.ShapeDtypeStruct((t, ATTN_WIDTH), BF16),
        scratch_shapes=[
            pltpu.VMEM((hps, seq, LANES), BF16),
            pltpu.VMEM((hps, seq, 2 * LANES), BF16),
            pltpu.VMEM((hps, 2 * tq, LANES), BF16),
            pltpu.VMEM((hps, 2 * tq, LANES), F32),
            pltpu.VMEM((hps, 2 * tq, LANES), F32),
            pltpu.VMEM((hps, 2 * tq, 2 * LANES), F32),
            pltpu.VMEM((hps, 2 * tq, tk), F32),
            pltpu.VMEM((hps, 2 * tq, tk), F32),
            pltpu.VMEM((hps, 2 * tq, tk), BF16),
        ],
        compiler_params=_params(("parallel", "parallel", "arbitrary"), VMEM_LIMIT),
        name="diff_attn",
    )(lam.reshape(1), proj, proj, proj, bias, subln_g.reshape(1, LANES))


def _s5_kernel(u_ref, perm_ref, permt_ref, wb_ref, wc_ref, pw_ref, d_ref, wglu_ref, o_ref,
               xr_ref, xi_ref, cr_ref, ci_ref, kr_ref, ki_ref, *, tb):
    @pl.when(pl.program_id(1) == 0)
    def _():
        cr_ref[...] = jnp.zeros(cr_ref.shape, F32)
        ci_ref[...] = jnp.zeros(ci_ref.shape, F32)

    un = u_ref[...]
    uh = un.astype(BF16)
    ul = (un - uh.astype(F32)).astype(BF16)
    up = jnp.dot(perm_ref[...], jnp.concatenate([uh, ul], axis=1), preferred_element_type=F32)
    ub = up[:, :S5_WIDTH].astype(BF16)
    u = up[:, :S5_WIDTH] + up[:, S5_WIDTH:]
    nblk = S5_WIDTH // S5_BLK
    sw = S5_LANES // nblk
    tpb = sw // LANES
    for i in range(nblk):
        bu = jnp.dot(ub[:, i * S5_BLK:(i + 1) * S5_BLK], wb_ref[i], preferred_element_type=F32)
        for k in range(tpb):
            xr_ref[i * tpb + k] = bu[:, k * LANES:(k + 1) * LANES]
            xi_ref[i * tpb + k] = bu[:, sw + k * LANES:sw + (k + 1) * LANES]

    sl = tb // SUBLANES
    for c in range(S5_LANES // S5_CHUNK):
        tiles = range(c * S5_CHUNK // LANES, (c + 1) * S5_CHUNK // LANES)
        lss = [slice(n * LANES, (n + 1) * LANES) for n in tiles]
        a = [(pw_ref[0, 0, :, ls], pw_ref[1, 0, :, ls]) for ls in lss]

        def local_step(t, carry, tiles=tiles, a=a):
            rows = pl.ds(pl.multiple_of(t * SUBLANES, SUBLANES), SUBLANES)
            out = []
            for n, (ar, ai), (pr, pi) in zip(tiles, a, carry):
                xr = ar * pr - ai * pi + xr_ref[n, rows, :]
                xi = ar * pi + ai * pr + xi_ref[n, rows, :]
                xr_ref[n, rows, :] = xr
                xi_ref[n, rows, :] = xi
                out.append((xr, xi))
            return tuple(out)

        zero = jnp.zeros((SUBLANES, LANES), F32)
        ends = lax.fori_loop(0, sl, local_step, tuple((zero, zero) for _ in tiles), unroll=2)

        for ls, (er, ei) in zip(lss, ends):
            lr, li = pw_ref[0, sl - 1, 0:1, ls], pw_ref[1, sl - 1, 0:1, ls]
            kr, ki = cr_ref[:, ls], ci_ref[:, ls]
            for sub in range(SUBLANES):
                kr_ref[sub:sub + 1, ls] = kr
                ki_ref[sub:sub + 1, ls] = ki
                kr, ki = (er[sub:sub + 1, :] + lr * kr - li * ki, ei[sub:sub + 1, :] + lr * ki + li * kr)
            cr_ref[:, ls] = kr
            ci_ref[:, ls] = ki

        k8 = [(kr_ref[:, ls], ki_ref[:, ls]) for ls in lss]

        def fix_step(t, carry, tiles=tiles, lss=lss, k8=k8):
            rows = pl.ds(pl.multiple_of(t * SUBLANES, SUBLANES), SUBLANES)
            for n, ls, (kr8, ki8) in zip(tiles, lss, k8):
                pr = pw_ref[0, t, :, ls]
                pi = pw_ref[1, t, :, ls]
                xr_ref[n, rows, :] = xr_ref[n, rows, :] + pr * kr8 - pi * ki8
                xi_ref[n, rows, :] = xi_ref[n, rows, :] + pr * ki8 + pi * kr8
            return carry

        lax.fori_loop(0, sl, fix_step, 0, unroll=2)

    ys = []
    for i in range(nblk):
        xc = jnp.concatenate([xr_ref[i * tpb + k] for k in range(tpb)]
                             + [xi_ref[i * tpb + k] for k in range(tpb)], axis=1)
        ys.append(jnp.dot(xc.astype(BF16), wc_ref[i], preferred_element_type=F32))
    y = jnp.concatenate(ys, axis=1) + d_ref[...] * u
    z = 0.5 * y * (1.0 + jnp.tanh(math.sqrt(2.0 / math.pi) * (y + 0.044715 * (y * y * y))))
    gate = jax.nn.sigmoid(jnp.dot(z.astype(BF16), wglu_ref[...].astype(BF16), preferred_element_type=F32))
    o_ref[...] = jnp.dot(permt_ref[...], (z * gate).astype(BF16),
                         preferred_element_type=F32).astype(o_ref.dtype)


def _s5_prep(a_re, a_im, log_dt, b_re, b_im, c_re, c_im, steps):
    lr = jnp.minimum(a_re.astype(F32), -1e-4)
    li = a_im.astype(F32)
    dt = jnp.exp(log_dt.astype(F32))[:, None]
    mag = jnp.exp(lr * dt)
    ab_re = mag * jnp.cos(li * dt)
    ab_im = mag * jnp.sin(li * dt)
    den = lr * lr + li * li
    nr = ab_re - 1.0
    ni = ab_im
    f_re = (nr * lr + ni * li) / den
    f_im = (ni * lr - nr * li) / den
    br = b_re.astype(F32)
    bi = b_im.astype(F32)
    bb_re = f_re[..., None] * br - f_im[..., None] * bi
    bb_im = f_re[..., None] * bi + f_im[..., None] * br

    nblk = S5_WIDTH // S5_BLK
    gpb = S5_GROUPS // nblk
    eye = jnp.eye(gpb, dtype=F32)

    def bmat(bb):
        bb = bb.reshape(nblk, gpb, S5_STATE, S5_GROUP_CH)
        m = jnp.einsum('ignc,gh->igchn', bb, eye)
        return m.reshape(nblk, gpb * S5_GROUP_CH, gpb * S5_STATE)

    def cmat(cc):
        cc = cc.astype(F32).reshape(nblk, gpb, S5_GROUP_CH, S5_STATE)
        m = jnp.einsum('igcn,gh->ignhc', cc, eye)
        return m.reshape(nblk, gpb * S5_STATE, gpb * S5_GROUP_CH)

    w_b = jnp.concatenate([bmat(bb_re), bmat(bb_im)], axis=2).astype(BF16)
    w_c = jnp.concatenate([cmat(c_re), -cmat(c_im)], axis=1).astype(BF16)

    j = jnp.arange(1, steps + 1, dtype=F32)[:, None, None]
    m = jnp.exp(lr * dt * j)
    pows = jnp.stack([(m * jnp.cos(li * dt * j)).reshape(steps, 1, -1),
                      (m * jnp.sin(li * dt * j)).reshape(steps, 1, -1)])
    pows = jnp.broadcast_to(pows, (2, steps, SUBLANES, S5_LANES))
    return w_b, w_c, pows


def _s5_mixer(proj, w_b, w_c, tabs, d_skip, w_glu, batch, seq):
    t = proj.shape[0]
    tb = min(S5_TB, seq)
    nt = seq // tb
    sl = tb // SUBLANES
    r = np.arange(tb)
    perm = np.zeros((tb, tb), np.float32)
    perm[r, (r % SUBLANES) * sl + r // SUBLANES] = 1.0
    perm_t = jnp.asarray(perm.T, BF16)
    perm = jnp.asarray(perm, BF16)
    return pl.pallas_call(
        functools.partial(_s5_kernel, tb=tb),
        grid=(batch, nt),
        in_specs=[
            pl.BlockSpec((tb, S5_WIDTH), lambda b, i: (b * nt + i, 0)),
            pl.BlockSpec((tb, tb), lambda b, i: (0, 0)),
            pl.BlockSpec((tb, tb), lambda b, i: (0, 0)),
            pl.BlockSpec(w_b.shape, lambda b, i: (0, 0, 0)),
            pl.BlockSpec(w_c.shape, lambda b, i: (0, 0, 0)),
            pl.BlockSpec(tabs.shape, lambda b, i: (0, 0, 0, 0)),
            pl.BlockSpec((1, S5_WIDTH), lambda b, i: (0, 0)),
            pl.BlockSpec((S5_WIDTH, S5_WIDTH), lambda b, i: (0, 0)),
        ],
        out_specs=pl.BlockSpec((tb, S5_WIDTH), lambda b, i: (b * nt + i, 0)),
        out_shape=jax.ShapeDtypeStruct((t, S5_WIDTH), BF16),
        scratch_shapes=[
            pltpu.VMEM((S5_LANES // LANES, tb, LANES), F32),
            pltpu.VMEM((S5_LANES // LANES, tb, LANES), F32),
            pltpu.VMEM((1, S5_LANES), F32),
            pltpu.VMEM((1, S5_LANES), F32),
            pltpu.VMEM((SUBLANES, S5_LANES), F32),
            pltpu.VMEM((SUBLANES, S5_LANES), F32),
        ],
        compiler_params=_params(("parallel", "arbitrary"), VMEM_LIMIT),
        name="s5_mixer",
    )(proj, perm, perm_t, w_b, w_c, tabs, d_skip.reshape(1, S5_WIDTH), w_glu)


def _hgrn_consts(c):
    t = np.arange(c)[:, None]
    u = np.arange(c)[None, :]
    mats = [u <= t]
    masks = [np.eye(c, dtype=bool)]
    h = 1
    while h < c:
        mid = (t // (2 * h)) * 2 * h + h
        if 2 * h < SUBLANES:
            mats.append(np.where(t >= mid, (u >= mid) & (u <= t), (u > t) & (u < mid)))
        masks.append(((u // (2 * h)) == (t // (2 * h))) & (t >= mid) & (u < mid))
        h *= 2
    return (np.concatenate(mats, axis=0).astype(np.float32), np.stack(masks).astype(np.float32))


def _hgrn_kernel(q_ref, f_ref, i_ref, g_ref, lb_ref, ng_ref, mall_ref, mask_ref, o_ref,
                 e_ref, st_ref, *, c):
    @pl.when(pl.program_id(1) == 0)
    def _():
        st_ref[...] = jnp.zeros(st_ref.shape, F32)

    lb = lb_ref[...]
    fl = f_ref[...]
    la = jnp.log(lb)
    lbb = jnp.log1p(-lb) + (jnp.minimum(fl, 0.0) - jnp.log1p(jnp.exp(-jnp.abs(fl))))
    logf = jnp.maximum(la, lbb) + jnp.log1p(jnp.exp(-jnp.abs(la - lbb)))
    hi = logf.astype(BF16)
    lo = (logf - hi.astype(F32)).astype(BF16)
    e2 = jnp.dot(mall_ref[...], jnp.concatenate([hi, lo], axis=1), preferred_element_type=F32)
    es = e2[:, :HGRN_WIDTH] + e2[:, HGRN_WIDTH:]
    n_levels = mask_ref.shape[0] - 1
    n_small = mall_ref.shape[0] // c - 1
    gall = es[0:c]
    e_ref[0:c, :] = gall
    e_ref[c:2 * c, :] = gall[c - 1:c, :] - gall
    e_ref[2 * c:(2 + n_small) * c, :] = es[c:]
    for lv in range(n_small, n_levels):
        blk = 2 << lv
        g3 = gall.reshape(c // blk, blk, HGRN_WIDTH)
        gm = jnp.broadcast_to(g3[:, blk // 2 - 1:blk // 2, :], g3.shape)
        e_ref[(2 + lv) * c:(3 + lv) * c, :] = (-jnp.abs(g3 - gm)).reshape(c, HGRN_WIDTH)

    nt = (((1,), (1,)), ((), ()))
    for hd in range(HGRN_HEADS):
        ls = slice(hd * HGRN_DIM, (hd + 1) * HGRN_DIM)
        qr = q_ref[:, ls]
        q = qr * jax.nn.sigmoid(qr) * (HGRN_DIM ** -0.5)
        k = (1.0 - lb[:, ls]) * jax.nn.sigmoid(-fl[:, ls])
        vb = i_ref[:, ls].astype(BF16)
        p = lax.dot_general(q.astype(BF16), k.astype(BF16), nt, preferred_element_type=F32) * mask_ref[0]
        for lv in range(n_levels):
            fac = jnp.exp(e_ref[(2 + lv) * c:(3 + lv) * c, ls])
            s = lax.dot_general((q * fac).astype(BF16), (k * fac).astype(BF16), nt,
                                preferred_element_type=F32)
            p = p + s * mask_ref[lv + 1]
        o = jnp.dot(p.astype(BF16), vb, preferred_element_type=F32)
        gcum = e_ref[0:c, ls]
        st = st_ref[hd]
        o = o + lax.dot_general((q * jnp.exp(gcum)).astype(BF16), st.astype(BF16), nt,
                                preferred_element_type=F32)
        kd = (k * jnp.exp(e_ref[c:2 * c, ls])).astype(BF16)
        st_ref[hd] = st * jnp.exp(gcum[c - 1:c, :]) + lax.dot_general(
            vb, kd, (((0,), (0,)), ((), ())), preferred_element_type=F32)
        ms = jnp.mean(o * o, axis=-1, keepdims=True)
        o = o * lax.rsqrt(ms + EPS) * ng_ref[...] * jax.nn.sigmoid(g_ref[:, ls])
        o_ref[:, ls] = o.astype(o_ref.dtype)


def _hgrn_mixer(proj, lb, norm_g, batch, seq):
    t = proj.shape[0]
    c = min(HGRN_C, seq)
    nc = seq // c
    m_all, masks = _hgrn_consts(c)
    m_all = jnp.asarray(m_all, BF16)
    masks = jnp.asarray(masks, F32)

    def col(j):
        return pl.BlockSpec((c, HGRN_WIDTH), lambda b, i: (b * nc + i, j))

    return pl.pallas_call(
        functools.partial(_hgrn_kernel, c=c),
        grid=(batch, nc),
        in_specs=[
            col(1), col(2), col(3), col(4),
            pl.BlockSpec((1, HGRN_WIDTH), lambda b, i: (0, 0)),
            pl.BlockSpec((1, HGRN_DIM), lambda b, i: (0, 0)),
            pl.BlockSpec(m_all.shape, lambda b, i: (0, 0)),
            pl.BlockSpec(masks.shape, lambda b, i: (0, 0, 0)),
        ],
        out_specs=pl.BlockSpec((c, HGRN_WIDTH), lambda b, i: (b * nc + i, 0)),
        out_shape=jax.ShapeDtypeStruct((t, HGRN_WIDTH), BF16),
        scratch_shapes=[
            pltpu.VMEM(((masks.shape[0] + 1) * c, HGRN_WIDTH), F32),
            pltpu.VMEM((HGRN_HEADS, HGRN_DIM, HGRN_DIM), F32),
        ],
        compiler_params=_params(("parallel", "arbitrary"), VMEM_LIMIT),
        name="hgrn2_mixer",
    )(proj, proj, proj, proj, lb.reshape(1, HGRN_WIDTH), norm_g.reshape(1, HGRN_DIM), m_all, masks)


def _out_proj_kernel(h_ref, a_ref, s_ref, r_ref, w_ref, gn_ref, rw_ref,
                     ho_ref, hn_ref, eid_ref, gate_ref):
    s_lo, r_lo = ATTN_WIDTH, ATTN_WIDTH + S5_WIDTH
    acc = jnp.dot(a_ref[...], w_ref[0:s_lo, :], preferred_element_type=F32)
    acc = acc + jnp.dot(s_ref[...], w_ref[s_lo:r_lo, :], preferred_element_type=F32)
    acc = acc + jnp.dot(r_ref[...], w_ref[r_lo:, :], preferred_element_type=F32)
    h = h_ref[...] + acc
    ho_ref[...] = h
    ms = jnp.mean(h * h, axis=-1, keepdims=True)
    hn = h * lax.rsqrt(ms + EPS) * gn_ref[...]
    half = hn.shape[1] // 2
    hn_ref[...] = _pack_bf16_pair(hn[:, :half], hn[:, half:])

    hh = hn.astype(BF16)
    hl = (hn - hh.astype(F32)).astype(BF16)
    both = jnp.dot(hh, rw_ref[...], preferred_element_type=F32)
    logits = (both[:, :LANES] + both[:, LANES:]
              + jnp.dot(hl, rw_ref[:, :LANES], preferred_element_type=F32))
    lane = lax.broadcasted_iota(jnp.int32, logits.shape, 1).astype(F32)
    neg = -jnp.inf
    big = 1e9
    is_group = jnp.where(lane >= N_EXPERTS, jnp.where(lane < N_EXPERTS + N_GROUPS, 1.0, 0.0), 0.0)
    gl = jnp.where(is_group > 0, logits, neg)
    gmax = jnp.max(gl, axis=-1, keepdims=True)
    g_lane = jnp.min(jnp.where(gl == gmax, lane, big), axis=-1, keepdims=True)
    p_g = 1.0 / jnp.sum(jnp.exp(gl - gmax), axis=-1, keepdims=True)
    lo_lane = (g_lane - N_EXPERTS) * EXPERTS_PER_GROUP
    in_group = jnp.where(lane >= lo_lane, jnp.where(lane < lo_lane + EXPERTS_PER_GROUP, 1.0, 0.0), 0.0)
    el = jnp.where(in_group > 0, logits, neg)
    t1 = jnp.max(el, axis=-1, keepdims=True)
    i1 = jnp.min(jnp.where(el == t1, lane, big), axis=-1, keepdims=True)
    el2 = jnp.where(lane == i1, neg, el)
    t2 = jnp.max(el2, axis=-1, keepdims=True)
    i2 = jnp.min(jnp.where(el2 == t2, lane, big), axis=-1, keepdims=True)
    e21 = jnp.exp(t2 - t1)
    g1 = p_g / (1.0 + e21)
    g2 = p_g * e21 / (1.0 + e21)
    eid_ref[...] = jnp.where(lane == 0, i1, jnp.where(lane == 1, i2, 0.0)).astype(jnp.int32)
    gate_ref[...] = jnp.where(lane == 0, g1, jnp.where(lane == 1, g2, 0.0))


def _out_proj(h, attn, s5, hg, w_out, gn, r_w):
    t, d = h.shape
    tm = min(OUT_TM, t)

    def rows(w):
        return pl.BlockSpec((tm, w), lambda i: (i, 0))

    def full(a):
        return pl.BlockSpec(a.shape, lambda i: (0,) * a.ndim)

    return pl.pallas_call(
        _out_proj_kernel,
        grid=(t // tm,),
        in_specs=[rows(d), rows(ATTN_WIDTH), rows(S5_WIDTH), rows(HGRN_WIDTH),
                  full(w_out), pl.BlockSpec((1, d), lambda i: (0, 0)),
                  full(r_w)],
        out_specs=[rows(d), rows(d // 2), rows(LANES), rows(LANES)],
        out_shape=[jax.ShapeDtypeStruct((t, d), F32), jax.ShapeDtypeStruct((t, d // 2), jnp.uint32),
                   jax.ShapeDtypeStruct((t, LANES), jnp.int32), jax.ShapeDtypeStruct((t, LANES), F32)],
        compiler_params=_params(("parallel",), VMEM_LIMIT),
        name="out_proj_router",
    )(h, attn, s5, hg, w_out, gn.reshape(1, d), r_w)


def _moe_plan(eid, tm):
    flat = eid.reshape(-1)
    n_slots = flat.shape[0]
    onehot = (flat[:, None] == jnp.arange(N_EXPERTS, dtype=jnp.int32)[None, :]).astype(jnp.int32)
    csum = jnp.cumsum(onehot, axis=0)
    rank = jnp.sum(onehot * csum, axis=1) - 1
    counts = csum[-1]
    tiles = (counts + tm - 1) // tm
    tile_end = jnp.cumsum(tiles)
    tile_start = tile_end - tiles
    slot_start = jnp.cumsum(counts) - counts
    pos = jnp.sum(onehot * tile_start[None, :], axis=1) * tm + rank
    tok_sorted = jnp.argsort(flat, stable=True).astype(jnp.int32) // TOP_K
    tok_sorted = jnp.pad(tok_sorted, (0, tm))
    nt_max = (n_slots + N_EXPERTS * (tm - 1)) // tm
    n_used = tile_end[-1]
    j = jnp.minimum(jnp.arange(nt_max, dtype=jnp.int32), n_used - 1)
    tile_expert = jnp.sum((j[:, None] >= tile_end[None, :]).astype(jnp.int32), axis=1)
    onehot_t = (tile_expert[:, None] == jnp.arange(N_EXPERTS, dtype=jnp.int32)[None, :]).astype(jnp.int32)
    tile_slot0 = jnp.sum(onehot_t * (slot_start - tile_start * tm)[None, :], axis=1) + j * tm
    used = (tiles > 0).astype(jnp.int32)
    ordinal = jnp.cumsum(used) - used
    ids = jnp.arange(N_EXPERTS, dtype=jnp.int32)
    later = jnp.where((ids[None, :] > ids[:, None]) & (used[None, :] > 0), ids[None, :], N_EXPERTS)
    nxt = jnp.min(later, axis=1)
    nxt = jnp.where(nxt >= N_EXPERTS, -1, nxt)
    tile_wslot = jnp.sum(onehot_t * (ordinal % 2)[None, :], axis=1)
    tile_next = jnp.sum(onehot_t * nxt[None, :], axis=1)
    return dict(pos=pos.astype(jnp.int32), tok_sorted=tok_sorted, tile_expert=tile_expert.astype(jnp.int32),
                tile_slot0=tile_slot0.astype(jnp.int32), tile_wslot=tile_wslot.astype(jnp.int32),
                tile_next=tile_next.astype(jnp.int32), n_used=n_used.reshape(1).astype(jnp.int32),
                nt_max=nt_max)


def _expert_kernel(tok_ref, te_ref, s0_ref, ws_ref, ne_ref, nu_ref, hn_ref, wg_ref, wu_ref, wd_ref, y_ref,
                   xa_ref, xb_ref, wgf_ref, wuf_ref, wdf_ref, wgb_ref, wub_ref, wdb_ref, sem, wsem,
                   *, tm, layer):
    j = pl.program_id(0)
    n_used = nu_ref[0]
    bufs = (xa_ref, xb_ref)

    def weight_copies(expert, slot):
        return [pltpu.make_async_copy(src.at[layer, expert], dst.at[slot], wsem.at[slot])
                for src, dst in ((wg_ref, wgf_ref), (wu_ref, wuf_ref), (wd_ref, wdf_ref))]

    def row_copy(buf, r, tok):
        return pltpu.make_async_copy(hn_ref.at[pl.ds(tok, 1)], bufs[buf].at[pl.ds(r, 1)], sem.at[buf])

    def wait_gather(buf):
        def wait(r, carry):
            row_copy(buf, r, 0).wait()
            return carry

        lax.fori_loop(0, tm, wait, 0, unroll=DMA_UNROLL)

    @pl.when(j == 0)
    def _():
        for cp in weight_copies(te_ref[0], ws_ref[0]):
            cp.start(priority=WEIGHT_DMA_PRIORITY)
        base = s0_ref[0]

        def issue(r, carry):
            row_copy(0, r, tok_ref[base + r]).start()
            return carry

        lax.fori_loop(0, tm, issue, 0, unroll=DMA_UNROLL)

    def tile(buf):
        wait_gather(buf)
        prev = te_ref[jnp.maximum(j - 1, 0)]

        @pl.when(jnp.logical_or(j == 0, te_ref[j] != prev))
        def _():
            slot = ws_ref[j]

            @pl.when(ne_ref[j] >= 0)
            def _():
                for cp in weight_copies(ne_ref[j], 1 - slot):
                    cp.start(priority=WEIGHT_DMA_PRIORITY)

            for cp in weight_copies(te_ref[j], slot):
                cp.wait()
            wgb_ref[...] = wgf_ref[slot].astype(BF16)
            wub_ref[...] = wuf_ref[slot].astype(BF16)
            wdb_ref[...] = wdf_ref[slot].astype(BF16)

        nbase = s0_ref[jnp.minimum(j + 1, n_used - 1)]
        for r in range(tm):
            row_copy(1 - buf, r, tok_ref[nbase + r]).start()

        x_lo, x_hi = _unpack_bf16_pair(bufs[buf][...])
        x = jnp.concatenate([x_lo.astype(BF16), x_hi.astype(BF16)], axis=1)
        g = jnp.dot(x, wgb_ref[...], preferred_element_type=F32)
        u = jnp.dot(x, wub_ref[...], preferred_element_type=F32)
        hmid = (g * jax.nn.sigmoid(g) * u).astype(BF16)
        y = jnp.dot(hmid, wdb_ref[...], preferred_element_type=F32)
        half = y.shape[1] // 2
        y_ref[...] = _pack_bf16_pair(y[:, :half], y[:, half:])

        @pl.when(j == n_used - 1)
        def _():
            wait_gather(1 - buf)

    for parity in range(2):
        @pl.when(jnp.logical_and(j < n_used, lax.rem(j, 2) == parity))
        def _(parity=parity):
            tile(parity)

    @pl.when(j >= n_used)
    def _():
        y_ref[...] = jnp.zeros(y_ref.shape, y_ref.dtype)


def _moe_experts(hn, plan, w_gate, w_up, w_down, layer, tm):
    t, dp = hn.shape
    d, de = w_gate.shape[2], w_gate.shape[3]
    nt_max = plan["nt_max"]
    grid_spec = pltpu.PrefetchScalarGridSpec(
        num_scalar_prefetch=6,
        grid=(nt_max,),
        in_specs=[pl.BlockSpec(memory_space=pl.ANY)] * 4,
        out_specs=pl.BlockSpec((tm, dp), lambda j, *_: (j, 0)),
        scratch_shapes=[pltpu.VMEM((tm, dp), jnp.uint32), pltpu.VMEM((tm, dp), jnp.uint32),
                        pltpu.VMEM((2, d, de), F32), pltpu.VMEM((2, d, de), F32), pltpu.VMEM((2, de, d), F32),
                        pltpu.VMEM((d, de), BF16), pltpu.VMEM((d, de), BF16), pltpu.VMEM((de, d), BF16),
                        pltpu.SemaphoreType.DMA((2,)), pltpu.SemaphoreType.DMA((2,))],
    )
    return pl.pallas_call(
        functools.partial(_expert_kernel, tm=tm, layer=layer),
        grid_spec=grid_spec,
        out_shape=jax.ShapeDtypeStruct((nt_max * tm, dp), jnp.uint32),
        compiler_params=_params(("arbitrary",), VMEM_LIMIT),
        name="moe_experts",
    )(plan["tok_sorted"], plan["tile_expert"], plan["tile_slot0"], plan["tile_wslot"], plan["tile_next"],
      plan["n_used"], hn, w_gate, w_up, w_down)


def _combine_kernel(pos_ref, h_ref, gate_ref, gn_ref, y_ref, o_ref, ba_ref, bb_ref, sem, *, tm, final_norm):
    i = pl.program_id(0)
    n = pl.num_programs(0)
    bufs = (ba_ref, bb_ref)

    def row_copy(b, k, r, src_row):
        return pltpu.make_async_copy(y_ref.at[pl.ds(src_row, 1)], bufs[b].at[k, pl.ds(r, 1)], sem.at[b])

    def wait_gather(b):
        def wait(r, carry):
            for k in range(TOP_K):
                row_copy(b, k, r, 0).wait()
            return carry

        lax.fori_loop(0, tm, wait, 0, unroll=DMA_UNROLL)

    @pl.when(i == 0)
    def _():
        def issue(r, carry):
            for k in range(TOP_K):
                row_copy(0, k, r, pos_ref[r * TOP_K + k]).start()
            return carry

        lax.fori_loop(0, tm, issue, 0, unroll=DMA_UNROLL)

    def tile(b):
        wait_gather(b)
        base = jnp.minimum(i + 1, n - 1) * (tm * TOP_K)
        for r in range(tm):
            for k in range(TOP_K):
                row_copy(1 - b, k, r, pos_ref[base + r * TOP_K + k]).start(priority=(r * TOP_K + k) % 2)

        gates = gate_ref[...]
        y0_lo, y0_hi = _unpack_bf16_pair(bufs[b][0])
        y1_lo, y1_hi = _unpack_bf16_pair(bufs[b][1])
        g0, g1 = gates[:, 0:1], gates[:, 1:2]
        h = h_ref[...] + jnp.concatenate([g0 * y0_lo + g1 * y1_lo, g0 * y0_hi + g1 * y1_hi], axis=1)
        if final_norm:
            ms = jnp.mean(h * h, axis=-1, keepdims=True)
            h = h * lax.rsqrt(ms + EPS) * gn_ref[...]
        o_ref[...] = h

        @pl.when(i == n - 1)
        def _():
            wait_gather(1 - b)

    for parity in range(2):
        @pl.when(lax.rem(i, 2) == parity)
        def _(parity=parity):
            tile(parity)


def _moe_combine(h, gates, ys, pos, final_g, final_norm):
    t, d = h.shape
    tm = min(COMB_TM, t)
    grid_spec = pltpu.PrefetchScalarGridSpec(
        num_scalar_prefetch=1,
        grid=(t // tm,),
        in_specs=[
            pl.BlockSpec((tm, d), lambda i, p: (i, 0)),
            pl.BlockSpec((tm, LANES), lambda i, p: (i, 0)),
            pl.BlockSpec((1, d), lambda i, p: (0, 0)),
            pl.BlockSpec(memory_space=pl.ANY),
        ],
        out_specs=pl.BlockSpec((tm, d), lambda i, p: (i, 0)),
        scratch_shapes=[pltpu.VMEM((TOP_K, tm, d // 2), jnp.uint32), pltpu.VMEM((TOP_K, tm, d // 2), jnp.uint32),
                        pltpu.SemaphoreType.DMA((2,))],
    )
    return pl.pallas_call(
        functools.partial(_combine_kernel, tm=tm, final_norm=final_norm),
        grid_spec=grid_spec,
        out_shape=jax.ShapeDtypeStruct((t, d), F32),
        compiler_params=_params(("arbitrary",), VMEM_LIMIT),
        name="moe_combine",
    )(pos, h, gates, final_g.reshape(1, d), ys)


def kernel(x, w_in, w_out, mix_norm_g, ffn_norm_g, rel_bias, diff_lambda, attn_subln_g, s5_a_re, s5_a_im, s5_log_dt, s5_b_re, s5_b_im, s5_c_re, s5_c_im, s5_d, s5_w_glu, hgrn_lb_logits, hgrn_norm_g, moe_w_group, moe_w_router, moe_w_gate, moe_w_up, moe_w_down, final_norm_g):
    batch, seq, d = x.shape
    depth = w_in.shape[0]
    t = batch * seq
    h = x.reshape(t, d)

    lb_cum = jnp.cumsum(jax.nn.softmax(hgrn_lb_logits.astype(F32), axis=0), axis=0)
    lb_all = lb_cum - lb_cum[0:1]
    attn_bias = _attn_bias_tiles(rel_bias, min(ATTN_TQ, seq), min(ATTN_TK, seq))

    for l in range(depth):
        proj = _in_proj(h, mix_norm_g[l], w_in, l)

        lam_init = 0.8 - 0.6 * math.exp(-0.3 * l)
        lv = diff_lambda[l].astype(F32)
        lam = jnp.exp(jnp.sum(lv[0] * lv[1])) - jnp.exp(jnp.sum(lv[2] * lv[3])) + lam_init
        attn = _diff_attention(proj, attn_bias, lam, attn_subln_g[l], lam_init, batch, seq)

        w_b, w_c, tabs = _s5_prep(s5_a_re[l], s5_a_im[l], s5_log_dt[l], s5_b_re[l], s5_b_im[l],
                                  s5_c_re[l], s5_c_im[l], min(S5_TB, seq) // SUBLANES)
        s5 = _s5_mixer(proj, w_b, w_c, tabs, s5_d[l], s5_w_glu[l], batch, seq)

        hg = _hgrn_mixer(proj, lb_all[l], hgrn_norm_g[l], batch, seq)

        wo = w_out[l].astype(BF16)
        w_r = jnp.concatenate([moe_w_router[l], moe_w_group[l]], axis=1).astype(F32)
        w_r = jnp.pad(w_r, ((0, 0), (0, LANES - w_r.shape[1])))
        r_hi = w_r.astype(BF16)
        r_lo = (w_r - r_hi.astype(F32)).astype(BF16)
        h, hn, eid, gates = _out_proj(h, attn, s5, hg, wo, ffn_norm_g[l], jnp.concatenate([r_hi, r_lo], axis=1))

        plan = _moe_plan(eid[:, :TOP_K], MOE_TM)
        ys = _moe_experts(hn, plan, moe_w_gate, moe_w_up, moe_w_down, l, MOE_TM)
        h = _moe_combine(h, gates, ys, plan["pos"], final_norm_g, final_norm=(l == depth - 1))

    return h.reshape(batch, seq, d)
```

```python
import functools
import math

import numpy as np
import jax
import jax.numpy as jnp
from jax import lax
from jax.experimental import pallas as pl
from jax.experimental.pallas import tpu as pltpu

F32 = jnp.float32
BF16 = jnp.bfloat16
EPS = 1e-6
LOG2E = 1.4426950408889634

D_MODEL = 2048
ATTN_HEADS = 6
ATTN_QK_DIM = 64
ATTN_V_DIM = 128
ATTN_WIDTH = ATTN_HEADS * ATTN_V_DIM
REL_BUCKETS = 32
REL_MAX_DIST = 128
S5_GROUPS = 40
S5_GROUP_CH = 16
S5_STATE = 64
S5_WIDTH = S5_GROUPS * S5_GROUP_CH
S5_LANES = S5_GROUPS * S5_STATE
HGRN_HEADS = 5
HGRN_DIM = 128
HGRN_WIDTH = HGRN_HEADS * HGRN_DIM
N_GROUPS = 4
EXPERTS_PER_GROUP = 8
N_EXPERTS = N_GROUPS * EXPERTS_PER_GROUP
TOP_K = 2
D_EXPERT = 512

LANES = 128
SUBLANES = 8

IN_COLS = 3 * ATTN_WIDTH + S5_WIDTH + 4 * HGRN_WIDTH
PROJ_TN = 256
IN_COLS_PAD = -(-IN_COLS // PROJ_TN) * PROJ_TN
ATTN_COL_BLOCK = (IN_COLS_PAD - 3 * ATTN_WIDTH) // LANES

PROJ_TM = 2048
ATTN_TQ = 512
ATTN_TK = 512
ATTN_NEAR = 2
ATTN_ROWS = 32
ATTN_HPS = 1
S5_TB = 256
S5_CHUNK = 640
S5_BLK = 128
HGRN_C = 128
OUT_TM = 256
MOE_TM = 256
COMB_TM = 128
DMA_UNROLL = 8
WEIGHT_DMA_PRIORITY = 1

VMEM_LIMIT = 56 * 1024 * 1024


def _params(sem, vmem=None, flags=None):
    return pltpu.CompilerParams(dimension_semantics=sem, vmem_limit_bytes=vmem, flags=flags)


def _pack_bf16_pair(lo, hi):
    lo_bits = lax.bitcast_convert_type(lo.astype(BF16).astype(F32), jnp.uint32) >> 16
    hi_bits = lax.bitcast_convert_type(hi.astype(BF16).astype(F32), jnp.uint32) & jnp.uint32(0xFFFF0000)
    return lo_bits | hi_bits


def _unpack_bf16_pair(w):
    return (lax.bitcast_convert_type(w << 16, F32),
            lax.bitcast_convert_type(w & jnp.uint32(0xFFFF0000), F32))


def _in_proj_kernel(x_ref, g_ref, w_ref, wt_ref, o_ref, xn_ref, *, tail_tile):
    j = pl.program_id(1)

    @pl.when(j == 0)
    def _():
        x = x_ref[...]
        ms = jnp.mean(x * x, axis=-1, keepdims=True)
        xn_ref[...] = (x * lax.rsqrt(ms + EPS) * g_ref[...]).astype(BF16)

    w = jnp.where(j == tail_tile, wt_ref[...], w_ref[0])
    o_ref[...] = jnp.dot(xn_ref[...], w.astype(BF16), preferred_element_type=F32)


def _in_proj(h, g, w_all, layer):
    t, d = h.shape
    tm = min(PROJ_TM, t)
    tn = PROJ_TN
    a = 3 * ATTN_WIDTH
    rest = IN_COLS - a
    assert a % tn == 0 and IN_COLS_PAD - IN_COLS == tn - rest % tn
    rest_full, a_tiles = rest // tn, a // tn
    tail_tile = rest_full
    w_tail = lax.slice(w_all, (layer, 0, a + rest_full * tn), (layer + 1, d, IN_COLS))[0]
    w_tail = jnp.pad(w_tail, ((0, 0), (0, IN_COLS_PAD - IN_COLS)))

    def w_map(i, j):
        col = jnp.where(j < tail_tile, a_tiles + j, jnp.where(j == tail_tile, 0, j - tail_tile - 1))
        return (layer, 0, col)

    return pl.pallas_call(
        functools.partial(_in_proj_kernel, tail_tile=tail_tile),
        grid=(t // tm, IN_COLS_PAD // tn),
        in_specs=[
            pl.BlockSpec((tm, d), lambda i, j: (i, 0)),
            pl.BlockSpec((1, d), lambda i, j: (0, 0)),
            pl.BlockSpec((1, d, tn), w_map),
            pl.BlockSpec((d, tn), lambda i, j: (0, 0)),
        ],
        out_specs=pl.BlockSpec((tm, tn), lambda i, j: (i, j)),
        out_shape=jax.ShapeDtypeStruct((t, IN_COLS_PAD), F32),
        scratch_shapes=[pltpu.VMEM((tm, d), BF16)],
        compiler_params=_params(("parallel", "arbitrary"), VMEM_LIMIT),
        name="in_proj",
    )(h, g.reshape(1, d), w_all, w_tail)


def _attn_kernel(lam_ref, q_ref, k_ref, v_ref, bias_ref, g_ref, o_ref,
                 kb_ref, vb_ref, qz_ref, m_ref, a_ref, acc_ref, s0_ref, s1_ref, p_ref, *, tq, tk, out_scale):
    qi = pl.program_id(2)
    ratio = tq // tk
    heads = range(ATTN_HPS)

    @pl.when(qi == 0)
    def _():
        for h in heads:
            hl = slice(h * LANES, (h + 1) * LANES)
            kb_ref[h] = k_ref[:, hl].astype(BF16)
            vb_ref[h, :, 0:LANES] = v_ref[:, hl].astype(BF16)
            vb_ref[h, :, LANES:2 * LANES] = jnp.ones((vb_ref.shape[1], LANES), BF16)

    for h in heads:
        q = q_ref[:, h * LANES:(h + 1) * LANES] * (ATTN_QK_DIM ** -0.5 * LOG2E)
        lane = lax.broadcasted_iota(jnp.int32, q.shape, 1)
        qz_ref[h, 0:tq, :] = jnp.where(lane < ATTN_QK_DIM, q, 0.0).astype(BF16)
        qz_ref[h, tq:2 * tq, :] = jnp.where(lane >= ATTN_QK_DIM, q, 0.0).astype(BF16)
    m_ref[...] = jnp.full(m_ref.shape, -jnp.inf, F32)
    acc_ref[...] = jnp.zeros(acc_ref.shape, F32)

    n_col = tk // LANES
    groups = [slice(g * ATTN_ROWS, (g + 1) * ATTN_ROWS) for g in range(2 * tq // ATTN_ROWS)]

    def scores(j, s_ref):
        for h in heads:
            k = kb_ref[h, pl.ds(pl.multiple_of(j * tk, tk), tk), :]
            s_ref[h] = lax.dot_general(qz_ref[h], k, (((1,), (1,)), ((), ())), preferred_element_type=F32)

    def softmax_pv(j, s_ref, biased):
        if biased:
            kind = jnp.clip(j - ratio * qi + ATTN_NEAR, 0, ATTN_NEAR + ratio - 1)
        for h in heads:
            for rows in groups:
                cols = [s_ref[h, rows, i * LANES:(i + 1) * LANES] for i in range(n_col)]
                if biased:
                    b0 = rows.start % tq
                    cols = [x + bias_ref[h, kind, b0:b0 + ATTN_ROWS, i * LANES:(i + 1) * LANES]
                            for i, x in enumerate(cols)]
                    for i, x in enumerate(cols):
                        s_ref[h, rows, i * LANES:(i + 1) * LANES] = x
                mx = functools.reduce(jnp.maximum, cols)
                m_old = m_ref[h, rows, :]
                m_new = jnp.maximum(m_old, jnp.max(mx, axis=-1, keepdims=True))
                m_ref[h, rows, :] = m_new
                a_ref[h, rows, :] = jnp.exp2(m_old - m_new)
        for h in heads:
            for rows in groups:
                m_new = m_ref[h, rows, :]
                for i in range(n_col):
                    p_ref[h, rows, i * LANES:(i + 1) * LANES] = jnp.exp2(
                        s_ref[h, rows, i * LANES:(i + 1) * LANES] - m_new).astype(BF16)
        for h in heads:
            v = vb_ref[h, pl.ds(pl.multiple_of(j * tk, tk), tk), :]
            alpha = a_ref[h]
            acc_ref[h] = (jnp.concatenate([alpha, alpha], axis=1) * acc_ref[h]
                          + jnp.dot(p_ref[h], v, preferred_element_type=F32))

    n_tiles = ratio * (qi + 1)
    first = lax.rem(n_tiles, 2)
    n_far = jnp.maximum(ratio * qi - ATTN_NEAR + 1, 0)
    n_far_pairs = jnp.maximum(n_far - first, 0) // 2
    n_pairs = (n_tiles - first) // 2

    @pl.when(first == 1)
    def _():
        scores(0, s1_ref)
        scores(jnp.minimum(1, n_tiles - 1), s0_ref)
        softmax_pv(0, s1_ref, True)

    @pl.when(first == 0)
    def _():
        scores(0, s0_ref)

    def pair_body(jj, carry, biased):
        j = first + 2 * jj
        scores(j + 1, s1_ref)
        softmax_pv(j, s0_ref, biased)
        scores(jnp.minimum(j + 2, n_tiles - 1), s0_ref)
        softmax_pv(j + 1, s1_ref, biased)
        return carry

    lax.fori_loop(0, n_far_pairs, functools.partial(pair_body, biased=False), 0)
    lax.fori_loop(n_far_pairs, n_pairs, functools.partial(pair_body, biased=True), 0)

    for h in heads:
        inv_l = 1.0 / acc_ref[h, :, LANES:2 * LANES]
        o1 = acc_ref[h, 0:tq, 0:LANES] * inv_l[0:tq]
        o2 = acc_ref[h, tq:2 * tq, 0:LANES] * inv_l[tq:2 * tq]
        o = o1 - lam_ref[0] * o2
        ms = jnp.mean(o * o, axis=-1, keepdims=True)
        o_ref[:, h * LANES:(h + 1) * LANES] = (o * lax.rsqrt(ms + EPS) * g_ref[...] * out_scale).astype(o_ref.dtype)


def _t5_bucket(dist):
    n = jnp.maximum(dist, 0)
    max_exact = REL_BUCKETS // 2
    large = max_exact + (jnp.log(jnp.maximum(n, 1).astype(F32) / max_exact)
                         / math.log(REL_MAX_DIST / max_exact) * (REL_BUCKETS - max_exact)).astype(jnp.int32)
    large = jnp.minimum(large, REL_BUCKETS - 1)
    return jnp.where(n < max_exact, n, large)


def _attn_bias_tiles(rel_bias, tq, tk):
    table = rel_bias.astype(F32)
    far = table[REL_BUCKETS - 1]
    n = tq + tk
    m = jnp.arange(n)

    def lookup(dist):
        bucket = _t5_bucket(dist)
        out = jnp.zeros((ATTN_HEADS,) + dist.shape, F32)
        for b in range(REL_BUCKETS):
            out = jnp.where((bucket == b)[None], table[b][:, None], out)
        return out

    vecs = [jnp.zeros((ATTN_HEADS, n), F32)]
    for rel in range(-ATTN_NEAR + 1, tq // tk):
        dist = tq - m - rel * tk
        vecs.append(jnp.where((dist >= 0)[None], (lookup(dist) - far[:, None]) * LOG2E, -1e30))
    vecs = jnp.stack(vecs, axis=1)[:, :, None, :]
    kinds = vecs.shape[1]
    return pl.pallas_call(
        functools.partial(_bias_tile_kernel, tq=tq, tk=tk),
        grid=(ATTN_HEADS, kinds),
        in_specs=[pl.BlockSpec((1, 1, 1, n), lambda h, k: (h, k, 0, 0))],
        out_specs=pl.BlockSpec((1, 1, tq, tk), lambda h, k: (h, k, 0, 0)),
        out_shape=jax.ShapeDtypeStruct((ATTN_HEADS, kinds, tq, tk), F32),
        compiler_params=_params(("parallel", "parallel")),
        name="attn_bias_tiles",
    )(vecs)


def _bias_tile_kernel(v_ref, o_ref, *, tq, tk):
    x = jnp.broadcast_to(v_ref[0, 0], (tq, tq + tk))
    o_ref[0, 0] = pltpu.roll(x, 0, 1, stride=1, stride_axis=0)[:, tq:]


def _diff_attention(proj, bias, lam, subln_g, lam_init, batch, seq):
    t = proj.shape[0]
    tq = min(ATTN_TQ, seq)
    tk = min(ATTN_TK, seq)
    ratio = tq // tk
    hps = ATTN_HPS
    assert (ATTN_NEAR - 1) * tk >= REL_MAX_DIST and ratio * tk == tq
    assert ATTN_HEADS % hps == 0 and ATTN_COL_BLOCK % hps == 0
    nq = seq // tq
    kernel = functools.partial(_attn_kernel, tq=tq, tk=tk, out_scale=1.0 - lam_init)
    qb, kb, vb = (ATTN_COL_BLOCK // hps, (ATTN_COL_BLOCK + ATTN_HEADS) // hps,
                  (ATTN_COL_BLOCK + 2 * ATTN_HEADS) // hps)
    w = hps * LANES
    return pl.pallas_call(
        kernel,
        grid=(batch, ATTN_HEADS // hps, nq),
        in_specs=[
            pl.BlockSpec(memory_space=pltpu.SMEM),
            pl.BlockSpec((tq, w), lambda b, h, i: (b * nq + i, qb + h)),
            pl.BlockSpec((seq, w), lambda b, h, i: (b, kb + h)),
            pl.BlockSpec((seq, w), lambda b, h, i: (b, vb + h)),
            pl.BlockSpec((hps,) + bias.shape[1:], lambda b, h, i: (h, 0, 0, 0)),
            pl.BlockSpec((1, LANES), lambda b, h, i: (0, 0)),
        ],
        out_specs=pl.BlockSpec((tq, w), lambda b, h, i: (b * nq + i, h)),
        out_shape=jax.ShapeDtypeStruct((t, ATTN_WIDTH), BF16),
        scratch_shapes=[
            pltpu.VMEM((hps, seq, LANES), BF16),
            pltpu.VMEM((hps, seq, 2 * LANES), BF16),
            pltpu.VMEM((hps, 2 * tq, LANES), BF16),
            pltpu.VMEM((hps, 2 * tq, LANES), F32),
            pltpu.VMEM((hps, 2 * tq, LANES), F32),
            pltpu.VMEM((hps, 2 * tq, 2 * LANES), F32),
            pltpu.VMEM((hps, 2 * tq, tk), F32),
            pltpu.VMEM((hps, 2 * tq, tk), F32),
            pltpu.VMEM((hps, 2 * tq, tk), BF16),
        ],
        compiler_params=_params(("parallel", "parallel", "arbitrary"), VMEM_LIMIT),
        name="diff_attn",
    )(lam.reshape(1), proj, proj, proj, bias, subln_g.reshape(1, LANES))


def _s5_kernel(u_ref, perm_ref, permt_ref, wb_ref, wc_ref, pw_ref, d_ref, wglu_ref, o_ref,
               xr_ref, xi_ref, cr_ref, ci_ref, kr_ref, ki_ref, *, tb):
    @pl.when(pl.program_id(1) == 0)
    def _():
        cr_ref[...] = jnp.zeros(cr_ref.shape, F32)
        ci_ref[...] = jnp.zeros(ci_ref.shape, F32)

    un = u_ref[...]
    uh = un.astype(BF16)
    ul = (un - uh.astype(F32)).astype(BF16)
    up = jnp.dot(perm_ref[...], jnp.concatenate([uh, ul], axis=1), preferred_element_type=F32)
    ub = up[:, :S5_WIDTH].astype(BF16)
    u = up[:, :S5_WIDTH] + up[:, S5_WIDTH:]
    nblk = S5_WIDTH // S5_BLK
    sw = S5_LANES // nblk
    tpb = sw // LANES
    for i in range(nblk):
        bu = jnp.dot(ub[:, i * S5_BLK:(i + 1) * S5_BLK], wb_ref[i], preferred_element_type=F32)
        for k in range(tpb):
            xr_ref[i * tpb + k] = bu[:, k * LANES:(k + 1) * LANES]
            xi_ref[i * tpb + k] = bu[:, sw + k * LANES:sw + (k + 1) * LANES]

    sl = tb // SUBLANES
    for c in range(S5_LANES // S5_CHUNK):
        tiles = range(c * S5_CHUNK // LANES, (c + 1) * S5_CHUNK // LANES)
        lss = [slice(n * LANES, (n + 1) * LANES) for n in tiles]
        a = [(pw_ref[0, 0, :, ls], pw_ref[1, 0, :, ls]) for ls in lss]

        def local_step(t, carry, tiles=tiles, a=a):
            rows = pl.ds(pl.multiple_of(t * SUBLANES, SUBLANES), SUBLANES)
            out = []
            for n, (ar, ai), (pr, pi) in zip(tiles, a, carry):
                xr = ar * pr - ai * pi + xr_ref[n, rows, :]
                xi = ar * pi + ai * pr + xi_ref[n, rows, :]
                xr_ref[n, rows, :] = xr
                xi_ref[n, rows, :] = xi
                out.append((xr, xi))
            return tuple(out)

        zero = jnp.zeros((SUBLANES, LANES), F32)
        ends = lax.fori_loop(0, sl, local_step, tuple((zero, zero) for _ in tiles), unroll=2)

        for ls, (er, ei) in zip(lss, ends):
            lr, li = pw_ref[0, sl - 1, 0:1, ls], pw_ref[1, sl - 1, 0:1, ls]
            kr, ki = cr_ref[:, ls], ci_ref[:, ls]
            for sub in range(SUBLANES):
                kr_ref[sub:sub + 1, ls] = kr
                ki_ref[sub:sub + 1, ls] = ki
                kr, ki = (er[sub:sub + 1, :] + lr * kr - li * ki, ei[sub:sub + 1, :] + lr * ki + li * kr)
            cr_ref[:, ls] = kr
            ci_ref[:, ls] = ki

        k8 = [(kr_ref[:, ls], ki_ref[:, ls]) for ls in lss]

        def fix_step(t, carry, tiles=tiles, lss=lss, k8=k8):
            rows = pl.ds(pl.multiple_of(t * SUBLANES, SUBLANES), SUBLANES)
            for n, ls, (kr8, ki8) in zip(tiles, lss, k8):
                pr = pw_ref[0, t, :, ls]
                pi = pw_ref[1, t, :, ls]
                xr_ref[n, rows, :] = xr_ref[n, rows, :] + pr * kr8 - pi * ki8
                xi_ref[n, rows, :] = xi_ref[n, rows, :] + pr * ki8 + pi * kr8
            return carry

        lax.fori_loop(0, sl, fix_step, 0, unroll=2)

    ys = []
    for i in range(nblk):
        xc = jnp.concatenate([xr_ref[i * tpb + k] for k in range(tpb)]
                             + [xi_ref[i * tpb + k] for k in range(tpb)], axis=1)
        ys.append(jnp.dot(xc.astype(BF16), wc_ref[i], preferred_element_type=F32))
    y = jnp.concatenate(ys, axis=1) + d_ref[...] * u
    z = 0.5 * y * (1.0 + jnp.tanh(math.sqrt(2.0 / math.pi) * (y + 0.044715 * (y * y * y))))
    gate = jax.nn.sigmoid(jnp.dot(z.astype(BF16), wglu_ref[...].astype(BF16), preferred_element_type=F32))
    o_ref[...] = jnp.dot(permt_ref[...], (z * gate).astype(BF16),
                         preferred_element_type=F32).astype(o_ref.dtype)


def _s5_prep(a_re, a_im, log_dt, b_re, b_im, c_re, c_im, steps):
    lr = jnp.minimum(a_re.astype(F32), -1e-4)
    li = a_im.astype(F32)
    dt = jnp.exp(log_dt.astype(F32))[:, None]
    mag = jnp.exp(lr * dt)
    ab_re = mag * jnp.cos(li * dt)
    ab_im = mag * jnp.sin(li * dt)
    den = lr * lr + li * li
    nr = ab_re - 1.0
    ni = ab_im
    f_re = (nr * lr + ni * li) / den
    f_im = (ni * lr - nr * li) / den
    br = b_re.astype(F32)
    bi = b_im.astype(F32)
    bb_re = f_re[..., None] * br - f_im[..., None] * bi
    bb_im = f_re[..., None] * bi + f_im[..., None] * br

    nblk = S5_WIDTH // S5_BLK
    gpb = S5_GROUPS // nblk
    eye = jnp.eye(gpb, dtype=F32)

    def bmat(bb):
        bb = bb.reshape(nblk, gpb, S5_STATE, S5_GROUP_CH)
        m = jnp.einsum('ignc,gh->igchn', bb, eye)
        return m.reshape(nblk, gpb * S5_GROUP_CH, gpb * S5_STATE)

    def cmat(cc):
        cc = cc.astype(F32).reshape(nblk, gpb, S5_GROUP_CH, S5_STATE)
        m = jnp.einsum('igcn,gh->ignhc', cc, eye)
        return m.reshape(nblk, gpb * S5_STATE, gpb * S5_GROUP_CH)

    w_b = jnp.concatenate([bmat(bb_re), bmat(bb_im)], axis=2).astype(BF16)
    w_c = jnp.concatenate([cmat(c_re), -cmat(c_im)], axis=1).astype(BF16)

    j = jnp.arange(1, steps + 1, dtype=F32)[:, None, None]
    m = jnp.exp(lr * dt * j)
    pows = jnp.stack([(m * jnp.cos(li * dt * j)).reshape(steps, 1, -1),
                      (m * jnp.sin(li * dt * j)).reshape(steps, 1, -1)])
    pows = jnp.broadcast_to(pows, (2, steps, SUBLANES, S5_LANES))
    return w_b, w_c, pows


def _s5_mixer(proj, w_b, w_c, tabs, d_skip, w_glu, batch, seq):
    t = proj.shape[0]
    tb = min(S5_TB, seq)
    nt = seq // tb
    sl = tb // SUBLANES
    r = np.arange(tb)
    perm = np.zeros((tb, tb), np.float32)
    perm[r, (r % SUBLANES) * sl + r // SUBLANES] = 1.0
    perm_t = jnp.asarray(perm.T, BF16)
    perm = jnp.asarray(perm, BF16)
    return pl.pallas_call(
        functools.partial(_s5_kernel, tb=tb),
        grid=(batch, nt),
        in_specs=[
            pl.BlockSpec((tb, S5_WIDTH), lambda b, i: (b * nt + i, 0)),
            pl.BlockSpec((tb, tb), lambda b, i: (0, 0)),
            pl.BlockSpec((tb, tb), lambda b, i: (0, 0)),
            pl.BlockSpec(w_b.shape, lambda b, i: (0, 0, 0)),
            pl.BlockSpec(w_c.shape, lambda b, i: (0, 0, 0)),
            pl.BlockSpec(tabs.shape, lambda b, i: (0, 0, 0, 0)),
            pl.BlockSpec((1, S5_WIDTH), lambda b, i: (0, 0)),
            pl.BlockSpec((S5_WIDTH, S5_WIDTH), lambda b, i: (0, 0)),
        ],
        out_specs=pl.BlockSpec((tb, S5_WIDTH), lambda b, i: (b * nt + i, 0)),
        out_shape=jax.ShapeDtypeStruct((t, S5_WIDTH), BF16),
        scratch_shapes=[
            pltpu.VMEM((S5_LANES // LANES, tb, LANES), F32),
            pltpu.VMEM((S5_LANES // LANES, tb, LANES), F32),
            pltpu.VMEM((1, S5_LANES), F32),
            pltpu.VMEM((1, S5_LANES), F32),
            pltpu.VMEM((SUBLANES, S5_LANES), F32),
            pltpu.VMEM((SUBLANES, S5_LANES), F32),
        ],
        compiler_params=_params(("parallel", "arbitrary"), VMEM_LIMIT),
        name="s5_mixer",
    )(proj, perm, perm_t, w_b, w_c, tabs, d_skip.reshape(1, S5_WIDTH), w_glu)


def _hgrn_consts(c):
    t = np.arange(c)[:, None]
    u = np.arange(c)[None, :]
    mats = [u <= t]
    masks = [np.eye(c, dtype=bool)]
    h = 1
    while h < c:
        mid = (t // (2 * h)) * 2 * h + h
        if 2 * h < SUBLANES:
            mats.append(np.where(t >= mid, (u >= mid) & (u <= t), (u > t) & (u < mid)))
        masks.append(((u // (2 * h)) == (t // (2 * h))) & (t >= mid) & (u < mid))
        h *= 2
    return (np.concatenate(mats, axis=0).astype(np.float32), np.stack(masks).astype(np.float32))


def _hgrn_kernel(q_ref, f_ref, i_ref, g_ref, lb_ref, ng_ref, mall_ref, mask_ref, o_ref,
                 e_ref, st_ref, *, c):
    @pl.when(pl.program_id(1) == 0)
    def _():
        st_ref[...] = jnp.zeros(st_ref.shape, F32)

    lb = lb_ref[...]
    fl = f_ref[...]
    la = jnp.log(lb)
    lbb = jnp.log1p(-lb) + (jnp.minimum(fl, 0.0) - jnp.log1p(jnp.exp(-jnp.abs(fl))))
    logf = jnp.maximum(la, lbb) + jnp.log1p(jnp.exp(-jnp.abs(la - lbb)))
    hi = logf.astype(BF16)
    lo = (logf - hi.astype(F32)).astype(BF16)
    e2 = jnp.dot(mall_ref[...], jnp.concatenate([hi, lo], axis=1), preferred_element_type=F32)
    es = e2[:, :HGRN_WIDTH] + e2[:, HGRN_WIDTH:]
    n_levels = mask_ref.shape[0] - 1
    n_small = mall_ref.shape[0] // c - 1
    gall = es[0:c]
    e_ref[0:c, :] = gall
    e_ref[c:2 * c, :] = gall[c - 1:c, :] - gall
    e_ref[2 * c:(2 + n_small) * c, :] = es[c:]
    for lv in range(n_small, n_levels):
        blk = 2 << lv
        g3 = gall.reshape(c // blk, blk, HGRN_WIDTH)
        gm = jnp.broadcast_to(g3[:, blk // 2 - 1:blk // 2, :], g3.shape)
        e_ref[(2 + lv) * c:(3 + lv) * c, :] = (-jnp.abs(g3 - gm)).reshape(c, HGRN_WIDTH)

    nt = (((1,), (1,)), ((), ()))
    for hd in range(HGRN_HEADS):
        ls = slice(hd * HGRN_DIM, (hd + 1) * HGRN_DIM)
        qr = q_ref[:, ls]
        q = qr * jax.nn.sigmoid(qr) * (HGRN_DIM ** -0.5)
        k = (1.0 - lb[:, ls]) * jax.nn.sigmoid(-fl[:, ls])
        vb = i_ref[:, ls].astype(BF16)
        p = lax.dot_general(q.astype(BF16), k.astype(BF16), nt, preferred_element_type=F32) * mask_ref[0]
        for lv in range(n_levels):
            fac = jnp.exp(e_ref[(2 + lv) * c:(3 + lv) * c, ls])
            s = lax.dot_general((q * fac).astype(BF16), (k * fac).astype(BF16), nt,
                                preferred_element_type=F32)
            p = p + s * mask_ref[lv + 1]
        o = jnp.dot(p.astype(BF16), vb, preferred_element_type=F32)
        gcum = e_ref[0:c, ls]
        st = st_ref[hd]
        o = o + lax.dot_general((q * jnp.exp(gcum)).astype(BF16), st.astype(BF16), nt,
                                preferred_element_type=F32)
        kd = (k * jnp.exp(e_ref[c:2 * c, ls])).astype(BF16)
        st_ref[hd] = st * jnp.exp(gcum[c - 1:c, :]) + lax.dot_general(
            vb, kd, (((0,), (0,)), ((), ())), preferred_element_type=F32)
        ms = jnp.mean(o * o, axis=-1, keepdims=True)
        o = o * lax.rsqrt(ms + EPS) * ng_ref[...] * jax.nn.sigmoid(g_ref[:, ls])
        o_ref[:, ls] = o.astype(o_ref.dtype)


def _hgrn_mixer(proj, lb, norm_g, batch, seq):
    t = proj.shape[0]
    c = min(HGRN_C, seq)
    nc = seq // c
    m_all, masks = _hgrn_consts(c)
    m_all = jnp.asarray(m_all, BF16)
    masks = jnp.asarray(masks, F32)

    def col(j):
        return pl.BlockSpec((c, HGRN_WIDTH), lambda b, i: (b * nc + i, j))

    return pl.pallas_call(
        functools.partial(_hgrn_kernel, c=c),
        grid=(batch, nc),
        in_specs=[
            col(1), col(2), col(3), col(4),
            pl.BlockSpec((1, HGRN_WIDTH), lambda b, i: (0, 0)),
            pl.BlockSpec((1, HGRN_DIM), lambda b, i: (0, 0)),
            pl.BlockSpec(m_all.shape, lambda b, i: (0, 0)),
            pl.BlockSpec(masks.shape, lambda b, i: (0, 0, 0)),
        ],
        out_specs=pl.BlockSpec((c, HGRN_WIDTH), lambda b, i: (b * nc + i, 0)),
        out_shape=jax.ShapeDtypeStruct((t, HGRN_WIDTH), BF16),
        scratch_shapes=[
            pltpu.VMEM(((masks.shape[0] + 1) * c, HGRN_WIDTH), F32),
            pltpu.VMEM((HGRN_HEADS, HGRN_DIM, HGRN_DIM), F32),
        ],
        compiler_params=_params(("parallel", "arbitrary"), VMEM_LIMIT),
        name="hgrn2_mixer",
    )(proj, proj, proj, proj, lb.reshape(1, HGRN_WIDTH), norm_g.reshape(1, HGRN_DIM), m_all, masks)


def _out_proj_kernel(h_ref, a_ref, s_ref, r_ref, w_ref, gn_ref, rw_ref,
                     ho_ref, hn_ref, eid_ref, gate_ref):
    s_lo, r_lo = ATTN_WIDTH, ATTN_WIDTH + S5_WIDTH
    acc = jnp.dot(a_ref[...], w_ref[0:s_lo, :], preferred_element_type=F32)
    acc = acc + jnp.dot(s_ref[...], w_ref[s_lo:r_lo, :], preferred_element_type=F32)
    acc = acc + jnp.dot(r_ref[...], w_ref[r_lo:, :], preferred_element_type=F32)
    h = h_ref[...] + acc
    ho_ref[...] = h
    ms = jnp.mean(h * h, axis=-1, keepdims=True)
    hn = h * lax.rsqrt(ms + EPS) * gn_ref[...]
    half = hn.shape[1] // 2
    hn_ref[...] = _pack_bf16_pair(hn[:, :half], hn[:, half:])

    hh = hn.astype(BF16)
    hl = (hn - hh.astype(F32)).astype(BF16)
    both = jnp.dot(hh, rw_ref[...], preferred_element_type=F32)
    logits = (both[:, :LANES] + both[:, LANES:]
              + jnp.dot(hl, rw_ref[:, :LANES], preferred_element_type=F32))
    lane = lax.broadcasted_iota(jnp.int32, logits.shape, 1).astype(F32)
    neg = -jnp.inf
    big = 1e9
    is_group = jnp.where(lane >= N_EXPERTS, jnp.where(lane < N_EXPERTS + N_GROUPS, 1.0, 0.0), 0.0)
    gl = jnp.where(is_group > 0, logits, neg)
    gmax = jnp.max(gl, axis=-1, keepdims=True)
    g_lane = jnp.min(jnp.where(gl == gmax, lane, big), axis=-1, keepdims=True)
    p_g = 1.0 / jnp.sum(jnp.exp(gl - gmax), axis=-1, keepdims=True)
    lo_lane = (g_lane - N_EXPERTS) * EXPERTS_PER_GROUP
    in_group = jnp.where(lane >= lo_lane, jnp.where(lane < lo_lane + EXPERTS_PER_GROUP, 1.0, 0.0), 0.0)
    el = jnp.where(in_group > 0, logits, neg)
    t1 = jnp.max(el, axis=-1, keepdims=True)
    i1 = jnp.min(jnp.where(el == t1, lane, big), axis=-1, keepdims=True)
    el2 = jnp.where(lane == i1, neg, el)
    t2 = jnp.max(el2, axis=-1, keepdims=True)
    i2 = jnp.min(jnp.where(el2 == t2, lane, big), axis=-1, keepdims=True)
    e21 = jnp.exp(t2 - t1)
    g1 = p_g / (1.0 + e21)
    g2 = p_g * e21 / (1.0 + e21)
    eid_ref[...] = jnp.where(lane == 0, i1, jnp.where(lane == 1, i2, 0.0)).astype(jnp.int32)
    gate_ref[...] = jnp.where(lane == 0, g1, jnp.where(lane == 1, g2, 0.0))


def _out_proj(h, attn, s5, hg, w_out, gn, r_w):
    t, d = h.shape
    tm = min(OUT_TM, t)

    def rows(w):
        return pl.BlockSpec((tm, w), lambda i: (i, 0))

    def full(a):
        return pl.BlockSpec(a.shape, lambda i: (0,) * a.ndim)

    return pl.pallas_call(
        _out_proj_kernel,
        grid=(t // tm,),
        in_specs=[rows(d), rows(ATTN_WIDTH), rows(S5_WIDTH), rows(HGRN_WIDTH),
                  full(w_out), pl.BlockSpec((1, d), lambda i: (0, 0)),
                  full(r_w)],
        out_specs=[rows(d), rows(d // 2), rows(LANES), rows(LANES)],
        out_shape=[jax.ShapeDtypeStruct((t, d), F32), jax.ShapeDtypeStruct((t, d // 2), jnp.uint32),
                   jax.ShapeDtypeStruct((t, LANES), jnp.int32), jax.ShapeDtypeStruct((t, LANES), F32)],
        compiler_params=_params(("parallel",), VMEM_LIMIT),
        name="out_proj_router",
    )(h, attn, s5, hg, w_out, gn.reshape(1, d), r_w)


def _moe_plan(eid, tm):
    flat = eid.reshape(-1)
    n_slots = flat.shape[0]
    onehot = (flat[:, None] == jnp.arange(N_EXPERTS, dtype=jnp.int32)[None, :]).astype(jnp.int32)
    csum = jnp.cumsum(onehot, axis=0)
    rank = jnp.sum(onehot * csum, axis=1) - 1
    counts = csum[-1]
    tiles = (counts + tm - 1) // tm
    tile_end = jnp.cumsum(tiles)
    tile_start = tile_end - tiles
    slot_start = jnp.cumsum(counts) - counts
    pos = jnp.sum(onehot * tile_start[None, :], axis=1) * tm + rank
    tok_sorted = jnp.argsort(flat, stable=True).astype(jnp.int32) // TOP_K
    tok_sorted = jnp.pad(tok_sorted, (0, tm))
    nt_max = (n_slots + N_EXPERTS * (tm - 1)) // tm
    n_used = tile_end[-1]
    j = jnp.minimum(jnp.arange(nt_max, dtype=jnp.int32), n_used - 1)
    tile_expert = jnp.sum((j[:, None] >= tile_end[None, :]).astype(jnp.int32), axis=1)
    onehot_t = (tile_expert[:, None] == jnp.arange(N_EXPERTS, dtype=jnp.int32)[None, :]).astype(jnp.int32)
    tile_slot0 = jnp.sum(onehot_t * (slot_start - tile_start * tm)[None, :], axis=1) + j * tm
    used = (tiles > 0).astype(jnp.int32)
    ordinal = jnp.cumsum(used) - used
    ids = jnp.arange(N_EXPERTS, dtype=jnp.int32)
    later = jnp.where((ids[None, :] > ids[:, None]) & (used[None, :] > 0), ids[None, :], N_EXPERTS)
    nxt = jnp.min(later, axis=1)
    nxt = jnp.where(nxt >= N_EXPERTS, -1, nxt)
    tile_wslot = jnp.sum(onehot_t * (ordinal % 2)[None, :], axis=1)
    tile_next = jnp.sum(onehot_t * nxt[None, :], axis=1)
    return dict(pos=pos.astype(jnp.int32), tok_sorted=tok_sorted, tile_expert=tile_expert.astype(jnp.int32),
                tile_slot0=tile_slot0.astype(jnp.int32), tile_wslot=tile_wslot.astype(jnp.int32),
                tile_next=tile_next.astype(jnp.int32), n_used=n_used.reshape(1).astype(jnp.int32),
                nt_max=nt_max)


def _expert_kernel(tok_ref, te_ref, s0_ref, ws_ref, ne_ref, nu_ref, hn_ref, wg_ref, wu_ref, wd_ref, y_ref,
                   xa_ref, xb_ref, wgf_ref, wuf_ref, wdf_ref, wgb_ref, wub_ref, wdb_ref, sem, wsem,
                   *, tm, layer):
    j = pl.program_id(0)
    n_used = nu_ref[0]
    bufs = (xa_ref, xb_ref)

    def weight_copies(expert, slot):
        return [pltpu.make_async_copy(src.at[layer, expert], dst.at[slot], wsem.at[slot])
                for src, dst in ((wg_ref, wgf_ref), (wu_ref, wuf_ref), (wd_ref, wdf_ref))]

    def row_copy(buf, r, tok):
        return pltpu.make_async_copy(hn_ref.at[pl.ds(tok, 1)], bufs[buf].at[pl.ds(r, 1)], sem.at[buf])

    def wait_gather(buf):
        def wait(r, carry):
            row_copy(buf, r, 0).wait()
            return carry

        lax.fori_loop(0, tm, wait, 0, unroll=DMA_UNROLL)

    @pl.when(j == 0)
    def _():
        for cp in weight_copies(te_ref[0], ws_ref[0]):
            cp.start(priority=WEIGHT_DMA_PRIORITY)
        base = s0_ref[0]

        def issue(r, carry):
            row_copy(0, r, tok_ref[base + r]).start()
            return carry

        lax.fori_loop(0, tm, issue, 0, unroll=DMA_UNROLL)

    def tile(buf):
        wait_gather(buf)
        prev = te_ref[jnp.maximum(j - 1, 0)]

        @pl.when(jnp.logical_or(j == 0, te_ref[j] != prev))
        def _():
            slot = ws_ref[j]

            @pl.when(ne_ref[j] >= 0)
            def _():
                for cp in weight_copies(ne_ref[j], 1 - slot):
                    cp.start(priority=WEIGHT_DMA_PRIORITY)

            for cp in weight_copies(te_ref[j], slot):
                cp.wait()
            wgb_ref[...] = wgf_ref[slot].astype(BF16)
            wub_ref[...] = wuf_ref[slot].astype(BF16)
            wdb_ref[...] = wdf_ref[slot].astype(BF16)

        nbase = s0_ref[jnp.minimum(j + 1, n_used - 1)]
        for r in range(tm):
            row_copy(1 - buf, r, tok_ref[nbase + r]).start()

        x_lo, x_hi = _unpack_bf16_pair(bufs[buf][...])
        x = jnp.concatenate([x_lo.astype(BF16), x_hi.astype(BF16)], axis=1)
        g = jnp.dot(x, wgb_ref[...], preferred_element_type=F32)
        u = jnp.dot(x, wub_ref[...], preferred_element_type=F32)
        hmid = (g * jax.nn.sigmoid(g) * u).astype(BF16)
        y = jnp.dot(hmid, wdb_ref[...], preferred_element_type=F32)
        half = y.shape[1] // 2
        y_ref[...] = _pack_bf16_pair(y[:, :half], y[:, half:])

        @pl.when(j == n_used - 1)
        def _():
            wait_gather(1 - buf)

    for parity in range(2):
        @pl.when(jnp.logical_and(j < n_used, lax.rem(j, 2) == parity))
        def _(parity=parity):
            tile(parity)

    @pl.when(j >= n_used)
    def _():
        y_ref[...] = jnp.zeros(y_ref.shape, y_ref.dtype)


def _moe_experts(hn, plan, w_gate, w_up, w_down, layer, tm):
    t, dp = hn.shape
    d, de = w_gate.shape[2], w_gate.shape[3]
    nt_max = plan["nt_max"]
    grid_spec = pltpu.PrefetchScalarGridSpec(
        num_scalar_prefetch=6,
        grid=(nt_max,),
        in_specs=[pl.BlockSpec(memory_space=pl.ANY)] * 4,
        out_specs=pl.BlockSpec((tm, dp), lambda j, *_: (j, 0)),
        scratch_shapes=[pltpu.VMEM((tm, dp), jnp.uint32), pltpu.VMEM((tm, dp), jnp.uint32),
                        pltpu.VMEM((2, d, de), F32), pltpu.VMEM((2, d, de), F32), pltpu.VMEM((2, de, d), F32),
                        pltpu.VMEM((d, de), BF16), pltpu.VMEM((d, de), BF16), pltpu.VMEM((de, d), BF16),
                        pltpu.SemaphoreType.DMA((2,)), pltpu.SemaphoreType.DMA((2,))],
    )
    return pl.pallas_call(
        functools.partial(_expert_kernel, tm=tm, layer=layer),
        grid_spec=grid_spec,
        out_shape=jax.ShapeDtypeStruct((nt_max * tm, dp), jnp.uint32),
        compiler_params=_params(("arbitrary",), VMEM_LIMIT),
        name="moe_experts",
    )(plan["tok_sorted"], plan["tile_expert"], plan["tile_slot0"], plan["tile_wslot"], plan["tile_next"],
      plan["n_used"], hn, w_gate, w_up, w_down)


def _combine_kernel(pos_ref, h_ref, gate_ref, gn_ref, y_ref, o_ref, ba_ref, bb_ref, sem, *, tm, final_norm):
    i = pl.program_id(0)
    n = pl.num_programs(0)
    bufs = (ba_ref, bb_ref)

    def row_copy(b, k, r, src_row):
        return pltpu.make_async_copy(y_ref.at[pl.ds(src_row, 1)], bufs[b].at[k, pl.ds(r, 1)], sem.at[b])

    def wait_gather(b):
        def wait(r, carry):
            for k in range(TOP_K):
                row_copy(b, k, r, 0).wait()
            return carry

        lax.fori_loop(0, tm, wait, 0, unroll=DMA_UNROLL)

    @pl.when(i == 0)
    def _():
        def issue(r, carry):
            for k in range(TOP_K):
                row_copy(0, k, r, pos_ref[r * TOP_K + k]).start()
            return carry

        lax.fori_loop(0, tm, issue, 0, unroll=DMA_UNROLL)

    def tile(b):
        wait_gather(b)
        base = jnp.minimum(i + 1, n - 1) * (tm * TOP_K)
        for r in range(tm):
            for k in range(TOP_K):
                row_copy(1 - b, k, r, pos_ref[base + r * TOP_K + k]).start(priority=(r * TOP_K + k) % 2)

        gates = gate_ref[...]
        y0_lo, y0_hi = _unpack_bf16_pair(bufs[b][0])
        y1_lo, y1_hi = _unpack_bf16_pair(bufs[b][1])
        g0, g1 = gates[:, 0:1], gates[:, 1:2]
        h = h_ref[...] + jnp.concatenate([g0 * y0_lo + g1 * y1_lo, g0 * y0_hi + g1 * y1_hi], axis=1)
        if final_norm:
            ms = jnp.mean(h * h, axis=-1, keepdims=True)
            h = h * lax.rsqrt(ms + EPS) * gn_ref[...]
        o_ref[...] = h

        @pl.when(i == n - 1)
        def _():
            wait_gather(1 - b)

    for parity in range(2):
        @pl.when(lax.rem(i, 2) == parity)
        def _(parity=parity):
            tile(parity)


def _moe_combine(h, gates, ys, pos, final_g, final_norm):
    t, d = h.shape
    tm = min(COMB_TM, t)
    grid_spec = pltpu.PrefetchScalarGridSpec(
        num_scalar_prefetch=1,
        grid=(t // tm,),
        in_specs=[
            pl.BlockSpec((tm, d), lambda i, p: (i, 0)),
            pl.BlockSpec((tm, LANES), lambda i, p: (i, 0)),
            pl.BlockSpec((1, d), lambda i, p: (0, 0)),
            pl.BlockSpec(memory_space=pl.ANY),
        ],
        out_specs=pl.BlockSpec((tm, d), lambda i, p: (i, 0)),
        scratch_shapes=[pltpu.VMEM((TOP_K, tm, d // 2), jnp.uint32), pltpu.VMEM((TOP_K, tm, d // 2), jnp.uint32),
                        pltpu.SemaphoreType.DMA((2,))],
    )
    return pl.pallas_call(
        functools.partial(_combine_kernel, tm=tm, final_norm=final_norm),
        grid_spec=grid_spec,
        out_shape=jax.ShapeDtypeStruct((t, d), F32),
        compiler_params=_params(("arbitrary",), VMEM_LIMIT),
        name="moe_combine",
    )(pos, h, gates, final_g.reshape(1, d), ys)


def kernel(x, w_in, w_out, mix_norm_g, ffn_norm_g, rel_bias, diff_lambda, attn_subln_g, s5_a_re, s5_a_im, s5_log_dt, s5_b_re, s5_b_im, s5_c_re, s5_c_im, s5_d, s5_w_glu, hgrn_lb_logits, hgrn_norm_g, moe_w_group, moe_w_router, moe_w_gate, moe_w_up, moe_w_down, final_norm_g):
    batch, seq, d = x.shape
    depth = w_in.shape[0]
    t = batch * seq
    h = x.reshape(t, d)

    lb_cum = jnp.cumsum(jax.nn.softmax(hgrn_lb_logits.astype(F32), axis=0), axis=0)
    lb_all = lb_cum - lb_cum[0:1]
    attn_bias = _attn_bias_tiles(rel_bias, min(ATTN_TQ, seq), min(ATTN_TK, seq))

    for l in range(depth):
        proj = _in_proj(h, mix_norm_g[l], w_in, l)

        lam_init = 0.8 - 0.6 * math.exp(-0.3 * l)
        lv = diff_lambda[l].astype(F32)
        lam = jnp.exp(jnp.sum(lv[0] * lv[1])) - jnp.exp(jnp.sum(lv[2] * lv[3])) + lam_init
        attn = _diff_attention(proj, attn_bias, lam, attn_subln_g[l], lam_init, batch, seq)

        w_b, w_c, tabs = _s5_prep(s5_a_re[l], s5_a_im[l], s5_log_dt[l], s5_b_re[l], s5_b_im[l],
                                  s5_c_re[l], s5_c_im[l], min(S5_TB, seq) // SUBLANES)
        s5 = _s5_mixer(proj, w_b, w_c, tabs, s5_d[l], s5_w_glu[l], batch, seq)

        hg = _hgrn_mixer(proj, lb_all[l], hgrn_norm_g[l], batch, seq)

        wo = w_out[l].astype(BF16)
        w_r = jnp.concatenate([moe_w_router[l], moe_w_group[l]], axis=1).astype(F32)
        w_r = jnp.pad(w_r, ((0, 0), (0, LANES - w_r.shape[1])))
        r_hi = w_r.astype(BF16)
        r_lo = (w_r - r_hi.astype(F32)).astype(BF16)
        h, hn, eid, gates = _out_proj(h, attn, s5, hg, wo, ffn_norm_g[l], jnp.concatenate([r_hi, r_lo], axis=1))

        plan = _moe_plan(eid[:, :TOP_K], MOE_TM)
        ys = _moe_experts(hn, plan, moe_w_gate, moe_w_up, moe_w_down, l, MOE_TM)
        h = _moe_combine(h, gates, ys, plan["pos"], final_norm_g, final_norm=(l == depth - 1))

    return h.reshape(batch, seq, d)
```

```python
import functools
import math

import numpy as np
import jax
import jax.numpy as jnp
from jax import lax
from jax.experimental import pallas as pl
from jax.experimental.pallas import tpu as pltpu

F32 = jnp.float32
BF16 = jnp.bfloat16
EPS = 1e-6
LOG2E = 1.4426950408889634

D_MODEL = 2048
ATTN_HEADS = 6
ATTN_QK_DIM = 64
ATTN_V_DIM = 128
ATTN_WIDTH = ATTN_HEADS * ATTN_V_DIM
REL_BUCKETS = 32
REL_MAX_DIST = 128
S5_GROUPS = 40
S5_GROUP_CH = 16
S5_STATE = 64
S5_WIDTH = S5_GROUPS * S5_GROUP_CH
S5_LANES = S5_GROUPS * S5_STATE
HGRN_HEADS = 5
HGRN_DIM = 128
HGRN_WIDTH = HGRN_HEADS * HGRN_DIM
N_GROUPS = 4
EXPERTS_PER_GROUP = 8
N_EXPERTS = N_GROUPS * EXPERTS_PER_GROUP
TOP_K = 2
D_EXPERT = 512

LANES = 128
SUBLANES = 8

IN_COLS = 3 * ATTN_WIDTH + S5_WIDTH + 4 * HGRN_WIDTH
PROJ_TN = 256
IN_COLS_PAD = -(-IN_COLS // PROJ_TN) * PROJ_TN
ATTN_COL_BLOCK = (IN_COLS_PAD - 3 * ATTN_WIDTH) // LANES

PROJ_TM = 2048
ATTN_TQ = 512
ATTN_TK = 512
ATTN_NEAR = 2
ATTN_ROWS = 32
ATTN_HPS = 1
S5_TB = 256
S5_CHUNK = 640
S5_BLK = 128
HGRN_C = 128
OUT_TM = 256
MOE_TM = 256
COMB_TM = 128
DMA_UNROLL = 8
WEIGHT_DMA_PRIORITY = 1

VMEM_LIMIT = 56 * 1024 * 1024


def _params(sem, vmem=None, flags=None):
    return pltpu.CompilerParams(dimension_semantics=sem, vmem_limit_bytes=vmem, flags=flags)


def _pack_bf16_pair(lo, hi):
    lo_bits = lax.bitcast_convert_type(lo.astype(BF16).astype(F32), jnp.uint32) >> 16
    hi_bits = lax.bitcast_convert_type(hi.astype(BF16).astype(F32), jnp.uint32) & jnp.uint32(0xFFFF0000)
    return lo_bits | hi_bits


def _unpack_bf16_pair(w):
    return (lax.bitcast_convert_type(w << 16, F32),
            lax.bitcast_convert_type(w & jnp.uint32(0xFFFF0000), F32))


def _in_proj_kernel(x_ref, g_ref, w_ref, wt_ref, o_ref, xn_ref, *, tail_tile):
    j = pl.program_id(1)

    @pl.when(j == 0)
    def _():
        x = x_ref[...]
        ms = jnp.mean(x * x, axis=-1, keepdims=True)
        xn_ref[...] = (x * lax.rsqrt(ms + EPS) * g_ref[...]).astype(BF16)

    w = jnp.where(j == tail_tile, wt_ref[...], w_ref[0])
    o_ref[...] = jnp.dot(xn_ref[...], w.astype(BF16), preferred_element_type=F32)


def _in_proj(h, g, w_all, layer):
    t, d = h.shape
    tm = min(PROJ_TM, t)
    tn = PROJ_TN
    a = 3 * ATTN_WIDTH
    rest = IN_COLS - a
    assert a % tn == 0 and IN_COLS_PAD - IN_COLS == tn - rest % tn
    rest_full, a_tiles = rest // tn, a // tn
    tail_tile = rest_full
    w_tail = lax.slice(w_all, (layer, 0, a + rest_full * tn), (layer + 1, d, IN_COLS))[0]
    w_tail = jnp.pad(w_tail, ((0, 0), (0, IN_COLS_PAD - IN_COLS)))

    def w_map(i, j):
        col = jnp.where(j < tail_tile, a_tiles + j, jnp.where(j == tail_tile, 0, j - tail_tile - 1))
        return (layer, 0, col)

    return pl.pallas_call(
        functools.partial(_in_proj_kernel, tail_tile=tail_tile),
        grid=(t // tm, IN_COLS_PAD // tn),
        in_specs=[
            pl.BlockSpec((tm, d), lambda i, j: (i, 0)),
            pl.BlockSpec((1, d), lambda i, j: (0, 0)),
            pl.BlockSpec((1, d, tn), w_map),
            pl.BlockSpec((d, tn), lambda i, j: (0, 0)),
        ],
        out_specs=pl.BlockSpec((tm, tn), lambda i, j: (i, j)),
        out_shape=jax.ShapeDtypeStruct((t, IN_COLS_PAD), F32),
        scratch_shapes=[pltpu.VMEM((tm, d), BF16)],
        compiler_params=_params(("parallel", "arbitrary"), VMEM_LIMIT),
        name="in_proj",
    )(h, g.reshape(1, d), w_all, w_tail)


def _attn_kernel(lam_ref, q_ref, k_ref, v_ref, bias_ref, g_ref, o_ref,
                 kb_ref, vb_ref, qz_ref, m_ref, a_ref, acc_ref, s0_ref, s1_ref, p_ref, *, tq, tk, out_scale):
    qi = pl.program_id(2)
    ratio = tq // tk
    heads = range(ATTN_HPS)

    @pl.when(qi == 0)
    def _():
        for h in heads:
            hl = slice(h * LANES, (h + 1) * LANES)
            kb_ref[h] = k_ref[:, hl].astype(BF16)
            vb_ref[h, :, 0:LANES] = v_ref[:, hl].astype(BF16)
            vb_ref[h, :, LANES:2 * LANES] = jnp.ones((vb_ref.shape[1], LANES), BF16)

    for h in heads:
        q = q_ref[:, h * LANES:(h + 1) * LANES] * (ATTN_QK_DIM ** -0.5 * LOG2E)
        lane = lax.broadcasted_iota(jnp.int32, q.shape, 1)
        qz_ref[h, 0:tq, :] = jnp.where(lane < ATTN_QK_DIM, q, 0.0).astype(BF16)
        qz_ref[h, tq:2 * tq, :] = jnp.where(lane >= ATTN_QK_DIM, q, 0.0).astype(BF16)
    m_ref[...] = jnp.full(m_ref.shape, -jnp.inf, F32)
    acc_ref[...] = jnp.zeros(acc_ref.shape, F32)

    n_col = tk // LANES
    groups = [slice(g * ATTN_ROWS, (g + 1) * ATTN_ROWS) for g in range(2 * tq // ATTN_ROWS)]

    def scores(j, s_ref):
        for h in heads:
            k = kb_ref[h, pl.ds(pl.multiple_of(j * tk, tk), tk), :]
            s_ref[h] = lax.dot_general(qz_ref[h], k, (((1,), (1,)), ((), ())), preferred_element_type=F32)

    def softmax_pv(j, s_ref, biased):
        if biased:
            kind = jnp.clip(j - ratio * qi + ATTN_NEAR, 0, ATTN_NEAR + ratio - 1)
        for h in heads:
            for rows in groups:
                cols = [s_ref[h, rows, i * LANES:(i + 1) * LANES] for i in range(n_col)]
                if biased:
                    b0 = rows.start % tq
                    cols = [x + bias_ref[h, kind, b0:b0 + ATTN_ROWS, i * LANES:(i + 1) * LANES]
                            for i, x in enumerate(cols)]
                    for i, x in enumerate(cols):
                        s_ref[h, rows, i * LANES:(i + 1) * LANES] = x
                mx = functools.reduce(jnp.maximum, cols)
                m_old = m_ref[h, rows, :]
                m_new = jnp.maximum(m_old, jnp.max(mx, axis=-1, keepdims=True))
                m_ref[h, rows, :] = m_new
                a_ref[h, rows, :] = jnp.exp2(m_old - m_new)
        for h in heads:
            for rows in groups:
                m_new = m_ref[h, rows, :]
                for i in range(n_col):
                    p_ref[h, rows, i * LANES:(i + 1) * LANES] = jnp.exp2(
                        s_ref[h, rows, i * LANES:(i + 1) * LANES] - m_new).astype(BF16)
        for h in heads:
            v = vb_ref[h, pl.ds(pl.multiple_of(j * tk, tk), tk), :]
            alpha = a_ref[h]
            acc_ref[h] = (jnp.concatenate([alpha, alpha], axis=1) * acc_ref[h]
                          + jnp.dot(p_ref[h], v, preferred_element_type=F32))

    n_tiles = ratio * (qi + 1)
    first = lax.rem(n_tiles, 2)
    n_far = jnp.maximum(ratio * qi - ATTN_NEAR + 1, 0)
    n_far_pairs = jnp.maximum(n_far - first, 0) // 2
    n_pairs = (n_tiles - first) // 2

    @pl.when(first == 1)
    def _():
        scores(0, s1_ref)
        scores(jnp.minimum(1, n_tiles - 1), s0_ref)
        softmax_pv(0, s1_ref, True)

    @pl.when(first == 0)
    def _():
        scores(0, s0_ref)

    def pair_body(jj, carry, biased):
        j = first + 2 * jj
        scores(j + 1, s1_ref)
        softmax_pv(j, s0_ref, biased)
        scores(jnp.minimum(j + 2, n_tiles - 1), s0_ref)
        softmax_pv(j + 1, s1_ref, biased)
        return carry

    lax.fori_loop(0, n_far_pairs, functools.partial(pair_body, biased=False), 0)
    lax.fori_loop(n_far_pairs, n_pairs, functools.partial(pair_body, biased=True), 0)

    for h in heads:
        inv_l = 1.0 / acc_ref[h, :, LANES:2 * LANES]
        o1 = acc_ref[h, 0:tq, 0:LANES] * inv_l[0:tq]
        o2 = acc_ref[h, tq:2 * tq, 0:LANES] * inv_l[tq:2 * tq]
        o = o1 - lam_ref[0] * o2
        ms = jnp.mean(o * o, axis=-1, keepdims=True)
        o_ref[:, h * LANES:(h + 1) * LANES] = (o * lax.rsqrt(ms + EPS) * g_ref[...] * out_scale).astype(o_ref.dtype)


def _t5_bucket(dist):
    n = jnp.maximum(dist, 0)
    max_exact = REL_BUCKETS // 2
    large = max_exact + (jnp.log(jnp.maximum(n, 1).astype(F32) / max_exact)
                         / math.log(REL_MAX_DIST / max_exact) * (REL_BUCKETS - max_exact)).astype(jnp.int32)
    large = jnp.minimum(large, REL_BUCKETS - 1)
    return jnp.where(n < max_exact, n, large)


def _attn_bias_tiles(rel_bias, tq, tk):
    table = rel_bias.astype(F32)
    far = table[REL_BUCKETS - 1]
    n = tq + tk
    m = jnp.arange(n)

    def lookup(dist):
        bucket = _t5_bucket(dist)
        out = jnp.zeros((ATTN_HEADS,) + dist.shape, F32)
        for b in range(REL_BUCKETS):
            out = jnp.where((bucket == b)[None], table[b][:, None], out)
        return out

    vecs = [jnp.zeros((ATTN_HEADS, n), F32)]
    for rel in range(-ATTN_NEAR + 1, tq // tk):
        dist = tq - m - rel * tk
        vecs.append(jnp.where((dist >= 0)[None], (lookup(dist) - far[:, None]) * LOG2E, -1e30))
    vecs = jnp.stack(vecs, axis=1)[:, :, None, :]
    kinds = vecs.shape[1]
    return pl.pallas_call(
        functools.partial(_bias_tile_kernel, tq=tq, tk=tk),
        grid=(ATTN_HEADS, kinds),
        in_specs=[pl.BlockSpec((1, 1, 1, n), lambda h, k: (h, k, 0, 0))],
        out_specs=pl.BlockSpec((1, 1, tq, tk), lambda h, k: (h, k, 0, 0)),
        out_shape=jax.ShapeDtypeStruct((ATTN_HEADS, kinds, tq, tk), F32),
        compiler_params=_params(("parallel", "parallel")),
        name="attn_bias_tiles",
    )(vecs)


def _bias_tile_kernel(v_ref, o_ref, *, tq, tk):
    x = jnp.broadcast_to(v_ref[0, 0], (tq, tq + tk))
    o_ref[0, 0] = pltpu.roll(x, 0, 1, stride=1, stride_axis=0)[:, tq:]


def _diff_attention(proj, bias, lam, subln_g, lam_init, batch, seq):
    t = proj.shape[0]
    tq = min(ATTN_TQ, seq)
    tk = min(ATTN_TK, seq)
    ratio = tq // tk
    hps = ATTN_HPS
    assert (ATTN_NEAR - 1) * tk >= REL_MAX_DIST and ratio * tk == tq
    assert ATTN_HEADS % hps == 0 and ATTN_COL_BLOCK % hps == 0
    nq = seq // tq
    kernel = functools.partial(_attn_kernel, tq=tq, tk=tk, out_scale=1.0 - lam_init)
    qb, kb, vb = (ATTN_COL_BLOCK // hps, (ATTN_COL_BLOCK + ATTN_HEADS) // hps,
                  (ATTN_COL_BLOCK + 2 * ATTN_HEADS) // hps)
    w = hps * LANES
    return pl.pallas_call(
        kernel,
        grid=(batch, ATTN_HEADS // hps, nq),
        in_specs=[
            pl.BlockSpec(memory_space=pltpu.SMEM),
            pl.BlockSpec((tq, w), lambda b, h, i: (b * nq + i, qb + h)),
            pl.BlockSpec((seq, w), lambda b, h, i: (b, kb + h)),
            pl.BlockSpec((seq, w), lambda b, h, i: (b, vb + h)),
            pl.BlockSpec((hps,) + bias.shape[1:], lambda b, h, i: (h, 0, 0, 0)),
            pl.BlockSpec((1, LANES), lambda b, h, i: (0, 0)),
        ],
        out_specs=pl.BlockSpec((tq, w), lambda b, h, i: (b * nq + i, h)),
        out_shape=jax.ShapeDtypeStruct((t, ATTN_WIDTH), BF16),
        scratch_shapes=[
            pltpu.VMEM((hps, seq, LANES), BF16),
            pltpu.VMEM((hps, seq, 2 * LANES), BF16),
            pltpu.VMEM((hps, 2 * tq, LANES), BF16),
            pltpu.VMEM((hps, 2 * tq, LANES), F32),
            pltpu.VMEM((hps, 2 * tq, LANES), F32),
            pltpu.VMEM((hps, 2 * tq, 2 * LANES), F32),
            pltpu.VMEM((hps, 2 * tq, tk), F32),
            pltpu.VMEM((hps, 2 * tq, tk), F32),
            pltpu.VMEM((hps, 2 * tq, tk), BF16),
        ],
        compiler_params=_params(("parallel", "parallel", "arbitrary"), VMEM_LIMIT),
        name="diff_attn",
    )(lam.reshape(1), proj, proj, proj, bias, subln_g.reshape(1, LANES))


def _s5_kernel(u_ref, perm_ref, permt_ref, wb_ref, wc_ref, pw_ref, d_ref, wglu_ref, o_ref,
               xr_ref, xi_ref, cr_ref, ci_ref, kr_ref, ki_ref, *, tb):
    @pl.when(pl.program_id(1) == 0)
    def _():
        cr_ref[...] = jnp.zeros(cr_ref.shape, F32)
        ci_ref[...] = jnp.zeros(ci_ref.shape, F32)

    un = u_ref[...]
    uh = un.astype(BF16)
    ul = (un - uh.astype(F32)).astype(BF16)
    up = jnp.dot(perm_ref[...], jnp.concatenate([uh, ul], axis=1), preferred_element_type=F32)
    ub = up[:, :S5_WIDTH].astype(BF16)
    u = up[:, :S5_WIDTH] + up[:, S5_WIDTH:]
    nblk = S5_WIDTH // S5_BLK
    sw = S5_LANES // nblk
    tpb = sw // LANES
    for i in range(nblk):
        bu = jnp.dot(ub[:, i * S5_BLK:(i + 1) * S5_BLK], wb_ref[i], preferred_element_type=F32)
        for k in range(tpb):
            xr_ref[i * tpb + k] = bu[:, k * LANES:(k + 1) * LANES]
            xi_ref[i * tpb + k] = bu[:, sw + k * LANES:sw + (k + 1) * LANES]

    sl = tb // SUBLANES
    for c in range(S5_LANES // S5_CHUNK):
        tiles = range(c * S5_CHUNK // LANES, (c + 1) * S5_CHUNK // LANES)
        lss = [slice(n * LANES, (n + 1) * LANES) for n in tiles]
        a = [(pw_ref[0, 0, :, ls], pw_ref[1, 0, :, ls]) for ls in lss]

        def local_step(t, carry, tiles=tiles, a=a):
            rows = pl.ds(pl.multiple_of(t * SUBLANES, SUBLANES), SUBLANES)
            out = []
            for n, (ar, ai), (pr, pi) in zip(tiles, a, carry):
                xr = ar * pr - ai * pi + xr_ref[n, rows, :]
                xi = ar * pi + ai * pr + xi_ref[n, rows, :]
                xr_ref[n, rows, :] = xr
                xi_ref[n, rows, :] = xi
                out.append((xr, xi))
            return tuple(out)

        zero = jnp.zeros((SUBLANES, LANES), F32)
        ends = lax.fori_loop(0, sl, local_step, tuple((zero, zero) for _ in tiles), unroll=2)

        for ls, (er, ei) in zip(lss, ends):
            lr, li = pw_ref[0, sl - 1, 0:1, ls], pw_ref[1, sl - 1, 0:1, ls]
            kr, ki = cr_ref[:, ls], ci_ref[:, ls]
            for sub in range(SUBLANES):
                kr_ref[sub:sub + 1, ls] = kr
                ki_ref[sub:sub + 1, ls] = ki
                kr, ki = (er[sub:sub + 1, :] + lr * kr - li * ki, ei[sub:sub + 1, :] + lr * ki + li * kr)
            cr_ref[:, ls] = kr
            ci_ref[:, ls] = ki

        k8 = [(kr_ref[:, ls], ki_ref[:, ls]) for ls in lss]

        def fix_step(t, carry, tiles=tiles, lss=lss, k8=k8):
            rows = pl.ds(pl.multiple_of(t * SUBLANES, SUBLANES), SUBLANES)
            for n, ls, (kr8, ki8) in zip(tiles, lss, k8):
                pr = pw_ref[0, t, :, ls]
                pi = pw_ref[1, t, :, ls]
                xr_ref[n, rows, :] = xr_ref[n, rows, :] + pr * kr8 - pi * ki8
                xi_ref[n, rows, :] = xi_ref[n, rows, :] + pr * ki8 + pi * kr8
            return carry

        lax.fori_loop(0, sl, fix_step, 0, unroll=2)

    ys = []
    for i in range(nblk):
        xc = jnp.concatenate([xr_ref[i * tpb + k] for k in range(tpb)]
                             + [xi_ref[i * tpb + k] for k in range(tpb)], axis=1)
        ys.append(jnp.dot(xc.astype(BF16), wc_ref[i], preferred_element_type=F32))
    y = jnp.concatenate(ys, axis=1) + d_ref[...] * u
    z = 0.5 * y * (1.0 + jnp.tanh(math.sqrt(2.0 / math.pi) * (y + 0.044715 * (y * y * y))))
    gate = jax.nn.sigmoid(jnp.dot(z.astype(BF16), wglu_ref[...].astype(BF16), preferred_element_type=F32))
    o_ref[...] = jnp.dot(permt_ref[...], (z * gate).astype(BF16),
                         preferred_element_type=F32).astype(o_ref.dtype)


def _s5_prep(a_re, a_im, log_dt, b_re, b_im, c_re, c_im, steps):
    lr = jnp.minimum(a_re.astype(F32), -1e-4)
    li = a_im.astype(F32)
    dt = jnp.exp(log_dt.astype(F32))[:, None]
    mag = jnp.exp(lr * dt)
    ab_re = mag * jnp.cos(li * dt)
    ab_im = mag * jnp.sin(li * dt)
    den = lr * lr + li * li
    nr = ab_re - 1.0
    ni = ab_im
    f_re = (nr * lr + ni * li) / den
    f_im = (ni * lr - nr * li) / den
    br = b_re.astype(F32)
    bi = b_im.astype(F32)
    bb_re = f_re[..., None] * br - f_im[..., None] * bi
    bb_im = f_re[..., None] * bi + f_im[..., None] * br

    nblk = S5_WIDTH // S5_BLK
    gpb = S5_GROUPS // nblk
    eye = jnp.eye(gpb, dtype=F32)

    def bmat(bb):
        bb = bb.reshape(nblk, gpb, S5_STATE, S5_GROUP_CH)
        m = jnp.einsum('ignc,gh->igchn', bb, eye)
        return m.reshape(nblk, gpb * S5_GROUP_CH, gpb * S5_STATE)

    def cmat(cc):
        cc = cc.astype(F32).reshape(nblk, gpb, S5_GROUP_CH, S5_STATE)
        m = jnp.einsum('igcn,gh->ignhc', cc, eye)
        return m.reshape(nblk, gpb * S5_STATE, gpb * S5_GROUP_CH)

    w_b = jnp.concatenate([bmat(bb_re), bmat(bb_im)], axis=2).astype(BF16)
    w_c = jnp.concatenate([cmat(c_re), -cmat(c_im)], axis=1).astype(BF16)

    j = jnp.arange(1, steps + 1, dtype=F32)[:, None, None]
    m = jnp.exp(lr * dt * j)
    pows = jnp.stack([(m * jnp.cos(li * dt * j)).reshape(steps, 1, -1),
                      (m * jnp.sin(li * dt * j)).reshape(steps, 1, -1)])
    pows = jnp.broadcast_to(pows, (2, steps, SUBLANES, S5_LANES))
    return w_b, w_c, pows


def _s5_mixer(proj, w_b, w_c, tabs, d_skip, w_glu, batch, seq):
    t = proj.shape[0]
    tb = min(S5_TB, seq)
    nt = seq // tb
    sl = tb // SUBLANES
    r = np.arange(tb)
    perm = np.zeros((tb, tb), np.float32)
    perm[r, (r % SUBLANES) * sl + r // SUBLANES] = 1.0
    perm_t = jnp.asarray(perm.T, BF16)
    perm = jnp.asarray(perm, BF16)
    return pl.pallas_call(
        functools.partial(_s5_kernel, tb=tb),
        grid=(batch, nt),
        in_specs=[
            pl.BlockSpec((tb, S5_WIDTH), lambda b, i: (b * nt + i, 0)),
            pl.BlockSpec((tb, tb), lambda b, i: (0, 0)),
            pl.BlockSpec((tb, tb), lambda b, i: (0, 0)),
            pl.BlockSpec(w_b.shape, lambda b, i: (0, 0, 0)),
            pl.BlockSpec(w_c.shape, lambda b, i: (0, 0, 0)),
            pl.BlockSpec(tabs.shape, lambda b, i: (0, 0, 0, 0)),
            pl.BlockSpec((1, S5_WIDTH), lambda b, i: (0, 0)),
            pl.BlockSpec((S5_WIDTH, S5_WIDTH), lambda b, i: (0, 0)),
        ],
        out_specs=pl.BlockSpec((tb, S5_WIDTH), lambda b, i: (b * nt + i, 0)),
        out_shape=jax.ShapeDtypeStruct((t, S5_WIDTH), BF16),
        scratch_shapes=[
            pltpu.VMEM((S5_LANES // LANES, tb, LANES), F32),
            pltpu.VMEM((S5_LANES // LANES, tb, LANES), F32),
            pltpu.VMEM((1, S5_LANES), F32),
            pltpu.VMEM((1, S5_LANES), F32),
            pltpu.VMEM((SUBLANES, S5_LANES), F32),
            pltpu.VMEM((SUBLANES, S5_LANES), F32),
        ],
        compiler_params=_params(("parallel", "arbitrary"), VMEM_LIMIT),
        name="s5_mixer",
    )(proj, perm, perm_t, w_b, w_c, tabs, d_skip.reshape(1, S5_WIDTH), w_glu)


def _hgrn_consts(c):
    t = np.arange(c)[:, None]
    u = np.arange(c)[None, :]
    mats = [u <= t]
    masks = [np.eye(c, dtype=bool)]
    h = 1
    while h < c:
        mid = (t // (2 * h)) * 2 * h + h
        if 2 * h < SUBLANES:
            mats.append(np.where(t >= mid, (u >= mid) & (u <= t), (u > t) & (u < mid)))
        masks.append(((u // (2 * h)) == (t // (2 * h))) & (t >= mid) & (u < mid))
        h *= 2
    return (np.concatenate(mats, axis=0).astype(np.float32), np.stack(masks).astype(np.float32))


def _hgrn_kernel(q_ref, f_ref, i_ref, g_ref, lb_ref, ng_ref, mall_ref, mask_ref, o_ref,
                 e_ref, st_ref, *, c):
    @pl.when(pl.program_id(1) == 0)
    def _():
        st_ref[...] = jnp.zeros(st_ref.shape, F32)

    lb = lb_ref[...]
    fl = f_ref[...]
    la = jnp.log(lb)
    lbb = jnp.log1p(-lb) + (jnp.minimum(fl, 0.0) - jnp.log1p(jnp.exp(-jnp.abs(fl))))
    logf = jnp.maximum(la, lbb) + jnp.log1p(jnp.exp(-jnp.abs(la - lbb)))
    hi = logf.astype(BF16)
    lo = (logf - hi.astype(F32)).astype(BF16)
    e2 = jnp.dot(mall_ref[...], jnp.concatenate([hi, lo], axis=1), preferred_element_type=F32)
    es = e2[:, :HGRN_WIDTH] + e2[:, HGRN_WIDTH:]
    n_levels = mask_ref.shape[0] - 1
    n_small = mall_ref.shape[0] // c - 1
    gall = es[0:c]
    e_ref[0:c, :] = gall
    e_ref[c:2 * c, :] = gall[c - 1:c, :] - gall
    e_ref[2 * c:(2 + n_small) * c, :] = es[c:]
    for lv in range(n_small, n_levels):
        blk = 2 << lv
        g3 = gall.reshape(c // blk, blk, HGRN_WIDTH)
        gm = jnp.broadcast_to(g3[:, blk // 2 - 1:blk // 2, :], g3.shape)
        e_ref[(2 + lv) * c:(3 + lv) * c, :] = (-jnp.abs(g3 - gm)).reshape(c, HGRN_WIDTH)

    nt = (((1,), (1,)), ((), ()))
    for hd in range(HGRN_HEADS):
        ls = slice(hd * HGRN_DIM, (hd + 1) * HGRN_DIM)
        qr = q_ref[:, ls]
        q = qr * jax.nn.sigmoid(qr) * (HGRN_DIM ** -0.5)
        k = (1.0 - lb[:, ls]) * jax.nn.sigmoid(-fl[:, ls])
        vb = i_ref[:, ls].astype(BF16)
        p = lax.dot_general(q.astype(BF16), k.astype(BF16), nt, preferred_element_type=F32) * mask_ref[0]
        for lv in range(n_levels):
            fac = jnp.exp(e_ref[(2 + lv) * c:(3 + lv) * c, ls])
            s = lax.dot_general((q * fac).astype(BF16), (k * fac).astype(BF16), nt,
                                preferred_element_type=F32)
            p = p + s * mask_ref[lv + 1]
        o = jnp.dot(p.astype(BF16), vb, preferred_element_type=F32)
        gcum = e_ref[0:c, ls]
        st = st_ref[hd]
        o = o + lax.dot_general((q * jnp.exp(gcum)).astype(BF16), st.astype(BF16), nt,
                                preferred_element_type=F32)
        kd = (k * jnp.exp(e_ref[c:2 * c, ls])).astype(BF16)
        st_ref[hd] = st * jnp.exp(gcum[c - 1:c, :]) + lax.dot_general(
            vb, kd, (((0,), (0,)), ((), ())), preferred_element_type=F32)
        ms = jnp.mean(o * o, axis=-1, keepdims=True)
        o = o * lax.rsqrt(ms + EPS) * ng_ref[...] * jax.nn.sigmoid(g_ref[:, ls])
        o_ref[:, ls] = o.astype(o_ref.dtype)


def _hgrn_mixer(proj, lb, norm_g, batch, seq):
    t = proj.shape[0]
    c = min(HGRN_C, seq)
    nc = seq // c
    m_all, masks = _hgrn_consts(c)
    m_all = jnp.asarray(m_all, BF16)
    masks = jnp.asarray(masks, F32)

    def col(j):
        return pl.BlockSpec((c, HGRN_WIDTH), lambda b, i: (b * nc + i, j))

    return pl.pallas_call(
        functools.partial(_hgrn_kernel, c=c),
        grid=(batch, nc),
        in_specs=[
            col(1), col(2), col(3), col(4),
            pl.BlockSpec((1, HGRN_WIDTH), lambda b, i: (0, 0)),
            pl.BlockSpec((1, HGRN_DIM), lambda b, i: (0, 0)),
            pl.BlockSpec(m_all.shape, lambda b, i: (0, 0)),
            pl.BlockSpec(masks.shape, lambda b, i: (0, 0, 0)),
        ],
        out_specs=pl.BlockSpec((c, HGRN_WIDTH), lambda b, i: (b * nc + i, 0)),
        out_shape=jax.ShapeDtypeStruct((t, HGRN_WIDTH), BF16),
        scratch_shapes=[
            pltpu.VMEM(((masks.shape[0] + 1) * c, HGRN_WIDTH), F32),
            pltpu.VMEM((HGRN_HEADS, HGRN_DIM, HGRN_DIM), F32),
        ],
        compiler_params=_params(("parallel", "arbitrary"), VMEM_LIMIT),
        name="hgrn2_mixer",
    )(proj, proj, proj, proj, lb.reshape(1, HGRN_WIDTH), norm_g.reshape(1, HGRN_DIM), m_all, masks)


def _out_proj_kernel(h_ref, a_ref, s_ref, r_ref, w_ref, gn_ref, rw_ref,
                     ho_ref, hn_ref, eid_ref, gate_ref):
    s_lo, r_lo = ATTN_WIDTH, ATTN_WIDTH + S5_WIDTH
    acc = jnp.dot(a_ref[...], w_ref[0:s_lo, :], preferred_element_type=F32)
    acc = acc + jnp.dot(s_ref[...], w_ref[s_lo:r_lo, :], preferred_element_type=F32)
    acc = acc + jnp.dot(r_ref[...], w_ref[r_lo:, :], preferred_element_type=F32)
    h = h_ref[...] + acc
    ho_ref[...] = h
    ms = jnp.mean(h * h, axis=-1, keepdims=True)
    hn = h * lax.rsqrt(ms + EPS) * gn_ref[...]
    half = hn.shape[1] // 2
    hn_ref[...] = _pack_bf16_pair(hn[:, :half], hn[:, half:])

    hh = hn.astype(BF16)
    hl = (hn - hh.astype(F32)).astype(BF16)
    both = jnp.dot(hh, rw_ref[...], preferred_element_type=F32)
    logits = (both[:, :LANES] + both[:, LANES:]
              + jnp.dot(hl, rw_ref[:, :LANES], preferred_element_type=F32))
    lane = lax.broadcasted_iota(jnp.int32, logits.shape, 1).astype(F32)
    neg = -jnp.inf
    big = 1e9
    is_group = jnp.where(lane >= N_EXPERTS, jnp.where(lane < N_EXPERTS + N_GROUPS, 1.0, 0.0), 0.0)
    gl = jnp.where(is_group > 0, logits, neg)
    gmax = jnp.max(gl, axis=-1, keepdims=True)
    g_lane = jnp.min(jnp.where(gl == gmax, lane, big), axis=-1, keepdims=True)
    p_g = 1.0 / jnp.sum(jnp.exp(gl - gmax), axis=-1, keepdims=True)
    lo_lane = (g_lane - N_EXPERTS) * EXPERTS_PER_GROUP
    in_group = jnp.where(lane >= lo_lane, jnp.where(lane < lo_lane + EXPERTS_PER_GROUP, 1.0, 0.0), 0.0)
    el = jnp.where(in_group > 0, logits, neg)
    t1 = jnp.max(el, axis=-1, keepdims=True)
    i1 = jnp.min(jnp.where(el == t1, lane, big), axis=-1, keepdims=True)
    el2 = jnp.where(lane == i1, neg, el)
    t2 = jnp.max(el2, axis=-1, keepdims=True)
    i2 = jnp.min(jnp.where(el2 == t2, lane, big), axis=-1, keepdims=True)
    e21 = jnp.exp(t2 - t1)
    g1 = p_g / (1.0 + e21)
    g2 = p_g * e21 / (1.0 + e21)
    eid_ref[...] = jnp.where(lane == 0, i1, jnp.where(lane == 1, i2, 0.0)).astype(jnp.int32)
    gate_ref[...] = jnp.where(lane == 0, g1, jnp.where(lane == 1, g2, 0.0))


def _out_proj(h, attn, s5, hg, w_out, gn, r_w):
    t, d = h.shape
    tm = min(OUT_TM, t)

    def rows(w):
        return pl.BlockSpec((tm, w), lambda i: (i, 0))

    def full(a):
        return pl.BlockSpec(a.shape, lambda i: (0,) * a.ndim)

    return pl.pallas_call(
        _out_proj_kernel,
        grid=(t // tm,),
        in_specs=[rows(d), rows(ATTN_WIDTH), rows(S5_WIDTH), rows(HGRN_WIDTH),
                  full(w_out), pl.BlockSpec((1, d), lambda i: (0, 0)),
                  full(r_w)],
        out_specs=[rows(d), rows(d // 2), rows(LANES), rows(LANES)],
        out_shape=[jax.ShapeDtypeStruct((t, d), F32), jax.ShapeDtypeStruct((t, d // 2), jnp.uint32),
                   jax.ShapeDtypeStruct((t, LANES), jnp.int32), jax.ShapeDtypeStruct((t, LANES), F32)],
        compiler_params=_params(("parallel",), VMEM_LIMIT),
        name="out_proj_router",
    )(h, attn, s5, hg, w_out, gn.reshape(1, d), r_w)


def _moe_plan(eid, tm):
    flat = eid.reshape(-1)
    n_slots = flat.shape[0]
    onehot = (flat[:, None] == jnp.arange(N_EXPERTS, dtype=jnp.int32)[None, :]).astype(jnp.int32)
    order = jnp.argsort(flat, stable=True).astype(jnp.int32)
    sorted_pos = jnp.argsort(order).astype(jnp.int32)
    counts = jnp.sum(onehot, axis=0)
    tiles = (counts + tm - 1) // tm
    tile_end = jnp.cumsum(tiles)
    tile_start = tile_end - tiles
    slot_start = jnp.cumsum(counts) - counts
    pos = jnp.sum(onehot * (tile_start * tm - slot_start)[None, :], axis=1) + sorted_pos
    tok_sorted = order // TOP_K
    tok_sorted = jnp.pad(tok_sorted, (0, tm))
    nt_max = (n_slots + N_EXPERTS * (tm - 1)) // tm
    n_used = tile_end[-1]
    j = jnp.minimum(jnp.arange(nt_max, dtype=jnp.int32), n_used - 1)
    tile_expert = jnp.sum((j[:, None] >= tile_end[None, :]).astype(jnp.int32), axis=1)
    onehot_t = (tile_expert[:, None] == jnp.arange(N_EXPERTS, dtype=jnp.int32)[None, :]).astype(jnp.int32)
    tile_slot0 = jnp.sum(onehot_t * (slot_start - tile_start * tm)[None, :], axis=1) + j * tm
    used = (tiles > 0).astype(jnp.int32)
    ordinal = jnp.cumsum(used) - used
    ids = jnp.arange(N_EXPERTS, dtype=jnp.int32)
    later = jnp.where((ids[None, :] > ids[:, None]) & (used[None, :] > 0), ids[None, :], N_EXPERTS)
    nxt = jnp.min(later, axis=1)
    nxt = jnp.where(nxt >= N_EXPERTS, -1, nxt)
    tile_wslot = jnp.sum(onehot_t * (ordinal % 2)[None, :], axis=1)
    tile_next = jnp.sum(onehot_t * nxt[None, :], axis=1)
    return dict(pos=pos.astype(jnp.int32), tok_sorted=tok_sorted, tile_expert=tile_expert.astype(jnp.int32),
                tile_slot0=tile_slot0.astype(jnp.int32), tile_wslot=tile_wslot.astype(jnp.int32),
                tile_next=tile_next.astype(jnp.int32), n_used=n_used.reshape(1).astype(jnp.int32),
                nt_max=nt_max)


def _expert_kernel(tok_ref, te_ref, s0_ref, ws_ref, ne_ref, nu_ref, hn_ref, wg_ref, wu_ref, wd_ref, y_ref,
                   xa_ref, xb_ref, wgf_ref, wuf_ref, wdf_ref, wgb_ref, wub_ref, wdb_ref, sem, wsem,
                   *, tm, layer):
    j = pl.program_id(0)
    n_used = nu_ref[0]
    bufs = (xa_ref, xb_ref)

    def weight_copies(expert, slot):
        return [pltpu.make_async_copy(src.at[layer, expert], dst.at[slot], wsem.at[slot])
                for src, dst in ((wg_ref, wgf_ref), (wu_ref, wuf_ref), (wd_ref, wdf_ref))]

    def row_copy(buf, r, tok):
        return pltpu.make_async_copy(hn_ref.at[pl.ds(tok, 1)], bufs[buf].at[pl.ds(r, 1)], sem.at[buf])

    def wait_gather(buf):
        def wait(r, carry):
            row_copy(buf, r, 0).wait()
            return carry

        lax.fori_loop(0, tm, wait, 0, unroll=DMA_UNROLL)

    @pl.when(j == 0)
    def _():
        for cp in weight_copies(te_ref[0], ws_ref[0]):
            cp.start(priority=WEIGHT_DMA_PRIORITY)
        base = s0_ref[0]

        def issue(r, carry):
            row_copy(0, r, tok_ref[base + r]).start()
            return carry

        lax.fori_loop(0, tm, issue, 0, unroll=DMA_UNROLL)

    def tile(buf):
        wait_gather(buf)
        prev = te_ref[jnp.maximum(j - 1, 0)]

        @pl.when(jnp.logical_or(j == 0, te_ref[j] != prev))
        def _():
            slot = ws_ref[j]

            @pl.when(ne_ref[j] >= 0)
            def _():
                for cp in weight_copies(ne_ref[j], 1 - slot):
                    cp.start(priority=WEIGHT_DMA_PRIORITY)

            for cp in weight_copies(te_ref[j], slot):
                cp.wait()
            wgb_ref[...] = wgf_ref[slot].astype(BF16)
            wub_ref[...] = wuf_ref[slot].astype(BF16)
            wdb_ref[...] = wdf_ref[slot].astype(BF16)

        nbase = s0_ref[jnp.minimum(j + 1, n_used - 1)]
        for r in range(tm):
            row_copy(1 - buf, r, tok_ref[nbase + r]).start()

        x_lo, x_hi = _unpack_bf16_pair(bufs[buf][...])
        x = jnp.concatenate([x_lo.astype(BF16), x_hi.astype(BF16)], axis=1)
        g = jnp.dot(x, wgb_ref[...], preferred_element_type=F32)
        u = jnp.dot(x, wub_ref[...], preferred_element_type=F32)
        hmid = (g * jax.nn.sigmoid(g) * u).astype(BF16)
        y = jnp.dot(hmid, wdb_ref[...], preferred_element_type=F32)
        half = y.shape[1] // 2
        y_ref[...] = _pack_bf16_pair(y[:, :half], y[:, half:])

        @pl.when(j == n_used - 1)
        def _():
            wait_gather(1 - buf)

    for parity in range(2):
        @pl.when(jnp.logical_and(j < n_used, lax.rem(j, 2) == parity))
        def _(parity=parity):
            tile(parity)

    @pl.when(j >= n_used)
    def _():
        y_ref[...] = jnp.zeros(y_ref.shape, y_ref.dtype)


def _moe_experts(hn, plan, w_gate, w_up, w_down, layer, tm):
    t, dp = hn.shape
    d, de = w_gate.shape[2], w_gate.shape[3]
    nt_max = plan["nt_max"]
    grid_spec = pltpu.PrefetchScalarGridSpec(
        num_scalar_prefetch=6,
        grid=(nt_max,),
        in_specs=[pl.BlockSpec(memory_space=pl.ANY)] * 4,
        out_specs=pl.BlockSpec((tm, dp), lambda j, *_: (j, 0)),
        scratch_shapes=[pltpu.VMEM((tm, dp), jnp.uint32), pltpu.VMEM((tm, dp), jnp.uint32),
                        pltpu.VMEM((2, d, de), F32), pltpu.VMEM((2, d, de), F32), pltpu.VMEM((2, de, d), F32),
                        pltpu.VMEM((d, de), BF16), pltpu.VMEM((d, de), BF16), pltpu.VMEM((de, d), BF16),
                        pltpu.SemaphoreType.DMA((2,)), pltpu.SemaphoreType.DMA((2,))],
    )
    return pl.pallas_call(
        functools.partial(_expert_kernel, tm=tm, layer=layer),
        grid_spec=grid_spec,
        out_shape=jax.ShapeDtypeStruct((nt_max * tm, dp), jnp.uint32),
        compiler_params=_params(("arbitrary",), VMEM_LIMIT),
        name="moe_experts",
    )(plan["tok_sorted"], plan["tile_expert"], plan["tile_slot0"], plan["tile_wslot"], plan["tile_next"],
      plan["n_used"], hn, w_gate, w_up, w_down)


def _combine_kernel(pos_ref, h_ref, gate_ref, gn_ref, y_ref, o_ref, ba_ref, bb_ref, sem, *, tm, final_norm):
    i = pl.program_id(0)
    n = pl.num_programs(0)
    bufs = (ba_ref, bb_ref)

    def row_copy(b, k, r, src_row):
        return pltpu.make_async_copy(y_ref.at[pl.ds(src_row, 1)], bufs[b].at[k, pl.ds(r, 1)], sem.at[b])

    def wait_gather(b):
        def wait(r, carry):
            for k in range(TOP_K):
                row_copy(b, k, r, 0).wait()
            return carry

        lax.fori_loop(0, tm, wait, 0, unroll=DMA_UNROLL)

    @pl.when(i == 0)
    def _():
        def issue(r, carry):
            for k in range(TOP_K):
                row_copy(0, k, r, pos_ref[r * TOP_K + k]).start()
            return carry

        lax.fori_loop(0, tm, issue, 0, unroll=DMA_UNROLL)

    def tile(b):
        wait_gather(b)
        base = jnp.minimum(i + 1, n - 1) * (tm * TOP_K)
        for r in range(tm):
            for k in range(TOP_K):
                row_copy(1 - b, k, r, pos_ref[base + r * TOP_K + k]).start(priority=(r * TOP_K + k) % 2)

        gates = gate_ref[...]
        y0_lo, y0_hi = _unpack_bf16_pair(bufs[b][0])
        y1_lo, y1_hi = _unpack_bf16_pair(bufs[b][1])
        g0, g1 = gates[:, 0:1], gates[:, 1:2]
        h = h_ref[...] + jnp.concatenate([g0 * y0_lo + g1 * y1_lo, g0 * y0_hi + g1 * y1_hi], axis=1)
        if final_norm:
            ms = jnp.mean(h * h, axis=-1, keepdims=True)
            h = h * lax.rsqrt(ms + EPS) * gn_ref[...]
        o_ref[...] = h

        @pl.when(i == n - 1)
        def _():
            wait_gather(1 - b)

    for parity in range(2):
        @pl.when(lax.rem(i, 2) == parity)
        def _(parity=parity):
            tile(parity)


def _moe_combine(h, gates, ys, pos, final_g, final_norm):
    t, d = h.shape
    tm = min(COMB_TM, t)
    grid_spec = pltpu.PrefetchScalarGridSpec(
        num_scalar_prefetch=1,
        grid=(t // tm,),
        in_specs=[
            pl.BlockSpec((tm, d), lambda i, p: (i, 0)),
            pl.BlockSpec((tm, LANES), lambda i, p: (i, 0)),
            pl.BlockSpec((1, d), lambda i, p: (0, 0)),
            pl.BlockSpec(memory_space=pl.ANY),
        ],
        out_specs=pl.BlockSpec((tm, d), lambda i, p: (i, 0)),
        scratch_shapes=[pltpu.VMEM((TOP_K, tm, d // 2), jnp.uint32), pltpu.VMEM((TOP_K, tm, d // 2), jnp.uint32),
                        pltpu.SemaphoreType.DMA((2,))],
    )
    return pl.pallas_call(
        functools.partial(_combine_kernel, tm=tm, final_norm=final_norm),
        grid_spec=grid_spec,
        out_shape=jax.ShapeDtypeStruct((t, d), F32),
        compiler_params=_params(("arbitrary",), VMEM_LIMIT),
        name="moe_combine",
    )(pos, h, gates, final_g.reshape(1, d), ys)


def kernel(x, w_in, w_out, mix_norm_g, ffn_norm_g, rel_bias, diff_lambda, attn_subln_g, s5_a_re, s5_a_im, s5_log_dt, s5_b_re, s5_b_im, s5_c_re, s5_c_im, s5_d, s5_w_glu, hgrn_lb_logits, hgrn_norm_g, moe_w_group, moe_w_router, moe_w_gate, moe_w_up, moe_w_down, final_norm_g):
    batch, seq, d = x.shape
    depth = w_in.shape[0]
    t = batch * seq
    h = x.reshape(t, d)

    lb_cum = jnp.cumsum(jax.nn.softmax(hgrn_lb_logits.astype(F32), axis=0), axis=0)
    lb_all = lb_cum - lb_cum[0:1]
    attn_bias = _attn_bias_tiles(rel_bias, min(ATTN_TQ, seq), min(ATTN_TK, seq))

    for l in range(depth):
        proj = _in_proj(h, mix_norm_g[l], w_in, l)

        lam_init = 0.8 - 0.6 * math.exp(-0.3 * l)
        lv = diff_lambda[l].astype(F32)
        lam = jnp.exp(jnp.sum(lv[0] * lv[1])) - jnp.exp(jnp.sum(lv[2] * lv[3])) + lam_init
        attn = _diff_attention(proj, attn_bias, lam, attn_subln_g[l], lam_init, batch, seq)

        w_b, w_c, tabs = _s5_prep(s5_a_re[l], s5_a_im[l], s5_log_dt[l], s5_b_re[l], s5_b_im[l],
                                  s5_c_re[l], s5_c_im[l], min(S5_TB, seq) // SUBLANES)
        s5 = _s5_mixer(proj, w_b, w_c, tabs, s5_d[l], s5_w_glu[l], batch, seq)

        hg = _hgrn_mixer(proj, lb_all[l], hgrn_norm_g[l], batch, seq)

        wo = w_out[l].astype(BF16)
        w_r = jnp.concatenate([moe_w_router[l], moe_w_group[l]], axis=1).astype(F32)
        w_r = jnp.pad(w_r, ((0, 0), (0, LANES - w_r.shape[1])))
        r_hi = w_r.astype(BF16)
        r_lo = (w_r - r_hi.astype(F32)).astype(BF16)
        h, hn, eid, gates = _out_proj(h, attn, s5, hg, wo, ffn_norm_g[l], jnp.concatenate([r_hi, r_lo], axis=1))

        plan = _moe_plan(eid[:, :TOP_K], MOE_TM)
        ys = _moe_experts(hn, plan, moe_w_gate, moe_w_up, moe_w_down, l, MOE_TM)
        h = _moe_combine(h, gates, ys, plan["pos"], final_norm_g, final_norm=(l == depth - 1))

    return h.reshape(batch, seq, d)
```

```python
import functools
import math

import numpy as np
import jax
import jax.numpy as jnp
from jax import lax
from jax.experimental import pallas as pl
from jax.experimental.pallas import tpu as pltpu

F32 = jnp.float32
BF16 = jnp.bfloat16
EPS = 1e-6
LOG2E = 1.4426950408889634

D_MODEL = 2048
ATTN_HEADS = 6
ATTN_QK_DIM = 64
ATTN_V_DIM = 128
ATTN_WIDTH = ATTN_HEADS * ATTN_V_DIM
REL_BUCKETS = 32
REL_MAX_DIST = 128
S5_GROUPS = 40
S5_GROUP_CH = 16
S5_STATE = 64
S5_WIDTH = S5_GROUPS * S5_GROUP_CH
S5_LANES = S5_GROUPS * S5_STATE
HGRN_HEADS = 5
HGRN_DIM = 128
HGRN_WIDTH = HGRN_HEADS * HGRN_DIM
N_GROUPS = 4
EXPERTS_PER_GROUP = 8
N_EXPERTS = N_GROUPS * EXPERTS_PER_GROUP
TOP_K = 2
D_EXPERT = 512

LANES = 128
SUBLANES = 8

IN_COLS = 3 * ATTN_WIDTH + S5_WIDTH + 4 * HGRN_WIDTH
PROJ_TN = 256
IN_COLS_PAD = -(-IN_COLS // PROJ_TN) * PROJ_TN
ATTN_COL_BLOCK = (IN_COLS_PAD - 3 * ATTN_WIDTH) // LANES

PROJ_TM = 2048
ATTN_TQ = 512
ATTN_TK = 512
ATTN_NEAR = 2
ATTN_ROWS = 32
ATTN_HPS = 1
S5_TB = 512
S5_CHUNK = 640
S5_BLK = 128
HGRN_C = 128
OUT_TM = 256
MOE_TM = 256
COMB_TM = 128
DMA_UNROLL = 8
WEIGHT_DMA_PRIORITY = 1

VMEM_LIMIT = 56 * 1024 * 1024


def _params(sem, vmem=None, flags=None):
    return pltpu.CompilerParams(dimension_semantics=sem, vmem_limit_bytes=vmem, flags=flags)


def _pack_bf16_pair(lo, hi):
    lo_bits = lax.bitcast_convert_type(lo.astype(BF16).astype(F32), jnp.uint32) >> 16
    hi_bits = lax.bitcast_convert_type(hi.astype(BF16).astype(F32), jnp.uint32) & jnp.uint32(0xFFFF0000)
    return lo_bits | hi_bits


def _unpack_bf16_pair(w):
    return (lax.bitcast_convert_type(w << 16, F32),
            lax.bitcast_convert_type(w & jnp.uint32(0xFFFF0000), F32))


def _in_proj_kernel(x_ref, g_ref, w_ref, wt_ref, o_ref, xn_ref, *, tail_tile):
    j = pl.program_id(1)

    @pl.when(j == 0)
    def _():
        x = x_ref[...]
        ms = jnp.mean(x * x, axis=-1, keepdims=True)
        xn_ref[...] = (x * lax.rsqrt(ms + EPS) * g_ref[...]).astype(BF16)

    w = jnp.where(j == tail_tile, wt_ref[...], w_ref[0])
    o_ref[...] = jnp.dot(xn_ref[...], w.astype(BF16), preferred_element_type=F32)


def _in_proj(h, g, w_all, layer):
    t, d = h.shape
    tm = min(PROJ_TM, t)
    tn = PROJ_TN
    a = 3 * ATTN_WIDTH
    rest = IN_COLS - a
    assert a % tn == 0 and IN_COLS_PAD - IN_COLS == tn - rest % tn
    rest_full, a_tiles = rest // tn, a // tn
    tail_tile = rest_full
    w_tail = lax.slice(w_all, (layer, 0, a + rest_full * tn), (layer + 1, d, IN_COLS))[0]
    w_tail = jnp.pad(w_tail, ((0, 0), (0, IN_COLS_PAD - IN_COLS)))

    def w_map(i, j):
        col = jnp.where(j < tail_tile, a_tiles + j, jnp.where(j == tail_tile, 0, j - tail_tile - 1))
        return (layer, 0, col)

    return pl.pallas_call(
        functools.partial(_in_proj_kernel, tail_tile=tail_tile),
        grid=(t // tm, IN_COLS_PAD // tn),
        in_specs=[
            pl.BlockSpec((tm, d), lambda i, j: (i, 0)),
            pl.BlockSpec((1, d), lambda i, j: (0, 0)),
            pl.BlockSpec((1, d, tn), w_map),
            pl.BlockSpec((d, tn), lambda i, j: (0, 0)),
        ],
        out_specs=pl.BlockSpec((tm, tn), lambda i, j: (i, j)),
        out_shape=jax.ShapeDtypeStruct((t, IN_COLS_PAD), F32),
        scratch_shapes=[pltpu.VMEM((tm, d), BF16)],
        compiler_params=_params(("parallel", "arbitrary"), VMEM_LIMIT),
        name="in_proj",
    )(h, g.reshape(1, d), w_all, w_tail)


def _attn_kernel(lam_ref, q_ref, k_ref, v_ref, bias_ref, g_ref, o_ref,
                 kb_ref, vb_ref, qz_ref, m_ref, a_ref, acc_ref, s0_ref, s1_ref, p_ref, *, tq, tk, out_scale):
    qi = pl.program_id(2)
    ratio = tq // tk
    heads = range(ATTN_HPS)

    @pl.when(qi == 0)
    def _():
        for h in heads:
            hl = slice(h * LANES, (h + 1) * LANES)
            kb_ref[h] = k_ref[:, hl].astype(BF16)
            vb_ref[h, :, 0:LANES] = v_ref[:, hl].astype(BF16)
            vb_ref[h, :, LANES:2 * LANES] = jnp.ones((vb_ref.shape[1], LANES), BF16)

    for h in heads:
        q = q_ref[:, h * LANES:(h + 1) * LANES] * (ATTN_QK_DIM ** -0.5 * LOG2E)
        lane = lax.broadcasted_iota(jnp.int32, q.shape, 1)
        qz_ref[h, 0:tq, :] = jnp.where(lane < ATTN_QK_DIM, q, 0.0).astype(BF16)
        qz_ref[h, tq:2 * tq, :] = jnp.where(lane >= ATTN_QK_DIM, q, 0.0).astype(BF16)
    m_ref[...] = jnp.full(m_ref.shape, -jnp.inf, F32)
    acc_ref[...] = jnp.zeros(acc_ref.shape, F32)

    n_col = tk // LANES
    groups = [slice(g * ATTN_ROWS, (g + 1) * ATTN_ROWS) for g in range(2 * tq // ATTN_ROWS)]

    def scores(j, s_ref):
        for h in heads:
            k = kb_ref[h, pl.ds(pl.multiple_of(j * tk, tk), tk), :]
            s_ref[h] = lax.dot_general(qz_ref[h], k, (((1,), (1,)), ((), ())), preferred_element_type=F32)

    def softmax_pv(j, s_ref, biased):
        if biased:
            kind = jnp.clip(j - ratio * qi + ATTN_NEAR, 0, ATTN_NEAR + ratio - 1)
        for h in heads:
            for rows in groups:
                cols = [s_ref[h, rows, i * LANES:(i + 1) * LANES] for i in range(n_col)]
                if biased:
                    b0 = rows.start % tq
                    cols = [x + bias_ref[h, kind, b0:b0 + ATTN_ROWS, i * LANES:(i + 1) * LANES]
                            for i, x in enumerate(cols)]
                    for i, x in enumerate(cols):
                        s_ref[h, rows, i * LANES:(i + 1) * LANES] = x
                mx = functools.reduce(jnp.maximum, cols)
                m_old = m_ref[h, rows, :]
                m_new = jnp.maximum(m_old, jnp.max(mx, axis=-1, keepdims=True))
                m_ref[h, rows, :] = m_new
                a_ref[h, rows, :] = jnp.exp2(m_old - m_new)
        for h in heads:
            for rows in groups:
                m_new = m_ref[h, rows, :]
                for i in range(n_col):
                    p_ref[h, rows, i * LANES:(i + 1) * LANES] = jnp.exp2(
                        s_ref[h, rows, i * LANES:(i + 1) * LANES] - m_new).astype(BF16)
        for h in heads:
            v = vb_ref[h, pl.ds(pl.multiple_of(j * tk, tk), tk), :]
            alpha = a_ref[h]
            acc_ref[h] = (jnp.concatenate([alpha, alpha], axis=1) * acc_ref[h]
                          + jnp.dot(p_ref[h], v, preferred_element_type=F32))

    n_tiles = ratio * (qi + 1)
    first = lax.rem(n_tiles, 2)
    n_far = jnp.maximum(ratio * qi - ATTN_NEAR + 1, 0)
    n_far_pairs = jnp.maximum(n_far - first, 0) // 2
    n_pairs = (n_tiles - first) // 2

    @pl.when(first == 1)
    def _():
        scores(0, s1_ref)
        scores(jnp.minimum(1, n_tiles - 1), s0_ref)
        softmax_pv(0, s1_ref, True)

    @pl.when(first == 0)
    def _():
        scores(0, s0_ref)

    def pair_body(jj, carry, biased):
        j = first + 2 * jj
        scores(j + 1, s1_ref)
        softmax_pv(j, s0_ref, biased)
        scores(jnp.minimum(j + 2, n_tiles - 1), s0_ref)
        softmax_pv(j + 1, s1_ref, biased)
        return carry

    lax.fori_loop(0, n_far_pairs, functools.partial(pair_body, biased=False), 0)
    lax.fori_loop(n_far_pairs, n_pairs, functools.partial(pair_body, biased=True), 0)

    for h in heads:
        inv_l = 1.0 / acc_ref[h, :, LANES:2 * LANES]
        o1 = acc_ref[h, 0:tq, 0:LANES] * inv_l[0:tq]
        o2 = acc_ref[h, tq:2 * tq, 0:LANES] * inv_l[tq:2 * tq]
        o = o1 - lam_ref[0] * o2
        ms = jnp.mean(o * o, axis=-1, keepdims=True)
        o_ref[:, h * LANES:(h + 1) * LANES] = (o * lax.rsqrt(ms + EPS) * g_ref[...] * out_scale).astype(o_ref.dtype)


def _t5_bucket(dist):
    n = jnp.maximum(dist, 0)
    max_exact = REL_BUCKETS // 2
    large = max_exact + (jnp.log(jnp.maximum(n, 1).astype(F32) / max_exact)
                         / math.log(REL_MAX_DIST / max_exact) * (REL_BUCKETS - max_exact)).astype(jnp.int32)
    large = jnp.minimum(large, REL_BUCKETS - 1)
    return jnp.where(n < max_exact, n, large)


def _attn_bias_tiles(rel_bias, tq, tk):
    table = rel_bias.astype(F32)
    far = table[REL_BUCKETS - 1]
    n = tq + tk
    m = jnp.arange(n)

    def lookup(dist):
        bucket = _t5_bucket(dist)
        out = jnp.zeros((ATTN_HEADS,) + dist.shape, F32)
        for b in range(REL_BUCKETS):
            out = jnp.where((bucket == b)[None], table[b][:, None], out)
        return out

    vecs = [jnp.zeros((ATTN_HEADS, n), F32)]
    for rel in range(-ATTN_NEAR + 1, tq // tk):
        dist = tq - m - rel * tk
        vecs.append(jnp.where((dist >= 0)[None], (lookup(dist) - far[:, None]) * LOG2E, -1e30))
    vecs = jnp.stack(vecs, axis=1)[:, :, None, :]
    kinds = vecs.shape[1]
    return pl.pallas_call(
        functools.partial(_bias_tile_kernel, tq=tq, tk=tk),
        grid=(ATTN_HEADS, kinds),
        in_specs=[pl.BlockSpec((1, 1, 1, n), lambda h, k: (h, k, 0, 0))],
        out_specs=pl.BlockSpec((1, 1, tq, tk), lambda h, k: (h, k, 0, 0)),
        out_shape=jax.ShapeDtypeStruct((ATTN_HEADS, kinds, tq, tk), F32),
        compiler_params=_params(("parallel", "parallel")),
        name="attn_bias_tiles",
    )(vecs)


def _bias_tile_kernel(v_ref, o_ref, *, tq, tk):
    x = jnp.broadcast_to(v_ref[0, 0], (tq, tq + tk))
    o_ref[0, 0] = pltpu.roll(x, 0, 1, stride=1, stride_axis=0)[:, tq:]


def _diff_attention(proj, bias, lam, subln_g, lam_init, batch, seq):
    t = proj.shape[0]
    tq = min(ATTN_TQ, seq)
    tk = min(ATTN_TK, seq)
    ratio = tq // tk
    hps = ATTN_HPS
    assert (ATTN_NEAR - 1) * tk >= REL_MAX_DIST and ratio * tk == tq
    assert ATTN_HEADS % hps == 0 and ATTN_COL_BLOCK % hps == 0
    nq = seq // tq
    kernel = functools.partial(_attn_kernel, tq=tq, tk=tk, out_scale=1.0 - lam_init)
    qb, kb, vb = (ATTN_COL_BLOCK // hps, (ATTN_COL_BLOCK + ATTN_HEADS) // hps,
                  (ATTN_COL_BLOCK + 2 * ATTN_HEADS) // hps)
    w = hps * LANES
    return pl.pallas_call(
        kernel,
        grid=(batch, ATTN_HEADS // hps, nq),
        in_specs=[
            pl.BlockSpec(memory_space=pltpu.SMEM),
            pl.BlockSpec((tq, w), lambda b, h, i: (b * nq + i, qb + h)),
            pl.BlockSpec((seq, w), lambda b, h, i: (b, kb + h)),
            pl.BlockSpec((seq, w), lambda b, h, i: (b, vb + h)),
            pl.BlockSpec((hps,) + bias.shape[1:], lambda b, h, i: (h, 0, 0, 0)),
            pl.BlockSpec((1, LANES), lambda b, h, i: (0, 0)),
        ],
        out_specs=pl.BlockSpec((tq, w), lambda b, h, i: (b * nq + i, h)),
        out_shape=jax.ShapeDtypeStruct((t, ATTN_WIDTH), BF16),
        scratch_shapes=[
            pltpu.VMEM((hps, seq, LANES), BF16),
            pltpu.VMEM((hps, seq, 2 * LANES), BF16),
            pltpu.VMEM((hps, 2 * tq, LANES), BF16),
            pltpu.VMEM((hps, 2 * tq, LANES), F32),
            pltpu.VMEM((hps, 2 * tq, LANES), F32),
            pltpu.VMEM((hps, 2 * tq, 2 * LANES), F32),
            pltpu.VMEM((hps, 2 * tq, tk), F32),
            pltpu.VMEM((hps, 2 * tq, tk), F32),
            pltpu.VMEM((hps, 2 * tq, tk), BF16),
        ],
        compiler_params=_params(("parallel", "parallel", "arbitrary"), VMEM_LIMIT),
        name="diff_attn",
    )(lam.reshape(1), proj, proj, proj, bias, subln_g.reshape(1, LANES))


def _s5_kernel(u_ref, perm_ref, permt_ref, wb_ref, wc_ref, pw_ref, d_ref, wglu_ref, o_ref,
               xr_ref, xi_ref, cr_ref, ci_ref, kr_ref, ki_ref, *, tb):
    @pl.when(pl.program_id(1) == 0)
    def _():
        cr_ref[...] = jnp.zeros(cr_ref.shape, F32)
        ci_ref[...] = jnp.zeros(ci_ref.shape, F32)

    un = u_ref[...]
    uh = un.astype(BF16)
    ul = (un - uh.astype(F32)).astype(BF16)
    up = jnp.dot(perm_ref[...], jnp.concatenate([uh, ul], axis=1), preferred_element_type=F32)
    ub = up[:, :S5_WIDTH].astype(BF16)
    u = up[:, :S5_WIDTH] + up[:, S5_WIDTH:]
    nblk = S5_WIDTH // S5_BLK
    sw = S5_LANES // nblk
    tpb = sw // LANES
    for i in range(nblk):
        bu = jnp.dot(ub[:, i * S5_BLK:(i + 1) * S5_BLK], wb_ref[i], preferred_element_type=F32)
        for k in range(tpb):
            xr_ref[i * tpb + k] = bu[:, k * LANES:(k + 1) * LANES]
            xi_ref[i * tpb + k] = bu[:, sw + k * LANES:sw + (k + 1) * LANES]

    sl = tb // SUBLANES
    for c in range(S5_LANES // S5_CHUNK):
        tiles = range(c * S5_CHUNK // LANES, (c + 1) * S5_CHUNK // LANES)
        lss = [slice(n * LANES, (n + 1) * LANES) for n in tiles]
        a = [(pw_ref[0, 0, :, ls], pw_ref[1, 0, :, ls]) for ls in lss]

        def local_step(t, carry, tiles=tiles, a=a):
            rows = pl.ds(pl.multiple_of(t * SUBLANES, SUBLANES), SUBLANES)
            out = []
            for n, (ar, ai), (pr, pi) in zip(tiles, a, carry):
                xr = ar * pr - ai * pi + xr_ref[n, rows, :]
                xi = ar * pi + ai * pr + xi_ref[n, rows, :]
                xr_ref[n, rows, :] = xr
                xi_ref[n, rows, :] = xi
                out.append((xr, xi))
            return tuple(out)

        zero = jnp.zeros((SUBLANES, LANES), F32)
        ends = lax.fori_loop(0, sl, local_step, tuple((zero, zero) for _ in tiles), unroll=2)

        for ls, (er, ei) in zip(lss, ends):
            lr, li = pw_ref[0, sl - 1, 0:1, ls], pw_ref[1, sl - 1, 0:1, ls]
            kr, ki = cr_ref[:, ls], ci_ref[:, ls]
            for sub in range(SUBLANES):
                kr_ref[sub:sub + 1, ls] = kr
                ki_ref[sub:sub + 1, ls] = ki
                kr, ki = (er[sub:sub + 1, :] + lr * kr - li * ki, ei[sub:sub + 1, :] + lr * ki + li * kr)
            cr_ref[:, ls] = kr
            ci_ref[:, ls] = ki

        k8 = [(kr_ref[:, ls], ki_ref[:, ls]) for ls in lss]

        def fix_step(t, carry, tiles=tiles, lss=lss, k8=k8):
            rows = pl.ds(pl.multiple_of(t * SUBLANES, SUBLANES), SUBLANES)
            for n, ls, (kr8, ki8) in zip(tiles, lss, k8):
                pr = pw_ref[0, t, :, ls]
                pi = pw_ref[1, t, :, ls]
                xr_ref[n, rows, :] = xr_ref[n, rows, :] + pr * kr8 - pi * ki8
                xi_ref[n, rows, :] = xi_ref[n, rows, :] + pr * ki8 + pi * kr8
            return carry

        lax.fori_loop(0, sl, fix_step, 0, unroll=2)

    ys = []
    for i in range(nblk):
        xc = jnp.concatenate([xr_ref[i * tpb + k] for k in range(tpb)]
                             + [xi_ref[i * tpb + k] for k in range(tpb)], axis=1)
        ys.append(jnp.dot(xc.astype(BF16), wc_ref[i], preferred_element_type=F32))
    y = jnp.concatenate(ys, axis=1) + d_ref[...] * u
    z = 0.5 * y * (1.0 + jnp.tanh(math.sqrt(2.0 / math.pi) * (y + 0.044715 * (y * y * y))))
    gate = jax.nn.sigmoid(jnp.dot(z.astype(BF16), wglu_ref[...].astype(BF16), preferred_element_type=F32))
    o_ref[...] = jnp.dot(permt_ref[...], (z * gate).astype(BF16),
                         preferred_element_type=F32).astype(o_ref.dtype)


def _s5_prep(a_re, a_im, log_dt, b_re, b_im, c_re, c_im, steps):
    lr = jnp.minimum(a_re.astype(F32), -1e-4)
    li = a_im.astype(F32)
    dt = jnp.exp(log_dt.astype(F32))[:, None]
    mag = jnp.exp(lr * dt)
    ab_re = mag * jnp.cos(li * dt)
    ab_im = mag * jnp.sin(li * dt)
    den = lr * lr + li * li
    nr = ab_re - 1.0
    ni = ab_im
    f_re = (nr * lr + ni * li) / den
    f_im = (ni * lr - nr * li) / den
    br = b_re.astype(F32)
    bi = b_im.astype(F32)
    bb_re = f_re[..., None] * br - f_im[..., None] * bi
    bb_im = f_re[..., None] * bi + f_im[..., None] * br

    nblk = S5_WIDTH // S5_BLK
    gpb = S5_GROUPS // nblk
    eye = jnp.eye(gpb, dtype=F32)

    def bmat(bb):
        bb = bb.reshape(nblk, gpb, S5_STATE, S5_GROUP_CH)
        m = jnp.einsum('ignc,gh->igchn', bb, eye)
        return m.reshape(nblk, gpb * S5_GROUP_CH, gpb * S5_STATE)

    def cmat(cc):
        cc = cc.astype(F32).reshape(nblk, gpb, S5_GROUP_CH, S5_STATE)
        m = jnp.einsum('igcn,gh->ignhc', cc, eye)
        return m.reshape(nblk, gpb * S5_STATE, gpb * S5_GROUP_CH)

    w_b = jnp.concatenate([bmat(bb_re), bmat(bb_im)], axis=2).astype(BF16)
    w_c = jnp.concatenate([cmat(c_re), -cmat(c_im)], axis=1).astype(BF16)

    j = jnp.arange(1, steps + 1, dtype=F32)[:, None, None]
    m = jnp.exp(lr * dt * j)
    pows = jnp.stack([(m * jnp.cos(li * dt * j)).reshape(steps, 1, -1),
                      (m * jnp.sin(li * dt * j)).reshape(steps, 1, -1)])
    pows = jnp.broadcast_to(pows, (2, steps, SUBLANES, S5_LANES))
    return w_b, w_c, pows


def _s5_mixer(proj, w_b, w_c, tabs, d_skip, w_glu, batch, seq):
    t = proj.shape[0]
    tb = min(S5_TB, seq)
    nt = seq // tb
    sl = tb // SUBLANES
    r = np.arange(tb)
    perm = np.zeros((tb, tb), np.float32)
    perm[r, (r % SUBLANES) * sl + r // SUBLANES] = 1.0
    perm_t = jnp.asarray(perm.T, BF16)
    perm = jnp.asarray(perm, BF16)
    return pl.pallas_call(
        functools.partial(_s5_kernel, tb=tb),
        grid=(batch, nt),
        in_specs=[
            pl.BlockSpec((tb, S5_WIDTH), lambda b, i: (b * nt + i, 0)),
            pl.BlockSpec((tb, tb), lambda b, i: (0, 0)),
            pl.BlockSpec((tb, tb), lambda b, i: (0, 0)),
            pl.BlockSpec(w_b.shape, lambda b, i: (0, 0, 0)),
            pl.BlockSpec(w_c.shape, lambda b, i: (0, 0, 0)),
            pl.BlockSpec(tabs.shape, lambda b, i: (0, 0, 0, 0)),
            pl.BlockSpec((1, S5_WIDTH), lambda b, i: (0, 0)),
            pl.BlockSpec((S5_WIDTH, S5_WIDTH), lambda b, i: (0, 0)),
        ],
        out_specs=pl.BlockSpec((tb, S5_WIDTH), lambda b, i: (b * nt + i, 0)),
        out_shape=jax.ShapeDtypeStruct((t, S5_WIDTH), BF16),
        scratch_shapes=[
            pltpu.VMEM((S5_LANES // LANES, tb, LANES), F32),
            pltpu.VMEM((S5_LANES // LANES, tb, LANES), F32),
            pltpu.VMEM((1, S5_LANES), F32),
            pltpu.VMEM((1, S5_LANES), F32),
            pltpu.VMEM((SUBLANES, S5_LANES), F32),
            pltpu.VMEM((SUBLANES, S5_LANES), F32),
        ],
        compiler_params=_params(("parallel", "arbitrary"), VMEM_LIMIT),
        name="s5_mixer",
    )(proj, perm, perm_t, w_b, w_c, tabs, d_skip.reshape(1, S5_WIDTH), w_glu)


def _hgrn_consts(c):
    t = np.arange(c)[:, None]
    u = np.arange(c)[None, :]
    mats = [u <= t]
    masks = [np.eye(c, dtype=bool)]
    h = 1
    while h < c:
        mid = (t // (2 * h)) * 2 * h + h
        if 2 * h < SUBLANES:
            mats.append(np.where(t >= mid, (u >= mid) & (u <= t), (u > t) & (u < mid)))
        masks.append(((u // (2 * h)) == (t // (2 * h))) & (t >= mid) & (u < mid))
        h *= 2
    return (np.concatenate(mats, axis=0).astype(np.float32), np.stack(masks).astype(np.float32))


def _hgrn_kernel(q_ref, f_ref, i_ref, g_ref, lb_ref, ng_ref, mall_ref, mask_ref, o_ref,
                 e_ref, st_ref, *, c):
    @pl.when(pl.program_id(1) == 0)
    def _():
        st_ref[...] = jnp.zeros(st_ref.shape, F32)

    lb = lb_ref[...]
    fl = f_ref[...]
    la = jnp.log(lb)
    lbb = jnp.log1p(-lb) + (jnp.minimum(fl, 0.0) - jnp.log1p(jnp.exp(-jnp.abs(fl))))
    logf = jnp.maximum(la, lbb) + jnp.log1p(jnp.exp(-jnp.abs(la - lbb)))
    hi = logf.astype(BF16)
    lo = (logf - hi.astype(F32)).astype(BF16)
    e2 = jnp.dot(mall_ref[...], jnp.concatenate([hi, lo], axis=1), preferred_element_type=F32)
    es = e2[:, :HGRN_WIDTH] + e2[:, HGRN_WIDTH:]
    n_levels = mask_ref.shape[0] - 1
    n_small = mall_ref.shape[0] // c - 1
    gall = es[0:c]
    e_ref[0:c, :] = gall
    e_ref[c:2 * c, :] = gall[c - 1:c, :] - gall
    e_ref[2 * c:(2 + n_small) * c, :] = es[c:]
    for lv in range(n_small, n_levels):
        blk = 2 << lv
        g3 = gall.reshape(c // blk, blk, HGRN_WIDTH)
        gm = jnp.broadcast_to(g3[:, blk // 2 - 1:blk // 2, :], g3.shape)
        e_ref[(2 + lv) * c:(3 + lv) * c, :] = (-jnp.abs(g3 - gm)).reshape(c, HGRN_WIDTH)

    nt = (((1,), (1,)), ((), ()))
    for hd in range(HGRN_HEADS):
        ls = slice(hd * HGRN_DIM, (hd + 1) * HGRN_DIM)
        qr = q_ref[:, ls]
        q = qr * jax.nn.sigmoid(qr) * (HGRN_DIM ** -0.5)
        k = (1.0 - lb[:, ls]) * jax.nn.sigmoid(-fl[:, ls])
        vb = i_ref[:, ls].astype(BF16)
        p = lax.dot_general(q.astype(BF16), k.astype(BF16), nt, preferred_element_type=F32) * mask_ref[0]
        for lv in range(n_levels):
            fac = jnp.exp(e_ref[(2 + lv) * c:(3 + lv) * c, ls])
            s = lax.dot_general((q * fac).astype(BF16), (k * fac).astype(BF16), nt,
                                preferred_element_type=F32)
            p = p + s * mask_ref[lv + 1]
        o = jnp.dot(p.astype(BF16), vb, preferred_element_type=F32)
        gcum = e_ref[0:c, ls]
        st = st_ref[hd]
        o = o + lax.dot_general((q * jnp.exp(gcum)).astype(BF16), st.astype(BF16), nt,
                                preferred_element_type=F32)
        kd = (k * jnp.exp(e_ref[c:2 * c, ls])).astype(BF16)
        st_ref[hd] = st * jnp.exp(gcum[c - 1:c, :]) + lax.dot_general(
            vb, kd, (((0,), (0,)), ((), ())), preferred_element_type=F32)
        ms = jnp.mean(o * o, axis=-1, keepdims=True)
        o = o * lax.rsqrt(ms + EPS) * ng_ref[...] * jax.nn.sigmoid(g_ref[:, ls])
        o_ref[:, ls] = o.astype(o_ref.dtype)


def _hgrn_mixer(proj, lb, norm_g, batch, seq):
    t = proj.shape[0]
    c = min(HGRN_C, seq)
    nc = seq // c
    m_all, masks = _hgrn_consts(c)
    m_all = jnp.asarray(m_all, BF16)
    masks = jnp.asarray(masks, F32)

    def col(j):
        return pl.BlockSpec((c, HGRN_WIDTH), lambda b, i: (b * nc + i, j))

    return pl.pallas_call(
        functools.partial(_hgrn_kernel, c=c),
        grid=(batch, nc),
        in_specs=[
            col(1), col(2), col(3), col(4),
            pl.BlockSpec((1, HGRN_WIDTH), lambda b, i: (0, 0)),
            pl.BlockSpec((1, HGRN_DIM), lambda b, i: (0, 0)),
            pl.BlockSpec(m_all.shape, lambda b, i: (0, 0)),
            pl.BlockSpec(masks.shape, lambda b, i: (0, 0, 0)),
        ],
        out_specs=pl.BlockSpec((c, HGRN_WIDTH), lambda b, i: (b * nc + i, 0)),
        out_shape=jax.ShapeDtypeStruct((t, HGRN_WIDTH), BF16),
        scratch_shapes=[
            pltpu.VMEM(((masks.shape[0] + 1) * c, HGRN_WIDTH), F32),
            pltpu.VMEM((HGRN_HEADS, HGRN_DIM, HGRN_DIM), F32),
        ],
        compiler_params=_params(("parallel", "arbitrary"), VMEM_LIMIT),
        name="hgrn2_mixer",
    )(proj, proj, proj, proj, lb.reshape(1, HGRN_WIDTH), norm_g.reshape(1, HGRN_DIM), m_all, masks)


def _out_proj_kernel(h_ref, a_ref, s_ref, r_ref, w_ref, gn_ref, rw_ref,
                     ho_ref, hn_ref, eid_ref, gate_ref):
    s_lo, r_lo = ATTN_WIDTH, ATTN_WIDTH + S5_WIDTH
    acc = jnp.dot(a_ref[...], w_ref[0:s_lo, :], preferred_element_type=F32)
    acc = acc + jnp.dot(s_ref[...], w_ref[s_lo:r_lo, :], preferred_element_type=F32)
    acc = acc + jnp.dot(r_ref[...], w_ref[r_lo:, :], preferred_element_type=F32)
    h = h_ref[...] + acc
    ho_ref[...] = h
    ms = jnp.mean(h * h, axis=-1, keepdims=True)
    hn = h * lax.rsqrt(ms + EPS) * gn_ref[...]
    half = hn.shape[1] // 2
    hn_ref[...] = _pack_bf16_pair(hn[:, :half], hn[:, half:])

    hh = hn.astype(BF16)
    hl = (hn - hh.astype(F32)).astype(BF16)
    both = jnp.dot(hh, rw_ref[...], preferred_element_type=F32)
    logits = (both[:, :LANES] + both[:, LANES:]
              + jnp.dot(hl, rw_ref[:, :LANES], preferred_element_type=F32))
    lane = lax.broadcasted_iota(jnp.int32, logits.shape, 1).astype(F32)
    neg = -jnp.inf
    big = 1e9
    is_group = jnp.where(lane >= N_EXPERTS, jnp.where(lane < N_EXPERTS + N_GROUPS, 1.0, 0.0), 0.0)
    gl = jnp.where(is_group > 0, logits, neg)
    gmax = jnp.max(gl, axis=-1, keepdims=True)
    g_lane = jnp.min(jnp.where(gl == gmax, lane, big), axis=-1, keepdims=True)
    p_g = 1.0 / jnp.sum(jnp.exp(gl - gmax), axis=-1, keepdims=True)
    lo_lane = (g_lane - N_EXPERTS) * EXPERTS_PER_GROUP
    in_group = jnp.where(lane >= lo_lane, jnp.where(lane < lo_lane + EXPERTS_PER_GROUP, 1.0, 0.0), 0.0)
    el = jnp.where(in_group > 0, logits, neg)
    t1 = jnp.max(el, axis=-1, keepdims=True)
    i1 = jnp.min(jnp.where(el == t1, lane, big), axis=-1, keepdims=True)
    el2 = jnp.where(lane == i1, neg, el)
    t2 = jnp.max(el2, axis=-1, keepdims=True)
    i2 = jnp.min(jnp.where(el2 == t2, lane, big), axis=-1, keepdims=True)
    e21 = jnp.exp(t2 - t1)
    g1 = p_g / (1.0 + e21)
    g2 = p_g * e21 / (1.0 + e21)
    eid_ref[...] = jnp.where(lane == 0, i1, jnp.where(lane == 1, i2, 0.0)).astype(jnp.int32)
    gate_ref[...] = jnp.where(lane == 0, g1, jnp.where(lane == 1, g2, 0.0))


def _out_proj(h, attn, s5, hg, w_out, gn, r_w):
    t, d = h.shape
    tm = min(OUT_TM, t)

    def rows(w):
        return pl.BlockSpec((tm, w), lambda i: (i, 0))

    def full(a):
        return pl.BlockSpec(a.shape, lambda i: (0,) * a.ndim)

    return pl.pallas_call(
        _out_proj_kernel,
        grid=(t // tm,),
        in_specs=[rows(d), rows(ATTN_WIDTH), rows(S5_WIDTH), rows(HGRN_WIDTH),
                  full(w_out), pl.BlockSpec((1, d), lambda i: (0, 0)),
                  full(r_w)],
        out_specs=[rows(d), rows(d // 2), rows(LANES), rows(LANES)],
        out_shape=[jax.ShapeDtypeStruct((t, d), F32), jax.ShapeDtypeStruct((t, d // 2), jnp.uint32),
                   jax.ShapeDtypeStruct((t, LANES), jnp.int32), jax.ShapeDtypeStruct((t, LANES), F32)],
        compiler_params=_params(("parallel",), VMEM_LIMIT),
        name="out_proj_router",
    )(h, attn, s5, hg, w_out, gn.reshape(1, d), r_w)


def _moe_plan(eid, tm):
    flat = eid.reshape(-1)
    n_slots = flat.shape[0]
    onehot = (flat[:, None] == jnp.arange(N_EXPERTS, dtype=jnp.int32)[None, :]).astype(jnp.int32)
    order = jnp.argsort(flat, stable=True).astype(jnp.int32)
    sorted_pos = jnp.argsort(order).astype(jnp.int32)
    counts = jnp.sum(onehot, axis=0)
    tiles = (counts + tm - 1) // tm
    tile_end = jnp.cumsum(tiles)
    tile_start = tile_end - tiles
    slot_start = jnp.cumsum(counts) - counts
    pos = jnp.sum(onehot * (tile_start * tm - slot_start)[None, :], axis=1) + sorted_pos
    tok_sorted = order // TOP_K
    tok_sorted = jnp.pad(tok_sorted, (0, tm))
    nt_max = (n_slots + N_EXPERTS * (tm - 1)) // tm
    n_used = tile_end[-1]
    j = jnp.minimum(jnp.arange(nt_max, dtype=jnp.int32), n_used - 1)
    tile_expert = jnp.sum((j[:, None] >= tile_end[None, :]).astype(jnp.int32), axis=1)
    onehot_t = (tile_expert[:, None] == jnp.arange(N_EXPERTS, dtype=jnp.int32)[None, :]).astype(jnp.int32)
    tile_slot0 = jnp.sum(onehot_t * (slot_start - tile_start * tm)[None, :], axis=1) + j * tm
    used = (tiles > 0).astype(jnp.int32)
    ordinal = jnp.cumsum(used) - used
    ids = jnp.arange(N_EXPERTS, dtype=jnp.int32)
    later = jnp.where((ids[None, :] > ids[:, None]) & (used[None, :] > 0), ids[None, :], N_EXPERTS)
    nxt = jnp.min(later, axis=1)
    nxt = jnp.where(nxt >= N_EXPERTS, -1, nxt)
    tile_wslot = jnp.sum(onehot_t * (ordinal % 2)[None, :], axis=1)
    tile_next = jnp.sum(onehot_t * nxt[None, :], axis=1)
    return dict(pos=pos.astype(jnp.int32), tok_sorted=tok_sorted, tile_expert=tile_expert.astype(jnp.int32),
                tile_slot0=tile_slot0.astype(jnp.int32), tile_wslot=tile_wslot.astype(jnp.int32),
                tile_next=tile_next.astype(jnp.int32), n_used=n_used.reshape(1).astype(jnp.int32),
                nt_max=nt_max)


def _expert_kernel(tok_ref, te_ref, s0_ref, ws_ref, ne_ref, nu_ref, hn_ref, wg_ref, wu_ref, wd_ref, y_ref,
                   xa_ref, xb_ref, wgf_ref, wuf_ref, wdf_ref, wgb_ref, wub_ref, wdb_ref, sem, wsem,
                   *, tm, layer):
    j = pl.program_id(0)
    n_used = nu_ref[0]
    bufs = (xa_ref, xb_ref)

    def weight_copies(expert, slot):
        return [pltpu.make_async_copy(src.at[layer, expert], dst.at[slot], wsem.at[slot])
                for src, dst in ((wg_ref, wgf_ref), (wu_ref, wuf_ref), (wd_ref, wdf_ref))]

    def row_copy(buf, r, tok):
        return pltpu.make_async_copy(hn_ref.at[pl.ds(tok, 1)], bufs[buf].at[pl.ds(r, 1)], sem.at[buf])

    def wait_gather(buf):
        def wait(r, carry):
            row_copy(buf, r, 0).wait()
            return carry

        lax.fori_loop(0, tm, wait, 0, unroll=DMA_UNROLL)

    @pl.when(j == 0)
    def _():
        for cp in weight_copies(te_ref[0], ws_ref[0]):
            cp.start(priority=WEIGHT_DMA_PRIORITY)
        base = s0_ref[0]

        def issue(r, carry):
            row_copy(0, r, tok_ref[base + r]).start()
            return carry

        lax.fori_loop(0, tm, issue, 0, unroll=DMA_UNROLL)

    def tile(buf):
        wait_gather(buf)
        prev = te_ref[jnp.maximum(j - 1, 0)]

        @pl.when(jnp.logical_or(j == 0, te_ref[j] != prev))
        def _():
            slot = ws_ref[j]

            @pl.when(ne_ref[j] >= 0)
            def _():
                for cp in weight_copies(ne_ref[j], 1 - slot):
                    cp.start(priority=WEIGHT_DMA_PRIORITY)

            for cp in weight_copies(te_ref[j], slot):
                cp.wait()
            wgb_ref[...] = wgf_ref[slot].astype(BF16)
            wub_ref[...] = wuf_ref[slot].astype(BF16)
            wdb_ref[...] = wdf_ref[slot].astype(BF16)

        nbase = s0_ref[jnp.minimum(j + 1, n_used - 1)]
        for r in range(tm):
            row_copy(1 - buf, r, tok_ref[nbase + r]).start()

        x_lo, x_hi = _unpack_bf16_pair(bufs[buf][...])
        x = jnp.concatenate([x_lo.astype(BF16), x_hi.astype(BF16)], axis=1)
        g = jnp.dot(x, wgb_ref[...], preferred_element_type=F32)
        u = jnp.dot(x, wub_ref[...], preferred_element_type=F32)
        hmid = (g * jax.nn.sigmoid(g) * u).astype(BF16)
        y = jnp.dot(hmid, wdb_ref[...], preferred_element_type=F32)
        half = y.shape[1] // 2
        y_ref[...] = _pack_bf16_pair(y[:, :half], y[:, half:])

        @pl.when(j == n_used - 1)
        def _():
            wait_gather(1 - buf)

    for parity in range(2):
        @pl.when(jnp.logical_and(j < n_used, lax.rem(j, 2) == parity))
        def _(parity=parity):
            tile(parity)

    @pl.when(j >= n_used)
    def _():
        y_ref[...] = jnp.zeros(y_ref.shape, y_ref.dtype)


def _moe_experts(hn, plan, w_gate, w_up, w_down, layer, tm):
    t, dp = hn.shape
    d, de = w_gate.shape[2], w_gate.shape[3]
    nt_max = plan["nt_max"]
    grid_spec = pltpu.PrefetchScalarGridSpec(
        num_scalar_prefetch=6,
        grid=(nt_max,),
        in_specs=[pl.BlockSpec(memory_space=pl.ANY)] * 4,
        out_specs=pl.BlockSpec((tm, dp), lambda j, *_: (j, 0)),
        scratch_shapes=[pltpu.VMEM((tm, dp), jnp.uint32), pltpu.VMEM((tm, dp), jnp.uint32),
                        pltpu.VMEM((2, d, de), F32), pltpu.VMEM((2, d, de), F32), pltpu.VMEM((2, de, d), F32),
                        pltpu.VMEM((d, de), BF16), pltpu.VMEM((d, de), BF16), pltpu.VMEM((de, d), BF16),
                        pltpu.SemaphoreType.DMA((2,)), pltpu.SemaphoreType.DMA((2,))],
    )
    return pl.pallas_call(
        functools.partial(_expert_kernel, tm=tm, layer=layer),
        grid_spec=grid_spec,
        out_shape=jax.ShapeDtypeStruct((nt_max * tm, dp), jnp.uint32),
        compiler_params=_params(("arbitrary",), VMEM_LIMIT),
        name="moe_experts",
    )(plan["tok_sorted"], plan["tile_expert"], plan["tile_slot0"], plan["tile_wslot"], plan["tile_next"],
      plan["n_used"], hn, w_gate, w_up, w_down)


def _combine_kernel(pos_ref, h_ref, gate_ref, gn_ref, y_ref, o_ref, ba_ref, bb_ref, sem, *, tm, final_norm):
    i = pl.program_id(0)
    n = pl.num_programs(0)
    bufs = (ba_ref, bb_ref)

    def row_copy(b, k, r, src_row):
        return pltpu.make_async_copy(y_ref.at[pl.ds(src_row, 1)], bufs[b].at[k, pl.ds(r, 1)], sem.at[b])

    def wait_gather(b):
        def wait(r, carry):
            for k in range(TOP_K):
                row_copy(b, k, r, 0).wait()
            return carry

        lax.fori_loop(0, tm, wait, 0, unroll=DMA_UNROLL)

    @pl.when(i == 0)
    def _():
        def issue(r, carry):
            for k in range(TOP_K):
                row_copy(0, k, r, pos_ref[r * TOP_K + k]).start()
            return carry

        lax.fori_loop(0, tm, issue, 0, unroll=DMA_UNROLL)

    def tile(b):
        wait_gather(b)
        base = jnp.minimum(i + 1, n - 1) * (tm * TOP_K)
        for r in range(tm):
            for k in range(TOP_K):
                row_copy(1 - b, k, r, pos_ref[base + r * TOP_K + k]).start(priority=(r * TOP_K + k) % 2)

        gates = gate_ref[...]
        y0_lo, y0_hi = _unpack_bf16_pair(bufs[b][0])
        y1_lo, y1_hi = _unpack_bf16_pair(bufs[b][1])
        g0, g1 = gates[:, 0:1], gates[:, 1:2]
        h = h_ref[...] + jnp.concatenate([g0 * y0_lo + g1 * y1_lo, g0 * y0_hi + g1 * y1_hi], axis=1)
        if final_norm:
            ms = jnp.mean(h * h, axis=-1, keepdims=True)
            h = h * lax.rsqrt(ms + EPS) * gn_ref[...]
        o_ref[...] = h

        @pl.when(i == n - 1)
        def _():
            wait_gather(1 - b)

    for parity in range(2):
        @pl.when(lax.rem(i, 2) == parity)
        def _(parity=parity):
            tile(parity)


def _moe_combine(h, gates, ys, pos, final_g, final_norm):
    t, d = h.shape
    tm = min(COMB_TM, t)
    grid_spec = pltpu.PrefetchScalarGridSpec(
        num_scalar_prefetch=1,
        grid=(t // tm,),
        in_specs=[
            pl.BlockSpec((tm, d), lambda i, p: (i, 0)),
            pl.BlockSpec((tm, LANES), lambda i, p: (i, 0)),
            pl.BlockSpec((1, d), lambda i, p: (0, 0)),
            pl.BlockSpec(memory_space=pl.ANY),
        ],
        out_specs=pl.BlockSpec((tm, d), lambda i, p: (i, 0)),
        scratch_shapes=[pltpu.VMEM((TOP_K, tm, d // 2), jnp.uint32), pltpu.VMEM((TOP_K, tm, d // 2), jnp.uint32),
                        pltpu.SemaphoreType.DMA((2,))],
    )
    return pl.pallas_call(
        functools.partial(_combine_kernel, tm=tm, final_norm=final_norm),
        grid_spec=grid_spec,
        out_shape=jax.ShapeDtypeStruct((t, d), F32),
        compiler_params=_params(("arbitrary",), VMEM_LIMIT),
        name="moe_combine",
    )(pos, h, gates, final_g.reshape(1, d), ys)


def kernel(x, w_in, w_out, mix_norm_g, ffn_norm_g, rel_bias, diff_lambda, attn_subln_g, s5_a_re, s5_a_im, s5_log_dt, s5_b_re, s5_b_im, s5_c_re, s5_c_im, s5_d, s5_w_glu, hgrn_lb_logits, hgrn_norm_g, moe_w_group, moe_w_router, moe_w_gate, moe_w_up, moe_w_down, final_norm_g):
    batch, seq, d = x.shape
    depth = w_in.shape[0]
    t = batch * seq
    h = x.reshape(t, d)

    lb_cum = jnp.cumsum(jax.nn.softmax(hgrn_lb_logits.astype(F32), axis=0), axis=0)
    lb_all = lb_cum - lb_cum[0:1]
    attn_bias = _attn_bias_tiles(rel_bias, min(ATTN_TQ, seq), min(ATTN_TK, seq))

    for l in range(depth):
        proj = _in_proj(h, mix_norm_g[l], w_in, l)

        lam_init = 0.8 - 0.6 * math.exp(-0.3 * l)
        lv = diff_lambda[l].astype(F32)
        lam = jnp.exp(jnp.sum(lv[0] * lv[1])) - jnp.exp(jnp.sum(lv[2] * lv[3])) + lam_init
        attn = _diff_attention(proj, attn_bias, lam, attn_subln_g[l], lam_init, batch, seq)

        w_b, w_c, tabs = _s5_prep(s5_a_re[l], s5_a_im[l], s5_log_dt[l], s5_b_re[l], s5_b_im[l],
                                  s5_c_re[l], s5_c_im[l], min(S5_TB, seq) // SUBLANES)
        s5 = _s5_mixer(proj, w_b, w_c, tabs, s5_d[l], s5_w_glu[l], batch, seq)

        hg = _hgrn_mixer(proj, lb_all[l], hgrn_norm_g[l], batch, seq)

        wo = w_out[l].astype(BF16)
        w_r = jnp.concatenate([moe_w_router[l], moe_w_group[l]], axis=1).astype(F32)
        w_r = jnp.pad(w_r, ((0, 0), (0, LANES - w_r.shape[1])))
        r_hi = w_r.astype(BF16)
        r_lo = (w_r - r_hi.astype(F32)).astype(BF16)
        h, hn, eid, gates = _out_proj(h, attn, s5, hg, wo, ffn_norm_g[l], jnp.concatenate([r_hi, r_lo], axis=1))

        plan = _moe_plan(eid[:, :TOP_K], MOE_TM)
        ys = _moe_experts(hn, plan, moe_w_gate, moe_w_up, moe_w_down, l, MOE_TM)
        h = _moe_combine(h, gates, ys, plan["pos"], final_norm_g, final_norm=(l == depth - 1))

    return h.reshape(batch, seq, d)
```
